```python
import jax, jax.numpy as jnp
from jax import lax
import numpy as np

D_MODEL = 2048
BATCH = 4
SEQ = 2048
DEPTH = 1
DEC_BATCH = 128
DEC_SEQ = 1
PAST_LEN = 16384
PAGE_SIZE = 128

D_A = D_MODEL // 2
HEAD = 64
H_A = D_A // HEAD
LORA_W = 64
LORA_A = 64
D_B = D_MODEL // 2
CONV_K = 31
PLE_DIM = 256
SHIFT_W = 3 * D_A + LORA_W + LORA_A
N_IN = SHIFT_W + D_A + 2 * D_B + D_B + 2 * D_MODEL
RMS_EPS = 1e-6
LN_EPS = 1e-5
GN_EPS = 64e-5

kernel_name = 'rwkv7_conformer_gated_hybrid_step'


def _rmsnorm(x, g):
    xf = x.astype(jnp.float32)
    xf = xf * lax.rsqrt(jnp.mean(xf * xf, axis=-1, keepdims=True) + RMS_EPS)
    return (xf * g.astype(jnp.float32)).astype(x.dtype)


def _wkv7_scan(S0, r, w, k, v, a, b):
    def step(S, inp):
        r_t, w_t, k_t, v_t, a_t, b_t = inp
        sa = jnp.einsum('bhij,bhj->bhi', S, a_t)
        S = S * w_t[:, :, None, :] + sa[..., None] * b_t[:, :, None, :] + v_t[..., None] * k_t[:, :, None, :]
        return S, jnp.einsum('bhij,bhj->bhi', S, r_t)
    seq = tuple(jnp.swapaxes(t.astype(jnp.float32), 0, 1) for t in (r, w, k, v, a, b))
    S, ys = lax.scan(step, S0.astype(jnp.float32), seq)
    return S, jnp.swapaxes(ys, 0, 1)


def _layer(x, p, shift_prev, wkv_prev, conv_prev, norm_g, w_in, shift_mu, w0, w_lora_b, a0, a_lora_b,
           k_k, k_a, r_k, lnx_g, lnx_b, w_proj_a, conv_w, conv_b, cln_g, cln_b, w_proj_b, w_out,
           w_ple, w_ple_gate):
    f32 = jnp.float32
    Bn, T, _ = x.shape
    dt = x.dtype
    xn = _rmsnorm(x, norm_g)
    proj = jnp.einsum('btd,dn->btn', xn, w_in)
    o1 = SHIFT_W
    o2 = o1 + D_A
    o3 = o2 + 2 * D_B
    o4 = o3 + D_B
    p_shift = proj[..., :o1]
    gate_a = proj[..., o1:o2]
    glu_in = proj[..., o2:o3]
    gate_b = proj[..., o3:o4]
    merge = proj[..., o4:]

    prev = jnp.concatenate([shift_prev[:, None].astype(dt), p_shift[:, :-1]], axis=1)
    xs = p_shift + shift_mu * (prev - p_shift)
    new_shift = p_shift[:, -1]
    r = xs[..., :D_A]
    k = xs[..., D_A:2 * D_A]
    v = xs[..., 2 * D_A:3 * D_A]
    xw = xs[..., 3 * D_A:3 * D_A + LORA_W]
    xa = xs[..., 3 * D_A + LORA_W:]
    w_log = -jax.nn.softplus(-(w0 + jnp.tanh(xw) @ w_lora_b).astype(f32)) - 0.5
    decay = jnp.exp(-jnp.exp(w_log))
    a = jax.nn.sigmoid((a0 + xa @ a_lora_b).astype(f32))
    hd = lambda t: t.reshape(Bn, T, H_A, HEAD)
    kk = hd(k.astype(f32) * k_k)
    kk = kk / jnp.maximum(jnp.sqrt(jnp.sum(kk * kk, axis=-1, keepdims=True)), 1e-12)
    kf = hd(k.astype(f32) * (1.0 + (a - 1.0) * k_a))
    rf = hd(r.astype(f32))
    vf = hd(v.astype(f32))
    S, o = _wkv7_scan(wkv_prev, rf, hd(decay), kf, vf, -kk, kk * hd(a))
    mu = jnp.mean(o, axis=-1, keepdims=True)
    var = jnp.mean(jnp.square(o - mu), axis=-1, keepdims=True)
    o = ((o - mu) * lax.rsqrt(var + GN_EPS)).reshape(Bn, T, D_A) * lnx_g + lnx_b
    bonus = jnp.sum(rf * kf * r_k, axis=-1, keepdims=True) * vf
    o = o + bonus.reshape(Bn, T, D_A)
    y_a = jnp.einsum('btc,cd->btd', (o * jax.nn.silu(gate_a.astype(f32))).astype(dt), w_proj_a)

    u = glu_in[..., :D_B] * jax.nn.sigmoid(glu_in[..., D_B:])
    ucat = jnp.concatenate([conv_prev.astype(dt), u], axis=1)
    c = lax.conv_general_dilated(ucat, conv_w[:, None, :].astype(dt), (1,), 'VALID',
                                 dimension_numbers=('NWC', 'WIO', 'NWC'),
                                 feature_group_count=D_B) + conv_b
    new_conv = ucat[:, -(CONV_K - 1):]
    cf = c.astype(f32)
    cm = jnp.mean(cf, axis=-1, keepdims=True)
    cv = jnp.mean(jnp.square(cf - cm), axis=-1, keepdims=True)
    cf = (cf - cm) * lax.rsqrt(cv + LN_EPS) * cln_g + cln_b
    cb = jax.nn.silu(cf) * jax.nn.silu(gate_b.astype(f32))
    y_b = jnp.einsum('btc,cd->btd', cb.astype(dt), w_proj_b)

    ga = merge[..., :D_MODEL]
    gb = merge[..., D_MODEL:]
    m = jax.nn.sigmoid(ga) * y_a + jax.nn.sigmoid(gb) * y_b
    h = x + jnp.einsum('btd,de->bte', m, w_out)
    h = h + jax.nn.sigmoid(jnp.einsum('btd,de->bte', h, w_ple_gate)) * jnp.einsum('btp,pd->btd', p, w_ple)
    return h, new_shift, S.astype(wkv_prev.dtype), new_conv


def setup_inputs(seed: int = 0) -> dict:
    key = jax.random.key(seed)
    ks = jax.random.split(key, 32)
    nrm = lambda k, s, sc: jax.random.normal(k, s, jnp.float32) * sc
    L = DEPTH
    return {
        'x_prompt': nrm(ks[0], (BATCH, SEQ, D_MODEL), 1.0),
        'x_sample': nrm(ks[1], (DEC_BATCH, DEC_SEQ, D_MODEL), 1.0),
        'state_shift': nrm(ks[2], (L, DEC_BATCH, SHIFT_W), 1.0),
        'state_wkv': nrm(ks[3], (L, DEC_BATCH, H_A, HEAD, HEAD), 0.3),
        'state_conv': nrm(ks[4], (L, DEC_BATCH, CONV_K - 1, D_B), 0.5),
        'p_prompt': nrm(ks[5], (L, BATCH, SEQ, PLE_DIM), 1.0),
        'p_sample': nrm(ks[6], (L, DEC_BATCH, DEC_SEQ, PLE_DIM), 1.0),
        'norm_g': 1.0 + nrm(ks[7], (L, D_MODEL), 0.01),
        'w_in': nrm(ks[8], (L, D_MODEL, N_IN), D_MODEL ** -0.5),
        'shift_mu': jax.random.uniform(ks[9], (L, SHIFT_W), jnp.float32),
        'w0': jax.random.uniform(ks[10], (L, D_A), jnp.float32, -6.0, 0.0),
        'w_lora_b': nrm(ks[11], (L, LORA_W, D_A), 0.1 * LORA_W ** -0.5),
        'a0': nrm(ks[12], (L, D_A), 0.1),
        'a_lora_b': nrm(ks[13], (L, LORA_A, D_A), 0.1 * LORA_A ** -0.5),
        'k_k': 0.85 + nrm(ks[14], (L, D_A), 0.02),
        'k_a': 1.0 + nrm(ks[15], (L, D_A), 0.02),
        'r_k': nrm(ks[16], (L, H_A, HEAD), 0.1),
        'lnx_g': 1.0 + nrm(ks[17], (L, D_A), 0.01),
        'lnx_b': nrm(ks[18], (L, D_A), 0.01),
        'w_proj_a': nrm(ks[19], (L, D_A, D_MODEL), D_A ** -0.5),
        'conv_w': nrm(ks[20], (L, CONV_K, D_B), CONV_K ** -0.5),
        'conv_b': nrm(ks[21], (L, D_B), 0.01),
        'cln_g': 1.0 + nrm(ks[22], (L, D_B), 0.01),
        'cln_b': nrm(ks[23], (L, D_B), 0.01),
        'w_proj_b': nrm(ks[24], (L, D_B, D_MODEL), D_B ** -0.5),
        'w_out': nrm(ks[25], (L, D_MODEL, D_MODEL), D_MODEL ** -0.5),
        'w_ple': nrm(ks[26], (L, PLE_DIM, D_MODEL), PLE_DIM ** -0.5),
        'w_ple_gate': nrm(ks[27], (L, D_MODEL, D_MODEL), D_MODEL ** -0.5),
        'final_g': 1.0 + nrm(ks[28], (D_MODEL,), 0.01),
    }


def reference(x_prompt, x_sample, state_shift, state_wkv, state_conv, p_prompt, p_sample,
              norm_g, w_in, shift_mu, w0, w_lora_b, a0, a_lora_b, k_k, k_a, r_k, lnx_g, lnx_b,
              w_proj_a, conv_w, conv_b, cln_g, cln_b, w_proj_b, w_out, w_ple, w_ple_gate, final_g):
    Bp = x_prompt.shape[0]
    hp, hs = x_prompt, x_sample
    ps_shift, ps_wkv, ps_conv = [], [], []
    ss_shift, ss_wkv, ss_conv = [], [], []
    for i in range(DEPTH):
        lw = (norm_g[i], w_in[i], shift_mu[i], w0[i], w_lora_b[i], a0[i], a_lora_b[i], k_k[i], k_a[i],
              r_k[i], lnx_g[i], lnx_b[i], w_proj_a[i], conv_w[i], conv_b[i], cln_g[i], cln_b[i],
              w_proj_b[i], w_out[i], w_ple[i], w_ple_gate[i])
        hp, sh, wk, cv = _layer(hp, p_prompt[i],
                                jnp.zeros((Bp, SHIFT_W), x_prompt.dtype),
                                jnp.zeros((Bp, H_A, HEAD, HEAD), jnp.float32),
                                jnp.zeros((Bp, CONV_K - 1, D_B), x_prompt.dtype), *lw)
        ps_shift.append(sh)
        ps_wkv.append(wk)
        ps_conv.append(cv)
        hs, sh, wk, cv = _layer(hs, p_sample[i], state_shift[i], state_wkv[i], state_conv[i], *lw)
        ss_shift.append(sh)
        ss_wkv.append(wk)
        ss_conv.append(cv)
    y_prompt = _rmsnorm(hp, final_g)
    y_sample = _rmsnorm(hs, final_g)
    return (y_prompt, y_sample, jnp.stack(ps_shift), jnp.stack(ps_wkv), jnp.stack(ps_conv),
            jnp.stack(ss_shift), jnp.stack(ss_wkv), jnp.stack(ss_conv))
```

```python
import functools

import jax
import jax.numpy as jnp
from jax import lax
from jax.experimental import pallas as pl
from jax.experimental.pallas import tpu as pltpu

F32 = jnp.float32
BF16 = jnp.bfloat16

HEAD = 64
PAIR = 2 * HEAD
LORA = 64
CHUNK = 64
RMS_EPS = 1e-6
LN_EPS = 1e-5
GN_EPS = 64e-5
VMEM_LIMIT = 56 * 1024 * 1024


def _params(n_axes, vmem=VMEM_LIMIT):
    return pltpu.CompilerParams(dimension_semantics=("arbitrary",) * n_axes, vmem_limit_bytes=vmem)


def _sigmoid(x):
    return 1.0 / (1.0 + jnp.exp(-x))


def _dot(a, b):
    return jnp.dot(a, b, preferred_element_type=F32)


def _dot_nt(a, b):
    return lax.dot_general(a, b, (((1,), (1,)), ((), ())), preferred_element_type=F32)


def _dot_tn(a, b):
    return lax.dot_general(a, b, (((0,), (0,)), ((), ())), preferred_element_type=F32)


def _rmsnorm_kernel(x_ref, g_ref, o_ref):
    x = x_ref[...]
    ms = jnp.mean(x * x, axis=-1, keepdims=True)
    o_ref[...] = (x * lax.rsqrt(ms + RMS_EPS) * g_ref[...]).astype(o_ref.dtype)


def _rmsnorm_bf16(x, g, tm):
    m, d = x.shape
    return pl.pallas_call(
        _rmsnorm_kernel,
        out_shape=jax.ShapeDtypeStruct((m, d), BF16),
        grid=(m // tm,),
        in_specs=[pl.BlockSpec((tm, d), lambda i: (i, 0)), pl.BlockSpec((1, d), lambda i: (0, 0))],
        out_specs=pl.BlockSpec((tm, d), lambda i: (i, 0)),
        compiler_params=_params(1),
        name="rmsnorm",
    )(x, g)


def _proj_kernel(x_ref, w_ref, o_ref, *, n_silu_blocks, n_plain_blocks):
    y = _dot(x_ref[...], w_ref[...])
    if n_plain_blocks is None:
        o_ref[...] = y
    else:
        j = pl.program_id(1)
        s = _sigmoid(y)
        o_ref[...] = jnp.where(j < n_silu_blocks, y * s, s)


def _project(xn, w, tm, tn, n_silu_blocks=0, plain=True):
    m, d = xn.shape
    n = w.shape[1]
    kern = functools.partial(_proj_kernel, n_silu_blocks=n_silu_blocks,
                             n_plain_blocks=None if plain else 0)
    return pl.pallas_call(
        kern,
        out_shape=jax.ShapeDtypeStruct((m, n), F32),
        grid=(m // tm, n // tn),
        in_specs=[pl.BlockSpec((tm, d), lambda i, j: (i, 0)), pl.BlockSpec((d, tn), lambda i, j: (0, j))],
        out_specs=pl.BlockSpec((tm, tn), lambda i, j: (i, j)),
        compiler_params=_params(2),
        name="in_proj",
    )(xn, w)


def _glu_kernel(x_ref, wa_ref, wb_ref, o_ref):
    x = x_ref[...]
    o_ref[...] = _dot(x, wa_ref[...]) * _sigmoid(_dot(x, wb_ref[...]))


def _project_glu(xn, wa, wb, tm, tn):
    m, d = xn.shape
    n = wa.shape[1]
    return pl.pallas_call(
        _glu_kernel,
        out_shape=jax.ShapeDtypeStruct((m, n), F32),
        grid=(m // tm, n // tn),
        in_specs=[pl.BlockSpec((tm, d), lambda i, j: (i, 0)),
                  pl.BlockSpec((d, tn), lambda i, j: (0, j)),
                  pl.BlockSpec((d, tn), lambda i, j: (0, j))],
        out_specs=pl.BlockSpec((tm, tn), lambda i, j: (i, j)),
        compiler_params=_params(2),
        name="in_proj_glu",
    )(xn, wa, wb)


def _softplus(x):
    return jnp.maximum(x, 0.0) + jnp.log(1.0 + jnp.exp(-jnp.abs(x)))


def _head_sum(x, block_ones):
    hi = x.astype(BF16)
    lo = (x - hi.astype(F32)).astype(BF16)
    return _dot(hi, block_ones) + _dot(lo, block_ones)


def _block_ones():
    ri = lax.broadcasted_iota(jnp.int32, (PAIR, PAIR), 0)
    ci = lax.broadcasted_iota(jnp.int32, (PAIR, PAIR), 1)
    return jnp.where((ri < HEAD) == (ci < HEAD), 1.0, 0.0).astype(BF16)


def _token_shift(p, prev, mu):
    return p + mu * (prev - p)


def _lora_out(xs, lora_w, d_a):
    z = xs[:, 3 * d_a:3 * d_a + 2 * LORA]
    lane = lax.broadcasted_iota(jnp.int32, z.shape, 1)
    z = jnp.where(lane < LORA, jnp.tanh(z), z)
    return _dot(z.astype(BF16), lora_w)


def _pair_vectors(xs, lo, par_ref, sl, d_a, block_ones):
    w0 = par_ref[0:1, sl]
    a0 = par_ref[1:2, sl]
    k_k = par_ref[2:3, sl]
    k_a = par_ref[3:4, sl]
    r = xs[:, sl]
    k = xs[:, d_a + sl.start:d_a + sl.stop]
    v = xs[:, 2 * d_a + sl.start:2 * d_a + sl.stop]
    w_log = -_softplus(-(w0 + lo[:, sl])) - 0.5
    log_decay = -jnp.exp(w_log)
    a = _sigmoid(a0 + lo[:, d_a + sl.start:d_a + sl.stop])
    kkr = k * k_k
    ss = _head_sum(kkr * kkr, block_ones)
    kk = kkr * jnp.minimum(lax.rsqrt(ss), 1e12)
    kf = k * (1.0 + (a - 1.0) * k_a)
    return r, kf, v, log_decay, kk, a


def _head_norm_bonus(y, r, kf, v, par_ref, sl, block_ones):
    r_k = par_ref[4:5, sl]
    g = par_ref[5:6, sl]
    b = par_ref[6:7, sl]
    mean = _head_sum(y, block_ones) * (1.0 / HEAD)
    d = y - mean
    var = _head_sum(d * d, block_ones) * (1.0 / HEAD)
    o = d * lax.rsqrt(var + GN_EPS) * g + b
    return o + _head_sum(r * kf * r_k, block_ones) * v


def _stack_heads(x, lane_lo):
    xb = x.astype(BF16)
    zero = jnp.zeros_like(xb)
    return jnp.concatenate([jnp.where(lane_lo, xb, zero), jnp.where(lane_lo, zero, xb)], axis=0)


def _cumsum_rows(x):
    n = x.shape[0]
    row = lax.broadcasted_iota(jnp.int32, x.shape, 0)
    s = 1
    while s < n:
        x = x + jnp.where(row >= s, pltpu.roll(x, s, axis=0), 0.0)
        s *= 2
    return x


def _wkv_chunk_kernel(ps_ref, sga_ref, mu_ref, par_ref, lora_ref, oa_ref, st_ref, carry_ref, *, d_a):
    c = pl.program_id(1)
    n_pairs = d_a // PAIR
    C = CHUNK

    @pl.when(c == 0)
    def _():
        carry_ref[...] = jnp.zeros_like(carry_ref)
        st_ref[...] = jnp.zeros_like(st_ref)

    p = ps_ref[...]
    row = lax.broadcasted_iota(jnp.int32, p.shape, 0)
    prev = jnp.where(row == 0, carry_ref[...], pltpu.roll(p, 1, axis=0))
    carry_ref[...] = p[C - 1:C, :]
    xs = _token_shift(p, prev, mu_ref[...])
    lo = _lora_out(xs, lora_ref[...], d_a)

    block_ones = _block_ones()
    lane_lo = lax.broadcasted_iota(jnp.int32, (C, PAIR), 1) < HEAD
    ri = lax.broadcasted_iota(jnp.int32, (PAIR, PAIR), 0)
    ci = lax.broadcasted_iota(jnp.int32, (PAIR, PAIR), 1)
    same_head = (ri < C) == (ci < C)
    strict_lower = same_head & (ci < ri)
    lower = same_head & (ci <= ri)
    eye = jnp.where(ri == ci, 1.0, 0.0)

    for pi in range(n_pairs):
        sl = slice(pi * PAIR, (pi + 1) * PAIR)
        r, kf, v, log_decay, kk, a = _pair_vectors(xs, lo, par_ref, sl, d_a, block_ones)
        cl = _cumsum_rows(log_decay)
        cend = cl[C - 1:C, :]
        e_neg = jnp.exp(-cl)
        e_end = jnp.exp(cend - cl)
        kka = kk * a
        a_s = _stack_heads(-kk * jnp.exp(cl - log_decay), lane_lo)
        r_s = _stack_heads(r * jnp.exp(cl), lane_lo)
        b_s = _stack_heads(kka * e_neg, lane_lo)
        k_s = _stack_heads(kf * e_neg, lane_lo)
        v_s = _stack_heads(v, lane_lo)
        bh_s = _stack_heads(kka * e_end, lane_lo)
        kh_s = _stack_heads(kf * e_end, lane_lo)

        m1 = _dot_nt(jnp.concatenate([a_s, r_s], axis=0), jnp.concatenate([b_s, k_s], axis=0))
        a_ab = jnp.where(strict_lower, m1[:PAIR, :PAIR], 0.0)
        a_ak = jnp.where(strict_lower, m1[:PAIR, PAIR:], 0.0)
        a_rb = jnp.where(lower, m1[PAIR:, :PAIR], 0.0)
        a_rk = jnp.where(lower, m1[PAIR:, PAIR:], 0.0)

        apow = _dot(a_ab.astype(BF16), a_ab.astype(BF16))
        tinv = eye + a_ab
        n_levels = CHUNK.bit_length() - 1
        for lvl in range(1, n_levels):
            ab = apow.astype(BF16)
            if lvl < n_levels - 1:
                both = _dot(jnp.concatenate([ab, tinv.astype(BF16)], axis=0), ab)
                apow = both[:PAIR]
                tinv = tinv + both[PAIR:]
            else:
                tinv = tinv + _dot(tinv.astype(BF16), ab)

        gy = _dot(jnp.concatenate([a_ak, a_rk], axis=0).astype(BF16), v_s)
        wx = _dot(tinv.astype(BF16), jnp.concatenate([a_s, gy[:PAIR].astype(BF16)], axis=1))
        sv = _dot_tn(kh_s, v_s)

        st = st_ref[0, pi]
        uy = _dot(jnp.concatenate([wx[:, :PAIR].astype(BF16), r_s], axis=0), st.astype(BF16))
        u_s = uy[:PAIR] + wx[:, PAIR:]
        u_b = u_s.astype(BF16)
        y_s = uy[PAIR:] + _dot(a_rb.astype(BF16), u_b) + gy[PAIR:]
        pend_col = jnp.transpose(jnp.broadcast_to(jnp.exp(cend), (PAIR, PAIR)))
        st_ref[0, pi] = pend_col * st + _dot_tn(bh_s, u_b) + sv

        y = y_s[:C] + y_s[C:]
        o = _head_norm_bonus(y, r, kf, v, par_ref, sl, block_ones)
        oa_ref[:, sl] = (o * sga_ref[:, sl]).astype(oa_ref.dtype)


def _wkv_prompt(p_shift, gates, mu, par, lora_w, batch, seq, d_a):
    n_chunks = seq // CHUNK
    shift_w = p_shift.shape[1]
    n_pairs = d_a // PAIR
    kern = functools.partial(_wkv_chunk_kernel, d_a=d_a)
    return pl.pallas_call(
        kern,
        out_shape=(jax.ShapeDtypeStruct((batch * seq, d_a), BF16),
                   jax.ShapeDtypeStruct((batch, n_pairs, PAIR, PAIR), F32)),
        grid=(batch, n_chunks),
        in_specs=[pl.BlockSpec((CHUNK, shift_w), lambda b, c: (b * n_chunks + c, 0)),
                  pl.BlockSpec((CHUNK, d_a), lambda b, c: (b * n_chunks + c, 0)),
                  pl.BlockSpec((1, shift_w), lambda b, c: (0, 0)),
                  pl.BlockSpec((8, d_a), lambda b, c: (0, 0)),
                  pl.BlockSpec((2 * LORA, 2 * d_a), lambda b, c: (0, 0))],
        out_specs=(pl.BlockSpec((CHUNK, d_a), lambda b, c: (b * n_chunks + c, 0)),
                   pl.BlockSpec((1, n_pairs, PAIR, PAIR), lambda b, c: (b, 0, 0, 0))),
        scratch_shapes=[pltpu.VMEM((1, shift_w), F32)],
        compiler_params=_params(2),
        name="wkv_chunked",
    )(p_shift, gates, mu, par, lora_w)


def _wkv_step_kernel(ps_ref, prev_ref, sga_ref, mu_ref, par_ref, lora_ref, s_ref, oa_ref, so_ref,
                     rkv_ref, vec_ref, y_ref, *, d_a, bb):
    n_heads = d_a // HEAD
    xs = _token_shift(ps_ref[...], prev_ref[...], mu_ref[...])
    lo = _lora_out(xs, lora_ref[...], d_a)
    block_ones = _block_ones()
    for pi in range(d_a // PAIR):
        sl = slice(pi * PAIR, (pi + 1) * PAIR)
        r, kf, v, log_decay, kk, a = _pair_vectors(xs, lo, par_ref, sl, d_a, block_ones)
        vecs = (r, jnp.exp(log_decay), kf, v, -kk, kk * a)
        for i, x in enumerate(vecs):
            rkv_ref[i, :, sl] = x
            for b in range(bb):
                vec_ref[i, b, :, sl] = x[b:b + 1, :]

    ri = lax.broadcasted_iota(jnp.int32, (HEAD, HEAD), 0)
    ci = lax.broadcasted_iota(jnp.int32, (HEAD, HEAD), 1)
    eye = ri == ci

    def body(b, carry):
        for h in range(n_heads):
            hs = slice(h * HEAD, (h + 1) * HEAD)
            row = lambda i: vec_ref[i, b, :, hs]
            r_row, w_row, k_row, v_row, a_row, b_row = (row(i) for i in range(6))
            s = s_ref[b, h]
            sa = jnp.sum(s * a_row, axis=1, keepdims=True)
            v_col = jnp.sum(jnp.where(eye, v_row, 0.0), axis=1, keepdims=True)
            s_new = s * w_row + sa * b_row + v_col * k_row
            so_ref[b, h] = s_new
            y_col = jnp.sum(s_new * r_row, axis=1, keepdims=True)
            y_ref[b, :, hs] = jnp.sum(jnp.where(eye, y_col, 0.0), axis=0, keepdims=True)
        return carry

    lax.fori_loop(0, bb, body, 0)

    for pi in range(d_a // PAIR):
        sl = slice(pi * PAIR, (pi + 1) * PAIR)
        y = jnp.concatenate([y_ref[b, :, sl] for b in range(bb)], axis=0)
        o = _head_norm_bonus(y, rkv_ref[0, :, sl], rkv_ref[2, :, sl], rkv_ref[3, :, sl],
                             par_ref, sl, block_ones)
        oa_ref[:, sl] = (o * sga_ref[:, sl]).astype(oa_ref.dtype)


def _wkv_sample(p_shift, shift_prev, gates, mu, par, lora_w, state, d_a, bb):
    batch, shift_w = p_shift.shape
    n_heads = d_a // HEAD
    kern = functools.partial(_wkv_step_kernel, d_a=d_a, bb=bb)
    return pl.pallas_call(
        kern,
        out_shape=(jax.ShapeDtypeStruct((batch, d_a), BF16),
                   jax.ShapeDtypeStruct(state.shape, F32)),
        grid=(batch // bb,),
        in_specs=[pl.BlockSpec((bb, shift_w), lambda i: (i, 0)),
                  pl.BlockSpec((bb, shift_w), lambda i: (i, 0)),
                  pl.BlockSpec((bb, d_a), lambda i: (i, 0)),
                  pl.BlockSpec((1, shift_w), lambda i: (0, 0)),
                  pl.BlockSpec((8, d_a), lambda i: (0, 0)),
                  pl.BlockSpec((2 * LORA, 2 * d_a), lambda i: (0, 0)),
                  pl.BlockSpec((bb, n_heads, HEAD, HEAD), lambda i: (i, 0, 0, 0))],
        out_specs=(pl.BlockSpec((bb, d_a), lambda i: (i, 0)),
                   pl.BlockSpec((bb, n_heads, HEAD, HEAD), lambda i: (i, 0, 0, 0))),
        scratch_shapes=[pltpu.VMEM((6, bb, d_a), F32), pltpu.VMEM((6, bb, 1, d_a), F32),
                        pltpu.VMEM((bb, 1, d_a), F32)],
        compiler_params=_params(1),
        name="wkv_step",
    )(p_shift, shift_prev, gates, mu, par, lora_w, state)


def _layernorm_silu_gate(c, cpar_ref, sgb):
    mean = jnp.mean(c, axis=-1, keepdims=True)
    d = c - mean
    var = jnp.mean(d * d, axis=-1, keepdims=True)
    cf = d * lax.rsqrt(var + LN_EPS) * cpar_ref[1:2, :] + cpar_ref[2:3, :]
    return cf * _sigmoid(cf) * sgb


def _conv_prompt_kernel(u_ref, sgb_ref, w_ref, cpar_ref, o_ref, buf_ref, c_ref, *, tt, taps, pad):
    t = pl.program_id(1)

    @pl.when(t == 0)
    def _():
        buf_ref[0:pad, :] = jnp.zeros((pad, buf_ref.shape[1]), F32)

    buf_ref[pad:pad + tt, :] = u_ref[...]
    d_b = u_ref.shape[1]
    off = pad - (taps - 1)
    for j in range(d_b // 128):
        ls = slice(j * 128, (j + 1) * 128)
        acc = jnp.broadcast_to(cpar_ref[0:1, ls], (tt, 128))
        for k in range(taps):
            acc = acc + w_ref[k:k + 1, ls] * buf_ref[off + k:off + k + tt, ls]
        c_ref[:, ls] = acc
    buf_ref[0:pad, :] = buf_ref[tt:tt + pad, :]
    o_ref[...] = _layernorm_silu_gate(c_ref[...], cpar_ref, sgb_ref[...]).astype(o_ref.dtype)


def _conv_prompt(u, gates, conv_w, cpar, batch, seq, gate_block, tt):
    d_b = u.shape[1]
    taps = conv_w.shape[0]
    pad = 32
    nt = seq // tt
    w_pad = jnp.zeros((pad, d_b), F32).at[:taps].set(conv_w)
    kern = functools.partial(_conv_prompt_kernel, tt=tt, taps=taps, pad=pad)
    return pl.pallas_call(
        kern,
        out_shape=jax.ShapeDtypeStruct((batch * seq, d_b), BF16),
        grid=(batch, nt),
        in_specs=[pl.BlockSpec((tt, d_b), lambda b, t: (b * nt + t, 0)),
                  pl.BlockSpec((tt, d_b), lambda b, t: (b * nt + t, gate_block)),
                  pl.BlockSpec((pad, d_b), lambda b, t: (0, 0)),
                  pl.BlockSpec((8, d_b), lambda b, t: (0, 0))],
        out_specs=pl.BlockSpec((tt, d_b), lambda b, t: (b * nt + t, 0)),
        scratch_shapes=[pltpu.VMEM((tt + pad, d_b), F32), pltpu.VMEM((tt, d_b), F32)],
        compiler_params=_params(2),
        name="conv_prompt",
    )(u, gates, w_pad, cpar)


def _conv_step_kernel(u_ref, prev_ref, sgb_ref, w_ref, cpar_ref, o_ref, *, taps):
    c = cpar_ref[0:1, :] + w_ref[taps - 1:taps, :] * u_ref[...]
    for k in range(taps - 1):
        c = c + w_ref[k:k + 1, :] * prev_ref[:, k, :]
    o_ref[...] = _layernorm_silu_gate(c, cpar_ref, sgb_ref[...]).astype(o_ref.dtype)


def _conv_sample(u, conv_prev, gates, conv_w, cpar, gate_block, bb):
    batch, d_b = u.shape
    taps = conv_w.shape[0]
    w_pad = jnp.zeros((32, d_b), F32).at[:taps].set(conv_w)
    kern = functools.partial(_conv_step_kernel, taps=taps)
    return pl.pallas_call(
        kern,
        out_shape=jax.ShapeDtypeStruct((batch, d_b), BF16),
        grid=(batch // bb,),
        in_specs=[pl.BlockSpec((bb, d_b), lambda i: (i, 0)),
                  pl.BlockSpec((bb, taps - 1, d_b), lambda i: (i, 0, 0)),
                  pl.BlockSpec((bb, d_b), lambda i: (i, gate_block)),
                  pl.BlockSpec((32, d_b), lambda i: (0, 0)),
                  pl.BlockSpec((8, d_b), lambda i: (0, 0))],
        out_specs=pl.BlockSpec((bb, d_b), lambda i: (i, 0)),
        compiler_params=_params(1),
        name="conv_step",
    )(u, conv_prev, gates, w_pad, cpar)


def _tail_kernel(oa_ref, cb_ref, sga_ref, sgb_ref, x_ref, p_ref, wa_ref, wb_ref, wout_ref, wpg_ref,
                 wple_ref, fg_ref, o_ref):
    m = sga_ref[...] * _dot(oa_ref[...], wa_ref[...]) + sgb_ref[...] * _dot(cb_ref[...], wb_ref[...])
    h = x_ref[...] + _dot(m.astype(BF16), wout_ref[...])
    gate = _sigmoid(_dot(h.astype(BF16), wpg_ref[...]))
    h = h + gate * _dot(p_ref[...].astype(BF16), wple_ref[...])
    ms = jnp.mean(h * h, axis=-1, keepdims=True)
    o_ref[...] = h * lax.rsqrt(ms + RMS_EPS) * fg_ref[...]


def _tail(oa, cb, gates, x, p, wa, wb, wout, wpg, wple, fg, tm, merge_block):
    m, d = x.shape
    d_a = oa.shape[1]
    d_b = cb.shape[1]
    ple = p.shape[1]
    row = lambda i: (i, 0)
    const = lambda i: (0, 0)
    resident = lambda shape: pl.BlockSpec(shape, const, pipeline_mode=pl.Buffered(1))
    return pl.pallas_call(
        _tail_kernel,
        out_shape=jax.ShapeDtypeStruct((m, d), F32),
        grid=(m // tm,),
        in_specs=[pl.BlockSpec((tm, d_a), row), pl.BlockSpec((tm, d_b), row),
                  pl.BlockSpec((tm, d), lambda i: (i, merge_block)),
                  pl.BlockSpec((tm, d), lambda i: (i, merge_block + 1)),
                  pl.BlockSpec((tm, d), row), pl.BlockSpec((tm, ple), row),
                  resident((d_a, d)), resident((d_b, d)), resident((d, d)), resident((d, d)),
                  resident((ple, d)), resident((1, d))],
        out_specs=pl.BlockSpec((tm, d), row),
        compiler_params=_params(1),
        name="tail",
    )(oa, cb, gates, gates, x, p, wa, wb, wout, wpg, wple, fg)


def _largest_tile(n, cap, align):
    t = min(n, cap)
    while n % t or t % align:
        t -= 1
    return t


def kernel(x_prompt, x_sample, state_shift, state_wkv, state_conv, p_prompt, p_sample, norm_g, w_in,
           shift_mu, w0, w_lora_b, a0, a_lora_b, k_k, k_a, r_k, lnx_g, lnx_b, w_proj_a, conv_w, conv_b,
           cln_g, cln_b, w_proj_b, w_out, w_ple, w_ple_gate, final_g):
    depth = w_in.shape[0]
    batch, seq, d = x_prompt.shape
    dec_batch, dec_seq, _ = x_sample.shape
    d_a = w_proj_a.shape[1]
    d_b = w_proj_b.shape[1]
    shift_w = shift_mu.shape[1]
    n_heads = d_a // HEAD
    n_pairs = d_a // PAIR
    taps = conv_w.shape[1]
    assert depth == 1 and dec_seq == 1 and d_a == d_b and 2 * d_a == d
    assert shift_w == 3 * d_a + 2 * LORA and d_a % PAIR == 0 and seq % CHUNK == 0

    o1 = shift_w
    o2 = o1 + d_a
    o3 = o2 + 2 * d_b
    o4 = o3 + d_b
    w = w_in[0]
    w_shift = w[:, :o1].astype(BF16)
    w_gates = jnp.concatenate([w[:, o1:o2], w[:, o3:o4], w[:, o4:]], axis=1).astype(BF16)
    w_glu_a = w[:, o2:o2 + d_b].astype(BF16)
    w_glu_b = w[:, o2 + d_b:o3].astype(BF16)
    wa = w_proj_a[0].astype(BF16)
    wb = w_proj_b[0].astype(BF16)
    wout = w_out[0].astype(BF16)
    wpg = w_ple_gate[0].astype(BF16)
    wple = w_ple[0].astype(BF16)
    lora_w = jnp.zeros((2 * LORA, 2 * d_a), F32)
    lora_w = lora_w.at[:LORA, :d_a].set(w_lora_b[0]).at[LORA:, d_a:].set(a_lora_b[0]).astype(BF16)
    zeros_a = jnp.zeros((d_a,), F32)
    par = jnp.stack([w0[0], a0[0], k_k[0], k_a[0], r_k[0].reshape(d_a), lnx_g[0], lnx_b[0], zeros_a])
    zeros_b = jnp.zeros((d_b,), F32)
    cpar = jnp.stack([conv_b[0], cln_g[0], cln_b[0]] + [zeros_b] * 5)
    mu = shift_mu
    g_in = norm_g[0].reshape(1, d)
    fg = final_g.reshape(1, d)
    gate_b_block = d_a // d_b
    merge_block = (d_a + d_b) // d

    def token_stage(x2, tm):
        xn = _rmsnorm_bf16(x2, g_in, _largest_tile(x2.shape[0], 256, 8))
        tn_shift = _largest_tile(o1, 1024, 128)
        p_shift = _project(xn, w_shift, tm, tn_shift)
        gates = _project(xn, w_gates, tm, d_a, n_silu_blocks=2, plain=False)
        u = _project_glu(xn, w_glu_a, w_glu_b, tm, d_b)
        return p_shift, gates, u

    m_p = batch * seq
    x2 = x_prompt.reshape(m_p, d)
    p_shift, gates, u = token_stage(x2, _largest_tile(m_p, 1024, 8))
    oa, st = _wkv_prompt(p_shift, gates, mu, par, lora_w, batch, seq, d_a)
    cb = _conv_prompt(u, gates, conv_w[0], cpar, batch, seq, gate_b_block, _largest_tile(seq, 256, 8))
    y_prompt = _tail(oa, cb, gates, x2, p_prompt[0].reshape(m_p, -1), wa, wb, wout, wpg, wple, fg,
                     _largest_tile(m_p, 256, 8), merge_block).reshape(batch, seq, d)
    new_shift_p = p_shift.reshape(batch, seq, shift_w)[:, -1][None]
    st = st.reshape(batch, n_pairs, 2, HEAD, 2, HEAD)
    new_wkv_p = jnp.stack([st[:, :, 0, :, 0, :], st[:, :, 1, :, 1, :]], axis=2)
    new_wkv_p = jnp.swapaxes(new_wkv_p.reshape(batch, n_heads, HEAD, HEAD), -1, -2)[None]
    new_conv_p = u.reshape(batch, seq, d_b)[:, seq - (taps - 1):][None]

    xs2 = x_sample.reshape(dec_batch, d)
    p_shift_s, gates_s, u_s = token_stage(xs2, _largest_tile(dec_batch, 1024, 8))
    bb = _largest_tile(dec_batch, 8, 8)
    oa_s, new_wkv_s = _wkv_sample(p_shift_s, state_shift[0], gates_s, mu, par, lora_w, state_wkv[0], d_a, bb)
    cb_s = _conv_sample(u_s, state_conv[0], gates_s, conv_w[0], cpar, gate_b_block, bb)
    y_sample = _tail(oa_s, cb_s, gates_s, xs2, p_sample[0].reshape(dec_batch, -1), wa, wb, wout, wpg,
                     wple, fg, _largest_tile(dec_batch, 256, 8), merge_block).reshape(dec_batch, 1, d)
    new_shift_s = p_shift_s[None]
    new_conv_s = jnp.concatenate([state_conv[0][:, 1:], u_s[:, None, :]], axis=1)[None]

    return (y_prompt, y_sample, new_shift_p, new_wkv_p, new_conv_p, new_shift_s, new_wkv_s[None],
            new_conv_s)
```

```python
import functools

import jax
import jax.numpy as jnp
from jax import lax
from jax.experimental import pallas as pl
from jax.experimental.pallas import tpu as pltpu

F32 = jnp.float32
BF16 = jnp.bfloat16

HEAD = 64
PAIR = 2 * HEAD
LORA = 64
CHUNK = 64
RMS_EPS = 1e-6
LN_EPS = 1e-5
GN_EPS = 64e-5
VMEM_LIMIT = 56 * 1024 * 1024


def _params(n_axes, vmem=VMEM_LIMIT):
    return pltpu.CompilerParams(dimension_semantics=("arbitrary",) * n_axes, vmem_limit_bytes=vmem)


def _sigmoid(x):
    return 1.0 / (1.0 + jnp.exp(-x))


def _dot(a, b):
    return jnp.dot(a, b, preferred_element_type=F32)


def _dot_nt(a, b):
    return lax.dot_general(a, b, (((1,), (1,)), ((), ())), preferred_element_type=F32)


def _dot_tn(a, b):
    return lax.dot_general(a, b, (((0,), (0,)), ((), ())), preferred_element_type=F32)


def _rmsnorm_kernel(x_ref, g_ref, o_ref):
    x = x_ref[...]
    ms = jnp.mean(x * x, axis=-1, keepdims=True)
    o_ref[...] = (x * lax.rsqrt(ms + RMS_EPS) * g_ref[...]).astype(o_ref.dtype)


def _rmsnorm_bf16(x, g, tm):
    m, d = x.shape
    return pl.pallas_call(
        _rmsnorm_kernel,
        out_shape=jax.ShapeDtypeStruct((m, d), BF16),
        grid=(m // tm,),
        in_specs=[pl.BlockSpec((tm, d), lambda i: (i, 0)), pl.BlockSpec((1, d), lambda i: (0, 0))],
        out_specs=pl.BlockSpec((tm, d), lambda i: (i, 0)),
        compiler_params=_params(1),
        name="rmsnorm",
    )(x, g)


def _proj_kernel(x_ref, w_ref, o_ref, *, n_silu_blocks, n_plain_blocks):
    y = _dot(x_ref[...], w_ref[...])
    if n_plain_blocks is None:
        o_ref[...] = y
    else:
        j = pl.program_id(1)
        s = _sigmoid(y)
        o_ref[...] = jnp.where(j < n_silu_blocks, y * s, s)


def _project(xn, w, tm, tn, n_silu_blocks=0, plain=True):
    m, d = xn.shape
    n = w.shape[1]
    kern = functools.partial(_proj_kernel, n_silu_blocks=n_silu_blocks,
                             n_plain_blocks=None if plain else 0)
    return pl.pallas_call(
        kern,
        out_shape=jax.ShapeDtypeStruct((m, n), F32),
        grid=(m // tm, n // tn),
        in_specs=[pl.BlockSpec((tm, d), lambda i, j: (i, 0)), pl.BlockSpec((d, tn), lambda i, j: (0, j))],
        out_specs=pl.BlockSpec((tm, tn), lambda i, j: (i, j)),
        compiler_params=_params(2),
        name="in_proj",
    )(xn, w)


def _glu_kernel(x_ref, wa_ref, wb_ref, o_ref):
    x = x_ref[...]
    o_ref[...] = _dot(x, wa_ref[...]) * _sigmoid(_dot(x, wb_ref[...]))


def _project_glu(xn, wa, wb, tm, tn):
    m, d = xn.shape
    n = wa.shape[1]
    return pl.pallas_call(
        _glu_kernel,
        out_shape=jax.ShapeDtypeStruct((m, n), F32),
        grid=(m // tm, n // tn),
        in_specs=[pl.BlockSpec((tm, d), lambda i, j: (i, 0)),
                  pl.BlockSpec((d, tn), lambda i, j: (0, j)),
                  pl.BlockSpec((d, tn), lambda i, j: (0, j))],
        out_specs=pl.BlockSpec((tm, tn), lambda i, j: (i, j)),
        compiler_params=_params(2),
        name="in_proj_glu",
    )(xn, wa, wb)


def _softplus(x):
    return jnp.maximum(x, 0.0) + jnp.log(1.0 + jnp.exp(-jnp.abs(x)))


def _head_sum(x, block_ones):
    hi = x.astype(BF16)
    lo = (x - hi.astype(F32)).astype(BF16)
    return _dot(hi, block_ones) + _dot(lo, block_ones)


def _block_ones():
    ri = lax.broadcasted_iota(jnp.int32, (PAIR, PAIR), 0)
    ci = lax.broadcasted_iota(jnp.int32, (PAIR, PAIR), 1)
    return jnp.where((ri < HEAD) == (ci < HEAD), 1.0, 0.0).astype(BF16)


def _token_shift(p, prev, mu):
    return p + mu * (prev - p)


def _lora_out(xs, lora_w, d_a):
    z = xs[:, 3 * d_a:3 * d_a + 2 * LORA]
    lane = lax.broadcasted_iota(jnp.int32, z.shape, 1)
    z = jnp.where(lane < LORA, jnp.tanh(z), z)
    return _dot(z.astype(BF16), lora_w)


def _pair_slices(d_a):
    return [slice(i * PAIR, (i + 1) * PAIR) for i in range(d_a // PAIR)]


def _pair_vectors(xs, lo, par_ref, d_a, block_ones):
    sls = _pair_slices(d_a)
    shifted = lambda sl, off: slice(off + sl.start, off + sl.stop)
    r = [xs[:, sl] for sl in sls]
    k = [xs[:, shifted(sl, d_a)] for sl in sls]
    v = [xs[:, shifted(sl, 2 * d_a)] for sl in sls]
    kkr = [ki * par_ref[2:3, sl] for ki, sl in zip(k, sls)]
    ss = [_head_sum(x * x, block_ones) for x in kkr]
    kk = [x * jnp.minimum(lax.rsqrt(s), 1e12) for x, s in zip(kkr, ss)]
    log_decay = [-jnp.exp(-_softplus(-(par_ref[0:1, sl] + lo[:, sl])) - 0.5) for sl in sls]
    a = [_sigmoid(par_ref[1:2, sl] + lo[:, shifted(sl, d_a)]) for sl in sls]
    kf = [ki * (1.0 + (ai - 1.0) * par_ref[3:4, sl]) for ki, ai, sl in zip(k, a, sls)]
    return r, kf, v, log_decay, kk, a


def _head_norm_bonus(y, r, kf, v, par_ref, d_a, block_ones):
    sls = _pair_slices(d_a)
    mean = [_head_sum(x, block_ones) * (1.0 / HEAD) for x in y]
    d = [x - m for x, m in zip(y, mean)]
    var = [_head_sum(x * x, block_ones) * (1.0 / HEAD) for x in d]
    bonus = [_head_sum(ri * ki * par_ref[4:5, sl], block_ones) * vi for ri, ki, vi, sl in zip(r, kf, v, sls)]
    return [x * lax.rsqrt(vr + GN_EPS) * par_ref[5:6, sl] + par_ref[6:7, sl] + bo
            for x, vr, bo, sl in zip(d, var, bonus, sls)]


def _stack_heads(x, lane_lo):
    xb = x.astype(BF16)
    zero = jnp.zeros_like(xb)
    return jnp.concatenate([jnp.where(lane_lo, xb, zero), jnp.where(lane_lo, zero, xb)], axis=0)


def _cumsum_rows(x):
    n = x.shape[0]
    row = lax.broadcasted_iota(jnp.int32, x.shape, 0)
    s = 1
    while s < n:
        x = x + jnp.where(row >= s, pltpu.roll(x, s, axis=0), 0.0)
        s *= 2
    return x


def _wkv_chunk_kernel(ps_ref, sga_ref, mu_ref, par_ref, lora_ref, oa_ref, st_ref, carry_ref, *, d_a):
    c = pl.program_id(1)
    n_pairs = d_a // PAIR
    C = CHUNK

    @pl.when(c == 0)
    def _():
        carry_ref[...] = jnp.zeros_like(carry_ref)
        st_ref[...] = jnp.zeros_like(st_ref)

    p = ps_ref[...]
    row = lax.broadcasted_iota(jnp.int32, p.shape, 0)
    prev = jnp.where(row == 0, carry_ref[...], pltpu.roll(p, 1, axis=0))
    carry_ref[...] = p[C - 1:C, :]
    xs = _token_shift(p, prev, mu_ref[...])
    lo = _lora_out(xs, lora_ref[...], d_a)

    block_ones = _block_ones()
    lane_lo = lax.broadcasted_iota(jnp.int32, (C, PAIR), 1) < HEAD
    ri = lax.broadcasted_iota(jnp.int32, (PAIR, PAIR), 0)
    ci = lax.broadcasted_iota(jnp.int32, (PAIR, PAIR), 1)
    same_head = (ri < C) == (ci < C)
    strict_lower = same_head & (ci < ri)
    lower = same_head & (ci <= ri)
    eye = jnp.where(ri == ci, 1.0, 0.0)

    sls = _pair_slices(d_a)
    r, kf, v, log_decay, kk, a = _pair_vectors(xs, lo, par_ref, d_a, block_ones)
    cl = [_cumsum_rows(x) for x in log_decay]
    cend = [x[C - 1:C, :] for x in cl]
    e_neg = [jnp.exp(-x) for x in cl]
    e_end = [jnp.exp(ce - x) for ce, x in zip(cend, cl)]
    kka = [x * y for x, y in zip(kk, a)]
    stack = lambda xs_: [_stack_heads(x, lane_lo) for x in xs_]
    a_s = stack([-x * jnp.exp(c_ - ld) for x, c_, ld in zip(kk, cl, log_decay)])
    r_s = stack([x * jnp.exp(c_) for x, c_ in zip(r, cl)])
    b_s = stack([x * e for x, e in zip(kka, e_neg)])
    k_s = stack([x * e for x, e in zip(kf, e_neg)])
    v_s = stack(v)
    bh_s = stack([x * e for x, e in zip(kka, e_end)])
    kh_s = stack([x * e for x, e in zip(kf, e_end)])
    pend_col = [jnp.transpose(jnp.broadcast_to(jnp.exp(ce), (PAIR, PAIR))) for ce in cend]

    m1 = [_dot_nt(jnp.concatenate([x, y], axis=0), jnp.concatenate([z, w], axis=0))
          for x, y, z, w in zip(a_s, r_s, b_s, k_s)]
    a_ab = [jnp.where(strict_lower, m[:PAIR, :PAIR], 0.0) for m in m1]
    a_akrk = [jnp.concatenate([jnp.where(strict_lower, m[:PAIR, PAIR:], 0.0),
                               jnp.where(lower, m[PAIR:, PAIR:], 0.0)], axis=0).astype(BF16) for m in m1]
    a_rb = [jnp.where(lower, m[PAIR:, :PAIR], 0.0).astype(BF16) for m in m1]
    gy = [_dot(x, y) for x, y in zip(a_akrk, v_s)]
    sv = [_dot_tn(x, y) for x, y in zip(kh_s, v_s)]

    apow = [_dot(x.astype(BF16), x.astype(BF16)) for x in a_ab]
    tinv = [eye + x for x in a_ab]
    n_levels = CHUNK.bit_length() - 1
    for lvl in range(1, n_levels):
        ab = [x.astype(BF16) for x in apow]
        if lvl < n_levels - 1:
            both = [_dot(jnp.concatenate([x, t.astype(BF16)], axis=0), x) for x, t in zip(ab, tinv)]
            apow = [x[:PAIR] for x in both]
            tinv = [t + x[PAIR:] for t, x in zip(tinv, both)]
        else:
            tinv = [t + _dot(t.astype(BF16), x) for t, x in zip(tinv, ab)]

    wx = [_dot(t.astype(BF16), jnp.concatenate([x, g[:PAIR].astype(BF16)], axis=1))
          for t, x, g in zip(tinv, a_s, gy)]
    st = [st_ref[0, pi] for pi in range(n_pairs)]
    uy = [_dot(jnp.concatenate([w[:, :PAIR].astype(BF16), x], axis=0), s_.astype(BF16))
          for w, x, s_ in zip(wx, r_s, st)]
    u_b = [(x[:PAIR] + w[:, PAIR:]).astype(BF16) for x, w in zip(uy, wx)]
    st_upd = [_dot_tn(x, u) for x, u in zip(bh_s, u_b)]
    y_u = [_dot(x, u) for x, u in zip(a_rb, u_b)]
    for pi in range(n_pairs):
        st_ref[0, pi] = pend_col[pi] * st[pi] + st_upd[pi] + sv[pi]
    y_s = [x[PAIR:] + yu + g[PAIR:] for x, yu, g in zip(uy, y_u, gy)]
    y = [x[:C] + x[C:] for x in y_s]
    o = _head_norm_bonus(y, r, kf, v, par_ref, d_a, block_ones)
    for oi, sl in zip(o, sls):
        oa_ref[:, sl] = (oi * sga_ref[:, sl]).astype(oa_ref.dtype)


def _wkv_prompt(p_shift, gates, mu, par, lora_w, batch, seq, d_a):
    n_chunks = seq // CHUNK
    shift_w = p_shift.shape[1]
    n_pairs = d_a // PAIR
    kern = functools.partial(_wkv_chunk_kernel, d_a=d_a)
    return pl.pallas_call(
        kern,
        out_shape=(jax.ShapeDtypeStruct((batch * seq, d_a), BF16),
                   jax.ShapeDtypeStruct((batch, n_pairs, PAIR, PAIR), F32)),
        grid=(batch, n_chunks),
        in_specs=[pl.BlockSpec((CHUNK, shift_w), lambda b, c: (b * n_chunks + c, 0)),
                  pl.BlockSpec((CHUNK, d_a), lambda b, c: (b * n_chunks + c, 0)),
                  pl.BlockSpec((1, shift_w), lambda b, c: (0, 0)),
                  pl.BlockSpec((8, d_a), lambda b, c: (0, 0)),
                  pl.BlockSpec((2 * LORA, 2 * d_a), lambda b, c: (0, 0))],
        out_specs=(pl.BlockSpec((CHUNK, d_a), lambda b, c: (b * n_chunks + c, 0)),
                   pl.BlockSpec((1, n_pairs, PAIR, PAIR), lambda b, c: (b, 0, 0, 0))),
        scratch_shapes=[pltpu.VMEM((1, shift_w), F32)],
        compiler_params=_params(2),
        name="wkv_chunked",
    )(p_shift, gates, mu, par, lora_w)


def _wkv_step_kernel(ps_ref, prev_ref, sga_ref, mu_ref, par_ref, lora_ref, s_ref, oa_ref, so_ref,
                     vec_ref, y_ref, *, d_a, bb):
    n_heads = d_a // HEAD
    xs = _token_shift(ps_ref[...], prev_ref[...], mu_ref[...])
    lo = _lora_out(xs, lora_ref[...], d_a)
    block_ones = _block_ones()
    sls = _pair_slices(d_a)
    r, kf, v, log_decay, kk, a = _pair_vectors(xs, lo, par_ref, d_a, block_ones)
    for pi, sl in enumerate(sls):
        vecs = (r[pi], jnp.exp(log_decay[pi]), kf[pi], v[pi], -kk[pi], kk[pi] * a[pi])
        for i, x in enumerate(vecs):
            for b in range(bb):
                vec_ref[i, b, :, sl] = x[b:b + 1, :]

    ri = lax.broadcasted_iota(jnp.int32, (HEAD, HEAD), 0)
    ci = lax.broadcasted_iota(jnp.int32, (HEAD, HEAD), 1)
    eye = ri == ci

    def body(b, carry):
        for h in range(n_heads):
            hs = slice(h * HEAD, (h + 1) * HEAD)
            row = lambda i: vec_ref[i, b, :, hs]
            r_row, w_row, k_row, v_row, a_row, b_row = (row(i) for i in range(6))
            s = s_ref[b, h]
            sa = jnp.sum(s * a_row, axis=1, keepdims=True)
            v_col = jnp.sum(jnp.where(eye, v_row, 0.0), axis=1, keepdims=True)
            s_new = s * w_row + sa * b_row + v_col * k_row
            so_ref[b, h] = s_new
            y_col = jnp.sum(s_new * r_row, axis=1, keepdims=True)
            y_ref[b, :, hs] = jnp.sum(jnp.where(eye, y_col, 0.0), axis=0, keepdims=True)
        return carry

    lax.fori_loop(0, bb, body, 0)

    y = [jnp.concatenate([y_ref[b, :, sl] for b in range(bb)], axis=0) for sl in sls]
    o = _head_norm_bonus(y, r, kf, v, par_ref, d_a, block_ones)
    for oi, sl in zip(o, sls):
        oa_ref[:, sl] = (oi * sga_ref[:, sl]).astype(oa_ref.dtype)


def _wkv_sample(p_shift, shift_prev, gates, mu, par, lora_w, state, d_a, bb):
    batch, shift_w = p_shift.shape
    n_heads = d_a // HEAD
    kern = functools.partial(_wkv_step_kernel, d_a=d_a, bb=bb)
    return pl.pallas_call(
        kern,
        out_shape=(jax.ShapeDtypeStruct((batch, d_a), BF16),
                   jax.ShapeDtypeStruct(state.shape, F32)),
        grid=(batch // bb,),
        in_specs=[pl.BlockSpec((bb, shift_w), lambda i: (i, 0)),
                  pl.BlockSpec((bb, shift_w), lambda i: (i, 0)),
                  pl.BlockSpec((bb, d_a), lambda i: (i, 0)),
                  pl.BlockSpec((1, shift_w), lambda i: (0, 0)),
                  pl.BlockSpec((8, d_a), lambda i: (0, 0)),
                  pl.BlockSpec((2 * LORA, 2 * d_a), lambda i: (0, 0)),
                  pl.BlockSpec((bb, n_heads, HEAD, HEAD), lambda i: (i, 0, 0, 0))],
        out_specs=(pl.BlockSpec((bb, d_a), lambda i: (i, 0)),
                   pl.BlockSpec((bb, n_heads, HEAD, HEAD), lambda i: (i, 0, 0, 0))),
        scratch_shapes=[pltpu.VMEM((6, bb, 1, d_a), F32), pltpu.VMEM((bb, 1, d_a), F32)],
        compiler_params=_params(1),
        name="wkv_step",
    )(p_shift, shift_prev, gates, mu, par, lora_w, state)


def _layernorm_silu_gate(c, cpar_ref, sgb):
    mean = jnp.mean(c, axis=-1, keepdims=True)
    d = c - mean
    var = jnp.mean(d * d, axis=-1, keepdims=True)
    cf = d * lax.rsqrt(var + LN_EPS) * cpar_ref[1:2, :] + cpar_ref[2:3, :]
    return cf * _sigmoid(cf) * sgb


def _conv_prompt_kernel(u_ref, sgb_ref, w_ref, cpar_ref, o_ref, buf_ref, c_ref, *, tt, taps, pad):
    t = pl.program_id(1)

    @pl.when(t == 0)
    def _():
        buf_ref[0:pad, :] = jnp.zeros((pad, buf_ref.shape[1]), F32)

    buf_ref[pad:pad + tt, :] = u_ref[...]
    d_b = u_ref.shape[1]
    off = pad - (taps - 1)
    for j in range(d_b // 128):
        ls = slice(j * 128, (j + 1) * 128)
        acc = jnp.broadcast_to(cpar_ref[0:1, ls], (tt, 128))
        for k in range(taps):
            acc = acc + w_ref[k:k + 1, ls] * buf_ref[off + k:off + k + tt, ls]
        c_ref[:, ls] = acc
    buf_ref[0:pad, :] = buf_ref[tt:tt + pad, :]
    o_ref[...] = _layernorm_silu_gate(c_ref[...], cpar_ref, sgb_ref[...]).astype(o_ref.dtype)


def _conv_prompt(u, gates, conv_w, cpar, batch, seq, gate_block, tt):
    d_b = u.shape[1]
    taps = conv_w.shape[0]
    pad = 32
    nt = seq // tt
    w_pad = jnp.zeros((pad, d_b), F32).at[:taps].set(conv_w)
    kern = functools.partial(_conv_prompt_kernel, tt=tt, taps=taps, pad=pad)
    return pl.pallas_call(
        kern,
        out_shape=jax.ShapeDtypeStruct((batch * seq, d_b), BF16),
        grid=(batch, nt),
        in_specs=[pl.BlockSpec((tt, d_b), lambda b, t: (b * nt + t, 0)),
                  pl.BlockSpec((tt, d_b), lambda b, t: (b * nt + t, gate_block)),
                  pl.BlockSpec((pad, d_b), lambda b, t: (0, 0)),
                  pl.BlockSpec((8, d_b), lambda b, t: (0, 0))],
        out_specs=pl.BlockSpec((tt, d_b), lambda b, t: (b * nt + t, 0)),
        scratch_shapes=[pltpu.VMEM((tt + pad, d_b), F32), pltpu.VMEM((tt, d_b), F32)],
        compiler_params=_params(2),
        name="conv_prompt",
    )(u, gates, w_pad, cpar)


def _conv_step_kernel(u_ref, prev_ref, sgb_ref, w_ref, cpar_ref, o_ref, *, taps):
    c = cpar_ref[0:1, :] + w_ref[taps - 1:taps, :] * u_ref[...]
    for k in range(taps - 1):
        c = c + w_ref[k:k + 1, :] * prev_ref[:, k, :]
    o_ref[...] = _layernorm_silu_gate(c, cpar_ref, sgb_ref[...]).astype(o_ref.dtype)


def _conv_sample(u, conv_prev, gates, conv_w, cpar, gate_block, bb):
    batch, d_b = u.shape
    taps = conv_w.shape[0]
    w_pad = jnp.zeros((32, d_b), F32).at[:taps].set(conv_w)
    kern = functools.partial(_conv_step_kernel, taps=taps)
    return pl.pallas_call(
        kern,
        out_shape=jax.ShapeDtypeStruct((batch, d_b), BF16),
        grid=(batch // bb,),
        in_specs=[pl.BlockSpec((bb, d_b), lambda i: (i, 0)),
                  pl.BlockSpec((bb, taps - 1, d_b), lambda i: (i, 0, 0)),
                  pl.BlockSpec((bb, d_b), lambda i: (i, gate_block)),
                  pl.BlockSpec((32, d_b), lambda i: (0, 0)),
                  pl.BlockSpec((8, d_b), lambda i: (0, 0))],
        out_specs=pl.BlockSpec((bb, d_b), lambda i: (i, 0)),
        compiler_params=_params(1),
        name="conv_step",
    )(u, conv_prev, gates, w_pad, cpar)


def _tail_kernel(oa_ref, cb_ref, sga_ref, sgb_ref, x_ref, p_ref, wa_ref, wb_ref, wout_ref, wpg_ref,
                 wple_ref, fg_ref, o_ref):
    m = sga_ref[...] * _dot(oa_ref[...], wa_ref[...]) + sgb_ref[...] * _dot(cb_ref[...], wb_ref[...])
    h = x_ref[...] + _dot(m.astype(BF16), wout_ref[...])
    gate = _sigmoid(_dot(h.astype(BF16), wpg_ref[...]))
    h = h + gate * _dot(p_ref[...].astype(BF16), wple_ref[...])
    ms = jnp.mean(h * h, axis=-1, keepdims=True)
    o_ref[...] = h * lax.rsqrt(ms + RMS_EPS) * fg_ref[...]


def _tail(oa, cb, gates, x, p, wa, wb, wout, wpg, wple, fg, tm, merge_block):
    m, d = x.shape
    d_a = oa.shape[1]
    d_b = cb.shape[1]
    ple = p.shape[1]
    row = lambda i: (i, 0)
    const = lambda i: (0, 0)
    resident = lambda shape: pl.BlockSpec(shape, const, pipeline_mode=pl.Buffered(1))
    return pl.pallas_call(
        _tail_kernel,
        out_shape=jax.ShapeDtypeStruct((m, d), F32),
        grid=(m // tm,),
        in_specs=[pl.BlockSpec((tm, d_a), row), pl.BlockSpec((tm, d_b), row),
                  pl.BlockSpec((tm, d), lambda i: (i, merge_block)),
                  pl.BlockSpec((tm, d), lambda i: (i, merge_block + 1)),
                  pl.BlockSpec((tm, d), row), pl.BlockSpec((tm, ple), row),
                  resident((d_a, d)), resident((d_b, d)), resident((d, d)), resident((d, d)),
                  resident((ple, d)), resident((1, d))],
        out_specs=pl.BlockSpec((tm, d), row),
        compiler_params=_params(1),
        name="tail",
    )(oa, cb, gates, gates, x, p, wa, wb, wout, wpg, wple, fg)


def _largest_tile(n, cap, align):
    t = min(n, cap)
    while n % t or t % align:
        t -= 1
    return t


def kernel(x_prompt, x_sample, state_shift, state_wkv, state_conv, p_prompt, p_sample, norm_g, w_in,
           shift_mu, w0, w_lora_b, a0, a_lora_b, k_k, k_a, r_k, lnx_g, lnx_b, w_proj_a, conv_w, conv_b,
           cln_g, cln_b, w_proj_b, w_out, w_ple, w_ple_gate, final_g):
    depth = w_in.shape[0]
    batch, seq, d = x_prompt.shape
    dec_batch, dec_seq, _ = x_sample.shape
    d_a = w_proj_a.shape[1]
    d_b = w_proj_b.shape[1]
    shift_w = shift_mu.shape[1]
    n_heads = d_a // HEAD
    n_pairs = d_a // PAIR
    taps = conv_w.shape[1]
    assert depth == 1 and dec_seq == 1 and d_a == d_b and 2 * d_a == d
    assert shift_w == 3 * d_a + 2 * LORA and d_a % PAIR == 0 and seq % CHUNK == 0

    o1 = shift_w
    o2 = o1 + d_a
    o3 = o2 + 2 * d_b
    o4 = o3 + d_b
    w = w_in[0]
    w_shift = w[:, :o1].astype(BF16)
    w_gates = jnp.concatenate([w[:, o1:o2], w[:, o3:o4], w[:, o4:]], axis=1).astype(BF16)
    w_glu_a = w[:, o2:o2 + d_b].astype(BF16)
    w_glu_b = w[:, o2 + d_b:o3].astype(BF16)
    wa = w_proj_a[0].astype(BF16)
    wb = w_proj_b[0].astype(BF16)
    wout = w_out[0].astype(BF16)
    wpg = w_ple_gate[0].astype(BF16)
    wple = w_ple[0].astype(BF16)
    lora_w = jnp.zeros((2 * LORA, 2 * d_a), F32)
    lora_w = lora_w.at[:LORA, :d_a].set(w_lora_b[0]).at[LORA:, d_a:].set(a_lora_b[0]).astype(BF16)
    zeros_a = jnp.zeros((d_a,), F32)
    par = jnp.stack([w0[0], a0[0], k_k[0], k_a[0], r_k[0].reshape(d_a), lnx_g[0], lnx_b[0], zeros_a])
    zeros_b = jnp.zeros((d_b,), F32)
    cpar = jnp.stack([conv_b[0], cln_g[0], cln_b[0]] + [zeros_b] * 5)
    mu = shift_mu
    g_in = norm_g[0].reshape(1, d)
    fg = final_g.reshape(1, d)
    gate_b_block = d_a // d_b
    merge_block = (d_a + d_b) // d

    def token_stage(x2, tm):
        xn = _rmsnorm_bf16(x2, g_in, _largest_tile(x2.shape[0], 256, 8))
        tn_shift = _largest_tile(o1, 1024, 128)
        p_shift = _project(xn, w_shift, tm, tn_shift)
        gates = _project(xn, w_gates, tm, d_a, n_silu_blocks=2, plain=False)
        u = _project_glu(xn, w_glu_a, w_glu_b, tm, d_b)
        return p_shift, gates, u

    m_p = batch * seq
    x2 = x_prompt.reshape(m_p, d)
    p_shift, gates, u = token_stage(x2, _largest_tile(m_p, 1024, 8))
    oa, st = _wkv_prompt(p_shift, gates, mu, par, lora_w, batch, seq, d_a)
    cb = _conv_prompt(u, gates, conv_w[0], cpar, batch, seq, gate_b_block, _largest_tile(seq, 256, 8))
    y_prompt = _tail(oa, cb, gates, x2, p_prompt[0].reshape(m_p, -1), wa, wb, wout, wpg, wple, fg,
                     _largest_tile(m_p, 256, 8), merge_block).reshape(batch, seq, d)
    new_shift_p = p_shift.reshape(batch, seq, shift_w)[:, -1][None]
    st = st.reshape(batch, n_pairs, 2, HEAD, 2, HEAD)
    new_wkv_p = jnp.stack([st[:, :, 0, :, 0, :], st[:, :, 1, :, 1, :]], axis=2)
    new_wkv_p = jnp.swapaxes(new_wkv_p.reshape(batch, n_heads, HEAD, HEAD), -1, -2)[None]
    new_conv_p = u.reshape(batch, seq, d_b)[:, seq - (taps - 1):][None]

    xs2 = x_sample.reshape(dec_batch, d)
    p_shift_s, gates_s, u_s = token_stage(xs2, _largest_tile(dec_batch, 1024, 8))
    bb = _largest_tile(dec_batch, 8, 8)
    oa_s, new_wkv_s = _wkv_sample(p_shift_s, state_shift[0], gates_s, mu, par, lora_w, state_wkv[0], d_a, bb)
    cb_s = _conv_sample(u_s, state_conv[0], gates_s, conv_w[0], cpar, gate_b_block, bb)
    y_sample = _tail(oa_s, cb_s, gates_s, xs2, p_sample[0].reshape(dec_batch, -1), wa, wb, wout, wpg,
                     wple, fg, _largest_tile(dec_batch, 256, 8), merge_block).reshape(dec_batch, 1, d)
    new_shift_s = p_shift_s[None]
    new_conv_s = jnp.concatenate([state_conv[0][:, 1:], u_s[:, None, :]], axis=1)[None]

    return (y_prompt, y_sample, new_shift_p, new_wkv_p, new_conv_p, new_shift_s, new_wkv_s[None],
            new_conv_s)
```

```python
import functools

import jax
import jax.numpy as jnp
from jax import lax
from jax.experimental import pallas as pl
from jax.experimental.pallas import tpu as pltpu

F32 = jnp.float32
BF16 = jnp.bfloat16

HEAD = 64
PAIR = 2 * HEAD
LORA = 64
CHUNK = 64
RMS_EPS = 1e-6
LN_EPS = 1e-5
GN_EPS = 64e-5
VMEM_LIMIT = 56 * 1024 * 1024


def _params(n_axes, vmem=VMEM_LIMIT):
    return pltpu.CompilerParams(dimension_semantics=("arbitrary",) * n_axes, vmem_limit_bytes=vmem)


def _sigmoid(x):
    return 1.0 / (1.0 + jnp.exp(-x))


def _dot(a, b):
    return jnp.dot(a, b, preferred_element_type=F32)


def _dot_nt(a, b):
    return lax.dot_general(a, b, (((1,), (1,)), ((), ())), preferred_element_type=F32)


def _dot_tn(a, b):
    return lax.dot_general(a, b, (((0,), (0,)), ((), ())), preferred_element_type=F32)


def _rmsnorm_kernel(x_ref, g_ref, o_ref):
    x = x_ref[...]
    ms = jnp.mean(x * x, axis=-1, keepdims=True)
    o_ref[...] = (x * lax.rsqrt(ms + RMS_EPS) * g_ref[...]).astype(o_ref.dtype)


def _rmsnorm_bf16(x, g, tm):
    m, d = x.shape
    return pl.pallas_call(
        _rmsnorm_kernel,
        out_shape=jax.ShapeDtypeStruct((m, d), BF16),
        grid=(m // tm,),
        in_specs=[pl.BlockSpec((tm, d), lambda i: (i, 0)), pl.BlockSpec((1, d), lambda i: (0, 0))],
        out_specs=pl.BlockSpec((tm, d), lambda i: (i, 0)),
        compiler_params=_params(1),
        name="rmsnorm",
    )(x, g)


def _proj_kernel(x_ref, w_ref, o_ref, *, n_silu_blocks, n_plain_blocks):
    y = _dot(x_ref[...], w_ref[...])
    if n_plain_blocks is None:
        o_ref[...] = y
    else:
        j = pl.program_id(1)
        s = _sigmoid(y)
        o_ref[...] = jnp.where(j < n_silu_blocks, y * s, s)


def _project(xn, w, tm, tn, n_silu_blocks=0, plain=True):
    m, d = xn.shape
    n = w.shape[1]
    kern = functools.partial(_proj_kernel, n_silu_blocks=n_silu_blocks,
                             n_plain_blocks=None if plain else 0)
    return pl.pallas_call(
        kern,
        out_shape=jax.ShapeDtypeStruct((m, n), F32),
        grid=(m // tm, n // tn),
        in_specs=[pl.BlockSpec((tm, d), lambda i, j: (i, 0)), pl.BlockSpec((d, tn), lambda i, j: (0, j))],
        out_specs=pl.BlockSpec((tm, tn), lambda i, j: (i, j)),
        compiler_params=_params(2),
        name="in_proj",
    )(xn, w)


def _glu_kernel(x_ref, wa_ref, wb_ref, o_ref):
    x = x_ref[...]
    o_ref[...] = _dot(x, wa_ref[...]) * _sigmoid(_dot(x, wb_ref[...]))


def _project_glu(xn, wa, wb, tm, tn):
    m, d = xn.shape
    n = wa.shape[1]
    return pl.pallas_call(
        _glu_kernel,
        out_shape=jax.ShapeDtypeStruct((m, n), F32),
        grid=(m // tm, n // tn),
        in_specs=[pl.BlockSpec((tm, d), lambda i, j: (i, 0)),
                  pl.BlockSpec((d, tn), lambda i, j: (0, j)),
                  pl.BlockSpec((d, tn), lambda i, j: (0, j))],
        out_specs=pl.BlockSpec((tm, tn), lambda i, j: (i, j)),
        compiler_params=_params(2),
        name="in_proj_glu",
    )(xn, wa, wb)


def _softplus(x):
    return jnp.maximum(x, 0.0) + jnp.log(1.0 + jnp.exp(-jnp.abs(x)))


def _head_sum(x, block_ones):
    hi = x.astype(BF16)
    lo = (x - hi.astype(F32)).astype(BF16)
    return _dot(hi, block_ones) + _dot(lo, block_ones)


def _head_sums(xs, block_ones):
    rows = xs[0].shape[0]
    s = _head_sum(jnp.concatenate(xs, axis=0), block_ones)
    return [s[i * rows:(i + 1) * rows] for i in range(len(xs))]


def _block_ones():
    ri = lax.broadcasted_iota(jnp.int32, (PAIR, PAIR), 0)
    ci = lax.broadcasted_iota(jnp.int32, (PAIR, PAIR), 1)
    return jnp.where((ri < HEAD) == (ci < HEAD), 1.0, 0.0).astype(BF16)


def _token_shift(p, prev, mu):
    return p + mu * (prev - p)


def _lora_out(xs, lora_w, d_a):
    z = xs[:, 3 * d_a:3 * d_a + 2 * LORA]
    lane = lax.broadcasted_iota(jnp.int32, z.shape, 1)
    z = jnp.where(lane < LORA, jnp.tanh(z), z)
    return _dot(z.astype(BF16), lora_w)


def _pair_slices(d_a):
    return [slice(i * PAIR, (i + 1) * PAIR) for i in range(d_a // PAIR)]


def _pair_vectors(xs, lo, par_ref, d_a, block_ones):
    sls = _pair_slices(d_a)
    shifted = lambda sl, off: slice(off + sl.start, off + sl.stop)
    r = [xs[:, sl] for sl in sls]
    k = [xs[:, shifted(sl, d_a)] for sl in sls]
    v = [xs[:, shifted(sl, 2 * d_a)] for sl in sls]
    kkr = [ki * par_ref[2:3, sl] for ki, sl in zip(k, sls)]
    ss = _head_sums([x * x for x in kkr], block_ones)
    kk = [x * jnp.minimum(lax.rsqrt(s), 1e12) for x, s in zip(kkr, ss)]
    log_decay = [-jnp.exp(-_softplus(-(par_ref[0:1, sl] + lo[:, sl])) - 0.5) for sl in sls]
    a = [_sigmoid(par_ref[1:2, sl] + lo[:, shifted(sl, d_a)]) for sl in sls]
    kf = [ki * (1.0 + (ai - 1.0) * par_ref[3:4, sl]) for ki, ai, sl in zip(k, a, sls)]
    return r, kf, v, log_decay, kk, a


def _head_norm_bonus(y, r, kf, v, par_ref, d_a, block_ones):
    sls = _pair_slices(d_a)
    mean = [s * (1.0 / HEAD) for s in _head_sums(y, block_ones)]
    d = [x - m for x, m in zip(y, mean)]
    var = [s * (1.0 / HEAD) for s in _head_sums([x * x for x in d], block_ones)]
    rk = _head_sums([ri * ki * par_ref[4:5, sl] for ri, ki, sl in zip(r, kf, sls)], block_ones)
    return [x * lax.rsqrt(vr + GN_EPS) * par_ref[5:6, sl] + par_ref[6:7, sl] + s * vi
            for x, vr, s, vi, sl in zip(d, var, rk, v, sls)]


def _stack_heads(x, lane_lo):
    xb = x.astype(BF16)
    zero = jnp.zeros_like(xb)
    return jnp.concatenate([jnp.where(lane_lo, xb, zero), jnp.where(lane_lo, zero, xb)], axis=0)


def _cumsum_rows(x):
    n = x.shape[0]
    row = lax.broadcasted_iota(jnp.int32, x.shape, 0)
    s = 1
    while s < n:
        x = x + jnp.where(row >= s, pltpu.roll(x, s, axis=0), 0.0)
        s *= 2
    return x


def _wkv_chunk_kernel(ps_ref, sga_ref, mu_ref, par_ref, lora_ref, oa_ref, st_ref, carry_ref, *, d_a):
    c = pl.program_id(1)
    n_pairs = d_a // PAIR
    C = CHUNK

    @pl.when(c == 0)
    def _():
        carry_ref[...] = jnp.zeros_like(carry_ref)
        st_ref[...] = jnp.zeros_like(st_ref)

    p = ps_ref[...]
    row = lax.broadcasted_iota(jnp.int32, p.shape, 0)
    prev = jnp.where(row == 0, carry_ref[...], pltpu.roll(p, 1, axis=0))
    carry_ref[...] = p[C - 1:C, :]
    xs = _token_shift(p, prev, mu_ref[...])
    lo = _lora_out(xs, lora_ref[...], d_a)

    block_ones = _block_ones()
    lane_lo = lax.broadcasted_iota(jnp.int32, (C, PAIR), 1) < HEAD
    ri = lax.broadcasted_iota(jnp.int32, (PAIR, PAIR), 0)
    ci = lax.broadcasted_iota(jnp.int32, (PAIR, PAIR), 1)
    same_head = (ri < C) == (ci < C)
    strict_lower = same_head & (ci < ri)
    lower = same_head & (ci <= ri)
    eye = jnp.where(ri == ci, 1.0, 0.0)

    sls = _pair_slices(d_a)
    r, kf, v, log_decay, kk, a = _pair_vectors(xs, lo, par_ref, d_a, block_ones)
    cl = [_cumsum_rows(x) for x in log_decay]
    cend = [x[C - 1:C, :] for x in cl]
    e_neg = [jnp.exp(-x) for x in cl]
    e_end = [jnp.exp(ce - x) for ce, x in zip(cend, cl)]
    kka = [x * y for x, y in zip(kk, a)]
    stack = lambda xs_: [_stack_heads(x, lane_lo) for x in xs_]
    a_s = stack([-x * jnp.exp(c_ - ld) for x, c_, ld in zip(kk, cl, log_decay)])
    r_s = stack([x * jnp.exp(c_) for x, c_ in zip(r, cl)])
    b_s = stack([x * e for x, e in zip(kka, e_neg)])
    k_s = stack([x * e for x, e in zip(kf, e_neg)])
    v_s = stack(v)
    bh_s = stack([x * e for x, e in zip(kka, e_end)])
    kh_s = stack([x * e for x, e in zip(kf, e_end)])
    pend_col = [jnp.transpose(jnp.broadcast_to(jnp.exp(ce), (PAIR, PAIR))) for ce in cend]

    m1 = [_dot_nt(jnp.concatenate([x, y], axis=0), jnp.concatenate([z, w], axis=0))
          for x, y, z, w in zip(a_s, r_s, b_s, k_s)]
    a_ab = [jnp.where(strict_lower, m[:PAIR, :PAIR], 0.0) for m in m1]
    a_akrk = [jnp.concatenate([jnp.where(strict_lower, m[:PAIR, PAIR:], 0.0),
                               jnp.where(lower, m[PAIR:, PAIR:], 0.0)], axis=0).astype(BF16) for m in m1]
    a_rb = [jnp.where(lower, m[PAIR:, :PAIR], 0.0).astype(BF16) for m in m1]
    gy = [_dot(x, y) for x, y in zip(a_akrk, v_s)]
    sv = [_dot_tn(x, y) for x, y in zip(kh_s, v_s)]

    apow = [_dot(x.astype(BF16), x.astype(BF16)) for x in a_ab]
    tinv = [eye + x for x in a_ab]
    n_levels = CHUNK.bit_length() - 1
    for lvl in range(1, n_levels):
        ab = [x.astype(BF16) for x in apow]
        if lvl < n_levels - 1:
            both = [_dot(jnp.concatenate([x, t.astype(BF16)], axis=0), x) for x, t in zip(ab, tinv)]
            apow = [x[:PAIR] for x in both]
            tinv = [t + x[PAIR:] for t, x in zip(tinv, both)]
        else:
            tinv = [t + _dot(t.astype(BF16), x) for t, x in zip(tinv, ab)]

    wx = [_dot(t.astype(BF16), jnp.concatenate([x, g[:PAIR].astype(BF16)], axis=1))
          for t, x, g in zip(tinv, a_s, gy)]
    st = [st_ref[0, pi] for pi in range(n_pairs)]
    uy = [_dot(jnp.concatenate([w[:, :PAIR].astype(BF16), x], axis=0), s_.astype(BF16))
          for w, x, s_ in zip(wx, r_s, st)]
    u_b = [(x[:PAIR] + w[:, PAIR:]).astype(BF16) for x, w in zip(uy, wx)]
    st_upd = [_dot_tn(x, u) for x, u in zip(bh_s, u_b)]
    y_u = [_dot(x, u) for x, u in zip(a_rb, u_b)]
    for pi in range(n_pairs):
        st_ref[0, pi] = pend_col[pi] * st[pi] + st_upd[pi] + sv[pi]
    y_s = [x[PAIR:] + yu + g[PAIR:] for x, yu, g in zip(uy, y_u, gy)]
    y = [x[:C] + x[C:] for x in y_s]
    o = _head_norm_bonus(y, r, kf, v, par_ref, d_a, block_ones)
    for oi, sl in zip(o, sls):
        oa_ref[:, sl] = (oi * sga_ref[:, sl]).astype(oa_ref.dtype)


def _wkv_prompt(p_shift, gates, mu, par, lora_w, batch, seq, d_a):
    n_chunks = seq // CHUNK
    shift_w = p_shift.shape[1]
    n_pairs = d_a // PAIR
    kern = functools.partial(_wkv_chunk_kernel, d_a=d_a)
    return pl.pallas_call(
        kern,
        out_shape=(jax.ShapeDtypeStruct((batch * seq, d_a), BF16),
                   jax.ShapeDtypeStruct((batch, n_pairs, PAIR, PAIR), F32)),
        grid=(batch, n_chunks),
        in_specs=[pl.BlockSpec((CHUNK, shift_w), lambda b, c: (b * n_chunks + c, 0)),
                  pl.BlockSpec((CHUNK, d_a), lambda b, c: (b * n_chunks + c, 0)),
                  pl.BlockSpec((1, shift_w), lambda b, c: (0, 0)),
                  pl.BlockSpec((8, d_a), lambda b, c: (0, 0)),
                  pl.BlockSpec((2 * LORA, 2 * d_a), lambda b, c: (0, 0))],
        out_specs=(pl.BlockSpec((CHUNK, d_a), lambda b, c: (b * n_chunks + c, 0)),
                   pl.BlockSpec((1, n_pairs, PAIR, PAIR), lambda b, c: (b, 0, 0, 0))),
        scratch_shapes=[pltpu.VMEM((1, shift_w), F32)],
        compiler_params=_params(2),
        name="wkv_chunked",
    )(p_shift, gates, mu, par, lora_w)


def _wkv_step_kernel(ps_ref, prev_ref, sga_ref, mu_ref, par_ref, lora_ref, s_ref, oa_ref, so_ref,
                     vec_ref, y_ref, *, d_a, bb, group):
    n_heads = d_a // HEAD
    half = HEAD // 2
    xs = _token_shift(ps_ref[...], prev_ref[...], mu_ref[...])
    lo = _lora_out(xs, lora_ref[...], d_a)
    block_ones = _block_ones()
    sls = _pair_slices(d_a)
    r, kf, v, log_decay, kk, a = _pair_vectors(xs, lo, par_ref, d_a, block_ones)

    lane_lo = lax.broadcasted_iota(jnp.int32, (bb, PAIR), 1) < HEAD
    for pi in range(len(sls)):
        v_hi = v[pi].astype(BF16).astype(F32)
        vecs = (-kk[pi], jnp.exp(log_decay[pi]), kk[pi] * a[pi], kf[pi], r[pi], v_hi, v[pi] - v_hi)
        for i, x in enumerate(vecs):
            swapped = pltpu.roll(x, HEAD, axis=1)
            vec_ref[i, 2 * pi] = jnp.where(lane_lo, x, swapped)
            vec_ref[i, 2 * pi + 1] = jnp.where(lane_lo, swapped, x)

    rows = n_heads * half
    m_in_head = lax.broadcasted_iota(jnp.int32, (rows, PAIR), 0) % half
    lane = lax.broadcasted_iota(jnp.int32, (rows, PAIR), 1)
    diag = (lane == 2 * m_in_head) | (lane == 2 * m_in_head + HEAD + 1)

    def expand(i, b):
        return jnp.concatenate(
            [jnp.broadcast_to(vec_ref[i, h, b:b + 1, :], (half, PAIR)) for h in range(n_heads)], axis=0)

    for g0 in range(0, bb, group):
        bs = range(g0, g0 + group)
        s = [s_ref[b] for b in bs]
        sa = [_dot((x * expand(0, b)).astype(BF16), block_ones) for x, b in zip(s, bs)]
        vb = [_dot(jnp.where(diag, expand(5, b), 0.0).astype(BF16), block_ones)
              + _dot(jnp.where(diag, expand(6, b), 0.0).astype(BF16), block_ones) for b in bs]
        s_new = [x * expand(1, b) + y * expand(2, b) + z * expand(3, b) for x, y, z, b in zip(s, sa, vb, bs)]
        for x, b in zip(s_new, bs):
            so_ref[b] = x
        yb = [_head_sum(x * expand(4, b), block_ones) for x, b in zip(s_new, bs)]
        for x, b in zip(yb, bs):
            yh = jnp.sum(jnp.where(diag, x, 0.0).reshape(n_heads, half, PAIR), axis=1)
            y_ref[b] = yh + pltpu.roll(yh, HEAD, axis=1)

    y = [jnp.where(lane_lo, y_ref[:, 2 * pi, :], y_ref[:, 2 * pi + 1, :]) for pi in range(len(sls))]
    o = _head_norm_bonus(y, r, kf, v, par_ref, d_a, block_ones)
    for oi, sl in zip(o, sls):
        oa_ref[:, sl] = (oi * sga_ref[:, sl]).astype(oa_ref.dtype)


def _wkv_sample(p_shift, shift_prev, gates, mu, par, lora_w, state, d_a, bb):
    batch, shift_w = p_shift.shape
    n_heads = d_a // HEAD
    rows = n_heads * HEAD // 2
    kern = functools.partial(_wkv_step_kernel, d_a=d_a, bb=bb, group=min(bb, 4))
    oa, new_state = pl.pallas_call(
        kern,
        out_shape=(jax.ShapeDtypeStruct((batch, d_a), BF16),
                   jax.ShapeDtypeStruct((batch, rows, PAIR), F32)),
        grid=(batch // bb,),
        in_specs=[pl.BlockSpec((bb, shift_w), lambda i: (i, 0)),
                  pl.BlockSpec((bb, shift_w), lambda i: (i, 0)),
                  pl.BlockSpec((bb, d_a), lambda i: (i, 0)),
                  pl.BlockSpec((1, shift_w), lambda i: (0, 0)),
                  pl.BlockSpec((8, d_a), lambda i: (0, 0)),
                  pl.BlockSpec((2 * LORA, 2 * d_a), lambda i: (0, 0)),
                  pl.BlockSpec((bb, rows, PAIR), lambda i: (i, 0, 0))],
        out_specs=(pl.BlockSpec((bb, d_a), lambda i: (i, 0)),
                   pl.BlockSpec((bb, rows, PAIR), lambda i: (i, 0, 0))),
        scratch_shapes=[pltpu.VMEM((7, n_heads, bb, PAIR), F32), pltpu.VMEM((bb, n_heads, PAIR), F32)],
        compiler_params=_params(1),
        name="wkv_step",
    )(p_shift, shift_prev, gates, mu, par, lora_w, state.reshape(batch, rows, PAIR))
    return oa, new_state.reshape(state.shape)


CONV_PAD = 32
CONV_STRIP = 16


def _conv_prompt_kernel(u_ref, w_ref, b_ref, o_ref, buf_ref, *, tt, taps):
    t = pl.program_id(1)

    @pl.when(t == 0)
    def _():
        buf_ref[0:CONV_PAD] = jnp.zeros((CONV_PAD,) + buf_ref.shape[1:], F32)

    buf_ref[CONV_PAD:CONV_PAD + tt] = u_ref[...]
    off = CONV_PAD - (taps - 1)
    w = [w_ref[k] for k in range(taps)]
    bias = b_ref[0]

    def strip(s, carry):
        t0 = s * CONV_STRIP
        acc = [bias] * CONV_STRIP
        for i in range(CONV_STRIP + taps - 1):
            x = buf_ref[off + t0 + i]
            for j in range(CONV_STRIP):
                if 0 <= i - j < taps:
                    acc[j] = acc[j] + w[i - j] * x
        for j in range(CONV_STRIP):
            o_ref[t0 + j] = acc[j]
        return carry

    lax.fori_loop(0, tt // CONV_STRIP, strip, 0)
    buf_ref[0:CONV_PAD] = buf_ref[tt:tt + CONV_PAD]


def _conv_prompt(u, conv_w, conv_b, batch, seq, tt):
    d_b = u.shape[1]
    taps = conv_w.shape[0]
    nt = seq // tt
    sub = d_b // 128
    assert taps - 1 <= CONV_PAD and tt % CONV_STRIP == 0 and tt >= CONV_PAD
    kern = functools.partial(_conv_prompt_kernel, tt=tt, taps=taps)
    c = pl.pallas_call(
        kern,
        out_shape=jax.ShapeDtypeStruct((batch * seq, sub, 128), F32),
        grid=(batch, nt),
        in_specs=[pl.BlockSpec((tt, sub, 128), lambda b, t: (b * nt + t, 0, 0)),
                  pl.BlockSpec((taps, sub, 128), lambda b, t: (0, 0, 0)),
                  pl.BlockSpec((1, sub, 128), lambda b, t: (0, 0, 0))],
        out_specs=pl.BlockSpec((tt, sub, 128), lambda b, t: (b * nt + t, 0, 0)),
        scratch_shapes=[pltpu.VMEM((tt + CONV_PAD, sub, 128), F32)],
        compiler_params=_params(2),
        name="conv_prompt",
    )(u.reshape(batch * seq, sub, 128), conv_w.reshape(taps, sub, 128), conv_b.reshape(1, sub, 128))
    return c.reshape(batch * seq, d_b)


def _conv_step_kernel(u_ref, prev_ref, w_ref, b_ref, o_ref, *, taps):
    c = b_ref[...] + w_ref[taps - 1:taps, :] * u_ref[...]
    for k in range(taps - 1):
        c = c + w_ref[k:k + 1, :] * prev_ref[:, k, :]
    o_ref[...] = c


def _conv_sample(u, conv_prev, conv_w, conv_b, bb):
    batch, d_b = u.shape
    taps = conv_w.shape[0]
    kern = functools.partial(_conv_step_kernel, taps=taps)
    return pl.pallas_call(
        kern,
        out_shape=jax.ShapeDtypeStruct((batch, d_b), F32),
        grid=(batch // bb,),
        in_specs=[pl.BlockSpec((bb, d_b), lambda i: (i, 0)),
                  pl.BlockSpec((bb, taps - 1, d_b), lambda i: (i, 0, 0)),
                  pl.BlockSpec((taps, d_b), lambda i: (0, 0)),
                  pl.BlockSpec((1, d_b), lambda i: (0, 0))],
        out_specs=pl.BlockSpec((bb, d_b), lambda i: (i, 0)),
        compiler_params=_params(1),
        name="conv_step",
    )(u, conv_prev, conv_w, conv_b.reshape(1, d_b))


def _tail_kernel(oa_ref, c_ref, gb_ref, sga_ref, sgb_ref, x_ref, p_ref, cln_ref, wa_ref, wb_ref, wout_ref,
                 wpg_ref, wple_ref, fg_ref, o_ref):
    c = c_ref[...]
    mean = jnp.mean(c, axis=-1, keepdims=True)
    dc = c - mean
    var = jnp.mean(dc * dc, axis=-1, keepdims=True)
    cf = dc * lax.rsqrt(var + LN_EPS) * cln_ref[0:1, :] + cln_ref[1:2, :]
    cb = (cf * _sigmoid(cf) * gb_ref[...]).astype(BF16)
    m = sga_ref[...] * _dot(oa_ref[...], wa_ref[...]) + sgb_ref[...] * _dot(cb, wb_ref[...])
    h = x_ref[...] + _dot(m.astype(BF16), wout_ref[...])
    gate = _sigmoid(_dot(h.astype(BF16), wpg_ref[...]))
    h = h + gate * _dot(p_ref[...].astype(BF16), wple_ref[...])
    ms = jnp.mean(h * h, axis=-1, keepdims=True)
    o_ref[...] = h * lax.rsqrt(ms + RMS_EPS) * fg_ref[...]


def _tail(oa, c, gates, x, p, cln, wa, wb, wout, wpg, wple, fg, tm, gate_b_block, merge_block):
    m, d = x.shape
    d_a = oa.shape[1]
    d_b = c.shape[1]
    ple = p.shape[1]
    row = lambda i: (i, 0)
    const = lambda i: (0, 0)
    resident = lambda shape: pl.BlockSpec(shape, const, pipeline_mode=pl.Buffered(1))
    return pl.pallas_call(
        _tail_kernel,
        out_shape=jax.ShapeDtypeStruct((m, d), F32),
        grid=(m // tm,),
        in_specs=[pl.BlockSpec((tm, d_a), row), pl.BlockSpec((tm, d_b), row),
                  pl.BlockSpec((tm, d_b), lambda i: (i, gate_b_block)),
                  pl.BlockSpec((tm, d), lambda i: (i, merge_block)),
                  pl.BlockSpec((tm, d), lambda i: (i, merge_block + 1)),
                  pl.BlockSpec((tm, d), row), pl.BlockSpec((tm, ple), row),
                  resident((2, d_b)),
                  resident((d_a, d)), resident((d_b, d)), resident((d, d)), resident((d, d)),
                  resident((ple, d)), resident((1, d))],
        out_specs=pl.BlockSpec((tm, d), row),
        compiler_params=_params(1),
        name="tail",
    )(oa, c, gates, gates, gates, x, p, cln, wa, wb, wout, wpg, wple, fg)


def _largest_tile(n, cap, align):
    t = min(n, cap)
    while n % t or t % align:
        t -= 1
    return t


def kernel(x_prompt, x_sample, state_shift, state_wkv, state_conv, p_prompt, p_sample, norm_g, w_in,
           shift_mu, w0, w_lora_b, a0, a_lora_b, k_k, k_a, r_k, lnx_g, lnx_b, w_proj_a, conv_w, conv_b,
           cln_g, cln_b, w_proj_b, w_out, w_ple, w_ple_gate, final_g):
    depth = w_in.shape[0]
    batch, seq, d = x_prompt.shape
    dec_batch, dec_seq, _ = x_sample.shape
    d_a = w_proj_a.shape[1]
    d_b = w_proj_b.shape[1]
    shift_w = shift_mu.shape[1]
    n_heads = d_a // HEAD
    n_pairs = d_a // PAIR
    taps = conv_w.shape[1]
    assert depth == 1 and dec_seq == 1 and d_a == d_b and 2 * d_a == d
    assert shift_w == 3 * d_a + 2 * LORA and d_a % PAIR == 0 and seq % CHUNK == 0

    o1 = shift_w
    o2 = o1 + d_a
    o3 = o2 + 2 * d_b
    o4 = o3 + d_b
    w = w_in[0]
    w_shift = w[:, :o1].astype(BF16)
    w_gates = jnp.concatenate([w[:, o1:o2], w[:, o3:o4], w[:, o4:]], axis=1).astype(BF16)
    w_glu_a = w[:, o2:o2 + d_b].astype(BF16)
    w_glu_b = w[:, o2 + d_b:o3].astype(BF16)
    wa = w_proj_a[0].astype(BF16)
    wb = w_proj_b[0].astype(BF16)
    wout = w_out[0].astype(BF16)
    wpg = w_ple_gate[0].astype(BF16)
    wple = w_ple[0].astype(BF16)
    lora_w = jnp.zeros((2 * LORA, 2 * d_a), F32)
    lora_w = lora_w.at[:LORA, :d_a].set(w_lora_b[0]).at[LORA:, d_a:].set(a_lora_b[0]).astype(BF16)
    zeros_a = jnp.zeros((d_a,), F32)
    par = jnp.stack([w0[0], a0[0], k_k[0], k_a[0], r_k[0].reshape(d_a), lnx_g[0], lnx_b[0], zeros_a])
    cln = jnp.stack([cln_g[0], cln_b[0]])
    mu = shift_mu
    g_in = norm_g[0].reshape(1, d)
    fg = final_g.reshape(1, d)
    gate_b_block = d_a // d_b
    merge_block = (d_a + d_b) // d

    def token_stage(x2, tm):
        xn = _rmsnorm_bf16(x2, g_in, _largest_tile(x2.shape[0], 256, 8))
        tn_shift = _largest_tile(o1, 1024, 128)
        p_shift = _project(xn, w_shift, tm, tn_shift)
        gates = _project(xn, w_gates, tm, d_a, n_silu_blocks=2, plain=False)
        u = _project_glu(xn, w_glu_a, w_glu_b, tm, d_b)
        return p_shift, gates, u

    m_p = batch * seq
    x2 = x_prompt.reshape(m_p, d)
    p_shift, gates, u = token_stage(x2, _largest_tile(m_p, 1024, 8))
    oa, st = _wkv_prompt(p_shift, gates, mu, par, lora_w, batch, seq, d_a)
    c = _conv_prompt(u, conv_w[0], conv_b[0], batch, seq, _largest_tile(seq, 256, CONV_STRIP))
    y_prompt = _tail(oa, c, gates, x2, p_prompt[0].reshape(m_p, -1), cln, wa, wb, wout, wpg, wple, fg,
                     _largest_tile(m_p, 256, 8), gate_b_block, merge_block).reshape(batch, seq, d)
    new_shift_p = p_shift.reshape(batch, seq, shift_w)[:, -1][None]
    st = st.reshape(batch, n_pairs, 2, HEAD, 2, HEAD)
    new_wkv_p = jnp.stack([st[:, :, 0, :, 0, :], st[:, :, 1, :, 1, :]], axis=2)
    new_wkv_p = jnp.swapaxes(new_wkv_p.reshape(batch, n_heads, HEAD, HEAD), -1, -2)[None]
    new_conv_p = u.reshape(batch, seq, d_b)[:, seq - (taps - 1):][None]

    xs2 = x_sample.reshape(dec_batch, d)
    p_shift_s, gates_s, u_s = token_stage(xs2, _largest_tile(dec_batch, 1024, 8))
    bb = _largest_tile(dec_batch, 8, 8)
    oa_s, new_wkv_s = _wkv_sample(p_shift_s, state_shift[0], gates_s, mu, par, lora_w, state_wkv[0], d_a, bb)
    c_s = _conv_sample(u_s, state_conv[0], conv_w[0], conv_b[0], bb)
    y_sample = _tail(oa_s, c_s, gates_s, xs2, p_sample[0].reshape(dec_batch, -1), cln, wa, wb, wout, wpg,
                     wple, fg, _largest_tile(dec_batch, 256, 8), gate_b_block,
                     merge_block).reshape(dec_batch, 1, d)
    new_shift_s = p_shift_s[None]
    new_conv_s = jnp.concatenate([state_conv[0][:, 1:], u_s[:, None, :]], axis=1)[None]

    return (y_prompt, y_sample, new_shift_p, new_wkv_p, new_conv_p, new_shift_s, new_wkv_s[None],
            new_conv_s)
```

```python
import functools

import jax
import jax.numpy as jnp
from jax import lax
from jax.experimental import pallas as pl
from jax.experimental.pallas import tpu as pltpu

F32 = jnp.float32
BF16 = jnp.bfloat16

HEAD = 64
PAIR = 2 * HEAD
LORA = 64
CHUNK = 64
RMS_EPS = 1e-6
LN_EPS = 1e-5
GN_EPS = 64e-5
VMEM_LIMIT = 56 * 1024 * 1024


def _params(n_axes, vmem=VMEM_LIMIT):
    return pltpu.CompilerParams(dimension_semantics=("arbitrary",) * n_axes, vmem_limit_bytes=vmem)


def _sigmoid(x):
    return 1.0 / (1.0 + jnp.exp(-x))


def _dot(a, b):
    return jnp.dot(a, b, preferred_element_type=F32)


def _dot_nt(a, b):
    return lax.dot_general(a, b, (((1,), (1,)), ((), ())), preferred_element_type=F32)


def _dot_tn(a, b):
    return lax.dot_general(a, b, (((0,), (0,)), ((), ())), preferred_element_type=F32)


def _rmsnorm_kernel(x_ref, g_ref, o_ref):
    x = x_ref[...]
    ms = jnp.mean(x * x, axis=-1, keepdims=True)
    o_ref[...] = (x * lax.rsqrt(ms + RMS_EPS) * g_ref[...]).astype(o_ref.dtype)


def _rmsnorm_bf16(x, g, tm):
    m, d = x.shape
    return pl.pallas_call(
        _rmsnorm_kernel,
        out_shape=jax.ShapeDtypeStruct((m, d), BF16),
        grid=(m // tm,),
        in_specs=[pl.BlockSpec((tm, d), lambda i: (i, 0)), pl.BlockSpec((1, d), lambda i: (0, 0))],
        out_specs=pl.BlockSpec((tm, d), lambda i: (i, 0)),
        compiler_params=_params(1),
        name="rmsnorm",
    )(x, g)


def _proj_kernel(x_ref, w_ref, o_ref, wb_ref, *, n_silu_blocks):
    @pl.when(pl.program_id(1) == 0)
    def _():
        wb_ref[...] = w_ref[...].astype(BF16)

    y = _dot(x_ref[...], wb_ref[...])
    if n_silu_blocks is None:
        o_ref[...] = y
    else:
        s = _sigmoid(y)
        o_ref[...] = jnp.where(pl.program_id(0) < n_silu_blocks, y * s, s)


def _project(xn, w, col_start, n_out, tm, tn, n_silu_blocks=None):
    m, d = xn.shape
    kern = functools.partial(_proj_kernel, n_silu_blocks=n_silu_blocks)
    return pl.pallas_call(
        kern,
        out_shape=jax.ShapeDtypeStruct((m, n_out), F32),
        grid=(n_out // tn, m // tm),
        in_specs=[pl.BlockSpec((tm, d), lambda j, i: (i, 0)),
                  pl.BlockSpec((pl.Element(d), pl.Element(tn)),
                               lambda j, i: (0, pl.multiple_of(col_start(j), 128)))],
        out_specs=pl.BlockSpec((tm, tn), lambda j, i: (i, j)),
        scratch_shapes=[pltpu.VMEM((d, tn), BF16)],
        compiler_params=_params(2),
        name="in_proj",
    )(xn, w)


def _glu_kernel(x_ref, wa_ref, wb_ref, o_ref, wab_ref, wbb_ref):
    @pl.when(pl.program_id(1) == 0)
    def _():
        wab_ref[...] = wa_ref[...].astype(BF16)
        wbb_ref[...] = wb_ref[...].astype(BF16)

    x = x_ref[...]
    o_ref[...] = _dot(x, wab_ref[...]) * _sigmoid(_dot(x, wbb_ref[...]))


def _project_glu(xn, w, col_a, col_b, n_out, tm, tn):
    m, d = xn.shape
    return pl.pallas_call(
        _glu_kernel,
        out_shape=jax.ShapeDtypeStruct((m, n_out), F32),
        grid=(n_out // tn, m // tm),
        in_specs=[pl.BlockSpec((tm, d), lambda j, i: (i, 0)),
                  pl.BlockSpec((pl.Element(d), pl.Element(tn)),
                               lambda j, i: (0, pl.multiple_of(col_a + j * tn, 128))),
                  pl.BlockSpec((pl.Element(d), pl.Element(tn)),
                               lambda j, i: (0, pl.multiple_of(col_b + j * tn, 128)))],
        out_specs=pl.BlockSpec((tm, tn), lambda j, i: (i, j)),
        scratch_shapes=[pltpu.VMEM((d, tn), BF16), pltpu.VMEM((d, tn), BF16)],
        compiler_params=_params(2),
        name="in_proj_glu",
    )(xn, w, w)


def _softplus(x):
    return jnp.maximum(x, 0.0) + jnp.log(1.0 + jnp.exp(-jnp.abs(x)))


def _head_sum(x, block_ones):
    hi = x.astype(BF16)
    lo = (x - hi.astype(F32)).astype(BF16)
    return _dot(hi, block_ones) + _dot(lo, block_ones)


def _head_sums(xs, block_ones):
    rows = xs[0].shape[0]
    s = _head_sum(jnp.concatenate(xs, axis=0), block_ones)
    return [s[i * rows:(i + 1) * rows] for i in range(len(xs))]


def _block_ones():
    ri = lax.broadcasted_iota(jnp.int32, (PAIR, PAIR), 0)
    ci = lax.broadcasted_iota(jnp.int32, (PAIR, PAIR), 1)
    return jnp.where((ri < HEAD) == (ci < HEAD), 1.0, 0.0).astype(BF16)


def _token_shift(p, prev, mu):
    return p + mu * (prev - p)


def _lora_out(xs, lora_w, d_a):
    z = xs[:, 3 * d_a:3 * d_a + 2 * LORA]
    lane = lax.broadcasted_iota(jnp.int32, z.shape, 1)
    z = jnp.where(lane < LORA, jnp.tanh(z), z)
    return _dot(z.astype(BF16), lora_w)


def _pair_slices(d_a):
    return [slice(i * PAIR, (i + 1) * PAIR) for i in range(d_a // PAIR)]


def _pair_vectors(xs, lo, par_ref, d_a, block_ones):
    sls = _pair_slices(d_a)
    shifted = lambda sl, off: slice(off + sl.start, off + sl.stop)
    r = [xs[:, sl] for sl in sls]
    k = [xs[:, shifted(sl, d_a)] for sl in sls]
    v = [xs[:, shifted(sl, 2 * d_a)] for sl in sls]
    kkr = [ki * par_ref[2:3, sl] for ki, sl in zip(k, sls)]
    ss = _head_sums([x * x for x in kkr], block_ones)
    kk = [x * jnp.minimum(lax.rsqrt(s), 1e12) for x, s in zip(kkr, ss)]
    log_decay = [-jnp.exp(-_softplus(-(par_ref[0:1, sl] + lo[:, sl])) - 0.5) for sl in sls]
    a = [_sigmoid(par_ref[1:2, sl] + lo[:, shifted(sl, d_a)]) for sl in sls]
    kf = [ki * (1.0 + (ai - 1.0) * par_ref[3:4, sl]) for ki, ai, sl in zip(k, a, sls)]
    return r, kf, v, log_decay, kk, a


def _head_norm_bonus(y, r, kf, v, par_ref, d_a, block_ones):
    sls = _pair_slices(d_a)
    mean = [s * (1.0 / HEAD) for s in _head_sums(y, block_ones)]
    d = [x - m for x, m in zip(y, mean)]
    var = [s * (1.0 / HEAD) for s in _head_sums([x * x for x in d], block_ones)]
    rk = _head_sums([ri * ki * par_ref[4:5, sl] for ri, ki, sl in zip(r, kf, sls)], block_ones)
    return [x * lax.rsqrt(vr + GN_EPS) * par_ref[5:6, sl] + par_ref[6:7, sl] + s * vi
            for x, vr, s, vi, sl in zip(d, var, rk, v, sls)]


def _stack_heads(x, lane_lo):
    xb = x.astype(BF16)
    zero = jnp.zeros_like(xb)
    return jnp.concatenate([jnp.where(lane_lo, xb, zero), jnp.where(lane_lo, zero, xb)], axis=0)


def _cumsum_rows(x):
    n = x.shape[0]
    row = lax.broadcasted_iota(jnp.int32, x.shape, 0)
    s = 1
    while s < n:
        x = x + jnp.where(row >= s, pltpu.roll(x, s, axis=0), 0.0)
        s *= 2
    return x


def _wkv_chunk_kernel(ps_ref, sga_ref, mu_ref, par_ref, lora_ref, oa_ref, st_ref, carry_ref, *, d_a):
    c = pl.program_id(1)
    n_pairs = d_a // PAIR
    C = CHUNK

    @pl.when(c == 0)
    def _():
        carry_ref[...] = jnp.zeros_like(carry_ref)
        st_ref[...] = jnp.zeros_like(st_ref)

    p = ps_ref[...]
    row = lax.broadcasted_iota(jnp.int32, p.shape, 0)
    prev = jnp.where(row == 0, carry_ref[...], pltpu.roll(p, 1, axis=0))
    carry_ref[...] = p[C - 1:C, :]
    xs = _token_shift(p, prev, mu_ref[...])
    lo = _lora_out(xs, lora_ref[...], d_a)

    block_ones = _block_ones()
    lane_lo = lax.broadcasted_iota(jnp.int32, (C, PAIR), 1) < HEAD
    ti = lax.broadcasted_iota(jnp.int32, (C, PAIR), 0)
    si = lax.broadcasted_iota(jnp.int32, (C, PAIR), 1) % C
    strict_lower = si < ti
    lower = si <= ti
    eye = jnp.where(si == ti, 1.0, 0.0)
    cat0 = lambda x, y: jnp.concatenate([x, y], axis=0)

    sls = _pair_slices(d_a)
    r, kf, v, log_decay, kk, a = _pair_vectors(xs, lo, par_ref, d_a, block_ones)
    cl = [_cumsum_rows(x) for x in log_decay]
    cend = [x[C - 1:C, :] for x in cl]
    e_neg = [jnp.exp(-x) for x in cl]
    e_end = [jnp.exp(ce - x) for ce, x in zip(cend, cl)]
    kka = [x * y for x, y in zip(kk, a)]
    stack = lambda xs_: [_stack_heads(x, lane_lo) for x in xs_]
    a_p = [(-x * jnp.exp(c_ - ld)).astype(BF16) for x, c_, ld in zip(kk, cl, log_decay)]
    r_p = [(x * jnp.exp(c_)).astype(BF16) for x, c_ in zip(r, cl)]
    a_s = stack(a_p)
    b_s = stack([x * e for x, e in zip(kka, e_neg)])
    k_s = stack([x * e for x, e in zip(kf, e_neg)])
    v_s = stack(v)
    bh_s = stack([x * e for x, e in zip(kka, e_end)])
    kh_s = stack([x * e for x, e in zip(kf, e_end)])
    pend_col = [jnp.transpose(jnp.broadcast_to(jnp.exp(ce), (PAIR, PAIR))) for ce in cend]

    m1 = [_dot_nt(cat0(x, y), cat0(z, w)) for x, y, z, w in zip(a_p, r_p, b_s, k_s)]
    a_ab = [jnp.where(strict_lower, m[:C, :PAIR], 0.0) for m in m1]
    a_akrk = [cat0(jnp.where(strict_lower, m[:C, PAIR:], 0.0), jnp.where(lower, m[C:, PAIR:], 0.0)).astype(BF16)
              for m in m1]
    a_rb = [jnp.where(lower, m[C:, :PAIR], 0.0).astype(BF16) for m in m1]
    gy = [_dot(x, y) for x, y in zip(a_akrk, v_s)]
    sv = [_dot_tn(x, y) for x, y in zip(kh_s, v_s)]

    apow = [_dot(x.astype(BF16), _stack_heads(x, lane_lo)) for x in a_ab]
    tinv = [eye + x for x in a_ab]
    n_levels = CHUNK.bit_length() - 1
    for lvl in range(1, n_levels):
        rhs = stack(apow)
        if lvl < n_levels - 1:
            both = [_dot(cat0(x.astype(BF16), t.astype(BF16)), y) for x, t, y in zip(apow, tinv, rhs)]
            apow = [x[:C] for x in both]
            tinv = [t + x[C:] for t, x in zip(tinv, both)]
        else:
            tinv = [t + _dot(t.astype(BF16), y) for t, y in zip(tinv, rhs)]

    g_s = stack([g[:C] for g in gy])
    wx = [_dot(t.astype(BF16), jnp.concatenate([x, g], axis=1)) for t, x, g in zip(tinv, a_s, g_s)]
    st = [st_ref[0, pi] for pi in range(n_pairs)]
    uy = [_dot(cat0(w[:, :PAIR].astype(BF16), x), s_.astype(BF16)) for w, x, s_ in zip(wx, r_p, st)]
    u_s = stack([x[:C] + w[:, PAIR:] for x, w in zip(uy, wx)])
    st_upd = [_dot_tn(x, u) for x, u in zip(bh_s, u_s)]
    y_u = [_dot(x, u) for x, u in zip(a_rb, u_s)]
    for pi in range(n_pairs):
        st_ref[0, pi] = pend_col[pi] * st[pi] + st_upd[pi] + sv[pi]
    y = [x[C:] + yu + g[C:] for x, yu, g in zip(uy, y_u, gy)]
    o = _head_norm_bonus(y, r, kf, v, par_ref, d_a, block_ones)
    for oi, sl in zip(o, sls):
        oa_ref[:, sl] = (oi * sga_ref[:, sl]).astype(oa_ref.dtype)


def _wkv_prompt(p_shift, gates, mu, par, lora_w, batch, seq, d_a):
    n_chunks = seq // CHUNK
    shift_w = p_shift.shape[1]
    n_pairs = d_a // PAIR
    kern = functools.partial(_wkv_chunk_kernel, d_a=d_a)
    return pl.pallas_call(
        kern,
        out_shape=(jax.ShapeDtypeStruct((batch * seq, d_a), BF16),
                   jax.ShapeDtypeStruct((batch, n_pairs, PAIR, PAIR), F32)),
        grid=(batch, n_chunks),
        in_specs=[pl.BlockSpec((CHUNK, shift_w), lambda b, c: (b * n_chunks + c, 0)),
                  pl.BlockSpec((CHUNK, d_a), lambda b, c: (b * n_chunks + c, 0)),
                  pl.BlockSpec((1, shift_w), lambda b, c: (0, 0)),
                  pl.BlockSpec((8, d_a), lambda b, c: (0, 0)),
                  pl.BlockSpec((2 * LORA, 2 * d_a), lambda b, c: (0, 0))],
        out_specs=(pl.BlockSpec((CHUNK, d_a), lambda b, c: (b * n_chunks + c, 0)),
                   pl.BlockSpec((1, n_pairs, PAIR, PAIR), lambda b, c: (b, 0, 0, 0))),
        scratch_shapes=[pltpu.VMEM((1, shift_w), F32)],
        compiler_params=_params(2),
        name="wkv_chunked",
    )(p_shift, gates, mu, par, lora_w)


def _wkv_step_kernel(ps_ref, prev_ref, sga_ref, mu_ref, par_ref, lora_ref, s_ref, oa_ref, so_ref,
                     vec_ref, y_ref, *, d_a, bb, group):
    n_heads = d_a // HEAD
    half = HEAD // 2
    xs = _token_shift(ps_ref[...], prev_ref[...], mu_ref[...])
    lo = _lora_out(xs, lora_ref[...], d_a)
    block_ones = _block_ones()
    sls = _pair_slices(d_a)
    r, kf, v, log_decay, kk, a = _pair_vectors(xs, lo, par_ref, d_a, block_ones)

    lane_lo = lax.broadcasted_iota(jnp.int32, (bb, PAIR), 1) < HEAD
    for pi in range(len(sls)):
        v_hi = v[pi].astype(BF16).astype(F32)
        vecs = (-kk[pi], jnp.exp(log_decay[pi]), kk[pi] * a[pi], kf[pi], r[pi], v_hi, v[pi] - v_hi)
        for i, x in enumerate(vecs):
            swapped = pltpu.roll(x, HEAD, axis=1)
            vec_ref[i, 2 * pi] = jnp.where(lane_lo, x, swapped)
            vec_ref[i, 2 * pi + 1] = jnp.where(lane_lo, swapped, x)

    rows = n_heads * half
    m_in_head = lax.broadcasted_iota(jnp.int32, (rows, PAIR), 0) % half
    lane = lax.broadcasted_iota(jnp.int32, (rows, PAIR), 1)
    diag = (lane == 2 * m_in_head) | (lane == 2 * m_in_head + HEAD + 1)

    def expand(i, b):
        return jnp.concatenate(
            [jnp.broadcast_to(vec_ref[i, h, b:b + 1, :], (half, PAIR)) for h in range(n_heads)], axis=0)

    for g0 in range(0, bb, group):
        bs = range(g0, g0 + group)
        s = [s_ref[b] for b in bs]
        sa = [_dot((x * expand(0, b)).astype(BF16), block_ones) for x, b in zip(s, bs)]
        vb = [_dot(jnp.where(diag, expand(5, b), 0.0).astype(BF16), block_ones)
              + _dot(jnp.where(diag, expand(6, b), 0.0).astype(BF16), block_ones) for b in bs]
        s_new = [x * expand(1, b) + y * expand(2, b) + z * expand(3, b) for x, y, z, b in zip(s, sa, vb, bs)]
        for x, b in zip(s_new, bs):
            so_ref[b] = x
        yb = [_head_sum(x * expand(4, b), block_ones) for x, b in zip(s_new, bs)]
        for x, b in zip(yb, bs):
            yh = jnp.sum(jnp.where(diag, x, 0.0).reshape(n_heads, half, PAIR), axis=1)
            y_ref[b] = yh + pltpu.roll(yh, HEAD, axis=1)

    y = [jnp.where(lane_lo, y_ref[:, 2 * pi, :], y_ref[:, 2 * pi + 1, :]) for pi in range(len(sls))]
    o = _head_norm_bonus(y, r, kf, v, par_ref, d_a, block_ones)
    for oi, sl in zip(o, sls):
        oa_ref[:, sl] = (oi * sga_ref[:, sl]).astype(oa_ref.dtype)


def _wkv_sample(p_shift, shift_prev, gates, mu, par, lora_w, state, d_a, bb):
    batch, shift_w = p_shift.shape
    n_heads = d_a // HEAD
    rows = n_heads * HEAD // 2
    kern = functools.partial(_wkv_step_kernel, d_a=d_a, bb=bb, group=min(bb, 4))
    oa, new_state = pl.pallas_call(
        kern,
        out_shape=(jax.ShapeDtypeStruct((batch, d_a), BF16),
                   jax.ShapeDtypeStruct((batch, rows, PAIR), F32)),
        grid=(batch // bb,),
        in_specs=[pl.BlockSpec((bb, shift_w), lambda i: (i, 0)),
                  pl.BlockSpec((bb, shift_w), lambda i: (i, 0)),
                  pl.BlockSpec((bb, d_a), lambda i: (i, 0)),
                  pl.BlockSpec((1, shift_w), lambda i: (0, 0)),
                  pl.BlockSpec((8, d_a), lambda i: (0, 0)),
                  pl.BlockSpec((2 * LORA, 2 * d_a), lambda i: (0, 0)),
                  pl.BlockSpec((bb, rows, PAIR), lambda i: (i, 0, 0))],
        out_specs=(pl.BlockSpec((bb, d_a), lambda i: (i, 0)),
                   pl.BlockSpec((bb, rows, PAIR), lambda i: (i, 0, 0))),
        scratch_shapes=[pltpu.VMEM((7, n_heads, bb, PAIR), F32), pltpu.VMEM((bb, n_heads, PAIR), F32)],
        compiler_params=_params(1),
        name="wkv_step",
    )(p_shift, shift_prev, gates, mu, par, lora_w, state.reshape(batch, rows, PAIR))
    return oa, new_state.reshape(state.shape)


CONV_PAD = 32
CONV_STRIP = 16


def _conv_prompt_kernel(u_ref, w_ref, b_ref, o_ref, buf_ref, c_ref, *, tt, taps):
    t = pl.program_id(1)
    sub = buf_ref.shape[1]
    lanes = [slice(s * 128, (s + 1) * 128) for s in range(sub)]

    @pl.when(t == 0)
    def _():
        buf_ref[0:CONV_PAD] = jnp.zeros((CONV_PAD,) + buf_ref.shape[1:], F32)

    u = u_ref[...]
    buf_ref[CONV_PAD:CONV_PAD + tt] = jnp.swapaxes(jnp.stack([u[:, ls] for ls in lanes], axis=0), 0, 1)
    off = CONV_PAD - (taps - 1)
    w = [w_ref[k] for k in range(taps)]
    bias = b_ref[0]

    def strip(s, carry):
        t0 = s * CONV_STRIP
        acc = [bias] * CONV_STRIP
        for i in range(CONV_STRIP + taps - 1):
            x = buf_ref[off + t0 + i]
            for j in range(CONV_STRIP):
                if 0 <= i - j < taps:
                    acc[j] = acc[j] + w[i - j] * x
        for j in range(CONV_STRIP):
            c_ref[t0 + j] = acc[j]
        return carry

    lax.fori_loop(0, tt // CONV_STRIP, strip, 0)
    buf_ref[0:CONV_PAD] = buf_ref[tt:tt + CONV_PAD]
    c = jnp.swapaxes(c_ref[...], 0, 1)
    for s, ls in enumerate(lanes):
        o_ref[:, ls] = c[s]


def _conv_prompt(u, conv_w, conv_b, batch, seq, tt):
    d_b = u.shape[1]
    taps = conv_w.shape[0]
    nt = seq // tt
    sub = d_b // 128
    assert taps - 1 <= CONV_PAD and tt % CONV_STRIP == 0 and tt >= CONV_PAD
    kern = functools.partial(_conv_prompt_kernel, tt=tt, taps=taps)
    return pl.pallas_call(
        kern,
        out_shape=jax.ShapeDtypeStruct((batch * seq, d_b), F32),
        grid=(batch, nt),
        in_specs=[pl.BlockSpec((tt, d_b), lambda b, t: (b * nt + t, 0)),
                  pl.BlockSpec((taps, sub, 128), lambda b, t: (0, 0, 0)),
                  pl.BlockSpec((1, sub, 128), lambda b, t: (0, 0, 0))],
        out_specs=pl.BlockSpec((tt, d_b), lambda b, t: (b * nt + t, 0)),
        scratch_shapes=[pltpu.VMEM((tt + CONV_PAD, sub, 128), F32), pltpu.VMEM((tt, sub, 128), F32)],
        compiler_params=_params(2),
        name="conv_prompt",
    )(u, conv_w.reshape(taps, sub, 128), conv_b.reshape(1, sub, 128))


def _conv_step_kernel(u_ref, prev_ref, w_ref, b_ref, o_ref, *, taps):
    c = b_ref[...] + w_ref[taps - 1:taps, :] * u_ref[...]
    for k in range(taps - 1):
        c = c + w_ref[k:k + 1, :] * prev_ref[:, k, :]
    o_ref[...] = c


def _conv_sample(u, conv_prev, conv_w, conv_b, bb):
    batch, d_b = u.shape
    taps = conv_w.shape[0]
    kern = functools.partial(_conv_step_kernel, taps=taps)
    return pl.pallas_call(
        kern,
        out_shape=jax.ShapeDtypeStruct((batch, d_b), F32),
        grid=(batch // bb,),
        in_specs=[pl.BlockSpec((bb, d_b), lambda i: (i, 0)),
                  pl.BlockSpec((bb, taps - 1, d_b), lambda i: (i, 0, 0)),
                  pl.BlockSpec((taps, d_b), lambda i: (0, 0)),
                  pl.BlockSpec((1, d_b), lambda i: (0, 0))],
        out_specs=pl.BlockSpec((bb, d_b), lambda i: (i, 0)),
        compiler_params=_params(1),
        name="conv_step",
    )(u, conv_prev, conv_w, conv_b.reshape(1, d_b))


def _tail_kernel(oa_ref, c_ref, gb_ref, sga_ref, sgb_ref, x_ref, p_ref, cln_ref, wa_ref, wb_ref, wout_ref,
                 wpg_ref, wple_ref, fg_ref, o_ref):
    c = c_ref[...]
    mean = jnp.mean(c, axis=-1, keepdims=True)
    dc = c - mean
    var = jnp.mean(dc * dc, axis=-1, keepdims=True)
    cf = dc * lax.rsqrt(var + LN_EPS) * cln_ref[0:1, :] + cln_ref[1:2, :]
    cb = (cf * _sigmoid(cf) * gb_ref[...]).astype(BF16)
    m = sga_ref[...] * _dot(oa_ref[...], wa_ref[...]) + sgb_ref[...] * _dot(cb, wb_ref[...])
    h = x_ref[...] + _dot(m.astype(BF16), wout_ref[...])
    gate = _sigmoid(_dot(h.astype(BF16), wpg_ref[...]))
    h = h + gate * _dot(p_ref[...].astype(BF16), wple_ref[...])
    ms = jnp.mean(h * h, axis=-1, keepdims=True)
    o_ref[...] = h * lax.rsqrt(ms + RMS_EPS) * fg_ref[...]


def _tail(oa, c, gates, x, p, cln, wa, wb, wout, wpg, wple, fg, tm, gate_b_block, merge_block):
    m, d = x.shape
    d_a = oa.shape[1]
    d_b = c.shape[1]
    ple = p.shape[1]
    row = lambda i: (i, 0)
    const = lambda i: (0, 0)
    resident = lambda shape: pl.BlockSpec(shape, const, pipeline_mode=pl.Buffered(1))
    return pl.pallas_call(
        _tail_kernel,
        out_shape=jax.ShapeDtypeStruct((m, d), F32),
        grid=(m // tm,),
        in_specs=[pl.BlockSpec((tm, d_a), row), pl.BlockSpec((tm, d_b), row),
                  pl.BlockSpec((tm, d_b), lambda i: (i, gate_b_block)),
                  pl.BlockSpec((tm, d), lambda i: (i, merge_block)),
                  pl.BlockSpec((tm, d), lambda i: (i, merge_block + 1)),
                  pl.BlockSpec((tm, d), row), pl.BlockSpec((tm, ple), row),
                  resident((2, d_b)),
                  resident((d_a, d)), resident((d_b, d)), resident((d, d)), resident((d, d)),
                  resident((ple, d)), resident((1, d))],
        out_specs=pl.BlockSpec((tm, d), row),
        compiler_params=_params(1),
        name="tail",
    )(oa, c, gates, gates, gates, x, p, cln, wa, wb, wout, wpg, wple, fg)


def _largest_tile(n, cap, align):
    t = min(n, cap)
    while n % t or t % align:
        t -= 1
    return t


def kernel(x_prompt, x_sample, state_shift, state_wkv, state_conv, p_prompt, p_sample, norm_g, w_in,
           shift_mu, w0, w_lora_b, a0, a_lora_b, k_k, k_a, r_k, lnx_g, lnx_b, w_proj_a, conv_w, conv_b,
           cln_g, cln_b, w_proj_b, w_out, w_ple, w_ple_gate, final_g):
    depth = w_in.shape[0]
    batch, seq, d = x_prompt.shape
    dec_batch, dec_seq, _ = x_sample.shape
    d_a = w_proj_a.shape[1]
    d_b = w_proj_b.shape[1]
    shift_w = shift_mu.shape[1]
    n_heads = d_a // HEAD
    n_pairs = d_a // PAIR
    taps = conv_w.shape[1]
    assert depth == 1 and dec_seq == 1 and d_a == d_b and 2 * d_a == d
    assert shift_w == 3 * d_a + 2 * LORA and d_a % PAIR == 0 and seq % CHUNK == 0

    o1 = shift_w
    o2 = o1 + d_a
    o3 = o2 + 2 * d_b
    o4 = o3 + d_b
    w = w_in[0]
    wa = w_proj_a[0].astype(BF16)
    wb = w_proj_b[0].astype(BF16)
    wout = w_out[0].astype(BF16)
    wpg = w_ple_gate[0].astype(BF16)
    wple = w_ple[0].astype(BF16)
    lora_w = jnp.zeros((2 * LORA, 2 * d_a), F32)
    lora_w = lora_w.at[:LORA, :d_a].set(w_lora_b[0]).at[LORA:, d_a:].set(a_lora_b[0]).astype(BF16)
    zeros_a = jnp.zeros((d_a,), F32)
    par = jnp.stack([w0[0], a0[0], k_k[0], k_a[0], r_k[0].reshape(d_a), lnx_g[0], lnx_b[0], zeros_a])
    cln = jnp.stack([cln_g[0], cln_b[0]])
    mu = shift_mu
    g_in = norm_g[0].reshape(1, d)
    fg = final_g.reshape(1, d)
    gate_b_block = d_a // d_b
    merge_block = (d_a + d_b) // d

    def token_stage(x2, tm):
        xn = _rmsnorm_bf16(x2, g_in, _largest_tile(x2.shape[0], 256, 8))
        tn_shift = _largest_tile(o1, 1024, 128)
        p_shift = _project(xn, w, lambda j: j * tn_shift, o1, tm, tn_shift)
        gate_cols = lambda j: jnp.where(j == 0, o1, o3 + (j - 1) * d_a)
        gates = _project(xn, w, gate_cols, d_a + d_b + 2 * d, tm, d_a, n_silu_blocks=2)
        u = _project_glu(xn, w, o2, o2 + d_b, d_b, tm, _largest_tile(d_b, 512, 128))
        return p_shift, gates, u

    m_p = batch * seq
    x2 = x_prompt.reshape(m_p, d)
    p_shift, gates, u = token_stage(x2, _largest_tile(m_p, 1024, 8))
    oa, st = _wkv_prompt(p_shift, gates, mu, par, lora_w, batch, seq, d_a)
    c = _conv_prompt(u, conv_w[0], conv_b[0], batch, seq, _largest_tile(seq, 256, CONV_STRIP))
    y_prompt = _tail(oa, c, gates, x2, p_prompt[0].reshape(m_p, -1), cln, wa, wb, wout, wpg, wple, fg,
                     _largest_tile(m_p, 256, 8), gate_b_block, merge_block).reshape(batch, seq, d)
    new_shift_p = p_shift.reshape(batch, seq, shift_w)[:, -1][None]
    st = st.reshape(batch, n_pairs, 2, HEAD, 2, HEAD)
    new_wkv_p = jnp.stack([st[:, :, 0, :, 0, :], st[:, :, 1, :, 1, :]], axis=2)
    new_wkv_p = jnp.swapaxes(new_wkv_p.reshape(batch, n_heads, HEAD, HEAD), -1, -2)[None]
    new_conv_p = u.reshape(batch, seq, d_b)[:, seq - (taps - 1):][None]

    xs2 = x_sample.reshape(dec_batch, d)
    p_shift_s, gates_s, u_s = token_stage(xs2, _largest_tile(dec_batch, 1024, 8))
    bb = _largest_tile(dec_batch, 8, 8)
    oa_s, new_wkv_s = _wkv_sample(p_shift_s, state_shift[0], gates_s, mu, par, lora_w, state_wkv[0], d_a, bb)
    c_s = _conv_sample(u_s, state_conv[0], conv_w[0], conv_b[0], bb)
    y_sample = _tail(oa_s, c_s, gates_s, xs2, p_sample[0].reshape(dec_batch, -1), cln, wa, wb, wout, wpg,
                     wple, fg, _largest_tile(dec_batch, 256, 8), gate_b_block,
                     merge_block).reshape(dec_batch, 1, d)
    new_shift_s = p_shift_s[None]
    new_conv_s = jnp.concatenate([state_conv[0][:, 1:], u_s[:, None, :]], axis=1)[None]

    return (y_prompt, y_sample, new_shift_p, new_wkv_p, new_conv_p, new_shift_s, new_wkv_s[None],
            new_conv_s)
```

```python
import functools
import math

import jax
import jax.numpy as jnp
from jax import lax
from jax.experimental import pallas as pl
from jax.experimental.pallas import tpu as pltpu

F32 = jnp.float32
BF16 = jnp.bfloat16

HEAD = 64
PAIR = 2 * HEAD
LORA = 64
CHUNK = 64
WKV_SEQS_PER_STEP = 4
RMS_EPS = 1e-6
LN_EPS = 1e-5
GN_EPS = 64e-5
LOG_DECAY_SCALE = math.exp(-0.5)
VMEM_LIMIT = 56 * 1024 * 1024


def _params(n_axes, vmem=VMEM_LIMIT):
    return pltpu.CompilerParams(dimension_semantics=("arbitrary",) * n_axes, vmem_limit_bytes=vmem)


def _sigmoid(x):
    return 1.0 / (1.0 + jnp.exp(-x))


def _dot(a, b):
    return jnp.dot(a, b, preferred_element_type=F32)


def _dot_nt(a, b):
    return lax.dot_general(a, b, (((1,), (1,)), ((), ())), preferred_element_type=F32)


def _dot_tn(a, b):
    return lax.dot_general(a, b, (((0,), (0,)), ((), ())), preferred_element_type=F32)


def _rmsnorm_kernel(x_ref, g_ref, o_ref):
    x = x_ref[...]
    ms = jnp.mean(x * x, axis=-1, keepdims=True)
    o_ref[...] = (x * lax.rsqrt(ms + RMS_EPS) * g_ref[...]).astype(o_ref.dtype)


def _rmsnorm_bf16(x, g, tm):
    m, d = x.shape
    return pl.pallas_call(
        _rmsnorm_kernel,
        out_shape=jax.ShapeDtypeStruct((m, d), BF16),
        grid=(m // tm,),
        in_specs=[pl.BlockSpec((tm, d), lambda i: (i, 0)), pl.BlockSpec((1, d), lambda i: (0, 0))],
        out_specs=pl.BlockSpec((tm, d), lambda i: (i, 0)),
        compiler_params=_params(1),
        name="rmsnorm",
    )(x, g)


def _proj_kernel(x_ref, w_ref, o_ref, wb_ref, *, n_silu_blocks):
    @pl.when(pl.program_id(1) == 0)
    def _():
        wb_ref[...] = w_ref[...].astype(BF16)

    y = _dot(x_ref[...], wb_ref[...])
    if n_silu_blocks is None:
        o_ref[...] = y
    else:
        s = _sigmoid(y)
        o_ref[...] = jnp.where(pl.program_id(0) < n_silu_blocks, y * s, s)


def _project(xn, w, col_start, n_out, tm, tn, n_silu_blocks=None):
    m, d = xn.shape
    kern = functools.partial(_proj_kernel, n_silu_blocks=n_silu_blocks)
    return pl.pallas_call(
        kern,
        out_shape=jax.ShapeDtypeStruct((m, n_out), F32),
        grid=(n_out // tn, m // tm),
        in_specs=[pl.BlockSpec((tm, d), lambda j, i: (i, 0)),
                  pl.BlockSpec((pl.Element(d), pl.Element(tn)),
                               lambda j, i: (0, pl.multiple_of(col_start(j), 128)))],
        out_specs=pl.BlockSpec((tm, tn), lambda j, i: (i, j)),
        scratch_shapes=[pltpu.VMEM((d, tn), BF16)],
        compiler_params=_params(2),
        name="in_proj",
    )(xn, w)


def _glu_kernel(x_ref, wa_ref, wb_ref, o_ref, wab_ref, wbb_ref):
    @pl.when(pl.program_id(1) == 0)
    def _():
        wab_ref[...] = wa_ref[...].astype(BF16)
        wbb_ref[...] = wb_ref[...].astype(BF16)

    x = x_ref[...]
    o_ref[...] = _dot(x, wab_ref[...]) * _sigmoid(_dot(x, wbb_ref[...]))


def _project_glu(xn, w, col_a, col_b, n_out, tm, tn):
    m, d = xn.shape
    return pl.pallas_call(
        _glu_kernel,
        out_shape=jax.ShapeDtypeStruct((m, n_out), F32),
        grid=(n_out // tn, m // tm),
        in_specs=[pl.BlockSpec((tm, d), lambda j, i: (i, 0)),
                  pl.BlockSpec((pl.Element(d), pl.Element(tn)),
                               lambda j, i: (0, pl.multiple_of(col_a + j * tn, 128))),
                  pl.BlockSpec((pl.Element(d), pl.Element(tn)),
                               lambda j, i: (0, pl.multiple_of(col_b + j * tn, 128)))],
        out_specs=pl.BlockSpec((tm, tn), lambda j, i: (i, j)),
        scratch_shapes=[pltpu.VMEM((d, tn), BF16), pltpu.VMEM((d, tn), BF16)],
        compiler_params=_params(2),
        name="in_proj_glu",
    )(xn, w, w)


def _head_sum(x, block_ones, split=False):
    hi = x.astype(BF16)
    if not split:
        return _dot(hi, block_ones)
    lo = (x - hi.astype(F32)).astype(BF16)
    return _dot(hi, block_ones) + _dot(lo, block_ones)


def _head_sums(xs, block_ones, split=False):
    rows = xs[0].shape[0]
    s = _head_sum(jnp.concatenate(xs, axis=0), block_ones, split)
    return [s[i * rows:(i + 1) * rows] for i in range(len(xs))]


def _block_ones():
    ri = lax.broadcasted_iota(jnp.int32, (PAIR, PAIR), 0)
    ci = lax.broadcasted_iota(jnp.int32, (PAIR, PAIR), 1)
    return jnp.where((ri < HEAD) == (ci < HEAD), 1.0, 0.0).astype(BF16)


def _token_shift(p, prev, mu):
    return p + mu * (prev - p)


def _lora_out(xs, lora_w, d_a):
    z = xs[:, 3 * d_a:3 * d_a + 2 * LORA]
    lane = lax.broadcasted_iota(jnp.int32, z.shape, 1)
    z = jnp.where(lane < LORA, jnp.tanh(z), z)
    return _dot(z.astype(BF16), lora_w)


def _pair_slices(d_a):
    return [slice(i * PAIR, (i + 1) * PAIR) for i in range(d_a // PAIR)]


def _pair_vectors(xs_list, lo_list, par_ref, d_a, block_ones):
    sls = _pair_slices(d_a)
    shifted = lambda sl, off: slice(off + sl.start, off + sl.stop)
    items = [(xs, lo, sl) for xs, lo in zip(xs_list, lo_list) for sl in sls]
    r = [xs[:, sl] for xs, _, sl in items]
    k = [xs[:, shifted(sl, d_a)] for xs, _, sl in items]
    v = [xs[:, shifted(sl, 2 * d_a)] for xs, _, sl in items]
    kkr = [ki * par_ref[2:3, sl] for ki, (_, _, sl) in zip(k, items)]
    ss = _head_sums([x * x for x in kkr], block_ones)
    kk = [x * jnp.minimum(lax.rsqrt(s), 1e12) for x, s in zip(kkr, ss)]
    log_decay = [-LOG_DECAY_SCALE * _sigmoid(par_ref[0:1, sl] + lo[:, sl]) for _, lo, sl in items]
    a = [_sigmoid(par_ref[1:2, sl] + lo[:, shifted(sl, d_a)]) for _, lo, sl in items]
    kf = [ki * (1.0 + (ai - 1.0) * par_ref[3:4, sl]) for ki, ai, (_, _, sl) in zip(k, a, items)]
    return r, kf, v, log_decay, kk, a


def _head_norm_bonus(y, r, kf, v, par_ref, d_a, block_ones):
    sls = _pair_slices(d_a) * (len(y) // (d_a // PAIR))
    mean = [s * (1.0 / HEAD) for s in _head_sums(y, block_ones, split=True)]
    d = [x - m for x, m in zip(y, mean)]
    var = [s * (1.0 / HEAD) for s in _head_sums([x * x for x in d], block_ones)]
    rk = _head_sums([ri * ki * par_ref[4:5, sl] for ri, ki, sl in zip(r, kf, sls)], block_ones)
    return [x * lax.rsqrt(vr + GN_EPS) * par_ref[5:6, sl] + par_ref[6:7, sl] + s * vi
            for x, vr, s, vi, sl in zip(d, var, rk, v, sls)]


def _stack_heads(x, lane_lo):
    xb = x.astype(BF16)
    zero = jnp.zeros_like(xb)
    return jnp.concatenate([jnp.where(lane_lo, xb, zero), jnp.where(lane_lo, zero, xb)], axis=0)


def _cumsum_rows(xs):
    n = xs[0].shape[0]
    tri = (lax.broadcasted_iota(jnp.int32, (n, n), 1) <= lax.broadcasted_iota(jnp.int32, (n, n), 0))
    tri = jnp.where(tri, 1.0, 0.0).astype(BF16)
    x = jnp.concatenate(xs, axis=1)
    hi = x.astype(BF16)
    lo = (x - hi.astype(F32)).astype(BF16)
    s = _dot(tri, hi) + _dot(tri, lo)
    return [s[:, i * PAIR:(i + 1) * PAIR] for i in range(len(xs))]


def _wkv_chunk_kernel(ps_ref, sga_ref, mu_ref, par_ref, lora_ref, oa_ref, st_ref, carry_ref, *, d_a, nb):
    c = pl.program_id(1)
    n_pairs = d_a // PAIR
    C = CHUNK

    @pl.when(c == 0)
    def _():
        carry_ref[...] = jnp.zeros_like(carry_ref)
        st_ref[...] = jnp.zeros_like(st_ref)

    xs, lo = [], []
    row = lax.broadcasted_iota(jnp.int32, ps_ref.shape[1:], 0)
    for bi in range(nb):
        p = ps_ref[bi]
        prev = jnp.where(row == 0, carry_ref[bi], pltpu.roll(p, 1, axis=0))
        carry_ref[bi] = p[C - 1:C, :]
        xs.append(_token_shift(p, prev, mu_ref[...]))
        lo.append(_lora_out(xs[-1], lora_ref[...], d_a))

    block_ones = _block_ones()
    lane_lo = lax.broadcasted_iota(jnp.int32, (C, PAIR), 1) < HEAD
    ti = lax.broadcasted_iota(jnp.int32, (C, PAIR), 0)
    si = lax.broadcasted_iota(jnp.int32, (C, PAIR), 1) % C
    strict_lower = si < ti
    lower2 = (lax.broadcasted_iota(jnp.int32, (C, 2 * PAIR), 1) % C
              <= lax.broadcasted_iota(jnp.int32, (C, 2 * PAIR), 0))
    eye = jnp.where(si == ti, 1.0, 0.0)
    cat0 = lambda x, y: jnp.concatenate([x, y], axis=0)

    r, kf, v, log_decay, kk, a = _pair_vectors(xs, lo, par_ref, d_a, block_ones)
    cl = _cumsum_rows(log_decay)
    cend = [x[C - 1:C, :] for x in cl]
    e_neg = [jnp.exp(-x) for x in cl]
    e_end = [jnp.exp(ce - x) for ce, x in zip(cend, cl)]
    kka = [x * y for x, y in zip(kk, a)]
    stack = lambda xs_: [_stack_heads(x, lane_lo) for x in xs_]
    a_p = [(-x * jnp.exp(c_ - ld)).astype(BF16) for x, c_, ld in zip(kk, cl, log_decay)]
    r_p = [(x * jnp.exp(c_)).astype(BF16) for x, c_ in zip(r, cl)]
    a_s = stack(a_p)
    b_s = stack([x * e for x, e in zip(kka, e_neg)])
    k_s = stack([x * e for x, e in zip(kf, e_neg)])
    v_s = stack(v)
    bh_s = stack([x * e for x, e in zip(kka, e_end)])
    kh_s = stack([x * e for x, e in zip(kf, e_end)])
    pend_col = [jnp.transpose(jnp.broadcast_to(jnp.exp(ce), (PAIR, PAIR))) for ce in cend]

    m1 = [_dot_nt(cat0(x, y), cat0(z, w)) for x, y, z, w in zip(a_p, r_p, b_s, k_s)]
    a_ab = [jnp.where(strict_lower, m[:C, :PAIR], 0.0) for m in m1]
    a_ak = [jnp.where(strict_lower, m[:C, PAIR:], 0.0).astype(BF16) for m in m1]
    a_rbk = [jnp.where(lower2, m[C:], 0.0).astype(BF16) for m in m1]
    g = [_dot(x, y) for x, y in zip(a_ak, v_s)]

    apow = [_dot(x.astype(BF16), _stack_heads(x, lane_lo)) for x in a_ab]
    tinv = [eye + x for x in a_ab]
    n_levels = CHUNK.bit_length() - 1
    for lvl in range(1, n_levels):
        rhs = stack(apow)
        if lvl < n_levels - 1:
            both = [_dot(cat0(x.astype(BF16), t.astype(BF16)), y) for x, t, y in zip(apow, tinv, rhs)]
            apow = [x[:C] for x in both]
            tinv = [t + x[C:] for t, x in zip(tinv, both)]
        else:
            tinv = [t + _dot(t.astype(BF16), y) for t, y in zip(tinv, rhs)]

    g_s = stack(g)
    wx = [_dot(t.astype(BF16), jnp.concatenate([x, y], axis=1)) for t, x, y in zip(tinv, a_s, g_s)]
    seq_pair = [(bi, pi) for bi in range(nb) for pi in range(n_pairs)]
    st = [st_ref[bi, pi] for bi, pi in seq_pair]
    uy = [_dot(cat0(w[:, :PAIR].astype(BF16), x), s_.astype(BF16)) for w, x, s_ in zip(wx, r_p, st)]
    uv_s = [cat0(_stack_heads(x[:C] + w[:, PAIR:], lane_lo), y) for x, w, y in zip(uy, wx, v_s)]
    st_upd = [_dot_tn(cat0(x, y), z) for x, y, z in zip(bh_s, kh_s, uv_s)]
    y_uv = [_dot(x, z) for x, z in zip(a_rbk, uv_s)]
    for i, (bi, pi) in enumerate(seq_pair):
        st_ref[bi, pi] = pend_col[i] * st[i] + st_upd[i]
    y = [x[C:] + z for x, z in zip(uy, y_uv)]
    o = _head_norm_bonus(y, r, kf, v, par_ref, d_a, block_ones)
    sls = _pair_slices(d_a)
    for oi, (bi, pi) in zip(o, seq_pair):
        oa_ref[bi, :, sls[pi]] = (oi * sga_ref[bi, :, sls[pi]]).astype(oa_ref.dtype)


def _wkv_prompt(p_shift, gates, mu, par, lora_w, batch, seq, d_a):
    n_chunks = seq // CHUNK
    shift_w = p_shift.shape[1]
    n_pairs = d_a // PAIR
    nb = WKV_SEQS_PER_STEP if batch % WKV_SEQS_PER_STEP == 0 else 1
    kern = functools.partial(_wkv_chunk_kernel, d_a=d_a, nb=nb)
    oa, st = pl.pallas_call(
        kern,
        out_shape=(jax.ShapeDtypeStruct((batch, seq, d_a), BF16),
                   jax.ShapeDtypeStruct((batch, n_pairs, PAIR, PAIR), F32)),
        grid=(batch // nb, n_chunks),
        in_specs=[pl.BlockSpec((nb, CHUNK, shift_w), lambda b, c: (b, c, 0)),
                  pl.BlockSpec((nb, CHUNK, d_a), lambda b, c: (b, c, 0)),
                  pl.BlockSpec((1, shift_w), lambda b, c: (0, 0)),
                  pl.BlockSpec((8, d_a), lambda b, c: (0, 0)),
                  pl.BlockSpec((2 * LORA, 2 * d_a), lambda b, c: (0, 0))],
        out_specs=(pl.BlockSpec((nb, CHUNK, d_a), lambda b, c: (b, c, 0)),
                   pl.BlockSpec((nb, n_pairs, PAIR, PAIR), lambda b, c: (b, 0, 0, 0))),
        scratch_shapes=[pltpu.VMEM((nb, 1, shift_w), F32)],
        compiler_params=_params(2),
        name="wkv_chunked",
    )(p_shift.reshape(batch, seq, shift_w), gates.reshape(batch, seq, -1), mu, par, lora_w)
    return oa.reshape(batch * seq, d_a), st


def _wkv_step_kernel(ps_ref, prev_ref, sga_ref, mu_ref, par_ref, lora_ref, s_ref, oa_ref, so_ref,
                     vec_ref, y_ref, *, d_a, bb, group):
    n_heads = d_a // HEAD
    half = HEAD // 2
    xs = _token_shift(ps_ref[...], prev_ref[...], mu_ref[...])
    lo = _lora_out(xs, lora_ref[...], d_a)
    block_ones = _block_ones()
    sls = _pair_slices(d_a)
    r, kf, v, log_decay, kk, a = _pair_vectors([xs], [lo], par_ref, d_a, block_ones)

    lane_lo = lax.broadcasted_iota(jnp.int32, (bb, PAIR), 1) < HEAD
    for pi in range(len(sls)):
        v_hi = v[pi].astype(BF16).astype(F32)
        vecs = (-kk[pi], jnp.exp(log_decay[pi]), kk[pi] * a[pi], kf[pi], r[pi], v_hi, v[pi] - v_hi)
        for i, x in enumerate(vecs):
            swapped = pltpu.roll(x, HEAD, axis=1)
            vec_ref[i, 2 * pi] = jnp.where(lane_lo, x, swapped)
            vec_ref[i, 2 * pi + 1] = jnp.where(lane_lo, swapped, x)

    rows = n_heads * half
    m_in_head = lax.broadcasted_iota(jnp.int32, (rows, PAIR), 0) % half
    lane = lax.broadcasted_iota(jnp.int32, (rows, PAIR), 1)
    diag = (lane == 2 * m_in_head) | (lane == 2 * m_in_head + HEAD + 1)

    def expand(i, b):
        return jnp.concatenate(
            [jnp.broadcast_to(vec_ref[i, h, b:b + 1, :], (half, PAIR)) for h in range(n_heads)], axis=0)

    for g0 in range(0, bb, group):
        bs = range(g0, g0 + group)
        s = [s_ref[b] for b in bs]
        sa = [_dot((x * expand(0, b)).astype(BF16), block_ones) for x, b in zip(s, bs)]
        vb = [_dot(jnp.where(diag, expand(5, b), 0.0).astype(BF16), block_ones)
              + _dot(jnp.where(diag, expand(6, b), 0.0).astype(BF16), block_ones) for b in bs]
        s_new = [x * expand(1, b) + y * expand(2, b) + z * expand(3, b) for x, y, z, b in zip(s, sa, vb, bs)]
        for x, b in zip(s_new, bs):
            so_ref[b] = x
        yb = [_head_sum(x * expand(4, b), block_ones, split=True) for x, b in zip(s_new, bs)]
        for x, b in zip(yb, bs):
            yh = jnp.sum(jnp.where(diag, x, 0.0).reshape(n_heads, half, PAIR), axis=1)
            y_ref[b] = yh + pltpu.roll(yh, HEAD, axis=1)

    y = [jnp.where(lane_lo, y_ref[:, 2 * pi, :], y_ref[:, 2 * pi + 1, :]) for pi in range(len(sls))]
    o = _head_norm_bonus(y, r, kf, v, par_ref, d_a, block_ones)
    for oi, sl in zip(o, sls):
        oa_ref[:, sl] = (oi * sga_ref[:, sl]).astype(oa_ref.dtype)


def _wkv_sample(p_shift, shift_prev, gates, mu, par, lora_w, state, d_a, bb):
    batch, shift_w = p_shift.shape
    n_heads = d_a // HEAD
    rows = n_heads * HEAD // 2
    kern = functools.partial(_wkv_step_kernel, d_a=d_a, bb=bb, group=min(bb, 4))
    oa, new_state = pl.pallas_call(
        kern,
        out_shape=(jax.ShapeDtypeStruct((batch, d_a), BF16),
                   jax.ShapeDtypeStruct((batch, rows, PAIR), F32)),
        grid=(batch // bb,),
        in_specs=[pl.BlockSpec((bb, shift_w), lambda i: (i, 0)),
                  pl.BlockSpec((bb, shift_w), lambda i: (i, 0)),
                  pl.BlockSpec((bb, d_a), lambda i: (i, 0)),
                  pl.BlockSpec((1, shift_w), lambda i: (0, 0)),
                  pl.BlockSpec((8, d_a), lambda i: (0, 0)),
                  pl.BlockSpec((2 * LORA, 2 * d_a), lambda i: (0, 0)),
                  pl.BlockSpec((bb, rows, PAIR), lambda i: (i, 0, 0))],
        out_specs=(pl.BlockSpec((bb, d_a), lambda i: (i, 0)),
                   pl.BlockSpec((bb, rows, PAIR), lambda i: (i, 0, 0))),
        scratch_shapes=[pltpu.VMEM((7, n_heads, bb, PAIR), F32), pltpu.VMEM((bb, n_heads, PAIR), F32)],
        compiler_params=_params(1),
        name="wkv_step",
    )(p_shift, shift_prev, gates, mu, par, lora_w, state.reshape(batch, rows, PAIR))
    return oa, new_state.reshape(state.shape)


CONV_PAD = 32
CONV_STRIP = 16


def _conv_prompt_kernel(u_ref, w_ref, b_ref, o_ref, buf_ref, c_ref, *, tt, taps):
    t = pl.program_id(1)
    sub = buf_ref.shape[1]
    lanes = [slice(s * 128, (s + 1) * 128) for s in range(sub)]

    @pl.when(t == 0)
    def _():
        buf_ref[0:CONV_PAD] = jnp.zeros((CONV_PAD,) + buf_ref.shape[1:], F32)

    u = u_ref[...]
    buf_ref[CONV_PAD:CONV_PAD + tt] = jnp.swapaxes(jnp.stack([u[:, ls] for ls in lanes], axis=0), 0, 1)
    off = CONV_PAD - (taps - 1)
    w = [w_ref[k] for k in range(taps)]
    bias = b_ref[0]

    def strip(s, carry):
        t0 = s * CONV_STRIP
        acc = [bias] * CONV_STRIP
        for i in range(CONV_STRIP + taps - 1):
            x = buf_ref[off + t0 + i]
            for j in range(CONV_STRIP):
                if 0 <= i - j < taps:
                    acc[j] = acc[j] + w[i - j] * x
        for j in range(CONV_STRIP):
            c_ref[t0 + j] = acc[j]
        return carry

    lax.fori_loop(0, tt // CONV_STRIP, strip, 0)
    buf_ref[0:CONV_PAD] = buf_ref[tt:tt + CONV_PAD]
    c = jnp.swapaxes(c_ref[...], 0, 1)
    for s, ls in enumerate(lanes):
        o_ref[:, ls] = c[s]


def _conv_prompt(u, conv_w, conv_b, batch, seq, tt):
    d_b = u.shape[1]
    taps = conv_w.shape[0]
    nt = seq // tt
    sub = d_b // 128
    assert taps - 1 <= CONV_PAD and tt % CONV_STRIP == 0 and tt >= CONV_PAD
    kern = functools.partial(_conv_prompt_kernel, tt=tt, taps=taps)
    return pl.pallas_call(
        kern,
        out_shape=jax.ShapeDtypeStruct((batch * seq, d_b), F32),
        grid=(batch, nt),
        in_specs=[pl.BlockSpec((tt, d_b), lambda b, t: (b * nt + t, 0)),
                  pl.BlockSpec((taps, sub, 128), lambda b, t: (0, 0, 0)),
                  pl.BlockSpec((1, sub, 128), lambda b, t: (0, 0, 0))],
        out_specs=pl.BlockSpec((tt, d_b), lambda b, t: (b * nt + t, 0)),
        scratch_shapes=[pltpu.VMEM((tt + CONV_PAD, sub, 128), F32), pltpu.VMEM((tt, sub, 128), F32)],
        compiler_params=_params(2),
        name="conv_prompt",
    )(u, conv_w.reshape(taps, sub, 128), conv_b.reshape(1, sub, 128))


def _conv_step_kernel(u_ref, prev_ref, w_ref, b_ref, o_ref, *, taps):
    c = b_ref[...] + w_ref[taps - 1:taps, :] * u_ref[...]
    for k in range(taps - 1):
        c = c + w_ref[k:k + 1, :] * prev_ref[:, k, :]
    o_ref[...] = c


def _conv_sample(u, conv_prev, conv_w, conv_b, bb):
    batch, d_b = u.shape
    taps = conv_w.shape[0]
    kern = functools.partial(_conv_step_kernel, taps=taps)
    return pl.pallas_call(
        kern,
        out_shape=jax.ShapeDtypeStruct((batch, d_b), F32),
        grid=(batch // bb,),
        in_specs=[pl.BlockSpec((bb, d_b), lambda i: (i, 0)),
                  pl.BlockSpec((bb, taps - 1, d_b), lambda i: (i, 0, 0)),
                  pl.BlockSpec((taps, d_b), lambda i: (0, 0)),
                  pl.BlockSpec((1, d_b), lambda i: (0, 0))],
        out_specs=pl.BlockSpec((bb, d_b), lambda i: (i, 0)),
        compiler_params=_params(1),
        name="conv_step",
    )(u, conv_prev, conv_w, conv_b.reshape(1, d_b))


def _tail_kernel(oa_ref, c_ref, gb_ref, sga_ref, sgb_ref, x_ref, p_ref, cln_ref, wa_ref, wb_ref, wout_ref,
                 wpg_ref, wple_ref, fg_ref, o_ref):
    c = c_ref[...]
    mean = jnp.mean(c, axis=-1, keepdims=True)
    dc = c - mean
    var = jnp.mean(dc * dc, axis=-1, keepdims=True)
    cf = dc * lax.rsqrt(var + LN_EPS) * cln_ref[0:1, :] + cln_ref[1:2, :]
    cb = (cf * _sigmoid(cf) * gb_ref[...]).astype(BF16)
    m = sga_ref[...] * _dot(oa_ref[...], wa_ref[...]) + sgb_ref[...] * _dot(cb, wb_ref[...])
    h = x_ref[...] + _dot(m.astype(BF16), wout_ref[...])
    gate = _sigmoid(_dot(h.astype(BF16), wpg_ref[...]))
    h = h + gate * _dot(p_ref[...].astype(BF16), wple_ref[...])
    ms = jnp.mean(h * h, axis=-1, keepdims=True)
    o_ref[...] = h * lax.rsqrt(ms + RMS_EPS) * fg_ref[...]


def _tail(oa, c, gates, x, p, cln, wa, wb, wout, wpg, wple, fg, tm, gate_b_block, merge_block):
    m, d = x.shape
    d_a = oa.shape[1]
    d_b = c.shape[1]
    ple = p.shape[1]
    row = lambda i: (i, 0)
    const = lambda i: (0, 0)
    resident = lambda shape: pl.BlockSpec(shape, const, pipeline_mode=pl.Buffered(1))
    return pl.pallas_call(
        _tail_kernel,
        out_shape=jax.ShapeDtypeStruct((m, d), F32),
        grid=(m // tm,),
        in_specs=[pl.BlockSpec((tm, d_a), row), pl.BlockSpec((tm, d_b), row),
                  pl.BlockSpec((tm, d_b), lambda i: (i, gate_b_block)),
                  pl.BlockSpec((tm, d), lambda i: (i, merge_block)),
                  pl.BlockSpec((tm, d), lambda i: (i, merge_block + 1)),
                  pl.BlockSpec((tm, d), row), pl.BlockSpec((tm, ple), row),
                  resident((2, d_b)),
                  resident((d_a, d)), resident((d_b, d)), resident((d, d)), resident((d, d)),
                  resident((ple, d)), resident((1, d))],
        out_specs=pl.BlockSpec((tm, d), row),
        compiler_params=_params(1),
        name="tail",
    )(oa, c, gates, gates, gates, x, p, cln, wa, wb, wout, wpg, wple, fg)


def _largest_tile(n, cap, align):
    t = min(n, cap)
    while n % t or t % align:
        t -= 1
    return t


def kernel(x_prompt, x_sample, state_shift, state_wkv, state_conv, p_prompt, p_sample, norm_g, w_in,
           shift_mu, w0, w_lora_b, a0, a_lora_b, k_k, k_a, r_k, lnx_g, lnx_b, w_proj_a, conv_w, conv_b,
           cln_g, cln_b, w_proj_b, w_out, w_ple, w_ple_gate, final_g):
    depth = w_in.shape[0]
    batch, seq, d = x_prompt.shape
    dec_batch, dec_seq, _ = x_sample.shape
    d_a = w_proj_a.shape[1]
    d_b = w_proj_b.shape[1]
    shift_w = shift_mu.shape[1]
    n_heads = d_a // HEAD
    n_pairs = d_a // PAIR
    taps = conv_w.shape[1]
    assert depth == 1 and dec_seq == 1 and d_a == d_b and 2 * d_a == d
    assert shift_w == 3 * d_a + 2 * LORA and d_a % PAIR == 0 and seq % CHUNK == 0

    o1 = shift_w
    o2 = o1 + d_a
    o3 = o2 + 2 * d_b
    o4 = o3 + d_b
    w = w_in[0]
    wa = w_proj_a[0].astype(BF16)
    wb = w_proj_b[0].astype(BF16)
    wout = w_out[0].astype(BF16)
    wpg = w_ple_gate[0].astype(BF16)
    wple = w_ple[0].astype(BF16)
    lora_w = jnp.zeros((2 * LORA, 2 * d_a), F32)
    lora_w = lora_w.at[:LORA, :d_a].set(w_lora_b[0]).at[LORA:, d_a:].set(a_lora_b[0]).astype(BF16)
    zeros_a = jnp.zeros((d_a,), F32)
    par = jnp.stack([w0[0], a0[0], k_k[0], k_a[0], r_k[0].reshape(d_a), lnx_g[0], lnx_b[0], zeros_a])
    cln = jnp.stack([cln_g[0], cln_b[0]])
    mu = shift_mu
    g_in = norm_g[0].reshape(1, d)
    fg = final_g.reshape(1, d)
    gate_b_block = d_a // d_b
    merge_block = (d_a + d_b) // d

    def token_stage(x2, tm):
        xn = _rmsnorm_bf16(x2, g_in, _largest_tile(x2.shape[0], 512, 8))
        tn_shift = _largest_tile(o1, 1024, 128)
        p_shift = _project(xn, w, lambda j: j * tn_shift, o1, tm, tn_shift)
        gate_cols = lambda j: jnp.where(j == 0, o1, o3 + (j - 1) * d_a)
        gates = _project(xn, w, gate_cols, d_a + d_b + 2 * d, tm, d_a, n_silu_blocks=2)
        u = _project_glu(xn, w, o2, o2 + d_b, d_b, tm, _largest_tile(d_b, 512, 128))
        return p_shift, gates, u

    m_p = batch * seq
    x2 = x_prompt.reshape(m_p, d)
    p_shift, gates, u = token_stage(x2, _largest_tile(m_p, 1024, 8))
    oa, st = _wkv_prompt(p_shift, gates, mu, par, lora_w, batch, seq, d_a)
    c = _conv_prompt(u, conv_w[0], conv_b[0], batch, seq, _largest_tile(seq, 256, CONV_STRIP))
    y_prompt = _tail(oa, c, gates, x2, p_prompt[0].reshape(m_p, -1), cln, wa, wb, wout, wpg, wple, fg,
                     _largest_tile(m_p, 256, 8), gate_b_block, merge_block).reshape(batch, seq, d)
    new_shift_p = p_shift.reshape(batch, seq, shift_w)[:, -1][None]
    st = st.reshape(batch, n_pairs, 2, HEAD, 2, HEAD)
    new_wkv_p = jnp.stack([st[:, :, 0, :, 0, :], st[:, :, 1, :, 1, :]], axis=2)
    new_wkv_p = jnp.swapaxes(new_wkv_p.reshape(batch, n_heads, HEAD, HEAD), -1, -2)[None]
    new_conv_p = u.reshape(batch, seq, d_b)[:, seq - (taps - 1):][None]

    xs2 = x_sample.reshape(dec_batch, d)
    p_shift_s, gates_s, u_s = token_stage(xs2, _largest_tile(dec_batch, 1024, 8))
    bb = _largest_tile(dec_batch, 8, 8)
    oa_s, new_wkv_s = _wkv_sample(p_shift_s, state_shift[0], gates_s, mu, par, lora_w, state_wkv[0], d_a, bb)
    c_s = _conv_sample(u_s, state_conv[0], conv_w[0], conv_b[0], bb)
    y_sample = _tail(oa_s, c_s, gates_s, xs2, p_sample[0].reshape(dec_batch, -1), cln, wa, wb, wout, wpg,
                     wple, fg, _largest_tile(dec_batch, 256, 8), gate_b_block,
                     merge_block).reshape(dec_batch, 1, d)
    new_shift_s = p_shift_s[None]
    new_conv_s = jnp.concatenate([state_conv[0][:, 1:], u_s[:, None, :]], axis=1)[None]

    return (y_prompt, y_sample, new_shift_p, new_wkv_p, new_conv_p, new_shift_s, new_wkv_s[None],
            new_conv_s)
```

```python
import functools
import math

import jax
import jax.numpy as jnp
from jax import lax
from jax.experimental import pallas as pl
from jax.experimental.pallas import tpu as pltpu

F32 = jnp.float32
BF16 = jnp.bfloat16

HEAD = 64
PAIR = 2 * HEAD
LORA = 64
CHUNK = 64
WKV_SEQS_PER_STEP = 4
RMS_EPS = 1e-6
LN_EPS = 1e-5
GN_EPS = 64e-5
LOG_DECAY_SCALE = math.exp(-0.5)
VMEM_LIMIT = 56 * 1024 * 1024


def _params(n_axes, vmem=VMEM_LIMIT):
    return pltpu.CompilerParams(dimension_semantics=("arbitrary",) * n_axes, vmem_limit_bytes=vmem)


def _sigmoid(x):
    return 1.0 / (1.0 + jnp.exp(-x))


def _dot(a, b):
    return jnp.dot(a, b, preferred_element_type=F32)


def _dot_nt(a, b):
    return lax.dot_general(a, b, (((1,), (1,)), ((), ())), preferred_element_type=F32)


def _dot_tn(a, b):
    return lax.dot_general(a, b, (((0,), (0,)), ((), ())), preferred_element_type=F32)


def _rmsnorm_kernel(xp_ref, xs_ref, g_ref, o_ref, *, n_prompt_blocks):
    def norm(x):
        ms = jnp.mean(x * x, axis=-1, keepdims=True)
        return (x * lax.rsqrt(ms + RMS_EPS) * g_ref[...]).astype(o_ref.dtype)

    i = pl.program_id(0)

    @pl.when(i < n_prompt_blocks)
    def _():
        o_ref[...] = norm(xp_ref[...])

    @pl.when(i == n_prompt_blocks)
    def _():
        o_ref[0:xs_ref.shape[0], :] = norm(xs_ref[...])


def _rmsnorm_bf16(x_prompt, x_sample, g, tm):
    m_p, d = x_prompt.shape
    m_s = x_sample.shape[0]
    n_blocks = m_p // tm
    assert m_p % tm == 0 and m_s <= tm
    kern = functools.partial(_rmsnorm_kernel, n_prompt_blocks=n_blocks)
    return pl.pallas_call(
        kern,
        out_shape=jax.ShapeDtypeStruct((m_p + m_s, d), BF16),
        grid=(n_blocks + 1,),
        in_specs=[pl.BlockSpec((tm, d), lambda i: (jnp.minimum(i, n_blocks - 1), 0)),
                  pl.BlockSpec((m_s, d), lambda i: (0, 0)),
                  pl.BlockSpec((1, d), lambda i: (0, 0))],
        out_specs=pl.BlockSpec((tm, d), lambda i: (i, 0)),
        compiler_params=_params(1),
        name="rmsnorm",
    )(x_prompt, x_sample, g)


def _proj_kernel(x_ref, w_ref, o_ref, wb_ref, *, n_silu_blocks):
    @pl.when(pl.program_id(1) == 0)
    def _():
        wb_ref[...] = w_ref[...].astype(BF16)

    y = _dot(x_ref[...], wb_ref[...])
    if n_silu_blocks is None:
        o_ref[...] = y
    else:
        s = _sigmoid(y)
        o_ref[...] = jnp.where(pl.program_id(0) < n_silu_blocks, y * s, s)


def _project(xn, w, col_start, n_out, tm, tn, n_silu_blocks=None):
    m, d = xn.shape
    kern = functools.partial(_proj_kernel, n_silu_blocks=n_silu_blocks)
    return pl.pallas_call(
        kern,
        out_shape=jax.ShapeDtypeStruct((m, n_out), F32),
        grid=(n_out // tn, m // tm),
        in_specs=[pl.BlockSpec((tm, d), lambda j, i: (i, 0)),
                  pl.BlockSpec((pl.Element(d), pl.Element(tn)),
                               lambda j, i: (0, pl.multiple_of(col_start(j), 128)))],
        out_specs=pl.BlockSpec((tm, tn), lambda j, i: (i, j)),
        scratch_shapes=[pltpu.VMEM((d, tn), BF16)],
        compiler_params=_params(2),
        name="in_proj",
    )(xn, w)


def _glu_kernel(x_ref, wa_ref, wb_ref, o_ref, wab_ref, wbb_ref):
    @pl.when(pl.program_id(1) == 0)
    def _():
        wab_ref[...] = wa_ref[...].astype(BF16)
        wbb_ref[...] = wb_ref[...].astype(BF16)

    x = x_ref[...]
    o_ref[...] = _dot(x, wab_ref[...]) * _sigmoid(_dot(x, wbb_ref[...]))


def _project_glu(xn, w, col_a, col_b, n_out, tm, tn):
    m, d = xn.shape
    return pl.pallas_call(
        _glu_kernel,
        out_shape=jax.ShapeDtypeStruct((m, n_out), F32),
        grid=(n_out // tn, m // tm),
        in_specs=[pl.BlockSpec((tm, d), lambda j, i: (i, 0)),
                  pl.BlockSpec((pl.Element(d), pl.Element(tn)),
                               lambda j, i: (0, pl.multiple_of(col_a + j * tn, 128))),
                  pl.BlockSpec((pl.Element(d), pl.Element(tn)),
                               lambda j, i: (0, pl.multiple_of(col_b + j * tn, 128)))],
        out_specs=pl.BlockSpec((tm, tn), lambda j, i: (i, j)),
        scratch_shapes=[pltpu.VMEM((d, tn), BF16), pltpu.VMEM((d, tn), BF16)],
        compiler_params=_params(2),
        name="in_proj_glu",
    )(xn, w, w)


def _head_sum(x, block_ones, split=False):
    hi = x.astype(BF16)
    if not split:
        return _dot(hi, block_ones)
    lo = (x - hi.astype(F32)).astype(BF16)
    return _dot(hi, block_ones) + _dot(lo, block_ones)


def _head_sums(xs, block_ones, split=False):
    rows = xs[0].shape[0]
    s = _head_sum(jnp.concatenate(xs, axis=0), block_ones, split)
    return [s[i * rows:(i + 1) * rows] for i in range(len(xs))]


def _block_ones():
    ri = lax.broadcasted_iota(jnp.int32, (PAIR, PAIR), 0)
    ci = lax.broadcasted_iota(jnp.int32, (PAIR, PAIR), 1)
    return jnp.where((ri < HEAD) == (ci < HEAD), 1.0, 0.0).astype(BF16)


def _token_shift(p, prev, mu):
    return p + mu * (prev - p)


def _lora_out(xs, lora_w, d_a):
    z = xs[:, 3 * d_a:3 * d_a + 2 * LORA]
    lane = lax.broadcasted_iota(jnp.int32, z.shape, 1)
    z = jnp.where(lane < LORA, jnp.tanh(z), z)
    return _dot(z.astype(BF16), lora_w)


def _pair_slices(d_a):
    return [slice(i * PAIR, (i + 1) * PAIR) for i in range(d_a // PAIR)]


def _pair_vectors(xs_list, lo_list, par_ref, d_a, block_ones):
    sls = _pair_slices(d_a)
    shifted = lambda sl, off: slice(off + sl.start, off + sl.stop)
    items = [(xs, lo, sl) for xs, lo in zip(xs_list, lo_list) for sl in sls]
    r = [xs[:, sl] for xs, _, sl in items]
    k = [xs[:, shifted(sl, d_a)] for xs, _, sl in items]
    v = [xs[:, shifted(sl, 2 * d_a)] for xs, _, sl in items]
    kkr = [ki * par_ref[2:3, sl] for ki, (_, _, sl) in zip(k, items)]
    ss = _head_sums([x * x for x in kkr], block_ones)
    kk = [x * jnp.minimum(lax.rsqrt(s), 1e12) for x, s in zip(kkr, ss)]
    log_decay = [-LOG_DECAY_SCALE * _sigmoid(par_ref[0:1, sl] + lo[:, sl]) for _, lo, sl in items]
    a = [_sigmoid(par_ref[1:2, sl] + lo[:, shifted(sl, d_a)]) for _, lo, sl in items]
    kf = [ki * (1.0 + (ai - 1.0) * par_ref[3:4, sl]) for ki, ai, (_, _, sl) in zip(k, a, items)]
    return r, kf, v, log_decay, kk, a


def _head_norm_bonus(y, r, kf, v, par_ref, d_a, block_ones):
    sls = _pair_slices(d_a) * (len(y) // (d_a // PAIR))
    mean = [s * (1.0 / HEAD) for s in _head_sums(y, block_ones, split=True)]
    d = [x - m for x, m in zip(y, mean)]
    var = [s * (1.0 / HEAD) for s in _head_sums([x * x for x in d], block_ones)]
    rk = _head_sums([ri * ki * par_ref[4:5, sl] for ri, ki, sl in zip(r, kf, sls)], block_ones)
    return [x * lax.rsqrt(vr + GN_EPS) * par_ref[5:6, sl] + par_ref[6:7, sl] + s * vi
            for x, vr, s, vi, sl in zip(d, var, rk, v, sls)]


def _stack_heads(x, lane_lo):
    xb = x.astype(BF16)
    zero = jnp.zeros_like(xb)
    return jnp.concatenate([jnp.where(lane_lo, xb, zero), jnp.where(lane_lo, zero, xb)], axis=0)


def _cumsum_rows(xs):
    n = xs[0].shape[0]
    tri = (lax.broadcasted_iota(jnp.int32, (n, n), 1) <= lax.broadcasted_iota(jnp.int32, (n, n), 0))
    tri = jnp.where(tri, 1.0, 0.0).astype(BF16)
    x = jnp.concatenate(xs, axis=1)
    hi = x.astype(BF16)
    lo = (x - hi.astype(F32)).astype(BF16)
    s = _dot(tri, hi) + _dot(tri, lo)
    return [s[:, i * PAIR:(i + 1) * PAIR] for i in range(len(xs))]


def _wkv_chunk_kernel(*refs, d_a, nb):
    ps_refs, sga_refs = refs[:nb], refs[nb:2 * nb]
    mu_ref, par_ref, lora_ref, oa_ref, st_ref, carry_ref = refs[2 * nb:]
    c = pl.program_id(1)
    n_pairs = d_a // PAIR
    C = CHUNK

    @pl.when(c == 0)
    def _():
        carry_ref[...] = jnp.zeros_like(carry_ref)
        st_ref[...] = jnp.zeros_like(st_ref)

    xs, lo = [], []
    row = lax.broadcasted_iota(jnp.int32, ps_refs[0].shape, 0)
    for bi in range(nb):
        p = ps_refs[bi][...]
        prev = jnp.where(row == 0, carry_ref[bi], pltpu.roll(p, 1, axis=0))
        carry_ref[bi] = p[C - 1:C, :]
        xs.append(_token_shift(p, prev, mu_ref[...]))
        lo.append(_lora_out(xs[-1], lora_ref[...], d_a))

    block_ones = _block_ones()
    lane_lo = lax.broadcasted_iota(jnp.int32, (C, PAIR), 1) < HEAD
    ti = lax.broadcasted_iota(jnp.int32, (C, PAIR), 0)
    si = lax.broadcasted_iota(jnp.int32, (C, PAIR), 1) % C
    strict_lower = si < ti
    lower2 = (lax.broadcasted_iota(jnp.int32, (C, 2 * PAIR), 1) % C
              <= lax.broadcasted_iota(jnp.int32, (C, 2 * PAIR), 0))
    eye = jnp.where(si == ti, 1.0, 0.0)
    cat0 = lambda x, y: jnp.concatenate([x, y], axis=0)

    r, kf, v, log_decay, kk, a = _pair_vectors(xs, lo, par_ref, d_a, block_ones)
    cl = _cumsum_rows(log_decay)
    cend = [x[C - 1:C, :] for x in cl]
    e_neg = [jnp.exp(-x) for x in cl]
    e_end = [jnp.exp(ce - x) for ce, x in zip(cend, cl)]
    kka = [x * y for x, y in zip(kk, a)]
    stack = lambda xs_: [_stack_heads(x, lane_lo) for x in xs_]
    a_p = [(-x * jnp.exp(c_ - ld)).astype(BF16) for x, c_, ld in zip(kk, cl, log_decay)]
    r_p = [(x * jnp.exp(c_)).astype(BF16) for x, c_ in zip(r, cl)]
    a_s = stack(a_p)
    b_s = stack([x * e for x, e in zip(kka, e_neg)])
    k_s = stack([x * e for x, e in zip(kf, e_neg)])
    v_s = stack(v)
    bh_s = stack([x * e for x, e in zip(kka, e_end)])
    kh_s = stack([x * e for x, e in zip(kf, e_end)])
    pend_col = [jnp.transpose(jnp.broadcast_to(jnp.exp(ce), (PAIR, PAIR))) for ce in cend]

    m1 = [_dot_nt(cat0(x, y), cat0(z, w)) for x, y, z, w in zip(a_p, r_p, b_s, k_s)]
    a_ab = [jnp.where(strict_lower, m[:C, :PAIR], 0.0) for m in m1]
    a_ak = [jnp.where(strict_lower, m[:C, PAIR:], 0.0).astype(BF16) for m in m1]
    a_rbk = [jnp.where(lower2, m[C:], 0.0).astype(BF16) for m in m1]
    g = [_dot(x, y) for x, y in zip(a_ak, v_s)]

    apow = [_dot(x.astype(BF16), _stack_heads(x, lane_lo)) for x in a_ab]
    tinv = [eye + x for x in a_ab]
    n_levels = CHUNK.bit_length() - 1
    for lvl in range(1, n_levels):
        rhs = stack(apow)
        if lvl < n_levels - 1:
            both = [_dot(cat0(x.astype(BF16), t.astype(BF16)), y) for x, t, y in zip(apow, tinv, rhs)]
            apow = [x[:C] for x in both]
            tinv = [t + x[C:] for t, x in zip(tinv, both)]
        else:
            tinv = [t + _dot(t.astype(BF16), y) for t, y in zip(tinv, rhs)]

    g_s = stack(g)
    wx = [_dot(t.astype(BF16), jnp.concatenate([x, y], axis=1)) for t, x, y in zip(tinv, a_s, g_s)]
    seq_pair = [(bi, pi) for bi in range(nb) for pi in range(n_pairs)]
    st = [st_ref[bi, pi] for bi, pi in seq_pair]
    uy = [_dot(cat0(w[:, :PAIR].astype(BF16), x), s_.astype(BF16)) for w, x, s_ in zip(wx, r_p, st)]
    uv_s = [cat0(_stack_heads(x[:C] + w[:, PAIR:], lane_lo), y) for x, w, y in zip(uy, wx, v_s)]
    st_upd = [_dot_tn(cat0(x, y), z) for x, y, z in zip(bh_s, kh_s, uv_s)]
    y_uv = [_dot(x, z) for x, z in zip(a_rbk, uv_s)]
    for i, (bi, pi) in enumerate(seq_pair):
        st_ref[bi, pi] = pend_col[i] * st[i] + st_upd[i]
    y = [x[C:] + z for x, z in zip(uy, y_uv)]
    o = _head_norm_bonus(y, r, kf, v, par_ref, d_a, block_ones)
    sls = _pair_slices(d_a)
    for oi, (bi, pi) in zip(o, seq_pair):
        oa_ref[bi, :, sls[pi]] = (oi * sga_refs[bi][:, sls[pi]]).astype(oa_ref.dtype)


def _wkv_prompt(p_shift, gates, mu, par, lora_w, batch, seq, d_a):
    n_chunks = seq // CHUNK
    shift_w = p_shift.shape[1]
    n_pairs = d_a // PAIR
    nb = WKV_SEQS_PER_STEP if batch % WKV_SEQS_PER_STEP == 0 else 1
    kern = functools.partial(_wkv_chunk_kernel, d_a=d_a, nb=nb)
    seq_rows = lambda bi: (lambda b, c: ((b * nb + bi) * n_chunks + c, 0))
    oa, st = pl.pallas_call(
        kern,
        out_shape=(jax.ShapeDtypeStruct((batch, seq, d_a), BF16),
                   jax.ShapeDtypeStruct((batch, n_pairs, PAIR, PAIR), F32)),
        grid=(batch // nb, n_chunks),
        in_specs=([pl.BlockSpec((CHUNK, shift_w), seq_rows(bi)) for bi in range(nb)]
                  + [pl.BlockSpec((CHUNK, d_a), seq_rows(bi)) for bi in range(nb)]
                  + [pl.BlockSpec((1, shift_w), lambda b, c: (0, 0)),
                     pl.BlockSpec((8, d_a), lambda b, c: (0, 0)),
                     pl.BlockSpec((2 * LORA, 2 * d_a), lambda b, c: (0, 0))]),
        out_specs=(pl.BlockSpec((nb, CHUNK, d_a), lambda b, c: (b, c, 0)),
                   pl.BlockSpec((nb, n_pairs, PAIR, PAIR), lambda b, c: (b, 0, 0, 0))),
        scratch_shapes=[pltpu.VMEM((nb, 1, shift_w), F32)],
        compiler_params=_params(2),
        name="wkv_chunked",
    )(*([p_shift] * nb + [gates] * nb), mu, par, lora_w)
    return oa.reshape(batch * seq, d_a), st


def _wkv_step_kernel(ps_ref, prev_ref, sga_ref, mu_ref, par_ref, lora_ref, s_ref, oa_ref, so_ref,
                     vec_ref, y_ref, *, d_a, bb, group):
    n_heads = d_a // HEAD
    half = HEAD // 2
    xs = _token_shift(ps_ref[...], prev_ref[...], mu_ref[...])
    lo = _lora_out(xs, lora_ref[...], d_a)
    block_ones = _block_ones()
    sls = _pair_slices(d_a)
    r, kf, v, log_decay, kk, a = _pair_vectors([xs], [lo], par_ref, d_a, block_ones)

    lane_lo = lax.broadcasted_iota(jnp.int32, (bb, PAIR), 1) < HEAD
    for pi in range(len(sls)):
        v_hi = v[pi].astype(BF16).astype(F32)
        vecs = (-kk[pi], jnp.exp(log_decay[pi]), kk[pi] * a[pi], kf[pi], r[pi], v_hi, v[pi] - v_hi)
        for i, x in enumerate(vecs):
            swapped = pltpu.roll(x, HEAD, axis=1)
            vec_ref[i, 2 * pi] = jnp.where(lane_lo, x, swapped)
            vec_ref[i, 2 * pi + 1] = jnp.where(lane_lo, swapped, x)

    rows = n_heads * half
    m_in_head = lax.broadcasted_iota(jnp.int32, (rows, PAIR), 0) % half
    lane = lax.broadcasted_iota(jnp.int32, (rows, PAIR), 1)
    diag = (lane == 2 * m_in_head) | (lane == 2 * m_in_head + HEAD + 1)

    def expand(i, b):
        return jnp.concatenate(
            [jnp.broadcast_to(vec_ref[i, h, b:b + 1, :], (half, PAIR)) for h in range(n_heads)], axis=0)

    def load_state(b):
        even = s_ref[b, :, pl.ds(0, half, stride=2), :]
        odd = s_ref[b, :, pl.ds(1, half, stride=2), :]
        return jnp.concatenate([even, odd], axis=-1).reshape(rows, PAIR)

    def store_state(b, x):
        x = x.reshape(n_heads, half, PAIR)
        so_ref[b, :, pl.ds(0, half, stride=2), :] = x[:, :, :HEAD]
        so_ref[b, :, pl.ds(1, half, stride=2), :] = x[:, :, HEAD:]

    for g0 in range(0, bb, group):
        bs = range(g0, g0 + group)
        s = [load_state(b) for b in bs]
        sa = [_dot((x * expand(0, b)).astype(BF16), block_ones) for x, b in zip(s, bs)]
        vb = [_dot(jnp.where(diag, expand(5, b), 0.0).astype(BF16), block_ones)
              + _dot(jnp.where(diag, expand(6, b), 0.0).astype(BF16), block_ones) for b in bs]
        s_new = [x * expand(1, b) + y * expand(2, b) + z * expand(3, b) for x, y, z, b in zip(s, sa, vb, bs)]
        for x, b in zip(s_new, bs):
            store_state(b, x)
        yb = [_head_sum(x * expand(4, b), block_ones, split=True) for x, b in zip(s_new, bs)]
        for x, b in zip(yb, bs):
            yh = jnp.sum(jnp.where(diag, x, 0.0).reshape(n_heads, half, PAIR), axis=1)
            y_ref[b] = yh + pltpu.roll(yh, HEAD, axis=1)

    y = [jnp.where(lane_lo, y_ref[:, 2 * pi, :], y_ref[:, 2 * pi + 1, :]) for pi in range(len(sls))]
    o = _head_norm_bonus(y, r, kf, v, par_ref, d_a, block_ones)
    for oi, sl in zip(o, sls):
        oa_ref[:, sl] = (oi * sga_ref[:, sl]).astype(oa_ref.dtype)


def _wkv_sample(p_shift, shift_prev, gates, mu, par, lora_w, state, d_a, bb, row0):
    batch, shift_w = shift_prev.shape
    n_heads = d_a // HEAD
    blk0 = row0 // bb
    kern = functools.partial(_wkv_step_kernel, d_a=d_a, bb=bb, group=min(bb, 4))
    return pl.pallas_call(
        kern,
        out_shape=(jax.ShapeDtypeStruct((batch, d_a), BF16),
                   jax.ShapeDtypeStruct(state.shape, F32)),
        grid=(batch // bb,),
        in_specs=[pl.BlockSpec((bb, shift_w), lambda i: (blk0 + i, 0)),
                  pl.BlockSpec((bb, shift_w), lambda i: (i, 0)),
                  pl.BlockSpec((bb, d_a), lambda i: (blk0 + i, 0)),
                  pl.BlockSpec((1, shift_w), lambda i: (0, 0)),
                  pl.BlockSpec((8, d_a), lambda i: (0, 0)),
                  pl.BlockSpec((2 * LORA, 2 * d_a), lambda i: (0, 0)),
                  pl.BlockSpec((bb, n_heads, HEAD, HEAD), lambda i: (i, 0, 0, 0))],
        out_specs=(pl.BlockSpec((bb, d_a), lambda i: (i, 0)),
                   pl.BlockSpec((bb, n_heads, HEAD, HEAD), lambda i: (i, 0, 0, 0))),
        scratch_shapes=[pltpu.VMEM((7, n_heads, bb, PAIR), F32), pltpu.VMEM((bb, n_heads, PAIR), F32)],
        compiler_params=_params(1),
        name="wkv_step",
    )(p_shift, shift_prev, gates, mu, par, lora_w, state)


CONV_PAD = 32
CONV_STRIP = 16


def _conv_prompt_kernel(u_ref, w_ref, b_ref, o_ref, buf_ref, c_ref, *, tt, taps):
    t = pl.program_id(1)
    sub = buf_ref.shape[1]
    lanes = [slice(s * 128, (s + 1) * 128) for s in range(sub)]

    @pl.when(t == 0)
    def _():
        buf_ref[0:CONV_PAD] = jnp.zeros((CONV_PAD,) + buf_ref.shape[1:], F32)

    u = u_ref[...]
    buf_ref[CONV_PAD:CONV_PAD + tt] = jnp.swapaxes(jnp.stack([u[:, ls] for ls in lanes], axis=0), 0, 1)
    off = CONV_PAD - (taps - 1)
    w = [w_ref[k] for k in range(taps)]
    bias = b_ref[0]

    def strip(s, carry):
        t0 = s * CONV_STRIP
        acc = [bias] * CONV_STRIP
        for i in range(CONV_STRIP + taps - 1):
            x = buf_ref[off + t0 + i]
            for j in range(CONV_STRIP):
                if 0 <= i - j < taps:
                    acc[j] = acc[j] + w[i - j] * x
        for j in range(CONV_STRIP):
            c_ref[t0 + j] = acc[j]
        return carry

    lax.fori_loop(0, tt // CONV_STRIP, strip, 0)
    buf_ref[0:CONV_PAD] = buf_ref[tt:tt + CONV_PAD]
    c = jnp.swapaxes(c_ref[...], 0, 1)
    for s, ls in enumerate(lanes):
        o_ref[:, ls] = c[s]


def _conv_prompt(u, conv_w, conv_b, batch, seq, tt):
    d_b = u.shape[1]
    taps = conv_w.shape[0]
    nt = seq // tt
    sub = d_b // 128
    assert taps - 1 <= CONV_PAD and tt % CONV_STRIP == 0 and tt >= CONV_PAD
    kern = functools.partial(_conv_prompt_kernel, tt=tt, taps=taps)
    return pl.pallas_call(
        kern,
        out_shape=jax.ShapeDtypeStruct((batch * seq, d_b), F32),
        grid=(batch, nt),
        in_specs=[pl.BlockSpec((tt, d_b), lambda b, t: (b * nt + t, 0)),
                  pl.BlockSpec((taps, sub, 128), lambda b, t: (0, 0, 0)),
                  pl.BlockSpec((1, sub, 128), lambda b, t: (0, 0, 0))],
        out_specs=pl.BlockSpec((tt, d_b), lambda b, t: (b * nt + t, 0)),
        scratch_shapes=[pltpu.VMEM((tt + CONV_PAD, sub, 128), F32), pltpu.VMEM((tt, sub, 128), F32)],
        compiler_params=_params(2),
        name="conv_prompt",
    )(u, conv_w.reshape(taps, sub, 128), conv_b.reshape(1, sub, 128))


def _conv_step_kernel(u_ref, prev_ref, w_ref, b_ref, o_ref, *, taps):
    c = b_ref[...] + w_ref[taps - 1:taps, :] * u_ref[...]
    for k in range(taps - 1):
        c = c + w_ref[k:k + 1, :] * prev_ref[:, k, :]
    o_ref[...] = c


def _conv_sample(u, conv_prev, conv_w, conv_b, bb, row0):
    batch = conv_prev.shape[0]
    d_b = u.shape[1]
    taps = conv_w.shape[0]
    blk0 = row0 // bb
    kern = functools.partial(_conv_step_kernel, taps=taps)
    return pl.pallas_call(
        kern,
        out_shape=jax.ShapeDtypeStruct((batch, d_b), F32),
        grid=(batch // bb,),
        in_specs=[pl.BlockSpec((bb, d_b), lambda i: (blk0 + i, 0)),
                  pl.BlockSpec((bb, taps - 1, d_b), lambda i: (i, 0, 0)),
                  pl.BlockSpec((taps, d_b), lambda i: (0, 0)),
                  pl.BlockSpec((1, d_b), lambda i: (0, 0))],
        out_specs=pl.BlockSpec((bb, d_b), lambda i: (i, 0)),
        compiler_params=_params(1),
        name="conv_step",
    )(u, conv_prev, conv_w, conv_b.reshape(1, d_b))


def _tail_kernel(oa_ref, c_ref, gb_ref, sga_ref, sgb_ref, x_ref, p_ref, cln_ref, wa_ref, wb_ref, wout_ref,
                 wpg_ref, wple_ref, fg_ref, o_ref):
    c = c_ref[...]
    mean = jnp.mean(c, axis=-1, keepdims=True)
    dc = c - mean
    var = jnp.mean(dc * dc, axis=-1, keepdims=True)
    cf = dc * lax.rsqrt(var + LN_EPS) * cln_ref[0:1, :] + cln_ref[1:2, :]
    cb = (cf * _sigmoid(cf) * gb_ref[...]).astype(BF16)
    m = sga_ref[...] * _dot(oa_ref[...], wa_ref[...]) + sgb_ref[...] * _dot(cb, wb_ref[...])
    h = x_ref[...] + _dot(m.astype(BF16), wout_ref[...])
    gate = _sigmoid(_dot(h.astype(BF16), wpg_ref[...]))
    h = h + gate * _dot(p_ref[...].astype(BF16), wple_ref[...])
    ms = jnp.mean(h * h, axis=-1, keepdims=True)
    o_ref[...] = h * lax.rsqrt(ms + RMS_EPS) * fg_ref[...]


def _tail(oa, c, gates, x, p, cln, wa, wb, wout, wpg, wple, fg, tm, gate_b_block, merge_block, row0=0):
    m, d = x.shape
    d_a = oa.shape[1]
    d_b = c.shape[1]
    ple = p.shape[1]
    blk0 = row0 // tm
    row = lambda i: (i, 0)
    const = lambda i: (0, 0)
    resident = lambda shape: pl.BlockSpec(shape, const, pipeline_mode=pl.Buffered(1))
    return pl.pallas_call(
        _tail_kernel,
        out_shape=jax.ShapeDtypeStruct((m, d), F32),
        grid=(m // tm,),
        in_specs=[pl.BlockSpec((tm, d_a), row), pl.BlockSpec((tm, d_b), row),
                  pl.BlockSpec((tm, d_b), lambda i: (blk0 + i, gate_b_block)),
                  pl.BlockSpec((tm, d), lambda i: (blk0 + i, merge_block)),
                  pl.BlockSpec((tm, d), lambda i: (blk0 + i, merge_block + 1)),
                  pl.BlockSpec((tm, d), row), pl.BlockSpec((tm, ple), row),
                  resident((2, d_b)),
                  resident((d_a, d)), resident((d_b, d)), resident((d, d)), resident((d, d)),
                  resident((ple, d)), resident((1, d))],
        out_specs=pl.BlockSpec((tm, d), row),
        compiler_params=_params(1),
        name="tail",
    )(oa, c, gates, gates, gates, x, p, cln, wa, wb, wout, wpg, wple, fg)


def _largest_tile(n, cap, align):
    t = min(n, cap)
    while n % t or t % align:
        t -= 1
    return t


def kernel(x_prompt, x_sample, state_shift, state_wkv, state_conv, p_prompt, p_sample, norm_g, w_in,
           shift_mu, w0, w_lora_b, a0, a_lora_b, k_k, k_a, r_k, lnx_g, lnx_b, w_proj_a, conv_w, conv_b,
           cln_g, cln_b, w_proj_b, w_out, w_ple, w_ple_gate, final_g):
    depth = w_in.shape[0]
    batch, seq, d = x_prompt.shape
    dec_batch, dec_seq, _ = x_sample.shape
    d_a = w_proj_a.shape[1]
    d_b = w_proj_b.shape[1]
    shift_w = shift_mu.shape[1]
    n_heads = d_a // HEAD
    n_pairs = d_a // PAIR
    taps = conv_w.shape[1]
    assert depth == 1 and dec_seq == 1 and d_a == d_b and 2 * d_a == d
    assert shift_w == 3 * d_a + 2 * LORA and d_a % PAIR == 0 and seq % CHUNK == 0

    o1 = shift_w
    o2 = o1 + d_a
    o3 = o2 + 2 * d_b
    o4 = o3 + d_b
    w = w_in[0]
    wa = w_proj_a[0].astype(BF16)
    wb = w_proj_b[0].astype(BF16)
    wout = w_out[0].astype(BF16)
    wpg = w_ple_gate[0].astype(BF16)
    wple = w_ple[0].astype(BF16)
    lora_w = jnp.zeros((2 * LORA, 2 * d_a), F32)
    lora_w = lora_w.at[:LORA, :d_a].set(w_lora_b[0]).at[LORA:, d_a:].set(a_lora_b[0]).astype(BF16)
    zeros_a = jnp.zeros((d_a,), F32)
    par = jnp.stack([w0[0], a0[0], k_k[0], k_a[0], r_k[0].reshape(d_a), lnx_g[0], lnx_b[0], zeros_a])
    cln = jnp.stack([cln_g[0], cln_b[0]])
    mu = shift_mu
    g_in = norm_g[0].reshape(1, d)
    fg = final_g.reshape(1, d)
    gate_b_block = d_a // d_b
    merge_block = (d_a + d_b) // d

    m_p = batch * seq
    m_all = m_p + dec_batch
    x2 = x_prompt.reshape(m_p, d)
    xs2 = x_sample.reshape(dec_batch, d)
    xn = _rmsnorm_bf16(x2, xs2, g_in, _largest_tile(m_p, 1024, 8))
    tm = _largest_tile(m_all, 1100, 16)
    tn_shift = _largest_tile(o1, 1024, 128)
    p_shift = _project(xn, w, lambda j: j * tn_shift, o1, tm, tn_shift)
    gate_cols = lambda j: jnp.where(j == 0, o1, o3 + (j - 1) * d_a)
    gates = _project(xn, w, gate_cols, d_a + d_b + 2 * d, tm, d_a, n_silu_blocks=2)
    u = _project_glu(xn, w, o2, o2 + d_b, d_b, tm, _largest_tile(d_b, 512, 128))

    oa, st = _wkv_prompt(p_shift, gates, mu, par, lora_w, batch, seq, d_a)
    c = _conv_prompt(u, conv_w[0], conv_b[0], batch, seq, _largest_tile(seq, 256, CONV_STRIP))
    y_prompt = _tail(oa, c, gates, x2, p_prompt[0].reshape(m_p, -1), cln, wa, wb, wout, wpg, wple, fg,
                     _largest_tile(m_p, 256, 8), gate_b_block, merge_block).reshape(batch, seq, d)
    new_shift_p = p_shift[:m_p].reshape(batch, seq, shift_w)[:, -1][None]
    st = st.reshape(batch, n_pairs, 2, HEAD, 2, HEAD)
    new_wkv_p = jnp.stack([st[:, :, 0, :, 0, :], st[:, :, 1, :, 1, :]], axis=2)
    new_wkv_p = jnp.swapaxes(new_wkv_p.reshape(batch, n_heads, HEAD, HEAD), -1, -2)[None]
    new_conv_p = u[:m_p].reshape(batch, seq, d_b)[:, seq - (taps - 1):][None]

    bb = _largest_tile(dec_batch, 8, 8)
    tm_s = _largest_tile(dec_batch, 256, 8)
    assert m_p % bb == 0 and m_p % tm_s == 0
    oa_s, new_wkv_s = _wkv_sample(p_shift, state_shift[0], gates, mu, par, lora_w, state_wkv[0], d_a, bb, m_p)
    c_s = _conv_sample(u, state_conv[0], conv_w[0], conv_b[0], bb, m_p)
    y_sample = _tail(oa_s, c_s, gates, xs2, p_sample[0].reshape(dec_batch, -1), cln, wa, wb, wout, wpg,
                     wple, fg, tm_s, gate_b_block, merge_block, row0=m_p).reshape(dec_batch, 1, d)
    new_shift_s = p_shift[m_p:][None]
    new_conv_s = jnp.concatenate([state_conv[0][:, 1:], u[m_p:, None, :]], axis=1)[None]

    return (y_prompt, y_sample, new_shift_p, new_wkv_p, new_conv_p, new_shift_s, new_wkv_s[None],
            new_conv_s)
```

```python
import functools
import math

import jax
import jax.numpy as jnp
from jax import lax
from jax.experimental import pallas as pl
from jax.experimental.pallas import tpu as pltpu

F32 = jnp.float32
BF16 = jnp.bfloat16

HEAD = 64
PAIR = 2 * HEAD
LORA = 64
CHUNK = 64
WKV_SEQS_PER_STEP = 4
RMS_EPS = 1e-6
LN_EPS = 1e-5
GN_EPS = 64e-5
LOG_DECAY_SCALE = math.exp(-0.5)
VMEM_LIMIT = 56 * 1024 * 1024


def _params(n_axes, vmem=VMEM_LIMIT):
    return pltpu.CompilerParams(dimension_semantics=("arbitrary",) * n_axes, vmem_limit_bytes=vmem)


def _sigmoid(x):
    return 1.0 / (1.0 + jnp.exp(-x))


def _dot(a, b):
    return jnp.dot(a, b, preferred_element_type=F32)


def _dot_nt(a, b):
    return lax.dot_general(a, b, (((1,), (1,)), ((), ())), preferred_element_type=F32)


def _dot_tn(a, b):
    return lax.dot_general(a, b, (((0,), (0,)), ((), ())), preferred_element_type=F32)


def _rmsnorm_kernel(xp_ref, xs_ref, g_ref, o_ref, *, n_prompt_blocks):
    def norm(x):
        ms = jnp.mean(x * x, axis=-1, keepdims=True)
        return (x * lax.rsqrt(ms + RMS_EPS) * g_ref[...]).astype(o_ref.dtype)

    i = pl.program_id(0)

    @pl.when(i < n_prompt_blocks)
    def _():
        o_ref[...] = norm(xp_ref[...])

    @pl.when(i == n_prompt_blocks)
    def _():
        o_ref[0:xs_ref.shape[0], :] = norm(xs_ref[...])


def _rmsnorm_bf16(x_prompt, x_sample, g, tm):
    m_p, d = x_prompt.shape
    m_s = x_sample.shape[0]
    n_blocks = m_p // tm
    assert m_p % tm == 0 and m_s <= tm
    kern = functools.partial(_rmsnorm_kernel, n_prompt_blocks=n_blocks)
    return pl.pallas_call(
        kern,
        out_shape=jax.ShapeDtypeStruct((m_p + m_s, d), BF16),
        grid=(n_blocks + 1,),
        in_specs=[pl.BlockSpec((tm, d), lambda i: (jnp.minimum(i, n_blocks - 1), 0)),
                  pl.BlockSpec((m_s, d), lambda i: (0, 0)),
                  pl.BlockSpec((1, d), lambda i: (0, 0))],
        out_specs=pl.BlockSpec((tm, d), lambda i: (i, 0)),
        compiler_params=_params(1),
        name="rmsnorm",
    )(x_prompt, x_sample, g)


def _proj_kernel(x_ref, w_ref, o_ref, wb_ref, *, n_silu_blocks):
    @pl.when(pl.program_id(1) == 0)
    def _():
        wb_ref[...] = w_ref[...].astype(BF16)

    y = _dot(x_ref[...], wb_ref[...])
    if n_silu_blocks is None:
        o_ref[...] = y
    else:
        s = _sigmoid(y)
        o_ref[...] = jnp.where(pl.program_id(0) < n_silu_blocks, y * s, s)


def _project(xn, w, col_start, n_out, tm, tn, n_silu_blocks=None):
    m, d = xn.shape
    kern = functools.partial(_proj_kernel, n_silu_blocks=n_silu_blocks)
    return pl.pallas_call(
        kern,
        out_shape=jax.ShapeDtypeStruct((m, n_out), F32),
        grid=(n_out // tn, m // tm),
        in_specs=[pl.BlockSpec((tm, d), lambda j, i: (i, 0)),
                  pl.BlockSpec((pl.Element(d), pl.Element(tn)),
                               lambda j, i: (0, pl.multiple_of(col_start(j), 128)))],
        out_specs=pl.BlockSpec((tm, tn), lambda j, i: (i, j)),
        scratch_shapes=[pltpu.VMEM((d, tn), BF16)],
        compiler_params=_params(2),
        name="in_proj",
    )(xn, w)


def _glu_kernel(x_ref, wa_ref, wb_ref, o_ref, wab_ref, wbb_ref):
    @pl.when(pl.program_id(1) == 0)
    def _():
        wab_ref[...] = wa_ref[...].astype(BF16)
        wbb_ref[...] = wb_ref[...].astype(BF16)

    x = x_ref[...]
    o_ref[...] = _dot(x, wab_ref[...]) * _sigmoid(_dot(x, wbb_ref[...]))


def _project_glu(xn, w, col_a, col_b, n_out, tm, tn):
    m, d = xn.shape
    return pl.pallas_call(
        _glu_kernel,
        out_shape=jax.ShapeDtypeStruct((m, n_out), F32),
        grid=(n_out // tn, m // tm),
        in_specs=[pl.BlockSpec((tm, d), lambda j, i: (i, 0)),
                  pl.BlockSpec((pl.Element(d), pl.Element(tn)),
                               lambda j, i: (0, pl.multiple_of(col_a + j * tn, 128))),
                  pl.BlockSpec((pl.Element(d), pl.Element(tn)),
                               lambda j, i: (0, pl.multiple_of(col_b + j * tn, 128)))],
        out_specs=pl.BlockSpec((tm, tn), lambda j, i: (i, j)),
        scratch_shapes=[pltpu.VMEM((d, tn), BF16), pltpu.VMEM((d, tn), BF16)],
        compiler_params=_params(2),
        name="in_proj_glu",
    )(xn, w, w)


def _head_sum(x, block_ones, split=False):
    hi = x.astype(BF16)
    if not split:
        return _dot(hi, block_ones)
    lo = (x - hi.astype(F32)).astype(BF16)
    return _dot(hi, block_ones) + _dot(lo, block_ones)


def _head_sums(xs, block_ones, split=False):
    rows = xs[0].shape[0]
    s = _head_sum(jnp.concatenate(xs, axis=0), block_ones, split)
    return [s[i * rows:(i + 1) * rows] for i in range(len(xs))]


def _block_ones():
    ri = lax.broadcasted_iota(jnp.int32, (PAIR, PAIR), 0)
    ci = lax.broadcasted_iota(jnp.int32, (PAIR, PAIR), 1)
    return jnp.where((ri < HEAD) == (ci < HEAD), 1.0, 0.0).astype(BF16)


def _token_shift(p, prev, mu):
    return p + mu * (prev - p)


def _lora_out(xs, lora_w, d_a):
    z = xs[:, 3 * d_a:3 * d_a + 2 * LORA]
    lane = lax.broadcasted_iota(jnp.int32, z.shape, 1)
    z = jnp.where(lane < LORA, jnp.tanh(z), z)
    return _dot(z.astype(BF16), lora_w)


def _pair_slices(d_a):
    return [slice(i * PAIR, (i + 1) * PAIR) for i in range(d_a // PAIR)]


def _pair_vectors(xs_list, lo_list, par_ref, d_a, block_ones):
    sls = _pair_slices(d_a)
    shifted = lambda sl, off: slice(off + sl.start, off + sl.stop)
    items = [(xs, lo, sl) for xs, lo in zip(xs_list, lo_list) for sl in sls]
    r = [xs[:, sl] for xs, _, sl in items]
    k = [xs[:, shifted(sl, d_a)] for xs, _, sl in items]
    v = [xs[:, shifted(sl, 2 * d_a)] for xs, _, sl in items]
    kkr = [ki * par_ref[2:3, sl] for ki, (_, _, sl) in zip(k, items)]
    ss = _head_sums([x * x for x in kkr], block_ones)
    kk = [x * jnp.minimum(lax.rsqrt(s), 1e12) for x, s in zip(kkr, ss)]
    log_decay = [-LOG_DECAY_SCALE * _sigmoid(par_ref[0:1, sl] + lo[:, sl]) for _, lo, sl in items]
    a = [_sigmoid(par_ref[1:2, sl] + lo[:, shifted(sl, d_a)]) for _, lo, sl in items]
    kf = [ki * (1.0 + (ai - 1.0) * par_ref[3:4, sl]) for ki, ai, (_, _, sl) in zip(k, a, items)]
    return r, kf, v, log_decay, kk, a


def _head_norm_bonus(y, r, kf, v, par_ref, d_a, block_ones):
    sls = _pair_slices(d_a) * (len(y) // (d_a // PAIR))
    mean = [s * (1.0 / HEAD) for s in _head_sums(y, block_ones, split=True)]
    d = [x - m for x, m in zip(y, mean)]
    var = [s * (1.0 / HEAD) for s in _head_sums([x * x for x in d], block_ones)]
    rk = _head_sums([ri * ki * par_ref[4:5, sl] for ri, ki, sl in zip(r, kf, sls)], block_ones)
    return [x * lax.rsqrt(vr + GN_EPS) * par_ref[5:6, sl] + par_ref[6:7, sl] + s * vi
            for x, vr, s, vi, sl in zip(d, var, rk, v, sls)]


def _stack_heads(x, lane_lo):
    xb = x.astype(BF16)
    zero = jnp.zeros_like(xb)
    return jnp.concatenate([jnp.where(lane_lo, xb, zero), jnp.where(lane_lo, zero, xb)], axis=0)


def _cumsum_rows(xs):
    n = xs[0].shape[0]
    tri = (lax.broadcasted_iota(jnp.int32, (n, n), 1) <= lax.broadcasted_iota(jnp.int32, (n, n), 0))
    tri = jnp.where(tri, 1.0, 0.0).astype(BF16)
    x = jnp.concatenate(xs, axis=1)
    hi = x.astype(BF16)
    lo = (x - hi.astype(F32)).astype(BF16)
    s = _dot(tri, hi) + _dot(tri, lo)
    return [s[:, i * PAIR:(i + 1) * PAIR] for i in range(len(xs))]


def _wkv_chunk_kernel(*refs, d_a, nb):
    ps_refs, sga_refs = refs[:nb], refs[nb:2 * nb]
    mu_ref, par_ref, lora_ref, oa_ref, st_ref, carry_ref = refs[2 * nb:]
    c = pl.program_id(1)
    n_pairs = d_a // PAIR
    C = CHUNK

    @pl.when(c == 0)
    def _():
        carry_ref[...] = jnp.zeros_like(carry_ref)
        st_ref[...] = jnp.zeros_like(st_ref)

    xs, lo = [], []
    row = lax.broadcasted_iota(jnp.int32, ps_refs[0].shape, 0)
    for bi in range(nb):
        p = ps_refs[bi][...]
        prev = jnp.where(row == 0, carry_ref[bi], pltpu.roll(p, 1, axis=0))
        carry_ref[bi] = p[C - 1:C, :]
        xs.append(_token_shift(p, prev, mu_ref[...]))
        lo.append(_lora_out(xs[-1], lora_ref[...], d_a))

    block_ones = _block_ones()
    lane_lo = lax.broadcasted_iota(jnp.int32, (C, PAIR), 1) < HEAD
    ti = lax.broadcasted_iota(jnp.int32, (C, PAIR), 0)
    si = lax.broadcasted_iota(jnp.int32, (C, PAIR), 1) % C
    strict_lower = si < ti
    lower2 = (lax.broadcasted_iota(jnp.int32, (C, 2 * PAIR), 1) % C
              <= lax.broadcasted_iota(jnp.int32, (C, 2 * PAIR), 0))
    eye = jnp.where(si == ti, 1.0, 0.0)
    cat0 = lambda x, y: jnp.concatenate([x, y], axis=0)

    r, kf, v, log_decay, kk, a = _pair_vectors(xs, lo, par_ref, d_a, block_ones)
    cl = _cumsum_rows(log_decay)
    cend = [x[C - 1:C, :] for x in cl]
    e_neg = [jnp.exp(-x) for x in cl]
    e_end = [jnp.exp(ce - x) for ce, x in zip(cend, cl)]
    kka = [x * y for x, y in zip(kk, a)]
    stack = lambda xs_: [_stack_heads(x, lane_lo) for x in xs_]
    a_p = [(-x * jnp.exp(c_ - ld)).astype(BF16) for x, c_, ld in zip(kk, cl, log_decay)]
    r_p = [(x * jnp.exp(c_)).astype(BF16) for x, c_ in zip(r, cl)]
    a_s = stack(a_p)
    b_s = stack([x * e for x, e in zip(kka, e_neg)])
    k_s = stack([x * e for x, e in zip(kf, e_neg)])
    v_s = stack(v)
    bh_s = stack([x * e for x, e in zip(kka, e_end)])
    kh_s = stack([x * e for x, e in zip(kf, e_end)])
    pend_col = [jnp.transpose(jnp.broadcast_to(jnp.exp(ce), (PAIR, PAIR))) for ce in cend]

    m1 = [_dot_nt(cat0(x, y), cat0(z, w)) for x, y, z, w in zip(a_p, r_p, b_s, k_s)]
    a_ab = [jnp.where(strict_lower, m[:C, :PAIR], 0.0) for m in m1]
    a_ak = [jnp.where(strict_lower, m[:C, PAIR:], 0.0).astype(BF16) for m in m1]
    a_rbk = [jnp.where(lower2, m[C:], 0.0).astype(BF16) for m in m1]
    g = [_dot(x, y) for x, y in zip(a_ak, v_s)]

    apow = [_dot(x.astype(BF16), _stack_heads(x, lane_lo)) for x in a_ab]
    tinv = [eye + x for x in a_ab]
    n_levels = CHUNK.bit_length() - 1
    for lvl in range(1, n_levels):
        rhs = stack(apow)
        if lvl < n_levels - 1:
            both = [_dot(cat0(x.astype(BF16), t.astype(BF16)), y) for x, t, y in zip(apow, tinv, rhs)]
            apow = [x[:C] for x in both]
            tinv = [t + x[C:] for t, x in zip(tinv, both)]
        else:
            tinv = [t + _dot(t.astype(BF16), y) for t, y in zip(tinv, rhs)]

    g_s = stack(g)
    wx = [_dot(t.astype(BF16), jnp.concatenate([x, y], axis=1)) for t, x, y in zip(tinv, a_s, g_s)]
    seq_pair = [(bi, pi) for bi in range(nb) for pi in range(n_pairs)]
    st = [st_ref[bi, pi] for bi, pi in seq_pair]
    uy = [_dot(cat0(w[:, :PAIR].astype(BF16), x), s_.astype(BF16)) for w, x, s_ in zip(wx, r_p, st)]
    uv_s = [cat0(_stack_heads(x[:C] + w[:, PAIR:], lane_lo), y) for x, w, y in zip(uy, wx, v_s)]
    st_upd = [_dot_tn(cat0(x, y), z) for x, y, z in zip(bh_s, kh_s, uv_s)]
    y_uv = [_dot(x, z) for x, z in zip(a_rbk, uv_s)]
    for i, (bi, pi) in enumerate(seq_pair):
        st_ref[bi, pi] = pend_col[i] * st[i] + st_upd[i]
    y = [x[C:] + z for x, z in zip(uy, y_uv)]
    o = _head_norm_bonus(y, r, kf, v, par_ref, d_a, block_ones)
    sls = _pair_slices(d_a)
    for oi, (bi, pi) in zip(o, seq_pair):
        oa_ref[bi, :, sls[pi]] = (oi * sga_refs[bi][:, sls[pi]]).astype(oa_ref.dtype)


def _wkv_prompt(p_shift, gates, mu, par, lora_w, batch, seq, d_a):
    n_chunks = seq // CHUNK
    shift_w = p_shift.shape[1]
    n_pairs = d_a // PAIR
    nb = WKV_SEQS_PER_STEP if batch % WKV_SEQS_PER_STEP == 0 else 1
    kern = functools.partial(_wkv_chunk_kernel, d_a=d_a, nb=nb)
    seq_rows = lambda bi: (lambda b, c: ((b * nb + bi) * n_chunks + c, 0))
    oa, st = pl.pallas_call(
        kern,
        out_shape=(jax.ShapeDtypeStruct((batch, seq, d_a), BF16),
                   jax.ShapeDtypeStruct((batch, n_pairs, PAIR, PAIR), F32)),
        grid=(batch // nb, n_chunks),
        in_specs=([pl.BlockSpec((CHUNK, shift_w), seq_rows(bi)) for bi in range(nb)]
                  + [pl.BlockSpec((CHUNK, d_a), seq_rows(bi)) for bi in range(nb)]
                  + [pl.BlockSpec((1, shift_w), lambda b, c: (0, 0)),
                     pl.BlockSpec((8, d_a), lambda b, c: (0, 0)),
                     pl.BlockSpec((2 * LORA, 2 * d_a), lambda b, c: (0, 0))]),
        out_specs=(pl.BlockSpec((nb, CHUNK, d_a), lambda b, c: (b, c, 0)),
                   pl.BlockSpec((nb, n_pairs, PAIR, PAIR), lambda b, c: (b, 0, 0, 0))),
        scratch_shapes=[pltpu.VMEM((nb, 1, shift_w), F32)],
        compiler_params=_params(2),
        name="wkv_chunked",
    )(*([p_shift] * nb + [gates] * nb), mu, par, lora_w)
    return oa.reshape(batch * seq, d_a), st


def _wkv_step_kernel(ps_ref, prev_ref, sga_ref, mu_ref, par_ref, lora_ref, s_ref, oa_ref, so_ref,
                     vec_ref, y_ref, *, d_a, bb, group):
    n_heads = d_a // HEAD
    half = HEAD // 2
    xs = _token_shift(ps_ref[...], prev_ref[...], mu_ref[...])
    lo = _lora_out(xs, lora_ref[...], d_a)
    block_ones = _block_ones()
    sls = _pair_slices(d_a)
    r, kf, v, log_decay, kk, a = _pair_vectors([xs], [lo], par_ref, d_a, block_ones)

    lane_lo = lax.broadcasted_iota(jnp.int32, (bb, PAIR), 1) < HEAD
    for pi in range(len(sls)):
        v_hi = v[pi].astype(BF16).astype(F32)
        vecs = (-kk[pi], jnp.exp(log_decay[pi]), kk[pi] * a[pi], kf[pi], r[pi], v_hi, v[pi] - v_hi)
        for i, x in enumerate(vecs):
            swapped = pltpu.roll(x, HEAD, axis=1)
            vec_ref[i, 2 * pi] = jnp.where(lane_lo, x, swapped)
            vec_ref[i, 2 * pi + 1] = jnp.where(lane_lo, swapped, x)

    rows = n_heads * half
    m_in_head = lax.broadcasted_iota(jnp.int32, (rows, PAIR), 0) % half
    lane = lax.broadcasted_iota(jnp.int32, (rows, PAIR), 1)
    diag = (lane == 2 * m_in_head) | (lane == 2 * m_in_head + HEAD + 1)

    def expand(i, b):
        return jnp.concatenate(
            [jnp.broadcast_to(vec_ref[i, h, b:b + 1, :], (half, PAIR)) for h in range(n_heads)], axis=0)

    for g0 in range(0, bb, group):
        bs = range(g0, g0 + group)
        s = [s_ref[b] for b in bs]
        sa = [_dot((x * expand(0, b)).astype(BF16), block_ones) for x, b in zip(s, bs)]
        vb = [_dot(jnp.where(diag, expand(5, b), 0.0).astype(BF16), block_ones)
              + _dot(jnp.where(diag, expand(6, b), 0.0).astype(BF16), block_ones) for b in bs]
        s_new = [x * expand(1, b) + y * expand(2, b) + z * expand(3, b) for x, y, z, b in zip(s, sa, vb, bs)]
        for x, b in zip(s_new, bs):
            so_ref[b] = x
        yb = [_head_sum(x * expand(4, b), block_ones, split=True) for x, b in zip(s_new, bs)]
        for x, b in zip(yb, bs):
            yh = jnp.sum(jnp.where(diag, x, 0.0).reshape(n_heads, half, PAIR), axis=1)
            y_ref[b] = yh + pltpu.roll(yh, HEAD, axis=1)

    y = [jnp.where(lane_lo, y_ref[:, 2 * pi, :], y_ref[:, 2 * pi + 1, :]) for pi in range(len(sls))]
    o = _head_norm_bonus(y, r, kf, v, par_ref, d_a, block_ones)
    for oi, sl in zip(o, sls):
        oa_ref[:, sl] = (oi * sga_ref[:, sl]).astype(oa_ref.dtype)


def _wkv_sample(p_shift, shift_prev, gates, mu, par, lora_w, state, d_a, bb, row0):
    batch, shift_w = shift_prev.shape
    n_heads = d_a // HEAD
    rows = n_heads * HEAD // 2
    blk0 = row0 // bb
    kern = functools.partial(_wkv_step_kernel, d_a=d_a, bb=bb, group=min(bb, 4))
    oa, new_state = pl.pallas_call(
        kern,
        out_shape=(jax.ShapeDtypeStruct((batch, d_a), BF16),
                   jax.ShapeDtypeStruct((batch, rows, PAIR), F32)),
        grid=(batch // bb,),
        in_specs=[pl.BlockSpec((bb, shift_w), lambda i: (blk0 + i, 0)),
                  pl.BlockSpec((bb, shift_w), lambda i: (i, 0)),
                  pl.BlockSpec((bb, d_a), lambda i: (blk0 + i, 0)),
                  pl.BlockSpec((1, shift_w), lambda i: (0, 0)),
                  pl.BlockSpec((8, d_a), lambda i: (0, 0)),
                  pl.BlockSpec((2 * LORA, 2 * d_a), lambda i: (0, 0)),
                  pl.BlockSpec((bb, rows, PAIR), lambda i: (i, 0, 0))],
        out_specs=(pl.BlockSpec((bb, d_a), lambda i: (i, 0)),
                   pl.BlockSpec((bb, rows, PAIR), lambda i: (i, 0, 0))),
        scratch_shapes=[pltpu.VMEM((7, n_heads, bb, PAIR), F32), pltpu.VMEM((bb, n_heads, PAIR), F32)],
        compiler_params=_params(1),
        name="wkv_step",
    )(p_shift, shift_prev, gates, mu, par, lora_w, state.reshape(batch, rows, PAIR))
    return oa, new_state.reshape(state.shape)


CONV_PAD = 32
CONV_STRIP = 16


def _conv_prompt_kernel(u_ref, w_ref, b_ref, o_ref, buf_ref, c_ref, *, tt, taps):
    t = pl.program_id(1)
    sub = buf_ref.shape[1]
    lanes = [slice(s * 128, (s + 1) * 128) for s in range(sub)]

    @pl.when(t == 0)
    def _():
        buf_ref[0:CONV_PAD] = jnp.zeros((CONV_PAD,) + buf_ref.shape[1:], F32)

    u = u_ref[...]
    buf_ref[CONV_PAD:CONV_PAD + tt] = jnp.swapaxes(jnp.stack([u[:, ls] for ls in lanes], axis=0), 0, 1)
    off = CONV_PAD - (taps - 1)
    w = [w_ref[k] for k in range(taps)]
    bias = b_ref[0]

    def strip(s, carry):
        t0 = s * CONV_STRIP
        acc = [bias] * CONV_STRIP
        for i in range(CONV_STRIP + taps - 1):
            x = buf_ref[off + t0 + i]
            for j in range(CONV_STRIP):
                if 0 <= i - j < taps:
                    acc[j] = acc[j] + w[i - j] * x
        for j in range(CONV_STRIP):
            c_ref[t0 + j] = acc[j]
        return carry

    lax.fori_loop(0, tt // CONV_STRIP, strip, 0)
    buf_ref[0:CONV_PAD] = buf_ref[tt:tt + CONV_PAD]
    c = jnp.swapaxes(c_ref[...], 0, 1)
    for s, ls in enumerate(lanes):
        o_ref[:, ls] = c[s]


def _conv_prompt(u, conv_w, conv_b, batch, seq, tt):
    d_b = u.shape[1]
    taps = conv_w.shape[0]
    nt = seq // tt
    sub = d_b // 128
    assert taps - 1 <= CONV_PAD and tt % CONV_STRIP == 0 and tt >= CONV_PAD
    kern = functools.partial(_conv_prompt_kernel, tt=tt, taps=taps)
    return pl.pallas_call(
        kern,
        out_shape=jax.ShapeDtypeStruct((batch * seq, d_b), F32),
        grid=(batch, nt),
        in_specs=[pl.BlockSpec((tt, d_b), lambda b, t: (b * nt + t, 0)),
                  pl.BlockSpec((taps, sub, 128), lambda b, t: (0, 0, 0)),
                  pl.BlockSpec((1, sub, 128), lambda b, t: (0, 0, 0))],
        out_specs=pl.BlockSpec((tt, d_b), lambda b, t: (b * nt + t, 0)),
        scratch_shapes=[pltpu.VMEM((tt + CONV_PAD, sub, 128), F32), pltpu.VMEM((tt, sub, 128), F32)],
        compiler_params=_params(2),
        name="conv_prompt",
    )(u, conv_w.reshape(taps, sub, 128), conv_b.reshape(1, sub, 128))


def _conv_step_kernel(u_ref, prev_ref, w_ref, b_ref, o_ref, *, taps):
    c = b_ref[...] + w_ref[taps - 1:taps, :] * u_ref[...]
    for k in range(taps - 1):
        c = c + w_ref[k:k + 1, :] * prev_ref[:, k, :]
    o_ref[...] = c


def _conv_sample(u, conv_prev, conv_w, conv_b, bb, row0):
    batch = conv_prev.shape[0]
    d_b = u.shape[1]
    taps = conv_w.shape[0]
    blk0 = row0 // bb
    kern = functools.partial(_conv_step_kernel, taps=taps)
    return pl.pallas_call(
        kern,
        out_shape=jax.ShapeDtypeStruct((batch, d_b), F32),
        grid=(batch // bb,),
        in_specs=[pl.BlockSpec((bb, d_b), lambda i: (blk0 + i, 0)),
                  pl.BlockSpec((bb, taps - 1, d_b), lambda i: (i, 0, 0)),
                  pl.BlockSpec((taps, d_b), lambda i: (0, 0)),
                  pl.BlockSpec((1, d_b), lambda i: (0, 0))],
        out_specs=pl.BlockSpec((bb, d_b), lambda i: (i, 0)),
        compiler_params=_params(1),
        name="conv_step",
    )(u, conv_prev, conv_w, conv_b.reshape(1, d_b))


def _tail_kernel(oa_ref, c_ref, gb_ref, sga_ref, sgb_ref, x_ref, p_ref, cln_ref, wa_ref, wb_ref, wout_ref,
                 wpg_ref, wple_ref, fg_ref, o_ref):
    c = c_ref[...]
    mean = jnp.mean(c, axis=-1, keepdims=True)
    dc = c - mean
    var = jnp.mean(dc * dc, axis=-1, keepdims=True)
    cf = dc * lax.rsqrt(var + LN_EPS) * cln_ref[0:1, :] + cln_ref[1:2, :]
    cb = (cf * _sigmoid(cf) * gb_ref[...]).astype(BF16)
    m = sga_ref[...] * _dot(oa_ref[...], wa_ref[...]) + sgb_ref[...] * _dot(cb, wb_ref[...])
    h = x_ref[...] + _dot(m.astype(BF16), wout_ref[...])
    gate = _sigmoid(_dot(h.astype(BF16), wpg_ref[...]))
    h = h + gate * _dot(p_ref[...].astype(BF16), wple_ref[...])
    ms = jnp.mean(h * h, axis=-1, keepdims=True)
    o_ref[...] = h * lax.rsqrt(ms + RMS_EPS) * fg_ref[...]


def _tail(oa, c, gates, x, p, cln, wa, wb, wout, wpg, wple, fg, tm, gate_b_block, merge_block, row0=0):
    m, d = x.shape
    d_a = oa.shape[1]
    d_b = c.shape[1]
    ple = p.shape[1]
    blk0 = row0 // tm
    row = lambda i: (i, 0)
    const = lambda i: (0, 0)
    resident = lambda shape: pl.BlockSpec(shape, const, pipeline_mode=pl.Buffered(1))
    return pl.pallas_call(
        _tail_kernel,
        out_shape=jax.ShapeDtypeStruct((m, d), F32),
        grid=(m // tm,),
        in_specs=[pl.BlockSpec((tm, d_a), row), pl.BlockSpec((tm, d_b), row),
                  pl.BlockSpec((tm, d_b), lambda i: (blk0 + i, gate_b_block)),
                  pl.BlockSpec((tm, d), lambda i: (blk0 + i, merge_block)),
                  pl.BlockSpec((tm, d), lambda i: (blk0 + i, merge_block + 1)),
                  pl.BlockSpec((tm, d), row), pl.BlockSpec((tm, ple), row),
                  resident((2, d_b)),
                  resident((d_a, d)), resident((d_b, d)), resident((d, d)), resident((d, d)),
                  resident((ple, d)), resident((1, d))],
        out_specs=pl.BlockSpec((tm, d), row),
        compiler_params=_params(1),
        name="tail",
    )(oa, c, gates, gates, gates, x, p, cln, wa, wb, wout, wpg, wple, fg)


def _largest_tile(n, cap, align):
    t = min(n, cap)
    while n % t or t % align:
        t -= 1
    return t


def kernel(x_prompt, x_sample, state_shift, state_wkv, state_conv, p_prompt, p_sample, norm_g, w_in,
           shift_mu, w0, w_lora_b, a0, a_lora_b, k_k, k_a, r_k, lnx_g, lnx_b, w_proj_a, conv_w, conv_b,
           cln_g, cln_b, w_proj_b, w_out, w_ple, w_ple_gate, final_g):
    depth = w_in.shape[0]
    batch, seq, d = x_prompt.shape
    dec_batch, dec_seq, _ = x_sample.shape
    d_a = w_proj_a.shape[1]
    d_b = w_proj_b.shape[1]
    shift_w = shift_mu.shape[1]
    n_heads = d_a // HEAD
    n_pairs = d_a // PAIR
    taps = conv_w.shape[1]
    assert depth == 1 and dec_seq == 1 and d_a == d_b and 2 * d_a == d
    assert shift_w == 3 * d_a + 2 * LORA and d_a % PAIR == 0 and seq % CHUNK == 0

    o1 = shift_w
    o2 = o1 + d_a
    o3 = o2 + 2 * d_b
    o4 = o3 + d_b
    w = w_in[0]
    wa = w_proj_a[0].astype(BF16)
    wb = w_proj_b[0].astype(BF16)
    wout = w_out[0].astype(BF16)
    wpg = w_ple_gate[0].astype(BF16)
    wple = w_ple[0].astype(BF16)
    lora_w = jnp.zeros((2 * LORA, 2 * d_a), F32)
    lora_w = lora_w.at[:LORA, :d_a].set(w_lora_b[0]).at[LORA:, d_a:].set(a_lora_b[0]).astype(BF16)
    zeros_a = jnp.zeros((d_a,), F32)
    par = jnp.stack([w0[0], a0[0], k_k[0], k_a[0], r_k[0].reshape(d_a), lnx_g[0], lnx_b[0], zeros_a])
    cln = jnp.stack([cln_g[0], cln_b[0]])
    mu = shift_mu
    g_in = norm_g[0].reshape(1, d)
    fg = final_g.reshape(1, d)
    gate_b_block = d_a // d_b
    merge_block = (d_a + d_b) // d

    m_p = batch * seq
    m_all = m_p + dec_batch
    x2 = x_prompt.reshape(m_p, d)
    xs2 = x_sample.reshape(dec_batch, d)
    xn = _rmsnorm_bf16(x2, xs2, g_in, _largest_tile(m_p, 1024, 8))
    tm = _largest_tile(m_all, 1100, 16)
    tn_shift = _largest_tile(o1, 1024, 128)
    p_shift = _project(xn, w, lambda j: j * tn_shift, o1, tm, tn_shift)
    gate_cols = lambda j: jnp.where(j == 0, o1, o3 + (j - 1) * d_a)
    gates = _project(xn, w, gate_cols, d_a + d_b + 2 * d, tm, d_a, n_silu_blocks=2)
    u = _project_glu(xn, w, o2, o2 + d_b, d_b, tm, _largest_tile(d_b, 512, 128))

    oa, st = _wkv_prompt(p_shift, gates, mu, par, lora_w, batch, seq, d_a)
    c = _conv_prompt(u, conv_w[0], conv_b[0], batch, seq, _largest_tile(seq, 256, CONV_STRIP))
    y_prompt = _tail(oa, c, gates, x2, p_prompt[0].reshape(m_p, -1), cln, wa, wb, wout, wpg, wple, fg,
                     _largest_tile(m_p, 256, 8), gate_b_block, merge_block).reshape(batch, seq, d)
    new_shift_p = p_shift[seq - 1:m_p:seq][None]
    st = st.reshape(batch, n_pairs, 2, HEAD, 2, HEAD)
    new_wkv_p = jnp.stack([st[:, :, 0, :, 0, :], st[:, :, 1, :, 1, :]], axis=2)
    new_wkv_p = jnp.swapaxes(new_wkv_p.reshape(batch, n_heads, HEAD, HEAD), -1, -2)[None]
    new_conv_p = jnp.stack([u[(b + 1) * seq - (taps - 1):(b + 1) * seq] for b in range(batch)])[None]

    bb = _largest_tile(dec_batch, 8, 8)
    tm_s = _largest_tile(dec_batch, 256, 8)
    assert m_p % bb == 0 and m_p % tm_s == 0
    oa_s, new_wkv_s = _wkv_sample(p_shift, state_shift[0], gates, mu, par, lora_w, state_wkv[0], d_a, bb, m_p)
    c_s = _conv_sample(u, state_conv[0], conv_w[0], conv_b[0], bb, m_p)
    y_sample = _tail(oa_s, c_s, gates, xs2, p_sample[0].reshape(dec_batch, -1), cln, wa, wb, wout, wpg,
                     wple, fg, tm_s, gate_b_block, merge_block, row0=m_p).reshape(dec_batch, 1, d)
    new_shift_s = p_shift[m_p:][None]
    new_conv_s = jnp.concatenate([state_conv[0][:, 1:], u[m_p:, None, :]], axis=1)[None]

    return (y_prompt, y_sample, new_shift_p, new_wkv_p, new_conv_p, new_shift_s, new_wkv_s[None],
            new_conv_s)
```

```python
import functools
import math

import jax
import jax.numpy as jnp
from jax import lax
from jax.experimental import pallas as pl
from jax.experimental.pallas import tpu as pltpu

F32 = jnp.float32
BF16 = jnp.bfloat16

HEAD = 64
PAIR = 2 * HEAD
LORA = 64
CHUNK = 64
WKV_SEQS_PER_STEP = 4
RMS_EPS = 1e-6
LN_EPS = 1e-5
GN_EPS = 64e-5
LOG_DECAY_SCALE = math.exp(-0.5)
VMEM_LIMIT = 56 * 1024 * 1024


def _params(n_axes, vmem=VMEM_LIMIT):
    return pltpu.CompilerParams(dimension_semantics=("arbitrary",) * n_axes, vmem_limit_bytes=vmem)


def _sigmoid(x):
    return 1.0 / (1.0 + jnp.exp(-x))


def _dot(a, b):
    return jnp.dot(a, b, preferred_element_type=F32)


def _dot_nt(a, b):
    return lax.dot_general(a, b, (((1,), (1,)), ((), ())), preferred_element_type=F32)


def _dot_tn(a, b):
    return lax.dot_general(a, b, (((0,), (0,)), ((), ())), preferred_element_type=F32)


def _rmsnorm_kernel(xp_ref, xs_ref, g_ref, o_ref, *, n_prompt_blocks):
    def norm(x):
        ms = jnp.mean(x * x, axis=-1, keepdims=True)
        return (x * lax.rsqrt(ms + RMS_EPS) * g_ref[...]).astype(o_ref.dtype)

    i = pl.program_id(0)

    @pl.when(i < n_prompt_blocks)
    def _():
        o_ref[...] = norm(xp_ref[...])

    @pl.when(i == n_prompt_blocks)
    def _():
        o_ref[0:xs_ref.shape[0], :] = norm(xs_ref[...])


def _rmsnorm_bf16(x_prompt, x_sample, g, tm):
    m_p, d = x_prompt.shape
    m_s = x_sample.shape[0]
    n_blocks = m_p // tm
    assert m_p % tm == 0 and m_s <= tm
    kern = functools.partial(_rmsnorm_kernel, n_prompt_blocks=n_blocks)
    return pl.pallas_call(
        kern,
        out_shape=jax.ShapeDtypeStruct((m_p + m_s, d), BF16),
        grid=(n_blocks + 1,),
        in_specs=[pl.BlockSpec((tm, d), lambda i: (jnp.minimum(i, n_blocks - 1), 0)),
                  pl.BlockSpec((m_s, d), lambda i: (0, 0)),
                  pl.BlockSpec((1, d), lambda i: (0, 0))],
        out_specs=pl.BlockSpec((tm, d), lambda i: (i, 0)),
        compiler_params=_params(1),
        name="rmsnorm",
    )(x_prompt, x_sample, g)


def _proj_kernel(x_ref, w_ref, o_ref, wb_ref, *, n_silu_blocks):
    @pl.when(pl.program_id(1) == 0)
    def _():
        wb_ref[...] = w_ref[...].astype(BF16)

    y = _dot(x_ref[...], wb_ref[...])
    if n_silu_blocks is None:
        o_ref[...] = y
    else:
        s = _sigmoid(y)
        o_ref[...] = jnp.where(pl.program_id(0) < n_silu_blocks, y * s, s)


def _project(xn, w, col_start, n_out, tm, tn, n_silu_blocks=None):
    m, d = xn.shape
    kern = functools.partial(_proj_kernel, n_silu_blocks=n_silu_blocks)
    return pl.pallas_call(
        kern,
        out_shape=jax.ShapeDtypeStruct((m, n_out), F32),
        grid=(n_out // tn, m // tm),
        in_specs=[pl.BlockSpec((tm, d), lambda j, i: (i, 0)),
                  pl.BlockSpec((pl.Element(d), pl.Element(tn)),
                               lambda j, i: (0, pl.multiple_of(col_start(j), 128)))],
        out_specs=pl.BlockSpec((tm, tn), lambda j, i: (i, j)),
        scratch_shapes=[pltpu.VMEM((d, tn), BF16)],
        compiler_params=_params(2),
        name="in_proj",
    )(xn, w)


def _glu_kernel(*refs, n_casts):
    x_ref, wa_ref, wb_ref = refs[:3]
    cast_in = refs[3:3 + n_casts]
    o_ref = refs[3 + n_casts]
    cast_out = refs[4 + n_casts:4 + 2 * n_casts]
    wab_ref, wbb_ref = refs[4 + 2 * n_casts:]

    @pl.when(pl.program_id(1) == 0)
    def _():
        wab_ref[...] = wa_ref[...].astype(BF16)
        wbb_ref[...] = wb_ref[...].astype(BF16)

    x = x_ref[...]
    o_ref[...] = _dot(x, wab_ref[...]) * _sigmoid(_dot(x, wbb_ref[...]))
    for src, dst in zip(cast_in, cast_out):
        dst[...] = src[...].astype(BF16)


def _project_glu(xn, w, col_a, col_b, n_out, tm, tn, casts):
    m, d = xn.shape
    nj, ni = n_out // tn, m // tm
    steps = nj * ni
    for a in casts:
        assert a.shape[0] % (16 * steps) == 0, (a.shape, steps)
    step_rows = lambda a: pl.BlockSpec((a.shape[0] // steps, a.shape[1]), lambda j, i: (j * ni + i, 0))
    kern = functools.partial(_glu_kernel, n_casts=len(casts))
    out = pl.pallas_call(
        kern,
        out_shape=[jax.ShapeDtypeStruct((m, n_out), F32)]
        + [jax.ShapeDtypeStruct(a.shape, BF16) for a in casts],
        grid=(nj, ni),
        in_specs=[pl.BlockSpec((tm, d), lambda j, i: (i, 0)),
                  pl.BlockSpec((pl.Element(d), pl.Element(tn)),
                               lambda j, i: (0, pl.multiple_of(col_a + j * tn, 128))),
                  pl.BlockSpec((pl.Element(d), pl.Element(tn)),
                               lambda j, i: (0, pl.multiple_of(col_b + j * tn, 128)))]
        + [step_rows(a) for a in casts],
        out_specs=[pl.BlockSpec((tm, tn), lambda j, i: (i, j))] + [step_rows(a) for a in casts],
        scratch_shapes=[pltpu.VMEM((d, tn), BF16), pltpu.VMEM((d, tn), BF16)],
        compiler_params=_params(2),
        name="in_proj_glu",
    )(xn, w, w, *casts)
    return out[0], out[1:]


def _head_sum(x, block_ones, split=False):
    hi = x.astype(BF16)
    if not split:
        return _dot(hi, block_ones)
    lo = (x - hi.astype(F32)).astype(BF16)
    return _dot(hi, block_ones) + _dot(lo, block_ones)


def _head_sums(xs, block_ones, split=False):
    rows = xs[0].shape[0]
    s = _head_sum(jnp.concatenate(xs, axis=0), block_ones, split)
    return [s[i * rows:(i + 1) * rows] for i in range(len(xs))]


def _block_ones():
    ri = lax.broadcasted_iota(jnp.int32, (PAIR, PAIR), 0)
    ci = lax.broadcasted_iota(jnp.int32, (PAIR, PAIR), 1)
    return jnp.where((ri < HEAD) == (ci < HEAD), 1.0, 0.0).astype(BF16)


def _token_shift(p, prev, mu):
    return p + mu * (prev - p)


def _lora_out(xs, lora_w, d_a):
    z = xs[:, 3 * d_a:3 * d_a + 2 * LORA]
    lane = lax.broadcasted_iota(jnp.int32, z.shape, 1)
    z = jnp.where(lane < LORA, jnp.tanh(z), z)
    return _dot(z.astype(BF16), lora_w)


def _pair_slices(d_a):
    return [slice(i * PAIR, (i + 1) * PAIR) for i in range(d_a // PAIR)]


def _pair_vectors(xs_list, lo_list, par_ref, d_a, block_ones):
    sls = _pair_slices(d_a)
    shifted = lambda sl, off: slice(off + sl.start, off + sl.stop)
    items = [(xs, lo, sl) for xs, lo in zip(xs_list, lo_list) for sl in sls]
    r = [xs[:, sl] for xs, _, sl in items]
    k = [xs[:, shifted(sl, d_a)] for xs, _, sl in items]
    v = [xs[:, shifted(sl, 2 * d_a)] for xs, _, sl in items]
    kkr = [ki * par_ref[2:3, sl] for ki, (_, _, sl) in zip(k, items)]
    ss = _head_sums([x * x for x in kkr], block_ones)
    kk = [x * jnp.minimum(lax.rsqrt(s), 1e12) for x, s in zip(kkr, ss)]
    log_decay = [-LOG_DECAY_SCALE * _sigmoid(par_ref[0:1, sl] + lo[:, sl]) for _, lo, sl in items]
    a = [_sigmoid(par_ref[1:2, sl] + lo[:, shifted(sl, d_a)]) for _, lo, sl in items]
    kf = [ki * (1.0 + (ai - 1.0) * par_ref[3:4, sl]) for ki, ai, (_, _, sl) in zip(k, a, items)]
    return r, kf, v, log_decay, kk, a


def _head_norm_bonus(y, r, kf, v, par_ref, d_a, block_ones):
    sls = _pair_slices(d_a) * (len(y) // (d_a // PAIR))
    mean = [s * (1.0 / HEAD) for s in _head_sums(y, block_ones, split=True)]
    d = [x - m for x, m in zip(y, mean)]
    var = [s * (1.0 / HEAD) for s in _head_sums([x * x for x in d], block_ones)]
    rk = _head_sums([ri * ki * par_ref[4:5, sl] for ri, ki, sl in zip(r, kf, sls)], block_ones)
    return [x * lax.rsqrt(vr + GN_EPS) * par_ref[5:6, sl] + par_ref[6:7, sl] + s * vi
            for x, vr, s, vi, sl in zip(d, var, rk, v, sls)]


def _stack_heads(x, lane_lo):
    xb = x.astype(BF16)
    zero = jnp.zeros_like(xb)
    return jnp.concatenate([jnp.where(lane_lo, xb, zero), jnp.where(lane_lo, zero, xb)], axis=0)


def _cumsum_rows(xs):
    n = xs[0].shape[0]
    tri = (lax.broadcasted_iota(jnp.int32, (n, n), 1) <= lax.broadcasted_iota(jnp.int32, (n, n), 0))
    tri = jnp.where(tri, 1.0, 0.0).astype(BF16)
    x = jnp.concatenate(xs, axis=1)
    hi = x.astype(BF16)
    lo = (x - hi.astype(F32)).astype(BF16)
    s = _dot(tri, hi) + _dot(tri, lo)
    return [s[:, i * PAIR:(i + 1) * PAIR] for i in range(len(xs))]


def _wkv_chunk_kernel(*refs, d_a, nb):
    ps_refs, sga_refs = refs[:nb], refs[nb:2 * nb]
    mu_ref, par_ref, lora_ref, oa_ref, st_ref, carry_ref = refs[2 * nb:]
    c = pl.program_id(1)
    n_pairs = d_a // PAIR
    C = CHUNK

    @pl.when(c == 0)
    def _():
        carry_ref[...] = jnp.zeros_like(carry_ref)
        st_ref[...] = jnp.zeros_like(st_ref)

    xs, lo = [], []
    row = lax.broadcasted_iota(jnp.int32, ps_refs[0].shape, 0)
    for bi in range(nb):
        p = ps_refs[bi][...]
        prev = jnp.where(row == 0, carry_ref[bi], pltpu.roll(p, 1, axis=0))
        carry_ref[bi] = p[C - 1:C, :]
        xs.append(_token_shift(p, prev, mu_ref[...]))
        lo.append(_lora_out(xs[-1], lora_ref[...], d_a))

    block_ones = _block_ones()
    lane_lo = lax.broadcasted_iota(jnp.int32, (C, PAIR), 1) < HEAD
    ti = lax.broadcasted_iota(jnp.int32, (C, PAIR), 0)
    si = lax.broadcasted_iota(jnp.int32, (C, PAIR), 1) % C
    strict_lower = si < ti
    lower2 = (lax.broadcasted_iota(jnp.int32, (C, 2 * PAIR), 1) % C
              <= lax.broadcasted_iota(jnp.int32, (C, 2 * PAIR), 0))
    eye = jnp.where(si == ti, 1.0, 0.0)
    cat0 = lambda x, y: jnp.concatenate([x, y], axis=0)

    r, kf, v, log_decay, kk, a = _pair_vectors(xs, lo, par_ref, d_a, block_ones)
    cl = _cumsum_rows(log_decay)
    cend = [x[C - 1:C, :] for x in cl]
    e_neg = [jnp.exp(-x) for x in cl]
    e_end = [jnp.exp(ce - x) for ce, x in zip(cend, cl)]
    kka = [x * y for x, y in zip(kk, a)]
    stack = lambda xs_: [_stack_heads(x, lane_lo) for x in xs_]
    a_p = [(-x * jnp.exp(c_ - ld)).astype(BF16) for x, c_, ld in zip(kk, cl, log_decay)]
    r_p = [(x * jnp.exp(c_)).astype(BF16) for x, c_ in zip(r, cl)]
    a_s = stack(a_p)
    b_s = stack([x * e for x, e in zip(kka, e_neg)])
    k_s = stack([x * e for x, e in zip(kf, e_neg)])
    v_s = stack(v)
    bh_s = stack([x * e for x, e in zip(kka, e_end)])
    kh_s = stack([x * e for x, e in zip(kf, e_end)])
    pend_col = [jnp.transpose(jnp.broadcast_to(jnp.exp(ce), (PAIR, PAIR))) for ce in cend]

    m1 = [_dot_nt(cat0(x, y), cat0(z, w)) for x, y, z, w in zip(a_p, r_p, b_s, k_s)]
    a_ab = [jnp.where(strict_lower, m[:C, :PAIR], 0.0) for m in m1]
    a_ak = [jnp.where(strict_lower, m[:C, PAIR:], 0.0).astype(BF16) for m in m1]
    a_rbk = [jnp.where(lower2, m[C:], 0.0).astype(BF16) for m in m1]
    g = [_dot(x, y) for x, y in zip(a_ak, v_s)]

    apow = [_dot(x.astype(BF16), _stack_heads(x, lane_lo)) for x in a_ab]
    tinv = [eye + x for x in a_ab]
    n_levels = CHUNK.bit_length() - 1
    for lvl in range(1, n_levels):
        rhs = stack(apow)
        if lvl < n_levels - 1:
            both = [_dot(cat0(x.astype(BF16), t.astype(BF16)), y) for x, t, y in zip(apow, tinv, rhs)]
            apow = [x[:C] for x in both]
            tinv = [t + x[C:] for t, x in zip(tinv, both)]
        else:
            tinv = [t + _dot(t.astype(BF16), y) for t, y in zip(tinv, rhs)]

    g_s = stack(g)
    wx = [_dot(t.astype(BF16), jnp.concatenate([x, y], axis=1)) for t, x, y in zip(tinv, a_s, g_s)]
    seq_pair = [(bi, pi) for bi in range(nb) for pi in range(n_pairs)]
    st = [st_ref[bi, pi] for bi, pi in seq_pair]
    uy = [_dot(cat0(w[:, :PAIR].astype(BF16), x), s_.astype(BF16)) for w, x, s_ in zip(wx, r_p, st)]
    uv_s = [cat0(_stack_heads(x[:C] + w[:, PAIR:], lane_lo), y) for x, w, y in zip(uy, wx, v_s)]
    st_upd = [_dot_tn(cat0(x, y), z) for x, y, z in zip(bh_s, kh_s, uv_s)]
    y_uv = [_dot(x, z) for x, z in zip(a_rbk, uv_s)]
    for i, (bi, pi) in enumerate(seq_pair):
        st_ref[bi, pi] = pend_col[i] * st[i] + st_upd[i]
    y = [x[C:] + z for x, z in zip(uy, y_uv)]
    o = _head_norm_bonus(y, r, kf, v, par_ref, d_a, block_ones)
    sls = _pair_slices(d_a)
    for oi, (bi, pi) in zip(o, seq_pair):
        oa_ref[bi, :, sls[pi]] = (oi * sga_refs[bi][:, sls[pi]]).astype(oa_ref.dtype)


def _wkv_prompt(p_shift, gates, mu, par, lora_w, batch, seq, d_a):
    n_chunks = seq // CHUNK
    shift_w = p_shift.shape[1]
    n_pairs = d_a // PAIR
    nb = WKV_SEQS_PER_STEP if batch % WKV_SEQS_PER_STEP == 0 else 1
    kern = functools.partial(_wkv_chunk_kernel, d_a=d_a, nb=nb)
    seq_rows = lambda bi: (lambda b, c: ((b * nb + bi) * n_chunks + c, 0))
    oa, st = pl.pallas_call(
        kern,
        out_shape=(jax.ShapeDtypeStruct((batch, seq, d_a), BF16),
                   jax.ShapeDtypeStruct((batch, n_pairs, PAIR, PAIR), F32)),
        grid=(batch // nb, n_chunks),
        in_specs=([pl.BlockSpec((CHUNK, shift_w), seq_rows(bi)) for bi in range(nb)]
                  + [pl.BlockSpec((CHUNK, d_a), seq_rows(bi)) for bi in range(nb)]
                  + [pl.BlockSpec((1, shift_w), lambda b, c: (0, 0)),
                     pl.BlockSpec((8, d_a), lambda b, c: (0, 0)),
                     pl.BlockSpec((2 * LORA, 2 * d_a), lambda b, c: (0, 0))]),
        out_specs=(pl.BlockSpec((nb, CHUNK, d_a), lambda b, c: (b, c, 0)),
                   pl.BlockSpec((nb, n_pairs, PAIR, PAIR), lambda b, c: (b, 0, 0, 0))),
        scratch_shapes=[pltpu.VMEM((nb, 1, shift_w), F32)],
        compiler_params=_params(2),
        name="wkv_chunked",
    )(*([p_shift] * nb + [gates] * nb), mu, par, lora_w)
    return oa.reshape(batch * seq, d_a), st


def _wkv_step_kernel(ps_ref, prev_ref, sga_ref, mu_ref, par_ref, lora_ref, s_ref, oa_ref, so_ref,
                     vec_ref, y_ref, *, d_a, bb, group):
    n_heads = d_a // HEAD
    half = HEAD // 2
    xs = _token_shift(ps_ref[...], prev_ref[...], mu_ref[...])
    lo = _lora_out(xs, lora_ref[...], d_a)
    block_ones = _block_ones()
    sls = _pair_slices(d_a)
    r, kf, v, log_decay, kk, a = _pair_vectors([xs], [lo], par_ref, d_a, block_ones)

    lane_lo = lax.broadcasted_iota(jnp.int32, (bb, PAIR), 1) < HEAD
    for pi in range(len(sls)):
        v_hi = v[pi].astype(BF16).astype(F32)
        vecs = (-kk[pi], jnp.exp(log_decay[pi]), kk[pi] * a[pi], kf[pi], r[pi], v_hi, v[pi] - v_hi)
        for i, x in enumerate(vecs):
            swapped = pltpu.roll(x, HEAD, axis=1)
            vec_ref[i, 2 * pi] = jnp.where(lane_lo, x, swapped)
            vec_ref[i, 2 * pi + 1] = jnp.where(lane_lo, swapped, x)

    rows = n_heads * half
    m_in_head = lax.broadcasted_iota(jnp.int32, (rows, PAIR), 0) % half
    lane = lax.broadcasted_iota(jnp.int32, (rows, PAIR), 1)
    diag = (lane == 2 * m_in_head) | (lane == 2 * m_in_head + HEAD + 1)

    def expand(i, b):
        return jnp.concatenate(
            [jnp.broadcast_to(vec_ref[i, h, b:b + 1, :], (half, PAIR)) for h in range(n_heads)], axis=0)

    for g0 in range(0, bb, group):
        bs = range(g0, g0 + group)
        s = [s_ref[b] for b in bs]
        sa = [_dot((x * expand(0, b)).astype(BF16), block_ones) for x, b in zip(s, bs)]
        vb = [_dot(jnp.where(diag, expand(5, b), 0.0).astype(BF16), block_ones)
              + _dot(jnp.where(diag, expand(6, b), 0.0).astype(BF16), block_ones) for b in bs]
        s_new = [x * expand(1, b) + y * expand(2, b) + z * expand(3, b) for x, y, z, b in zip(s, sa, vb, bs)]
        for x, b in zip(s_new, bs):
            so_ref[b] = x
        yb = [_head_sum(x * expand(4, b), block_ones, split=True) for x, b in zip(s_new, bs)]
        for x, b in zip(yb, bs):
            yh = jnp.sum(jnp.where(diag, x, 0.0).reshape(n_heads, half, PAIR), axis=1)
            y_ref[b] = yh + pltpu.roll(yh, HEAD, axis=1)

    y = [jnp.where(lane_lo, y_ref[:, 2 * pi, :], y_ref[:, 2 * pi + 1, :]) for pi in range(len(sls))]
    o = _head_norm_bonus(y, r, kf, v, par_ref, d_a, block_ones)
    for oi, sl in zip(o, sls):
        oa_ref[:, sl] = (oi * sga_ref[:, sl]).astype(oa_ref.dtype)


def _wkv_sample(p_shift, shift_prev, gates, mu, par, lora_w, state, d_a, bb, row0):
    batch, shift_w = shift_prev.shape
    n_heads = d_a // HEAD
    rows = n_heads * HEAD // 2
    blk0 = row0 // bb
    kern = functools.partial(_wkv_step_kernel, d_a=d_a, bb=bb, group=min(bb, 4))
    oa, new_state = pl.pallas_call(
        kern,
        out_shape=(jax.ShapeDtypeStruct((batch, d_a), BF16),
                   jax.ShapeDtypeStruct((batch, rows, PAIR), F32)),
        grid=(batch // bb,),
        in_specs=[pl.BlockSpec((bb, shift_w), lambda i: (blk0 + i, 0)),
                  pl.BlockSpec((bb, shift_w), lambda i: (i, 0)),
                  pl.BlockSpec((bb, d_a), lambda i: (blk0 + i, 0)),
                  pl.BlockSpec((1, shift_w), lambda i: (0, 0)),
                  pl.BlockSpec((8, d_a), lambda i: (0, 0)),
                  pl.BlockSpec((2 * LORA, 2 * d_a), lambda i: (0, 0)),
                  pl.BlockSpec((bb, rows, PAIR), lambda i: (i, 0, 0))],
        out_specs=(pl.BlockSpec((bb, d_a), lambda i: (i, 0)),
                   pl.BlockSpec((bb, rows, PAIR), lambda i: (i, 0, 0))),
        scratch_shapes=[pltpu.VMEM((7, n_heads, bb, PAIR), F32), pltpu.VMEM((bb, n_heads, PAIR), F32)],
        compiler_params=_params(1),
        name="wkv_step",
    )(p_shift, shift_prev, gates, mu, par, lora_w, state.reshape(batch, rows, PAIR))
    return oa, new_state.reshape(state.shape)


CONV_PAD = 32
CONV_STRIP = 16


def _conv_prompt_kernel(u_ref, w_ref, b_ref, o_ref, buf_ref, c_ref, *, tt, taps):
    t = pl.program_id(1)
    sub = buf_ref.shape[1]
    lanes = [slice(s * 128, (s + 1) * 128) for s in range(sub)]

    @pl.when(t == 0)
    def _():
        buf_ref[0:CONV_PAD] = jnp.zeros((CONV_PAD,) + buf_ref.shape[1:], F32)

    u = u_ref[...]
    buf_ref[CONV_PAD:CONV_PAD + tt] = jnp.swapaxes(jnp.stack([u[:, ls] for ls in lanes], axis=0), 0, 1)
    off = CONV_PAD - (taps - 1)
    w = [w_ref[k] for k in range(taps)]
    bias = b_ref[0]

    def strip(s, carry):
        t0 = s * CONV_STRIP
        acc = [bias] * CONV_STRIP
        for i in range(CONV_STRIP + taps - 1):
            x = buf_ref[off + t0 + i]
            for j in range(CONV_STRIP):
                if 0 <= i - j < taps:
                    acc[j] = acc[j] + w[i - j] * x
        for j in range(CONV_STRIP):
            c_ref[t0 + j] = acc[j]
        return carry

    lax.fori_loop(0, tt // CONV_STRIP, strip, 0)
    buf_ref[0:CONV_PAD] = buf_ref[tt:tt + CONV_PAD]
    c = jnp.swapaxes(c_ref[...], 0, 1)
    for s, ls in enumerate(lanes):
        o_ref[:, ls] = c[s]


def _conv_prompt(u, conv_w, conv_b, batch, seq, tt):
    d_b = u.shape[1]
    taps = conv_w.shape[0]
    nt = seq // tt
    sub = d_b // 128
    assert taps - 1 <= CONV_PAD and tt % CONV_STRIP == 0 and tt >= CONV_PAD
    kern = functools.partial(_conv_prompt_kernel, tt=tt, taps=taps)
    return pl.pallas_call(
        kern,
        out_shape=jax.ShapeDtypeStruct((batch * seq, d_b), F32),
        grid=(batch, nt),
        in_specs=[pl.BlockSpec((tt, d_b), lambda b, t: (b * nt + t, 0)),
                  pl.BlockSpec((taps, sub, 128), lambda b, t: (0, 0, 0)),
                  pl.BlockSpec((1, sub, 128), lambda b, t: (0, 0, 0))],
        out_specs=pl.BlockSpec((tt, d_b), lambda b, t: (b * nt + t, 0)),
        scratch_shapes=[pltpu.VMEM((tt + CONV_PAD, sub, 128), F32), pltpu.VMEM((tt, sub, 128), F32)],
        compiler_params=_params(2),
        name="conv_prompt",
    )(u, conv_w.reshape(taps, sub, 128), conv_b.reshape(1, sub, 128))


def _conv_step_kernel(u_ref, prev_ref, w_ref, b_ref, o_ref, hist_ref, *, taps):
    u = u_ref[...]
    c = b_ref[...] + w_ref[taps - 1:taps, :] * u
    for k in range(taps - 1):
        c = c + w_ref[k:k + 1, :] * prev_ref[:, k, :]
    o_ref[...] = c
    hist_ref[:, 0:taps - 2, :] = prev_ref[:, 1:taps - 1, :]
    hist_ref[:, taps - 2, :] = u


def _conv_sample(u, conv_prev, conv_w, conv_b, bb, row0):
    batch = conv_prev.shape[0]
    d_b = u.shape[1]
    taps = conv_w.shape[0]
    blk0 = row0 // bb
    kern = functools.partial(_conv_step_kernel, taps=taps)
    return pl.pallas_call(
        kern,
        out_shape=(jax.ShapeDtypeStruct((batch, d_b), F32),
                   jax.ShapeDtypeStruct(conv_prev.shape, F32)),
        grid=(batch // bb,),
        in_specs=[pl.BlockSpec((bb, d_b), lambda i: (blk0 + i, 0)),
                  pl.BlockSpec((bb, taps - 1, d_b), lambda i: (i, 0, 0)),
                  pl.BlockSpec((taps, d_b), lambda i: (0, 0)),
                  pl.BlockSpec((1, d_b), lambda i: (0, 0))],
        out_specs=(pl.BlockSpec((bb, d_b), lambda i: (i, 0)),
                   pl.BlockSpec((bb, taps - 1, d_b), lambda i: (i, 0, 0))),
        compiler_params=_params(1),
        name="conv_step",
    )(u, conv_prev, conv_w, conv_b.reshape(1, d_b))


def _tail_kernel(oa_ref, c_ref, gb_ref, sga_ref, sgb_ref, x_ref, p_ref, cln_ref, wa_ref, wb_ref, wout_ref,
                 wpg_ref, wple_ref, fg_ref, o_ref):
    c = c_ref[...]
    mean = jnp.mean(c, axis=-1, keepdims=True)
    dc = c - mean
    var = jnp.mean(dc * dc, axis=-1, keepdims=True)
    cf = dc * lax.rsqrt(var + LN_EPS) * cln_ref[0:1, :] + cln_ref[1:2, :]
    cb = (cf * _sigmoid(cf) * gb_ref[...]).astype(BF16)
    m = sga_ref[...] * _dot(oa_ref[...], wa_ref[...]) + sgb_ref[...] * _dot(cb, wb_ref[...])
    h = x_ref[...] + _dot(m.astype(BF16), wout_ref[...])
    gate = _sigmoid(_dot(h.astype(BF16), wpg_ref[...]))
    h = h + gate * _dot(p_ref[...].astype(BF16), wple_ref[...])
    ms = jnp.mean(h * h, axis=-1, keepdims=True)
    o_ref[...] = h * lax.rsqrt(ms + RMS_EPS) * fg_ref[...]


def _tail(oa, c, gates, x, p, cln, wa, wb, wout, wpg, wple, fg, tm, gate_b_block, merge_block, row0=0):
    m, d = x.shape
    d_a = oa.shape[1]
    d_b = c.shape[1]
    ple = p.shape[1]
    blk0 = row0 // tm
    row = lambda i: (i, 0)
    const = lambda i: (0, 0)
    resident = lambda shape: pl.BlockSpec(shape, const, pipeline_mode=pl.Buffered(1))
    return pl.pallas_call(
        _tail_kernel,
        out_shape=jax.ShapeDtypeStruct((m, d), F32),
        grid=(m // tm,),
        in_specs=[pl.BlockSpec((tm, d_a), row), pl.BlockSpec((tm, d_b), row),
                  pl.BlockSpec((tm, d_b), lambda i: (blk0 + i, gate_b_block)),
                  pl.BlockSpec((tm, d), lambda i: (blk0 + i, merge_block)),
                  pl.BlockSpec((tm, d), lambda i: (blk0 + i, merge_block + 1)),
                  pl.BlockSpec((tm, d), row), pl.BlockSpec((tm, ple), row),
                  resident((2, d_b)),
                  resident((d_a, d)), resident((d_b, d)), resident((d, d)), resident((d, d)),
                  resident((ple, d)), resident((1, d))],
        out_specs=pl.BlockSpec((tm, d), row),
        compiler_params=_params(1),
        name="tail",
    )(oa, c, gates, gates, gates, x, p, cln, wa, wb, wout, wpg, wple, fg)


def _largest_tile(n, cap, align):
    t = min(n, cap)
    while n % t or t % align:
        t -= 1
    return t


def kernel(x_prompt, x_sample, state_shift, state_wkv, state_conv, p_prompt, p_sample, norm_g, w_in,
           shift_mu, w0, w_lora_b, a0, a_lora_b, k_k, k_a, r_k, lnx_g, lnx_b, w_proj_a, conv_w, conv_b,
           cln_g, cln_b, w_proj_b, w_out, w_ple, w_ple_gate, final_g):
    depth = w_in.shape[0]
    batch, seq, d = x_prompt.shape
    dec_batch, dec_seq, _ = x_sample.shape
    d_a = w_proj_a.shape[1]
    d_b = w_proj_b.shape[1]
    shift_w = shift_mu.shape[1]
    n_heads = d_a // HEAD
    n_pairs = d_a // PAIR
    taps = conv_w.shape[1]
    assert depth == 1 and dec_seq == 1 and d_a == d_b and 2 * d_a == d
    assert shift_w == 3 * d_a + 2 * LORA and d_a % PAIR == 0 and seq % CHUNK == 0

    o1 = shift_w
    o2 = o1 + d_a
    o3 = o2 + 2 * d_b
    o4 = o3 + d_b
    w = w_in[0]
    lora_w = jnp.zeros((2 * LORA, 2 * d_a), F32)
    lora_w = lora_w.at[:LORA, :d_a].set(w_lora_b[0]).at[LORA:, d_a:].set(a_lora_b[0]).astype(BF16)
    zeros_a = jnp.zeros((d_a,), F32)
    par = jnp.stack([w0[0], a0[0], k_k[0], k_a[0], r_k[0].reshape(d_a), lnx_g[0], lnx_b[0], zeros_a])
    cln = jnp.stack([cln_g[0], cln_b[0]])
    mu = shift_mu
    g_in = norm_g[0].reshape(1, d)
    fg = final_g.reshape(1, d)
    gate_b_block = d_a // d_b
    merge_block = (d_a + d_b) // d

    m_p = batch * seq
    m_all = m_p + dec_batch
    x2 = x_prompt.reshape(m_p, d)
    xs2 = x_sample.reshape(dec_batch, d)
    xn = _rmsnorm_bf16(x2, xs2, g_in, _largest_tile(m_p, 1024, 8))
    tm = _largest_tile(m_all, 1100, 16)
    tn_shift = _largest_tile(o1, 1024, 128)
    p_shift = _project(xn, w, lambda j: j * tn_shift, o1, tm, tn_shift)
    gate_cols = lambda j: jnp.where(j == 0, o1, o3 + (j - 1) * d_a)
    gates = _project(xn, w, gate_cols, d_a + d_b + 2 * d, tm, d_a, n_silu_blocks=2)
    u, (wa, wb, wout, wpg, wple) = _project_glu(
        xn, w, o2, o2 + d_b, d_b, tm, _largest_tile(d_b, 512, 128),
        casts=[w_proj_a[0], w_proj_b[0], w_out[0], w_ple_gate[0], w_ple[0]])

    oa, st = _wkv_prompt(p_shift, gates, mu, par, lora_w, batch, seq, d_a)
    c = _conv_prompt(u, conv_w[0], conv_b[0], batch, seq, _largest_tile(seq, 256, CONV_STRIP))
    y_prompt = _tail(oa, c, gates, x2, p_prompt[0].reshape(m_p, -1), cln, wa, wb, wout, wpg, wple, fg,
                     _largest_tile(m_p, 256, 8), gate_b_block, merge_block).reshape(batch, seq, d)
    new_shift_p = jnp.concatenate([p_shift[(b + 1) * seq - 1:(b + 1) * seq] for b in range(batch)])[None]
    st = st.reshape(batch, n_pairs, 2, HEAD, 2, HEAD)
    new_wkv_p = jnp.stack([st[:, :, 0, :, 0, :], st[:, :, 1, :, 1, :]], axis=2)
    new_wkv_p = jnp.swapaxes(new_wkv_p.reshape(batch, n_heads, HEAD, HEAD), -1, -2)[None]
    new_conv_p = jnp.stack([u[(b + 1) * seq - (taps - 1):(b + 1) * seq] for b in range(batch)])[None]

    bb = _largest_tile(dec_batch, 8, 8)
    tm_s = _largest_tile(dec_batch, 256, 8)
    assert m_p % bb == 0 and m_p % tm_s == 0
    oa_s, new_wkv_s = _wkv_sample(p_shift, state_shift[0], gates, mu, par, lora_w, state_wkv[0], d_a, bb, m_p)
    c_s, new_conv_s = _conv_sample(u, state_conv[0], conv_w[0], conv_b[0], bb, m_p)
    y_sample = _tail(oa_s, c_s, gates, xs2, p_sample[0].reshape(dec_batch, -1), cln, wa, wb, wout, wpg,
                     wple, fg, tm_s, gate_b_block, merge_block, row0=m_p).reshape(dec_batch, 1, d)
    new_shift_s = p_shift[m_p:][None]
    new_conv_s = new_conv_s[None]

    return (y_prompt, y_sample, new_shift_p, new_wkv_p, new_conv_p, new_shift_s, new_wkv_s[None],
            new_conv_s)
```

```python
import functools
import math

import jax
import jax.numpy as jnp
from jax import lax
from jax.experimental import pallas as pl
from jax.experimental.pallas import tpu as pltpu

F32 = jnp.float32
BF16 = jnp.bfloat16

HEAD = 64
PAIR = 2 * HEAD
LORA = 64
CHUNK = 64
WKV_SEQS_PER_STEP = 4
RMS_EPS = 1e-6
LN_EPS = 1e-5
GN_EPS = 64e-5
LOG_DECAY_SCALE = math.exp(-0.5)
VMEM_LIMIT = 56 * 1024 * 1024


def _params(n_axes, vmem=VMEM_LIMIT):
    return pltpu.CompilerParams(dimension_semantics=("arbitrary",) * n_axes, vmem_limit_bytes=vmem)


def _sigmoid(x):
    return 1.0 / (1.0 + jnp.exp(-x))


def _dot(a, b):
    return jnp.dot(a, b, preferred_element_type=F32)


def _dot_nt(a, b):
    return lax.dot_general(a, b, (((1,), (1,)), ((), ())), preferred_element_type=F32)


def _dot_tn(a, b):
    return lax.dot_general(a, b, (((0,), (0,)), ((), ())), preferred_element_type=F32)


def _rmsnorm_kernel(xp_ref, xs_ref, g_ref, w_ref, o_ref, ol_ref, wb_ref, *, n_prompt_blocks):
    i = pl.program_id(0)

    @pl.when(i == 0)
    def _():
        wb_ref[...] = w_ref[...].astype(BF16)

    def norm(x):
        ms = jnp.mean(x * x, axis=-1, keepdims=True)
        return (x * lax.rsqrt(ms + RMS_EPS) * g_ref[...]).astype(o_ref.dtype)

    @pl.when(i < n_prompt_blocks)
    def _():
        xn = norm(xp_ref[...])
        o_ref[...] = xn
        ol_ref[...] = _dot(xn, wb_ref[...])

    @pl.when(i == n_prompt_blocks)
    def _():
        xn = norm(xs_ref[...])
        o_ref[0:xs_ref.shape[0], :] = xn
        ol_ref[0:xs_ref.shape[0], :] = _dot(xn, wb_ref[...])


def _rmsnorm_bf16(x_prompt, x_sample, g, w, col, n_cols, tm):
    m_p, d = x_prompt.shape
    m_s = x_sample.shape[0]
    n_blocks = m_p // tm
    assert m_p % tm == 0 and m_s <= tm and col % 128 == 0
    kern = functools.partial(_rmsnorm_kernel, n_prompt_blocks=n_blocks)
    return pl.pallas_call(
        kern,
        out_shape=(jax.ShapeDtypeStruct((m_p + m_s, d), BF16),
                   jax.ShapeDtypeStruct((m_p + m_s, n_cols), F32)),
        grid=(n_blocks + 1,),
        in_specs=[pl.BlockSpec((tm, d), lambda i: (jnp.minimum(i, n_blocks - 1), 0)),
                  pl.BlockSpec((m_s, d), lambda i: (0, 0)),
                  pl.BlockSpec((1, d), lambda i: (0, 0)),
                  pl.BlockSpec((pl.Element(d), pl.Element(n_cols)), lambda i: (0, col))],
        out_specs=(pl.BlockSpec((tm, d), lambda i: (i, 0)),
                   pl.BlockSpec((tm, n_cols), lambda i: (i, 0))),
        scratch_shapes=[pltpu.VMEM((d, n_cols), BF16)],
        compiler_params=_params(1),
        name="rmsnorm",
    )(x_prompt, x_sample, g, w)


def _proj_kernel(x_ref, w_ref, o_ref, wb_ref, *, n_silu_blocks):
    @pl.when(pl.program_id(1) == 0)
    def _():
        wb_ref[...] = w_ref[...].astype(BF16)

    y = _dot(x_ref[...], wb_ref[...])
    if n_silu_blocks is None:
        o_ref[...] = y
    else:
        s = _sigmoid(y)
        o_ref[...] = jnp.where(pl.program_id(0) < n_silu_blocks, y * s, s)


def _project(xn, w, col_start, n_out, tm, tn, n_silu_blocks=None):
    m, d = xn.shape
    kern = functools.partial(_proj_kernel, n_silu_blocks=n_silu_blocks)
    return pl.pallas_call(
        kern,
        out_shape=jax.ShapeDtypeStruct((m, n_out), F32),
        grid=(n_out // tn, m // tm),
        in_specs=[pl.BlockSpec((tm, d), lambda j, i: (i, 0)),
                  pl.BlockSpec((pl.Element(d), pl.Element(tn)),
                               lambda j, i: (0, pl.multiple_of(col_start(j), 128)))],
        out_specs=pl.BlockSpec((tm, tn), lambda j, i: (i, j)),
        scratch_shapes=[pltpu.VMEM((d, tn), BF16)],
        compiler_params=_params(2),
        name="in_proj",
    )(xn, w)


def _glu_kernel(*refs, n_casts):
    x_ref, wa_ref, wb_ref = refs[:3]
    cast_in = refs[3:3 + n_casts]
    o_ref = refs[3 + n_casts]
    cast_out = refs[4 + n_casts:4 + 2 * n_casts]
    wab_ref, wbb_ref = refs[4 + 2 * n_casts:]

    @pl.when(pl.program_id(1) == 0)
    def _():
        wab_ref[...] = wa_ref[...].astype(BF16)
        wbb_ref[...] = wb_ref[...].astype(BF16)

    x = x_ref[...]
    o_ref[...] = _dot(x, wab_ref[...]) * _sigmoid(_dot(x, wbb_ref[...]))
    for src, dst in zip(cast_in, cast_out):
        dst[...] = src[...].astype(BF16)


def _project_glu(xn, w, col_a, col_b, n_out, tm, tn, casts):
    m, d = xn.shape
    nj, ni = n_out // tn, m // tm
    steps = nj * ni
    for a in casts:
        assert a.shape[0] % (16 * steps) == 0, (a.shape, steps)
    step_rows = lambda a: pl.BlockSpec((a.shape[0] // steps, a.shape[1]), lambda j, i: (j * ni + i, 0))
    kern = functools.partial(_glu_kernel, n_casts=len(casts))
    out = pl.pallas_call(
        kern,
        out_shape=[jax.ShapeDtypeStruct((m, n_out), F32)]
        + [jax.ShapeDtypeStruct(a.shape, BF16) for a in casts],
        grid=(nj, ni),
        in_specs=[pl.BlockSpec((tm, d), lambda j, i: (i, 0)),
                  pl.BlockSpec((pl.Element(d), pl.Element(tn)),
                               lambda j, i: (0, pl.multiple_of(col_a + j * tn, 128))),
                  pl.BlockSpec((pl.Element(d), pl.Element(tn)),
                               lambda j, i: (0, pl.multiple_of(col_b + j * tn, 128)))]
        + [step_rows(a) for a in casts],
        out_specs=[pl.BlockSpec((tm, tn), lambda j, i: (i, j))] + [step_rows(a) for a in casts],
        scratch_shapes=[pltpu.VMEM((d, tn), BF16), pltpu.VMEM((d, tn), BF16)],
        compiler_params=_params(2),
        name="in_proj_glu",
    )(xn, w, w, *casts)
    return out[0], out[1:]


def _head_sum(x, block_ones, split=False):
    hi = x.astype(BF16)
    if not split:
        return _dot(hi, block_ones)
    lo = (x - hi.astype(F32)).astype(BF16)
    return _dot(hi, block_ones) + _dot(lo, block_ones)


def _head_sums(xs, block_ones, split=False):
    rows = xs[0].shape[0]
    s = _head_sum(jnp.concatenate(xs, axis=0), block_ones, split)
    return [s[i * rows:(i + 1) * rows] for i in range(len(xs))]


def _block_ones():
    ri = lax.broadcasted_iota(jnp.int32, (PAIR, PAIR), 0)
    ci = lax.broadcasted_iota(jnp.int32, (PAIR, PAIR), 1)
    return jnp.where((ri < HEAD) == (ci < HEAD), 1.0, 0.0).astype(BF16)


def _token_shift(p, prev, mu):
    return p + mu * (prev - p)


def _lora_out(z, lora_w):
    lane = lax.broadcasted_iota(jnp.int32, z.shape, 1)
    z = jnp.where(lane < LORA, jnp.tanh(z), z)
    return _dot(z.astype(BF16), lora_w)


def _pair_slices(d_a):
    return [slice(i * PAIR, (i + 1) * PAIR) for i in range(d_a // PAIR)]


def _pair_vectors(xs_list, lo_list, par_ref, d_a, block_ones):
    sls = _pair_slices(d_a)
    shifted = lambda sl, off: slice(off + sl.start, off + sl.stop)
    items = [(xs, lo, sl) for xs, lo in zip(xs_list, lo_list) for sl in sls]
    r = [xs[:, sl] for xs, _, sl in items]
    k = [xs[:, shifted(sl, d_a)] for xs, _, sl in items]
    v = [xs[:, shifted(sl, 2 * d_a)] for xs, _, sl in items]
    kkr = [ki * par_ref[2:3, sl] for ki, (_, _, sl) in zip(k, items)]
    ss = _head_sums([x * x for x in kkr], block_ones)
    kk = [x * jnp.minimum(lax.rsqrt(s), 1e12) for x, s in zip(kkr, ss)]
    log_decay = [-LOG_DECAY_SCALE * _sigmoid(par_ref[0:1, sl] + lo[:, sl]) for _, lo, sl in items]
    a = [_sigmoid(par_ref[1:2, sl] + lo[:, shifted(sl, d_a)]) for _, lo, sl in items]
    kf = [ki * (1.0 + (ai - 1.0) * par_ref[3:4, sl]) for ki, ai, (_, _, sl) in zip(k, a, items)]
    return r, kf, v, log_decay, kk, a


def _head_norm_bonus(y, r, kf, v, par_ref, d_a, block_ones):
    sls = _pair_slices(d_a) * (len(y) // (d_a // PAIR))
    mean = [s * (1.0 / HEAD) for s in _head_sums(y, block_ones, split=True)]
    d = [x - m for x, m in zip(y, mean)]
    var = [s * (1.0 / HEAD) for s in _head_sums([x * x for x in d], block_ones)]
    rk = _head_sums([ri * ki * par_ref[4:5, sl] for ri, ki, sl in zip(r, kf, sls)], block_ones)
    return [x * lax.rsqrt(vr + GN_EPS) * par_ref[5:6, sl] + par_ref[6:7, sl] + s * vi
            for x, vr, s, vi, sl in zip(d, var, rk, v, sls)]


def _stack_heads(x, lane_lo):
    xb = x.astype(BF16)
    zero = jnp.zeros_like(xb)
    return jnp.concatenate([jnp.where(lane_lo, xb, zero), jnp.where(lane_lo, zero, xb)], axis=0)


def _cumsum_rows(xs):
    n = xs[0].shape[0]
    tri = (lax.broadcasted_iota(jnp.int32, (n, n), 1) <= lax.broadcasted_iota(jnp.int32, (n, n), 0))
    tri = jnp.where(tri, 1.0, 0.0).astype(BF16)
    x = jnp.concatenate(xs, axis=1)
    hi = x.astype(BF16)
    lo = (x - hi.astype(F32)).astype(BF16)
    s = _dot(tri, hi) + _dot(tri, lo)
    return [s[:, i * PAIR:(i + 1) * PAIR] for i in range(len(xs))]


def _wkv_chunk_kernel(*refs, d_a, nb):
    ps_refs, pl_refs, sga_refs = refs[:nb], refs[nb:2 * nb], refs[2 * nb:3 * nb]
    mu_ref, par_ref, lora_ref, oa_ref, st_ref, carry_ref = refs[3 * nb:]
    c = pl.program_id(1)
    n_pairs = d_a // PAIR
    C = CHUNK
    n_rkv = 3 * d_a

    @pl.when(c == 0)
    def _():
        carry_ref[...] = jnp.zeros_like(carry_ref)
        st_ref[...] = jnp.zeros_like(st_ref)

    def shifted(p, prev_row, mu):
        row = lax.broadcasted_iota(jnp.int32, p.shape, 0)
        return _token_shift(p, jnp.where(row == 0, prev_row, pltpu.roll(p, 1, axis=0)), mu)

    xs, lo = [], []
    for bi in range(nb):
        p, pz = ps_refs[bi][...], pl_refs[bi][...]
        xs.append(shifted(p, carry_ref[bi, :, :n_rkv], mu_ref[:, :n_rkv]))
        lo.append(_lora_out(shifted(pz, carry_ref[bi, :, n_rkv:], mu_ref[:, n_rkv:]), lora_ref[...]))
        carry_ref[bi, :, :n_rkv] = p[C - 1:C, :]
        carry_ref[bi, :, n_rkv:] = pz[C - 1:C, :]

    block_ones = _block_ones()
    lane_lo = lax.broadcasted_iota(jnp.int32, (C, PAIR), 1) < HEAD
    ti = lax.broadcasted_iota(jnp.int32, (C, PAIR), 0)
    si = lax.broadcasted_iota(jnp.int32, (C, PAIR), 1) % C
    strict_lower = si < ti
    lower2 = (lax.broadcasted_iota(jnp.int32, (C, 2 * PAIR), 1) % C
              <= lax.broadcasted_iota(jnp.int32, (C, 2 * PAIR), 0))
    eye = jnp.where(si == ti, 1.0, 0.0)
    cat0 = lambda x, y: jnp.concatenate([x, y], axis=0)

    r, kf, v, log_decay, kk, a = _pair_vectors(xs, lo, par_ref, d_a, block_ones)
    cl = _cumsum_rows(log_decay)
    cend = [x[C - 1:C, :] for x in cl]
    e_neg = [jnp.exp(-x) for x in cl]
    e_end = [jnp.exp(ce - x) for ce, x in zip(cend, cl)]
    kka = [x * y for x, y in zip(kk, a)]
    stack = lambda xs_: [_stack_heads(x, lane_lo) for x in xs_]
    a_p = [(-x * jnp.exp(c_ - ld)).astype(BF16) for x, c_, ld in zip(kk, cl, log_decay)]
    r_p = [(x * jnp.exp(c_)).astype(BF16) for x, c_ in zip(r, cl)]
    a_s = stack(a_p)
    b_s = stack([x * e for x, e in zip(kka, e_neg)])
    k_s = stack([x * e for x, e in zip(kf, e_neg)])
    v_s = stack(v)
    bh_s = stack([x * e for x, e in zip(kka, e_end)])
    kh_s = stack([x * e for x, e in zip(kf, e_end)])
    pend_col = [jnp.transpose(jnp.broadcast_to(jnp.exp(ce), (PAIR, PAIR))) for ce in cend]

    m1 = [_dot_nt(cat0(x, y), cat0(z, w)) for x, y, z, w in zip(a_p, r_p, b_s, k_s)]
    a_ab = [jnp.where(strict_lower, m[:C, :PAIR], 0.0) for m in m1]
    a_ak = [jnp.where(strict_lower, m[:C, PAIR:], 0.0).astype(BF16) for m in m1]
    a_rbk = [jnp.where(lower2, m[C:], 0.0).astype(BF16) for m in m1]
    g = [_dot(x, y) for x, y in zip(a_ak, v_s)]

    apow = [_dot(x.astype(BF16), _stack_heads(x, lane_lo)) for x in a_ab]
    tinv = [eye + x for x in a_ab]
    n_levels = CHUNK.bit_length() - 1
    for lvl in range(1, n_levels):
        rhs = stack(apow)
        if lvl < n_levels - 1:
            both = [_dot(cat0(x.astype(BF16), t.astype(BF16)), y) for x, t, y in zip(apow, tinv, rhs)]
            apow = [x[:C] for x in both]
            tinv = [t + x[C:] for t, x in zip(tinv, both)]
        else:
            tinv = [t + _dot(t.astype(BF16), y) for t, y in zip(tinv, rhs)]

    g_s = stack(g)
    wx = [_dot(t.astype(BF16), jnp.concatenate([x, y], axis=1)) for t, x, y in zip(tinv, a_s, g_s)]
    seq_pair = [(bi, pi) for bi in range(nb) for pi in range(n_pairs)]
    st = [st_ref[bi, pi] for bi, pi in seq_pair]
    uy = [_dot(cat0(w[:, :PAIR].astype(BF16), x), s_.astype(BF16)) for w, x, s_ in zip(wx, r_p, st)]
    uv_s = [cat0(_stack_heads(x[:C] + w[:, PAIR:], lane_lo), y) for x, w, y in zip(uy, wx, v_s)]
    st_upd = [_dot_tn(cat0(x, y), z) for x, y, z in zip(bh_s, kh_s, uv_s)]
    y_uv = [_dot(x, z) for x, z in zip(a_rbk, uv_s)]
    for i, (bi, pi) in enumerate(seq_pair):
        st_ref[bi, pi] = pend_col[i] * st[i] + st_upd[i]
    y = [x[C:] + z for x, z in zip(uy, y_uv)]
    o = _head_norm_bonus(y, r, kf, v, par_ref, d_a, block_ones)
    sls = _pair_slices(d_a)
    for oi, (bi, pi) in zip(o, seq_pair):
        oa_ref[bi, :, sls[pi]] = (oi * sga_refs[bi][:, sls[pi]]).astype(oa_ref.dtype)


def _wkv_prompt(p_rkv, p_lora, gates, mu, par, lora_w, batch, seq, d_a):
    n_chunks = seq // CHUNK
    shift_w = mu.shape[1]
    n_pairs = d_a // PAIR
    nb = WKV_SEQS_PER_STEP if batch % WKV_SEQS_PER_STEP == 0 else 1
    kern = functools.partial(_wkv_chunk_kernel, d_a=d_a, nb=nb)
    seq_rows = lambda bi: (lambda b, c: ((b * nb + bi) * n_chunks + c, 0))
    oa, st = pl.pallas_call(
        kern,
        out_shape=(jax.ShapeDtypeStruct((batch, seq, d_a), BF16),
                   jax.ShapeDtypeStruct((batch, n_pairs, PAIR, PAIR), F32)),
        grid=(batch // nb, n_chunks),
        in_specs=([pl.BlockSpec((CHUNK, 3 * d_a), seq_rows(bi)) for bi in range(nb)]
                  + [pl.BlockSpec((CHUNK, 2 * LORA), seq_rows(bi)) for bi in range(nb)]
                  + [pl.BlockSpec((CHUNK, d_a), seq_rows(bi)) for bi in range(nb)]
                  + [pl.BlockSpec((1, shift_w), lambda b, c: (0, 0)),
                     pl.BlockSpec((8, d_a), lambda b, c: (0, 0)),
                     pl.BlockSpec((2 * LORA, 2 * d_a), lambda b, c: (0, 0))]),
        out_specs=(pl.BlockSpec((nb, CHUNK, d_a), lambda b, c: (b, c, 0)),
                   pl.BlockSpec((nb, n_pairs, PAIR, PAIR), lambda b, c: (b, 0, 0, 0))),
        scratch_shapes=[pltpu.VMEM((nb, 1, shift_w), F32)],
        compiler_params=_params(2),
        name="wkv_chunked",
    )(*([p_rkv] * nb + [p_lora] * nb + [gates] * nb), mu, par, lora_w)
    return oa.reshape(batch * seq, d_a), st


def _wkv_step_kernel(ps_ref, pl_ref, prev_ref, sga_ref, mu_ref, par_ref, lora_ref, s_ref, oa_ref, so_ref,
                     vec_ref, y_ref, *, d_a, bb, group):
    n_heads = d_a // HEAD
    half = HEAD // 2
    n_rkv = 3 * d_a
    xs = _token_shift(ps_ref[...], prev_ref[:, :n_rkv], mu_ref[:, :n_rkv])
    lo = _lora_out(_token_shift(pl_ref[...], prev_ref[:, n_rkv:], mu_ref[:, n_rkv:]), lora_ref[...])
    block_ones = _block_ones()
    sls = _pair_slices(d_a)
    r, kf, v, log_decay, kk, a = _pair_vectors([xs], [lo], par_ref, d_a, block_ones)

    lane_lo = lax.broadcasted_iota(jnp.int32, (bb, PAIR), 1) < HEAD
    for pi in range(len(sls)):
        v_hi = v[pi].astype(BF16).astype(F32)
        vecs = (-kk[pi], jnp.exp(log_decay[pi]), kk[pi] * a[pi], kf[pi], r[pi], v_hi, v[pi] - v_hi)
        for i, x in enumerate(vecs):
            swapped = pltpu.roll(x, HEAD, axis=1)
            vec_ref[i, 2 * pi] = jnp.where(lane_lo, x, swapped)
            vec_ref[i, 2 * pi + 1] = jnp.where(lane_lo, swapped, x)

    rows = n_heads * half
    m_in_head = lax.broadcasted_iota(jnp.int32, (rows, PAIR), 0) % half
    lane = lax.broadcasted_iota(jnp.int32, (rows, PAIR), 1)
    diag = (lane == 2 * m_in_head) | (lane == 2 * m_in_head + HEAD + 1)

    def expand(i, b):
        return jnp.concatenate(
            [jnp.broadcast_to(vec_ref[i, h, b:b + 1, :], (half, PAIR)) for h in range(n_heads)], axis=0)

    for g0 in range(0, bb, group):
        bs = range(g0, g0 + group)
        s = [s_ref[b] for b in bs]
        sa = [_dot((x * expand(0, b)).astype(BF16), block_ones) for x, b in zip(s, bs)]
        vb = [_dot(jnp.where(diag, expand(5, b), 0.0).astype(BF16), block_ones)
              + _dot(jnp.where(diag, expand(6, b), 0.0).astype(BF16), block_ones) for b in bs]
        s_new = [x * expand(1, b) + y * expand(2, b) + z * expand(3, b) for x, y, z, b in zip(s, sa, vb, bs)]
        for x, b in zip(s_new, bs):
            so_ref[b] = x
        yb = [_head_sum(x * expand(4, b), block_ones, split=True) for x, b in zip(s_new, bs)]
        for x, b in zip(yb, bs):
            yh = jnp.sum(jnp.where(diag, x, 0.0).reshape(n_heads, half, PAIR), axis=1)
            y_ref[b] = yh + pltpu.roll(yh, HEAD, axis=1)

    y = [jnp.where(lane_lo, y_ref[:, 2 * pi, :], y_ref[:, 2 * pi + 1, :]) for pi in range(len(sls))]
    o = _head_norm_bonus(y, r, kf, v, par_ref, d_a, block_ones)
    for oi, sl in zip(o, sls):
        oa_ref[:, sl] = (oi * sga_ref[:, sl]).astype(oa_ref.dtype)


def _wkv_sample(p_rkv, p_lora, shift_prev, gates, mu, par, lora_w, state, d_a, bb, row0):
    batch, shift_w = shift_prev.shape
    n_heads = d_a // HEAD
    rows = n_heads * HEAD // 2
    blk0 = row0 // bb
    kern = functools.partial(_wkv_step_kernel, d_a=d_a, bb=bb, group=min(bb, 4))
    oa, new_state = pl.pallas_call(
        kern,
        out_shape=(jax.ShapeDtypeStruct((batch, d_a), BF16),
                   jax.ShapeDtypeStruct((batch, rows, PAIR), F32)),
        grid=(batch // bb,),
        in_specs=[pl.BlockSpec((bb, 3 * d_a), lambda i: (blk0 + i, 0)),
                  pl.BlockSpec((bb, 2 * LORA), lambda i: (blk0 + i, 0)),
                  pl.BlockSpec((bb, shift_w), lambda i: (i, 0)),
                  pl.BlockSpec((bb, d_a), lambda i: (blk0 + i, 0)),
                  pl.BlockSpec((1, shift_w), lambda i: (0, 0)),
                  pl.BlockSpec((8, d_a), lambda i: (0, 0)),
                  pl.BlockSpec((2 * LORA, 2 * d_a), lambda i: (0, 0)),
                  pl.BlockSpec((bb, rows, PAIR), lambda i: (i, 0, 0))],
        out_specs=(pl.BlockSpec((bb, d_a), lambda i: (i, 0)),
                   pl.BlockSpec((bb, rows, PAIR), lambda i: (i, 0, 0))),
        scratch_shapes=[pltpu.VMEM((7, n_heads, bb, PAIR), F32), pltpu.VMEM((bb, n_heads, PAIR), F32)],
        compiler_params=_params(1),
        name="wkv_step",
    )(p_rkv, p_lora, shift_prev, gates, mu, par, lora_w, state.reshape(batch, rows, PAIR))
    return oa, new_state.reshape(state.shape)


CONV_PAD = 32
CONV_STRIP = 16


def _conv_prompt_kernel(u_ref, w_ref, b_ref, o_ref, buf_ref, c_ref, *, tt, taps):
    t = pl.program_id(1)
    sub = buf_ref.shape[1]
    lanes = [slice(s * 128, (s + 1) * 128) for s in range(sub)]

    @pl.when(t == 0)
    def _():
        buf_ref[0:CONV_PAD] = jnp.zeros((CONV_PAD,) + buf_ref.shape[1:], F32)

    u = u_ref[...]
    buf_ref[CONV_PAD:CONV_PAD + tt] = jnp.swapaxes(jnp.stack([u[:, ls] for ls in lanes], axis=0), 0, 1)
    off = CONV_PAD - (taps - 1)
    w = [w_ref[k] for k in range(taps)]
    bias = b_ref[0]

    def strip(s, carry):
        t0 = s * CONV_STRIP
        acc = [bias] * CONV_STRIP
        for i in range(CONV_STRIP + taps - 1):
            x = buf_ref[off + t0 + i]
            for j in range(CONV_STRIP):
                if 0 <= i - j < taps:
                    acc[j] = acc[j] + w[i - j] * x
        for j in range(CONV_STRIP):
            c_ref[t0 + j] = acc[j]
        return carry

    lax.fori_loop(0, tt // CONV_STRIP, strip, 0)
    buf_ref[0:CONV_PAD] = buf_ref[tt:tt + CONV_PAD]
    c = jnp.swapaxes(c_ref[...], 0, 1)
    for s, ls in enumerate(lanes):
        o_ref[:, ls] = c[s]


def _conv_prompt(u, conv_w, conv_b, batch, seq, tt):
    d_b = u.shape[1]
    taps = conv_w.shape[0]
    nt = seq // tt
    sub = d_b // 128
    assert taps - 1 <= CONV_PAD and tt % CONV_STRIP == 0 and tt >= CONV_PAD
    kern = functools.partial(_conv_prompt_kernel, tt=tt, taps=taps)
    return pl.pallas_call(
        kern,
        out_shape=jax.ShapeDtypeStruct((batch * seq, d_b), F32),
        grid=(batch, nt),
        in_specs=[pl.BlockSpec((tt, d_b), lambda b, t: (b * nt + t, 0)),
                  pl.BlockSpec((taps, sub, 128), lambda b, t: (0, 0, 0)),
                  pl.BlockSpec((1, sub, 128), lambda b, t: (0, 0, 0))],
        out_specs=pl.BlockSpec((tt, d_b), lambda b, t: (b * nt + t, 0)),
        scratch_shapes=[pltpu.VMEM((tt + CONV_PAD, sub, 128), F32), pltpu.VMEM((tt, sub, 128), F32)],
        compiler_params=_params(2),
        name="conv_prompt",
    )(u, conv_w.reshape(taps, sub, 128), conv_b.reshape(1, sub, 128))


def _conv_step_kernel(u_ref, prev_ref, w_ref, b_ref, o_ref, hist_ref, *, taps):
    u = u_ref[...]
    c = b_ref[...] + w_ref[taps - 1:taps, :] * u
    for k in range(taps - 1):
        c = c + w_ref[k:k + 1, :] * prev_ref[:, k, :]
    o_ref[...] = c
    hist_ref[:, 0:taps - 2, :] = prev_ref[:, 1:taps - 1, :]
    hist_ref[:, taps - 2, :] = u


def _conv_sample(u, conv_prev, conv_w, conv_b, bb, row0):
    batch = conv_prev.shape[0]
    d_b = u.shape[1]
    taps = conv_w.shape[0]
    blk0 = row0 // bb
    kern = functools.partial(_conv_step_kernel, taps=taps)
    return pl.pallas_call(
        kern,
        out_shape=(jax.ShapeDtypeStruct((batch, d_b), F32),
                   jax.ShapeDtypeStruct(conv_prev.shape, F32)),
        grid=(batch // bb,),
        in_specs=[pl.BlockSpec((bb, d_b), lambda i: (blk0 + i, 0)),
                  pl.BlockSpec((bb, taps - 1, d_b), lambda i: (i, 0, 0)),
                  pl.BlockSpec((taps, d_b), lambda i: (0, 0)),
                  pl.BlockSpec((1, d_b), lambda i: (0, 0))],
        out_specs=(pl.BlockSpec((bb, d_b), lambda i: (i, 0)),
                   pl.BlockSpec((bb, taps - 1, d_b), lambda i: (i, 0, 0))),
        compiler_params=_params(1),
        name="conv_step",
    )(u, conv_prev, conv_w, conv_b.reshape(1, d_b))


def _tail_kernel(oa_ref, c_ref, gb_ref, sga_ref, sgb_ref, x_ref, p_ref, cln_ref, wa_ref, wb_ref, wout_ref,
                 wpg_ref, wple_ref, fg_ref, o_ref):
    c = c_ref[...]
    mean = jnp.mean(c, axis=-1, keepdims=True)
    dc = c - mean
    var = jnp.mean(dc * dc, axis=-1, keepdims=True)
    cf = dc * lax.rsqrt(var + LN_EPS) * cln_ref[0:1, :] + cln_ref[1:2, :]
    cb = (cf * _sigmoid(cf) * gb_ref[...]).astype(BF16)
    m = sga_ref[...] * _dot(oa_ref[...], wa_ref[...]) + sgb_ref[...] * _dot(cb, wb_ref[...])
    h = x_ref[...] + _dot(m.astype(BF16), wout_ref[...])
    gate = _sigmoid(_dot(h.astype(BF16), wpg_ref[...]))
    h = h + gate * _dot(p_ref[...].astype(BF16), wple_ref[...])
    ms = jnp.mean(h * h, axis=-1, keepdims=True)
    o_ref[...] = h * lax.rsqrt(ms + RMS_EPS) * fg_ref[...]


def _tail(oa, c, gates, x, p, cln, wa, wb, wout, wpg, wple, fg, tm, gate_b_block, merge_block, row0=0):
    m, d = x.shape
    d_a = oa.shape[1]
    d_b = c.shape[1]
    ple = p.shape[1]
    blk0 = row0 // tm
    row = lambda i: (i, 0)
    const = lambda i: (0, 0)
    resident = lambda shape: pl.BlockSpec(shape, const, pipeline_mode=pl.Buffered(1))
    return pl.pallas_call(
        _tail_kernel,
        out_shape=jax.ShapeDtypeStruct((m, d), F32),
        grid=(m // tm,),
        in_specs=[pl.BlockSpec((tm, d_a), row), pl.BlockSpec((tm, d_b), row),
                  pl.BlockSpec((tm, d_b), lambda i: (blk0 + i, gate_b_block)),
                  pl.BlockSpec((tm, d), lambda i: (blk0 + i, merge_block)),
                  pl.BlockSpec((tm, d), lambda i: (blk0 + i, merge_block + 1)),
                  pl.BlockSpec((tm, d), row), pl.BlockSpec((tm, ple), row),
                  resident((2, d_b)),
                  resident((d_a, d)), resident((d_b, d)), resident((d, d)), resident((d, d)),
                  resident((ple, d)), resident((1, d))],
        out_specs=pl.BlockSpec((tm, d), row),
        compiler_params=_params(1),
        name="tail",
    )(oa, c, gates, gates, gates, x, p, cln, wa, wb, wout, wpg, wple, fg)


def _largest_tile(n, cap, align):
    t = min(n, cap)
    while n % t or t % align:
        t -= 1
    return t


def kernel(x_prompt, x_sample, state_shift, state_wkv, state_conv, p_prompt, p_sample, norm_g, w_in,
           shift_mu, w0, w_lora_b, a0, a_lora_b, k_k, k_a, r_k, lnx_g, lnx_b, w_proj_a, conv_w, conv_b,
           cln_g, cln_b, w_proj_b, w_out, w_ple, w_ple_gate, final_g):
    depth = w_in.shape[0]
    batch, seq, d = x_prompt.shape
    dec_batch, dec_seq, _ = x_sample.shape
    d_a = w_proj_a.shape[1]
    d_b = w_proj_b.shape[1]
    shift_w = shift_mu.shape[1]
    n_heads = d_a // HEAD
    n_pairs = d_a // PAIR
    taps = conv_w.shape[1]
    assert depth == 1 and dec_seq == 1 and d_a == d_b and 2 * d_a == d
    assert shift_w == 3 * d_a + 2 * LORA and d_a % PAIR == 0 and seq % CHUNK == 0

    o1 = shift_w
    o2 = o1 + d_a
    o3 = o2 + 2 * d_b
    o4 = o3 + d_b
    w = w_in[0]
    lora_w = jnp.zeros((2 * LORA, 2 * d_a), F32)
    lora_w = lora_w.at[:LORA, :d_a].set(w_lora_b[0]).at[LORA:, d_a:].set(a_lora_b[0]).astype(BF16)
    zeros_a = jnp.zeros((d_a,), F32)
    par = jnp.stack([w0[0], a0[0], k_k[0], k_a[0], r_k[0].reshape(d_a), lnx_g[0], lnx_b[0], zeros_a])
    cln = jnp.stack([cln_g[0], cln_b[0]])
    mu = shift_mu
    g_in = norm_g[0].reshape(1, d)
    fg = final_g.reshape(1, d)
    gate_b_block = d_a // d_b
    merge_block = (d_a + d_b) // d

    m_p = batch * seq
    m_all = m_p + dec_batch
    x2 = x_prompt.reshape(m_p, d)
    xs2 = x_sample.reshape(dec_batch, d)
    n_rkv = 3 * d_a
    xn, p_lora = _rmsnorm_bf16(x2, xs2, g_in, w, n_rkv, 2 * LORA, _largest_tile(m_p, 1024, 8))
    tm = _largest_tile(m_all, 1100, 16)
    tn_shift = _largest_tile(n_rkv, 1024, 128)
    p_rkv = _project(xn, w, lambda j: j * tn_shift, n_rkv, tm, tn_shift)
    gate_cols = lambda j: jnp.where(j == 0, o1, o3 + (j - 1) * d_a)
    gates = _project(xn, w, gate_cols, d_a + d_b + 2 * d, tm, d_a, n_silu_blocks=2)
    u, (wa, wb, wout, wpg, wple) = _project_glu(
        xn, w, o2, o2 + d_b, d_b, tm, _largest_tile(d_b, 512, 128),
        casts=[w_proj_a[0], w_proj_b[0], w_out[0], w_ple_gate[0], w_ple[0]])

    oa, st = _wkv_prompt(p_rkv, p_lora, gates, mu, par, lora_w, batch, seq, d_a)
    c = _conv_prompt(u, conv_w[0], conv_b[0], batch, seq, _largest_tile(seq, 256, CONV_STRIP))
    y_prompt = _tail(oa, c, gates, x2, p_prompt[0].reshape(m_p, -1), cln, wa, wb, wout, wpg, wple, fg,
                     _largest_tile(m_p, 256, 8), gate_b_block, merge_block).reshape(batch, seq, d)
    last_rows = lambda a: jnp.concatenate([a[(b + 1) * seq - 1:(b + 1) * seq] for b in range(batch)])
    new_shift_p = jnp.concatenate([last_rows(p_rkv), last_rows(p_lora)], axis=1)[None]
    st = st.reshape(batch, n_pairs, 2, HEAD, 2, HEAD)
    new_wkv_p = jnp.stack([st[:, :, 0, :, 0, :], st[:, :, 1, :, 1, :]], axis=2)
    new_wkv_p = jnp.swapaxes(new_wkv_p.reshape(batch, n_heads, HEAD, HEAD), -1, -2)[None]
    new_conv_p = jnp.stack([u[(b + 1) * seq - (taps - 1):(b + 1) * seq] for b in range(batch)])[None]

    bb = _largest_tile(dec_batch, 8, 8)
    tm_s = _largest_tile(dec_batch, 256, 8)
    assert m_p % bb == 0 and m_p % tm_s == 0
    oa_s, new_wkv_s = _wkv_sample(p_rkv, p_lora, state_shift[0], gates, mu, par, lora_w, state_wkv[0], d_a, bb,
                                  m_p)
    c_s, new_conv_s = _conv_sample(u, state_conv[0], conv_w[0], conv_b[0], bb, m_p)
    y_sample = _tail(oa_s, c_s, gates, xs2, p_sample[0].reshape(dec_batch, -1), cln, wa, wb, wout, wpg,
                     wple, fg, tm_s, gate_b_block, merge_block, row0=m_p).reshape(dec_batch, 1, d)
    new_shift_s = jnp.concatenate([p_rkv[m_p:], p_lora[m_p:]], axis=1)[None]
    new_conv_s = new_conv_s[None]

    return (y_prompt, y_sample, new_shift_p, new_wkv_p, new_conv_p, new_shift_s, new_wkv_s[None],
            new_conv_s)
```

```python
import functools
import math

import jax
import jax.numpy as jnp
from jax import lax
from jax.experimental import pallas as pl
from jax.experimental.pallas import tpu as pltpu

F32 = jnp.float32
BF16 = jnp.bfloat16

HEAD = 64
PAIR = 2 * HEAD
LORA = 64
CHUNK = 64
WKV_SEQS_PER_STEP = 4
RMS_EPS = 1e-6
LN_EPS = 1e-5
GN_EPS = 64e-5
LOG_DECAY_SCALE = math.exp(-0.5)
VMEM_LIMIT = 56 * 1024 * 1024


def _params(n_axes, vmem=VMEM_LIMIT):
    return pltpu.CompilerParams(dimension_semantics=("arbitrary",) * n_axes, vmem_limit_bytes=vmem)


def _sigmoid(x):
    return 1.0 / (1.0 + jnp.exp(-x))


def _dot(a, b):
    return jnp.dot(a, b, preferred_element_type=F32)


def _dot_nt(a, b):
    return lax.dot_general(a, b, (((1,), (1,)), ((), ())), preferred_element_type=F32)


def _dot_tn(a, b):
    return lax.dot_general(a, b, (((0,), (0,)), ((), ())), preferred_element_type=F32)


def _rmsnorm_kernel(xp_ref, xs_ref, g_ref, w_ref, o_ref, ol_ref, wb_ref, *, n_prompt_blocks):
    i = pl.program_id(0)

    @pl.when(i == 0)
    def _():
        wb_ref[...] = w_ref[...].astype(BF16)

    def norm(x):
        ms = jnp.mean(x * x, axis=-1, keepdims=True)
        return (x * lax.rsqrt(ms + RMS_EPS) * g_ref[...]).astype(o_ref.dtype)

    @pl.when(i < n_prompt_blocks)
    def _():
        xn = norm(xp_ref[...])
        o_ref[...] = xn
        ol_ref[...] = _dot(xn, wb_ref[...])

    @pl.when(i == n_prompt_blocks)
    def _():
        xn = norm(xs_ref[...])
        o_ref[0:xs_ref.shape[0], :] = xn
        ol_ref[0:xs_ref.shape[0], :] = _dot(xn, wb_ref[...])


def _rmsnorm_bf16(x_prompt, x_sample, g, w, col, n_cols, tm):
    m_p, d = x_prompt.shape
    m_s = x_sample.shape[0]
    n_blocks = m_p // tm
    assert m_p % tm == 0 and m_s <= tm and col % 128 == 0
    kern = functools.partial(_rmsnorm_kernel, n_prompt_blocks=n_blocks)
    return pl.pallas_call(
        kern,
        out_shape=(jax.ShapeDtypeStruct((m_p + m_s, d), BF16),
                   jax.ShapeDtypeStruct((m_p + m_s, n_cols), F32)),
        grid=(n_blocks + 1,),
        in_specs=[pl.BlockSpec((tm, d), lambda i: (jnp.minimum(i, n_blocks - 1), 0)),
                  pl.BlockSpec((m_s, d), lambda i: (0, 0)),
                  pl.BlockSpec((1, d), lambda i: (0, 0)),
                  pl.BlockSpec((pl.Element(d), pl.Element(n_cols)), lambda i: (0, col))],
        out_specs=(pl.BlockSpec((tm, d), lambda i: (i, 0)),
                   pl.BlockSpec((tm, n_cols), lambda i: (i, 0))),
        scratch_shapes=[pltpu.VMEM((d, n_cols), BF16)],
        compiler_params=_params(1),
        name="rmsnorm",
    )(x_prompt, x_sample, g, w)


def _proj_kernel(x_ref, w_ref, o_ref, wb_ref, *, n_silu_blocks):
    @pl.when(pl.program_id(1) == 0)
    def _():
        wb_ref[...] = w_ref[...].astype(BF16)

    y = _dot(x_ref[...], wb_ref[...])
    if n_silu_blocks is None:
        o_ref[...] = y
    else:
        s = _sigmoid(y)
        o_ref[...] = jnp.where(pl.program_id(0) < n_silu_blocks, y * s, s)


def _project(xn, w, col_start, n_out, tm, tn, n_silu_blocks=None):
    m, d = xn.shape
    kern = functools.partial(_proj_kernel, n_silu_blocks=n_silu_blocks)
    return pl.pallas_call(
        kern,
        out_shape=jax.ShapeDtypeStruct((m, n_out), F32),
        grid=(n_out // tn, m // tm),
        in_specs=[pl.BlockSpec((tm, d), lambda j, i: (i, 0)),
                  pl.BlockSpec((pl.Element(d), pl.Element(tn)),
                               lambda j, i: (0, pl.multiple_of(col_start(j), 128)))],
        out_specs=pl.BlockSpec((tm, tn), lambda j, i: (i, j)),
        scratch_shapes=[pltpu.VMEM((d, tn), BF16)],
        compiler_params=_params(2),
        name="in_proj",
    )(xn, w)


def _glu_kernel(*refs, n_casts):
    x_ref, wa_ref, wb_ref = refs[:3]
    cast_in = refs[3:3 + n_casts]
    o_ref = refs[3 + n_casts]
    cast_out = refs[4 + n_casts:4 + 2 * n_casts]
    wab_ref, wbb_ref = refs[4 + 2 * n_casts:]

    @pl.when(pl.program_id(1) == 0)
    def _():
        wab_ref[...] = wa_ref[...].astype(BF16)
        wbb_ref[...] = wb_ref[...].astype(BF16)

    x = x_ref[...]
    o_ref[...] = _dot(x, wab_ref[...]) * _sigmoid(_dot(x, wbb_ref[...]))
    for src, dst in zip(cast_in, cast_out):
        dst[...] = src[...].astype(BF16)


def _project_glu(xn, w, col_a, col_b, n_out, tm, tn, casts):
    m, d = xn.shape
    nj, ni = n_out // tn, m // tm
    steps = nj * ni
    for a in casts:
        assert a.shape[0] % (16 * steps) == 0, (a.shape, steps)
    step_rows = lambda a: pl.BlockSpec((a.shape[0] // steps, a.shape[1]), lambda j, i: (j * ni + i, 0))
    w_cols = lambda col: pl.BlockSpec((pl.Element(d), pl.Element(tn)),
                                      lambda j, i: (0, pl.multiple_of(col + j * tn, 128)))
    kern = functools.partial(_glu_kernel, n_casts=len(casts))
    out = pl.pallas_call(
        kern,
        out_shape=[jax.ShapeDtypeStruct((m, n_out), F32)]
        + [jax.ShapeDtypeStruct(a.shape, BF16) for a in casts],
        grid=(nj, ni),
        in_specs=[pl.BlockSpec((tm, d), lambda j, i: (i, 0)), w_cols(col_a), w_cols(col_b)]
        + [step_rows(a) for a in casts],
        out_specs=[pl.BlockSpec((tm, tn), lambda j, i: (i, j))] + [step_rows(a) for a in casts],
        scratch_shapes=[pltpu.VMEM((d, tn), BF16), pltpu.VMEM((d, tn), BF16)],
        compiler_params=_params(2),
        name="in_proj_glu",
    )(xn, w, w, *casts)
    return out[0], out[1:]


def _head_sum(x, block_ones, split=False):
    hi = x.astype(BF16)
    if not split:
        return _dot(hi, block_ones)
    lo = (x - hi.astype(F32)).astype(BF16)
    return _dot(hi, block_ones) + _dot(lo, block_ones)


def _head_sums(xs, block_ones, split=False):
    rows = xs[0].shape[0]
    s = _head_sum(jnp.concatenate(xs, axis=0), block_ones, split)
    return [s[i * rows:(i + 1) * rows] for i in range(len(xs))]


def _block_ones():
    ri = lax.broadcasted_iota(jnp.int32, (PAIR, PAIR), 0)
    ci = lax.broadcasted_iota(jnp.int32, (PAIR, PAIR), 1)
    return jnp.where((ri < HEAD) == (ci < HEAD), 1.0, 0.0).astype(BF16)


def _token_shift(p, prev, mu):
    return p + mu * (prev - p)


def _lora_out(z, lora_w):
    lane = lax.broadcasted_iota(jnp.int32, z.shape, 1)
    z = jnp.where(lane < LORA, jnp.tanh(z), z)
    return _dot(z.astype(BF16), lora_w)


def _pair_slices(d_a):
    return [slice(i * PAIR, (i + 1) * PAIR) for i in range(d_a // PAIR)]


def _pair_vectors(xs_list, lo_list, par_ref, d_a, block_ones):
    sls = _pair_slices(d_a)
    shifted = lambda sl, off: slice(off + sl.start, off + sl.stop)
    items = [(xs, lo, sl) for xs, lo in zip(xs_list, lo_list) for sl in sls]
    r = [xs[:, sl] for xs, _, sl in items]
    k = [xs[:, shifted(sl, d_a)] for xs, _, sl in items]
    v = [xs[:, shifted(sl, 2 * d_a)] for xs, _, sl in items]
    kkr = [ki * par_ref[2:3, sl] for ki, (_, _, sl) in zip(k, items)]
    ss = _head_sums([x * x for x in kkr], block_ones)
    kk = [x * jnp.minimum(lax.rsqrt(s), 1e12) for x, s in zip(kkr, ss)]
    log_decay = [-LOG_DECAY_SCALE * _sigmoid(par_ref[0:1, sl] + lo[:, sl]) for _, lo, sl in items]
    a = [_sigmoid(par_ref[1:2, sl] + lo[:, shifted(sl, d_a)]) for _, lo, sl in items]
    kf = [ki * (1.0 + (ai - 1.0) * par_ref[3:4, sl]) for ki, ai, (_, _, sl) in zip(k, a, items)]
    return r, kf, v, log_decay, kk, a


def _head_norm_bonus(y, r, kf, v, par_ref, d_a, block_ones):
    sls = _pair_slices(d_a) * (len(y) // (d_a // PAIR))
    mean = [s * (1.0 / HEAD) for s in _head_sums(y, block_ones, split=True)]
    d = [x - m for x, m in zip(y, mean)]
    var = [s * (1.0 / HEAD) for s in _head_sums([x * x for x in d], block_ones)]
    rk = _head_sums([ri * ki * par_ref[4:5, sl] for ri, ki, sl in zip(r, kf, sls)], block_ones)
    return [x * lax.rsqrt(vr + GN_EPS) * par_ref[5:6, sl] + par_ref[6:7, sl] + s * vi
            for x, vr, s, vi, sl in zip(d, var, rk, v, sls)]


def _stack_heads(x, lane_lo):
    xb = x.astype(BF16)
    zero = jnp.zeros_like(xb)
    return jnp.concatenate([jnp.where(lane_lo, xb, zero), jnp.where(lane_lo, zero, xb)], axis=0)


def _cumsum_rows(xs):
    n = xs[0].shape[0]
    tri = (lax.broadcasted_iota(jnp.int32, (n, n), 1) <= lax.broadcasted_iota(jnp.int32, (n, n), 0))
    tri = jnp.where(tri, 1.0, 0.0).astype(BF16)
    x = jnp.concatenate(xs, axis=1)
    hi = x.astype(BF16)
    lo = (x - hi.astype(F32)).astype(BF16)
    s = _dot(tri, hi) + _dot(tri, lo)
    return [s[:, i * PAIR:(i + 1) * PAIR] for i in range(len(xs))]


def _wkv_chunk_kernel(*refs, d_a, nb):
    ps_refs, pl_refs, sga_refs = refs[:nb], refs[nb:2 * nb], refs[2 * nb:3 * nb]
    mu_ref, par_ref, lora_ref, oa_ref, st_ref, carry_ref = refs[3 * nb:]
    c = pl.program_id(1)
    n_pairs = d_a // PAIR
    C = CHUNK
    n_rkv = 3 * d_a

    @pl.when(c == 0)
    def _():
        carry_ref[...] = jnp.zeros_like(carry_ref)
        st_ref[...] = jnp.zeros_like(st_ref)

    def shifted(p, prev_row, mu):
        row = lax.broadcasted_iota(jnp.int32, p.shape, 0)
        return _token_shift(p, jnp.where(row == 0, prev_row, pltpu.roll(p, 1, axis=0)), mu)

    xs, lo = [], []
    for bi in range(nb):
        p, pz = ps_refs[bi][...], pl_refs[bi][...]
        xs.append(shifted(p, carry_ref[bi, :, :n_rkv], mu_ref[:, :n_rkv]))
        lo.append(_lora_out(shifted(pz, carry_ref[bi, :, n_rkv:], mu_ref[:, n_rkv:]), lora_ref[...]))
        carry_ref[bi, :, :n_rkv] = p[C - 1:C, :]
        carry_ref[bi, :, n_rkv:] = pz[C - 1:C, :]

    block_ones = _block_ones()
    lane_lo = lax.broadcasted_iota(jnp.int32, (C, PAIR), 1) < HEAD
    ti = lax.broadcasted_iota(jnp.int32, (C, PAIR), 0)
    si = lax.broadcasted_iota(jnp.int32, (C, PAIR), 1) % C
    strict_lower = si < ti
    lower2 = (lax.broadcasted_iota(jnp.int32, (C, 2 * PAIR), 1) % C
              <= lax.broadcasted_iota(jnp.int32, (C, 2 * PAIR), 0))
    eye = jnp.where(si == ti, 1.0, 0.0)
    cat0 = lambda x, y: jnp.concatenate([x, y], axis=0)

    r, kf, v, log_decay, kk, a = _pair_vectors(xs, lo, par_ref, d_a, block_ones)
    cl = _cumsum_rows(log_decay)
    cend = [x[C - 1:C, :] for x in cl]
    e_neg = [jnp.exp(-x) for x in cl]
    e_end = [jnp.exp(ce - x) for ce, x in zip(cend, cl)]
    kka = [x * y for x, y in zip(kk, a)]
    stack = lambda xs_: [_stack_heads(x, lane_lo) for x in xs_]
    a_p = [(-x * jnp.exp(c_ - ld)).astype(BF16) for x, c_, ld in zip(kk, cl, log_decay)]
    r_p = [(x * jnp.exp(c_)).astype(BF16) for x, c_ in zip(r, cl)]
    a_s = stack(a_p)
    b_s = stack([x * e for x, e in zip(kka, e_neg)])
    k_s = stack([x * e for x, e in zip(kf, e_neg)])
    v_s = stack(v)
    bh_s = stack([x * e for x, e in zip(kka, e_end)])
    kh_s = stack([x * e for x, e in zip(kf, e_end)])
    pend_col = [jnp.transpose(jnp.broadcast_to(jnp.exp(ce), (PAIR, PAIR))) for ce in cend]

    m1 = [_dot_nt(cat0(x, y), cat0(z, w)) for x, y, z, w in zip(a_p, r_p, b_s, k_s)]
    a_ab = [jnp.where(strict_lower, m[:C, :PAIR], 0.0) for m in m1]
    a_ak = [jnp.where(strict_lower, m[:C, PAIR:], 0.0).astype(BF16) for m in m1]
    a_rbk = [jnp.where(lower2, m[C:], 0.0).astype(BF16) for m in m1]
    g = [_dot(x, y) for x, y in zip(a_ak, v_s)]

    apow = [_dot(x.astype(BF16), _stack_heads(x, lane_lo)) for x in a_ab]
    tinv = [eye + x for x in a_ab]
    n_levels = CHUNK.bit_length() - 1
    for lvl in range(1, n_levels):
        rhs = stack(apow)
        if lvl < n_levels - 1:
            both = [_dot(cat0(x.astype(BF16), t.astype(BF16)), y) for x, t, y in zip(apow, tinv, rhs)]
            apow = [x[:C] for x in both]
            tinv = [t + x[C:] for t, x in zip(tinv, both)]
        else:
            tinv = [t + _dot(t.astype(BF16), y) for t, y in zip(tinv, rhs)]

    g_s = stack(g)
    wx = [_dot(t.astype(BF16), jnp.concatenate([x, y], axis=1)) for t, x, y in zip(tinv, a_s, g_s)]
    seq_pair = [(bi, pi) for bi in range(nb) for pi in range(n_pairs)]
    st = [st_ref[bi, pi] for bi, pi in seq_pair]
    uy = [_dot(cat0(w[:, :PAIR].astype(BF16), x), s_.astype(BF16)) for w, x, s_ in zip(wx, r_p, st)]
    uv_s = [cat0(_stack_heads(x[:C] + w[:, PAIR:], lane_lo), y) for x, w, y in zip(uy, wx, v_s)]
    st_upd = [_dot_tn(cat0(x, y), z) for x, y, z in zip(bh_s, kh_s, uv_s)]
    y_uv = [_dot(x, z) for x, z in zip(a_rbk, uv_s)]
    for i, (bi, pi) in enumerate(seq_pair):
        st_ref[bi, pi] = pend_col[i] * st[i] + st_upd[i]
    y = [x[C:] + z for x, z in zip(uy, y_uv)]
    o = _head_norm_bonus(y, r, kf, v, par_ref, d_a, block_ones)
    sls = _pair_slices(d_a)
    for oi, (bi, pi) in zip(o, seq_pair):
        oa_ref[bi, :, sls[pi]] = (oi * sga_refs[bi][:, sls[pi]]).astype(oa_ref.dtype)


def _wkv_prompt(p_rkv, p_lora, gates, mu, par, lora_w, batch, seq, d_a):
    n_chunks = seq // CHUNK
    shift_w = mu.shape[1]
    n_pairs = d_a // PAIR
    nb = WKV_SEQS_PER_STEP if batch % WKV_SEQS_PER_STEP == 0 else 1
    kern = functools.partial(_wkv_chunk_kernel, d_a=d_a, nb=nb)
    seq_rows = lambda bi: (lambda b, c: ((b * nb + bi) * n_chunks + c, 0))
    oa, st = pl.pallas_call(
        kern,
        out_shape=(jax.ShapeDtypeStruct((batch, seq, d_a), BF16),
                   jax.ShapeDtypeStruct((batch, n_pairs, PAIR, PAIR), F32)),
        grid=(batch // nb, n_chunks),
        in_specs=([pl.BlockSpec((CHUNK, 3 * d_a), seq_rows(bi)) for bi in range(nb)]
                  + [pl.BlockSpec((CHUNK, 2 * LORA), seq_rows(bi)) for bi in range(nb)]
                  + [pl.BlockSpec((CHUNK, d_a), seq_rows(bi)) for bi in range(nb)]
                  + [pl.BlockSpec((1, shift_w), lambda b, c: (0, 0)),
                     pl.BlockSpec((8, d_a), lambda b, c: (0, 0)),
                     pl.BlockSpec((2 * LORA, 2 * d_a), lambda b, c: (0, 0))]),
        out_specs=(pl.BlockSpec((nb, CHUNK, d_a), lambda b, c: (b, c, 0)),
                   pl.BlockSpec((nb, n_pairs, PAIR, PAIR), lambda b, c: (b, 0, 0, 0))),
        scratch_shapes=[pltpu.VMEM((nb, 1, shift_w), F32)],
        compiler_params=_params(2),
        name="wkv_chunked",
    )(*([p_rkv] * nb + [p_lora] * nb + [gates] * nb), mu, par, lora_w)
    return oa.reshape(batch * seq, d_a), st


STEP_ROWS = 8


def _wkv_step_kernel(ps_ref, pl_ref, prev_ref, sga_ref, mu_ref, par_ref, lora_ref, s_ref, oa_ref, so_ref,
                     vt_ref, rkv_ref, y_ref, *, d_a):
    h = pl.program_id(0)
    n_heads = d_a // HEAD
    n_rkv = 3 * d_a
    sls = _pair_slices(d_a)

    @pl.when(h == 0)
    def _():
        xs = _token_shift(ps_ref[...], prev_ref[:, :n_rkv], mu_ref[:, :n_rkv])
        lo = _lora_out(_token_shift(pl_ref[...], prev_ref[:, n_rkv:], mu_ref[:, n_rkv:]), lora_ref[...])
        r, kf, v, log_decay, kk, a = _pair_vectors([xs], [lo], par_ref, d_a, _block_ones())
        for pi, sl in enumerate(sls):
            vecs = (-kk[pi], jnp.exp(log_decay[pi]), kk[pi] * a[pi], kf[pi], r[pi], v[pi])
            for i, x in enumerate(vecs):
                vt_ref[i, sl, :] = x.T
            for i, x in enumerate((r[pi], kf[pi], v[pi])):
                rkv_ref[i, :, sl] = x

    base = pl.multiple_of(h * HEAD, HEAD)
    a_t, w_t, b_t, k_t, r_t = (vt_ref[i, pl.ds(base, HEAD), :] for i in range(5))

    def rows(g, carry):
        i0 = pl.multiple_of(g * STEP_ROWS, STEP_ROWS)
        row0 = pl.multiple_of(base + i0, STEP_ROWS)
        v_rows = vt_ref[5, pl.ds(row0, STEP_ROWS), :]
        ys = []
        for ii in range(STEP_ROWS):
            s = s_ref[0, i0 + ii]
            sa = jnp.sum(s * a_t, axis=0, keepdims=True)
            s_new = s * w_t + sa * b_t + v_rows[ii:ii + 1, :] * k_t
            so_ref[0, i0 + ii] = s_new
            ys.append(jnp.sum(s_new * r_t, axis=0, keepdims=True))
        y_ref[pl.ds(row0, STEP_ROWS), :] = jnp.concatenate(ys, axis=0)
        return carry

    lax.fori_loop(0, HEAD // STEP_ROWS, rows, 0)

    @pl.when(h == n_heads - 1)
    def _():
        y = [y_ref[sl, :].T for sl in sls]
        r, kf, v = ([rkv_ref[i, :, sl] for sl in sls] for i in range(3))
        o = _head_norm_bonus(y, r, kf, v, par_ref, d_a, _block_ones())
        for oi, sl in zip(o, sls):
            oa_ref[:, sl] = (oi * sga_ref[:, sl]).astype(oa_ref.dtype)


def _wkv_sample(p_rkv, p_lora, shift_prev, gates, mu, par, lora_w, state_t, d_a, row0):
    batch, shift_w = shift_prev.shape
    n_heads = d_a // HEAD
    assert row0 % batch == 0 and state_t.shape == (n_heads, HEAD, HEAD, batch)
    blk0 = row0 // batch
    const = lambda h: (0, 0)
    kern = functools.partial(_wkv_step_kernel, d_a=d_a)
    return pl.pallas_call(
        kern,
        out_shape=(jax.ShapeDtypeStruct((batch, d_a), BF16),
                   jax.ShapeDtypeStruct(state_t.shape, F32)),
        grid=(n_heads,),
        in_specs=[pl.BlockSpec((batch, 3 * d_a), lambda h: (blk0, 0)),
                  pl.BlockSpec((batch, 2 * LORA), lambda h: (blk0, 0)),
                  pl.BlockSpec((batch, shift_w), const),
                  pl.BlockSpec((batch, d_a), lambda h: (blk0, 0)),
                  pl.BlockSpec((1, shift_w), const),
                  pl.BlockSpec((8, d_a), const),
                  pl.BlockSpec((2 * LORA, 2 * d_a), const),
                  pl.BlockSpec((1, HEAD, HEAD, batch), lambda h: (h, 0, 0, 0))],
        out_specs=(pl.BlockSpec((batch, d_a), const),
                   pl.BlockSpec((1, HEAD, HEAD, batch), lambda h: (h, 0, 0, 0))),
        scratch_shapes=[pltpu.VMEM((6, d_a, batch), F32), pltpu.VMEM((3, batch, d_a), F32),
                        pltpu.VMEM((d_a, batch), F32)],
        compiler_params=_params(1),
        name="wkv_step",
    )(p_rkv, p_lora, shift_prev, gates, mu, par, lora_w, state_t)


CONV_PAD = 32
CONV_STRIP = 16


def _conv_prompt_kernel(u_ref, w_ref, b_ref, o_ref, buf_ref, c_ref, *, tt, taps):
    t = pl.program_id(1)
    sub = buf_ref.shape[1]
    lanes = [slice(s * 128, (s + 1) * 128) for s in range(sub)]

    @pl.when(t == 0)
    def _():
        buf_ref[0:CONV_PAD] = jnp.zeros((CONV_PAD,) + buf_ref.shape[1:], F32)

    u = u_ref[...]
    buf_ref[CONV_PAD:CONV_PAD + tt] = jnp.swapaxes(jnp.stack([u[:, ls] for ls in lanes], axis=0), 0, 1)
    off = CONV_PAD - (taps - 1)
    w = [w_ref[k] for k in range(taps)]
    bias = b_ref[0]

    def strip(s, carry):
        t0 = s * CONV_STRIP
        acc = [bias] * CONV_STRIP
        for i in range(CONV_STRIP + taps - 1):
            x = buf_ref[off + t0 + i]
            for j in range(CONV_STRIP):
                if 0 <= i - j < taps:
                    acc[j] = acc[j] + w[i - j] * x
        for j in range(CONV_STRIP):
            c_ref[t0 + j] = acc[j]
        return carry

    lax.fori_loop(0, tt // CONV_STRIP, strip, 0)
    buf_ref[0:CONV_PAD] = buf_ref[tt:tt + CONV_PAD]
    c = jnp.swapaxes(c_ref[...], 0, 1)
    for s, ls in enumerate(lanes):
        o_ref[:, ls] = c[s]


def _conv_prompt(u, conv_w, conv_b, batch, seq, tt):
    d_b = u.shape[1]
    taps = conv_w.shape[0]
    nt = seq // tt
    sub = d_b // 128
    assert taps - 1 <= CONV_PAD and tt % CONV_STRIP == 0 and tt >= CONV_PAD
    kern = functools.partial(_conv_prompt_kernel, tt=tt, taps=taps)
    return pl.pallas_call(
        kern,
        out_shape=jax.ShapeDtypeStruct((batch * seq, d_b), F32),
        grid=(batch, nt),
        in_specs=[pl.BlockSpec((tt, d_b), lambda b, t: (b * nt + t, 0)),
                  pl.BlockSpec((taps, sub, 128), lambda b, t: (0, 0, 0)),
                  pl.BlockSpec((1, sub, 128), lambda b, t: (0, 0, 0))],
        out_specs=pl.BlockSpec((tt, d_b), lambda b, t: (b * nt + t, 0)),
        scratch_shapes=[pltpu.VMEM((tt + CONV_PAD, sub, 128), F32), pltpu.VMEM((tt, sub, 128), F32)],
        compiler_params=_params(2),
        name="conv_prompt",
    )(u, conv_w.reshape(taps, sub, 128), conv_b.reshape(1, sub, 128))


def _conv_step_kernel(u_ref, prev_ref, w_ref, b_ref, o_ref, hist_ref, *, taps):
    u = u_ref[...]
    c = b_ref[...] + w_ref[taps - 1:taps, :] * u
    for k in range(taps - 1):
        c = c + w_ref[k:k + 1, :] * prev_ref[k]
    o_ref[...] = c
    hist_ref[0:taps - 2] = prev_ref[1:taps - 1]
    hist_ref[taps - 2] = u


def _conv_sample(u, conv_prev_t, conv_w, conv_b, bb, row0):
    batch = conv_prev_t.shape[1]
    d_b = u.shape[1]
    taps = conv_w.shape[0]
    assert row0 % bb == 0 and batch % bb == 0
    blk0 = row0 // bb
    kern = functools.partial(_conv_step_kernel, taps=taps)
    return pl.pallas_call(
        kern,
        out_shape=(jax.ShapeDtypeStruct((batch, d_b), F32),
                   jax.ShapeDtypeStruct(conv_prev_t.shape, F32)),
        grid=(batch // bb,),
        in_specs=[pl.BlockSpec((bb, d_b), lambda i: (blk0 + i, 0)),
                  pl.BlockSpec((taps - 1, bb, d_b), lambda i: (0, i, 0)),
                  pl.BlockSpec((taps, d_b), lambda i: (0, 0)),
                  pl.BlockSpec((1, d_b), lambda i: (0, 0))],
        out_specs=(pl.BlockSpec((bb, d_b), lambda i: (i, 0)),
                   pl.BlockSpec((taps - 1, bb, d_b), lambda i: (0, i, 0))),
        compiler_params=_params(1),
        name="conv_step",
    )(u, conv_prev_t, conv_w, conv_b.reshape(1, d_b))


def _tail_kernel(oa_ref, c_ref, gb_ref, sga_ref, sgb_ref, x_ref, p_ref, cln_ref, wa_ref, wb_ref, wout_ref,
                 wpg_ref, wple_ref, fg_ref, o_ref):
    c = c_ref[...]
    mean = jnp.mean(c, axis=-1, keepdims=True)
    dc = c - mean
    var = jnp.mean(dc * dc, axis=-1, keepdims=True)
    cf = dc * lax.rsqrt(var + LN_EPS) * cln_ref[0:1, :] + cln_ref[1:2, :]
    cb = (cf * _sigmoid(cf) * gb_ref[...]).astype(BF16)
    m = sga_ref[...] * _dot(oa_ref[...], wa_ref[...]) + sgb_ref[...] * _dot(cb, wb_ref[...])
    h = x_ref[...] + _dot(m.astype(BF16), wout_ref[...])
    gate = _sigmoid(_dot(h.astype(BF16), wpg_ref[...]))
    h = h + gate * _dot(p_ref[...].astype(BF16), wple_ref[...])
    ms = jnp.mean(h * h, axis=-1, keepdims=True)
    o_ref[...] = h * lax.rsqrt(ms + RMS_EPS) * fg_ref[...]


def _tail(oa, c, gates, x, p, cln, wa, wb, wout, wpg, wple, fg, tm, gate_b_block, merge_block, row0=0):
    m, d = x.shape
    d_a = oa.shape[1]
    d_b = c.shape[1]
    ple = p.shape[1]
    blk0 = row0 // tm
    row = lambda i: (i, 0)
    const = lambda i: (0, 0)
    resident = lambda shape: pl.BlockSpec(shape, const, pipeline_mode=pl.Buffered(1))
    return pl.pallas_call(
        _tail_kernel,
        out_shape=jax.ShapeDtypeStruct((m, d), F32),
        grid=(m // tm,),
        in_specs=[pl.BlockSpec((tm, d_a), row), pl.BlockSpec((tm, d_b), row),
                  pl.BlockSpec((tm, d_b), lambda i: (blk0 + i, gate_b_block)),
                  pl.BlockSpec((tm, d), lambda i: (blk0 + i, merge_block)),
                  pl.BlockSpec((tm, d), lambda i: (blk0 + i, merge_block + 1)),
                  pl.BlockSpec((tm, d), row), pl.BlockSpec((tm, ple), row),
                  resident((2, d_b)),
                  resident((d_a, d)), resident((d_b, d)), resident((d, d)), resident((d, d)),
                  resident((ple, d)), resident((1, d))],
        out_specs=pl.BlockSpec((tm, d), row),
        compiler_params=_params(1),
        name="tail",
    )(oa, c, gates, gates, gates, x, p, cln, wa, wb, wout, wpg, wple, fg)


def _largest_tile(n, cap, align):
    t = min(n, cap)
    while n % t or t % align:
        t -= 1
    return t


def kernel(x_prompt, x_sample, state_shift, state_wkv, state_conv, p_prompt, p_sample, norm_g, w_in,
           shift_mu, w0, w_lora_b, a0, a_lora_b, k_k, k_a, r_k, lnx_g, lnx_b, w_proj_a, conv_w, conv_b,
           cln_g, cln_b, w_proj_b, w_out, w_ple, w_ple_gate, final_g):
    depth = w_in.shape[0]
    batch, seq, d = x_prompt.shape
    dec_batch, dec_seq, _ = x_sample.shape
    d_a = w_proj_a.shape[1]
    d_b = w_proj_b.shape[1]
    shift_w = shift_mu.shape[1]
    n_heads = d_a // HEAD
    n_pairs = d_a // PAIR
    taps = conv_w.shape[1]
    assert depth == 1 and dec_seq == 1 and d_a == d_b and 2 * d_a == d
    assert shift_w == 3 * d_a + 2 * LORA and d_a % PAIR == 0 and seq % CHUNK == 0

    o1 = shift_w
    o2 = o1 + d_a
    o3 = o2 + 2 * d_b
    o4 = o3 + d_b
    w = w_in[0]
    lora_w = jnp.zeros((2 * LORA, 2 * d_a), F32)
    lora_w = lora_w.at[:LORA, :d_a].set(w_lora_b[0]).at[LORA:, d_a:].set(a_lora_b[0]).astype(BF16)
    zeros_a = jnp.zeros((d_a,), F32)
    par = jnp.stack([w0[0], a0[0], k_k[0], k_a[0], r_k[0].reshape(d_a), lnx_g[0], lnx_b[0], zeros_a])
    cln = jnp.stack([cln_g[0], cln_b[0]])
    mu = shift_mu
    g_in = norm_g[0].reshape(1, d)
    fg = final_g.reshape(1, d)
    gate_b_block = d_a // d_b
    merge_block = (d_a + d_b) // d

    m_p = batch * seq
    m_all = m_p + dec_batch
    x2 = x_prompt.reshape(m_p, d)
    xs2 = x_sample.reshape(dec_batch, d)
    n_rkv = 3 * d_a
    xn, p_lora = _rmsnorm_bf16(x2, xs2, g_in, w, n_rkv, 2 * LORA, _largest_tile(m_p, 1024, 8))
    tm = _largest_tile(m_all, 1100, 16)
    tn_shift = _largest_tile(n_rkv, 1024, 128)
    p_rkv = _project(xn, w, lambda j: j * tn_shift, n_rkv, tm, tn_shift)
    gate_cols = lambda j: jnp.where(j == 0, o1, o3 + (j - 1) * d_a)
    gates = _project(xn, w, gate_cols, d_a + d_b + 2 * d, tm, d_a, n_silu_blocks=2)
    u, (wa, wb, wout, wpg, wple) = _project_glu(
        xn, w, o2, o2 + d_b, d_b, tm, _largest_tile(d_b, 512, 128),
        casts=[w_proj_a[0], w_proj_b[0], w_out[0], w_ple_gate[0], w_ple[0]])

    oa, st = _wkv_prompt(p_rkv, p_lora, gates, mu, par, lora_w, batch, seq, d_a)
    c = _conv_prompt(u, conv_w[0], conv_b[0], batch, seq, _largest_tile(seq, 512, CONV_STRIP))
    y_prompt = _tail(oa, c, gates, x2, p_prompt[0].reshape(m_p, -1), cln, wa, wb, wout, wpg, wple, fg,
                     _largest_tile(m_p, 256, 8), gate_b_block, merge_block).reshape(batch, seq, d)
    last_rows = lambda a: jnp.concatenate([a[(b + 1) * seq - 1:(b + 1) * seq] for b in range(batch)])
    new_shift_p = jnp.concatenate([last_rows(p_rkv), last_rows(p_lora)], axis=1)[None]
    st = st.reshape(batch, n_pairs, 2, HEAD, 2, HEAD)
    new_wkv_p = jnp.stack([st[:, :, 0, :, 0, :], st[:, :, 1, :, 1, :]], axis=2)
    new_wkv_p = jnp.swapaxes(new_wkv_p.reshape(batch, n_heads, HEAD, HEAD), -1, -2)[None]
    new_conv_p = jnp.stack([u[(b + 1) * seq - (taps - 1):(b + 1) * seq] for b in range(batch)])[None]

    tm_s = _largest_tile(dec_batch, 256, 8)
    assert m_p % tm_s == 0
    oa_s, new_wkv_t = _wkv_sample(p_rkv, p_lora, state_shift[0], gates, mu, par, lora_w,
                                  jnp.transpose(state_wkv[0], (1, 2, 3, 0)), d_a, m_p)
    new_wkv_s = jnp.transpose(new_wkv_t, (3, 0, 1, 2))
    c_s, new_conv_t = _conv_sample(u, jnp.transpose(state_conv[0], (1, 0, 2)), conv_w[0], conv_b[0],
                                   _largest_tile(dec_batch, 32, 8), m_p)
    new_conv_s = jnp.transpose(new_conv_t, (1, 0, 2))
    y_sample = _tail(oa_s, c_s, gates, xs2, p_sample[0].reshape(dec_batch, -1), cln, wa, wb, wout, wpg,
                     wple, fg, tm_s, gate_b_block, merge_block, row0=m_p).reshape(dec_batch, 1, d)
    new_shift_s = jnp.concatenate([p_rkv[m_p:], p_lora[m_p:]], axis=1)[None]
    new_conv_s = new_conv_s[None]

    return (y_prompt, y_sample, new_shift_p, new_wkv_p, new_conv_p, new_shift_s, new_wkv_s[None],
            new_conv_s)
```

```python
import functools
import math

import jax
import jax.numpy as jnp
from jax import lax
from jax.experimental import pallas as pl
from jax.experimental.pallas import tpu as pltpu

F32 = jnp.float32
BF16 = jnp.bfloat16

HEAD = 64
PAIR = 2 * HEAD
LORA = 64
CHUNK = 64
WKV_SEQS_PER_STEP = 4
RMS_EPS = 1e-6
LN_EPS = 1e-5
GN_EPS = 64e-5
LOG_DECAY_SCALE = math.exp(-0.5)
VMEM_LIMIT = 56 * 1024 * 1024


def _params(n_axes, vmem=VMEM_LIMIT):
    return pltpu.CompilerParams(dimension_semantics=("arbitrary",) * n_axes, vmem_limit_bytes=vmem)


def _sigmoid(x):
    return 1.0 / (1.0 + jnp.exp(-x))


def _dot(a, b):
    return jnp.dot(a, b, preferred_element_type=F32)


def _dot_nt(a, b):
    return lax.dot_general(a, b, (((1,), (1,)), ((), ())), preferred_element_type=F32)


def _dot_tn(a, b):
    return lax.dot_general(a, b, (((0,), (0,)), ((), ())), preferred_element_type=F32)


def _rmsnorm_kernel(xp_ref, xs_ref, g_ref, w_ref, o_ref, ol_ref, wb_ref, *, n_prompt_blocks):
    i = pl.program_id(0)

    @pl.when(i == 0)
    def _():
        wb_ref[...] = w_ref[...].astype(BF16)

    def norm(x):
        ms = jnp.mean(x * x, axis=-1, keepdims=True)
        return (x * lax.rsqrt(ms + RMS_EPS) * g_ref[...]).astype(o_ref.dtype)

    @pl.when(i < n_prompt_blocks)
    def _():
        xn = norm(xp_ref[...])
        o_ref[...] = xn
        ol_ref[...] = _dot(xn, wb_ref[...])

    @pl.when(i == n_prompt_blocks)
    def _():
        xn = norm(xs_ref[...])
        o_ref[0:xs_ref.shape[0], :] = xn
        ol_ref[0:xs_ref.shape[0], :] = _dot(xn, wb_ref[...])


def _rmsnorm_bf16(x_prompt, x_sample, g, w, col, n_cols, tm):
    m_p, d = x_prompt.shape
    m_s = x_sample.shape[0]
    n_blocks = m_p // tm
    assert m_p % tm == 0 and m_s <= tm and col % 128 == 0
    kern = functools.partial(_rmsnorm_kernel, n_prompt_blocks=n_blocks)
    return pl.pallas_call(
        kern,
        out_shape=(jax.ShapeDtypeStruct((m_p + m_s, d), BF16),
                   jax.ShapeDtypeStruct((m_p + m_s, n_cols), F32)),
        grid=(n_blocks + 1,),
        in_specs=[pl.BlockSpec((tm, d), lambda i: (jnp.minimum(i, n_blocks - 1), 0)),
                  pl.BlockSpec((m_s, d), lambda i: (0, 0)),
                  pl.BlockSpec((1, d), lambda i: (0, 0)),
                  pl.BlockSpec((pl.Element(d), pl.Element(n_cols)), lambda i: (0, col))],
        out_specs=(pl.BlockSpec((tm, d), lambda i: (i, 0)),
                   pl.BlockSpec((tm, n_cols), lambda i: (i, 0))),
        scratch_shapes=[pltpu.VMEM((d, n_cols), BF16)],
        compiler_params=_params(1),
        name="rmsnorm",
    )(x_prompt, x_sample, g, w)


def _proj_kernel(x_ref, w_ref, o_ref, wb_ref, *, n_silu_blocks):
    @pl.when(pl.program_id(1) == 0)
    def _():
        wb_ref[...] = w_ref[...].astype(BF16)

    y = _dot(x_ref[...], wb_ref[...])
    if n_silu_blocks is None:
        o_ref[...] = y
    else:
        s = _sigmoid(y)
        o_ref[...] = jnp.where(pl.program_id(0) < n_silu_blocks, y * s, s)


def _project(xn, w, col_start, n_out, tm, tn, n_silu_blocks=None):
    m, d = xn.shape
    kern = functools.partial(_proj_kernel, n_silu_blocks=n_silu_blocks)
    return pl.pallas_call(
        kern,
        out_shape=jax.ShapeDtypeStruct((m, n_out), F32),
        grid=(n_out // tn, m // tm),
        in_specs=[pl.BlockSpec((tm, d), lambda j, i: (i, 0)),
                  pl.BlockSpec((pl.Element(d), pl.Element(tn)),
                               lambda j, i: (0, pl.multiple_of(col_start(j), 128)))],
        out_specs=pl.BlockSpec((tm, tn), lambda j, i: (i, j)),
        scratch_shapes=[pltpu.VMEM((d, tn), BF16)],
        compiler_params=_params(2),
        name="in_proj",
    )(xn, w)


def _glu_kernel(*refs, n_casts):
    x_ref, wa_ref, wb_ref = refs[:3]
    cast_in = refs[3:3 + n_casts]
    o_ref = refs[3 + n_casts]
    cast_out = refs[4 + n_casts:4 + 2 * n_casts]
    wab_ref, wbb_ref = refs[4 + 2 * n_casts:]

    @pl.when(pl.program_id(1) == 0)
    def _():
        wab_ref[...] = wa_ref[...].astype(BF16)
        wbb_ref[...] = wb_ref[...].astype(BF16)

    x = x_ref[...]
    o_ref[...] = _dot(x, wab_ref[...]) * _sigmoid(_dot(x, wbb_ref[...]))
    for src, dst in zip(cast_in, cast_out):
        dst[...] = src[...].astype(BF16)


def _project_glu(xn, w, col_a, col_b, n_out, tm, tn, casts):
    m, d = xn.shape
    nj, ni = n_out // tn, m // tm
    steps = nj * ni
    for a in casts:
        assert a.shape[0] % (16 * steps) == 0, (a.shape, steps)
    step_rows = lambda a: pl.BlockSpec((a.shape[0] // steps, a.shape[1]), lambda j, i: (j * ni + i, 0))
    w_cols = lambda col: pl.BlockSpec((pl.Element(d), pl.Element(tn)),
                                      lambda j, i: (0, pl.multiple_of(col + j * tn, 128)))
    kern = functools.partial(_glu_kernel, n_casts=len(casts))
    out = pl.pallas_call(
        kern,
        out_shape=[jax.ShapeDtypeStruct((m, n_out), F32)]
        + [jax.ShapeDtypeStruct(a.shape, BF16) for a in casts],
        grid=(nj, ni),
        in_specs=[pl.BlockSpec((tm, d), lambda j, i: (i, 0)), w_cols(col_a), w_cols(col_b)]
        + [step_rows(a) for a in casts],
        out_specs=[pl.BlockSpec((tm, tn), lambda j, i: (i, j))] + [step_rows(a) for a in casts],
        scratch_shapes=[pltpu.VMEM((d, tn), BF16), pltpu.VMEM((d, tn), BF16)],
        compiler_params=_params(2),
        name="in_proj_glu",
    )(xn, w, w, *casts)
    return out[0], out[1:]


def _head_sum(x, block_ones, split=False):
    hi = x.astype(BF16)
    if not split:
        return _dot(hi, block_ones)
    lo = (x - hi.astype(F32)).astype(BF16)
    return _dot(hi, block_ones) + _dot(lo, block_ones)


def _head_sums(xs, block_ones, split=False):
    rows = xs[0].shape[0]
    s = _head_sum(jnp.concatenate(xs, axis=0), block_ones, split)
    return [s[i * rows:(i + 1) * rows] for i in range(len(xs))]


def _block_ones():
    ri = lax.broadcasted_iota(jnp.int32, (PAIR, PAIR), 0)
    ci = lax.broadcasted_iota(jnp.int32, (PAIR, PAIR), 1)
    return jnp.where((ri < HEAD) == (ci < HEAD), 1.0, 0.0).astype(BF16)


def _token_shift(p, prev, mu):
    return p + mu * (prev - p)


def _lora_out(z, lora_w):
    lane = lax.broadcasted_iota(jnp.int32, z.shape, 1)
    z = jnp.where(lane < LORA, jnp.tanh(z), z)
    return _dot(z.astype(BF16), lora_w)


def _pair_slices(d_a):
    return [slice(i * PAIR, (i + 1) * PAIR) for i in range(d_a // PAIR)]


def _pair_vectors(xs_list, lo_list, par_ref, d_a, block_ones):
    sls = _pair_slices(d_a)
    shifted = lambda sl, off: slice(off + sl.start, off + sl.stop)
    items = [(xs, lo, sl) for xs, lo in zip(xs_list, lo_list) for sl in sls]
    r = [xs[:, sl] for xs, _, sl in items]
    k = [xs[:, shifted(sl, d_a)] for xs, _, sl in items]
    v = [xs[:, shifted(sl, 2 * d_a)] for xs, _, sl in items]
    kkr = [ki * par_ref[2:3, sl] for ki, (_, _, sl) in zip(k, items)]
    ss = _head_sums([x * x for x in kkr], block_ones)
    kk = [x * jnp.minimum(lax.rsqrt(s), 1e12) for x, s in zip(kkr, ss)]
    log_decay = [-LOG_DECAY_SCALE * _sigmoid(par_ref[0:1, sl] + lo[:, sl]) for _, lo, sl in items]
    a = [_sigmoid(par_ref[1:2, sl] + lo[:, shifted(sl, d_a)]) for _, lo, sl in items]
    kf = [ki * (1.0 + (ai - 1.0) * par_ref[3:4, sl]) for ki, ai, (_, _, sl) in zip(k, a, items)]
    return r, kf, v, log_decay, kk, a


def _head_norm_bonus(y, r, kf, v, par_ref, d_a, block_ones):
    sls = _pair_slices(d_a) * (len(y) // (d_a // PAIR))
    mean = [s * (1.0 / HEAD) for s in _head_sums(y, block_ones, split=True)]
    d = [x - m for x, m in zip(y, mean)]
    var = [s * (1.0 / HEAD) for s in _head_sums([x * x for x in d], block_ones)]
    rk = _head_sums([ri * ki * par_ref[4:5, sl] for ri, ki, sl in zip(r, kf, sls)], block_ones)
    return [x * lax.rsqrt(vr + GN_EPS) * par_ref[5:6, sl] + par_ref[6:7, sl] + s * vi
            for x, vr, s, vi, sl in zip(d, var, rk, v, sls)]


def _stack_heads(x, lane_lo):
    xb = x.astype(BF16)
    zero = jnp.zeros_like(xb)
    return jnp.concatenate([jnp.where(lane_lo, xb, zero), jnp.where(lane_lo, zero, xb)], axis=0)


def _cumsum_rows(xs):
    n = xs[0].shape[0]
    tri = (lax.broadcasted_iota(jnp.int32, (n, n), 1) <= lax.broadcasted_iota(jnp.int32, (n, n), 0))
    tri = jnp.where(tri, 1.0, 0.0).astype(BF16)
    x = jnp.concatenate(xs, axis=1)
    hi = x.astype(BF16)
    lo = (x - hi.astype(F32)).astype(BF16)
    s = _dot(tri, hi) + _dot(tri, lo)
    return [s[:, i * PAIR:(i + 1) * PAIR] for i in range(len(xs))]


def _wkv_chunk_kernel(*refs, d_a, nb):
    ps_refs, pl_refs, sga_refs = refs[:nb], refs[nb:2 * nb], refs[2 * nb:3 * nb]
    mu_ref, par_ref, lora_ref, oa_ref, sfin_ref, st_ref, carry_ref = refs[3 * nb:]
    c = pl.program_id(1)
    n_pairs = d_a // PAIR
    C = CHUNK
    n_rkv = 3 * d_a

    @pl.when(c == 0)
    def _():
        carry_ref[...] = jnp.zeros_like(carry_ref)
        st_ref[...] = jnp.zeros_like(st_ref)

    def shifted(p, prev_row, mu):
        row = lax.broadcasted_iota(jnp.int32, p.shape, 0)
        return _token_shift(p, jnp.where(row == 0, prev_row, pltpu.roll(p, 1, axis=0)), mu)

    xs, lo = [], []
    for bi in range(nb):
        p, pz = ps_refs[bi][...], pl_refs[bi][...]
        xs.append(shifted(p, carry_ref[bi, :, :n_rkv], mu_ref[:, :n_rkv]))
        lo.append(_lora_out(shifted(pz, carry_ref[bi, :, n_rkv:], mu_ref[:, n_rkv:]), lora_ref[...]))
        carry_ref[bi, :, :n_rkv] = p[C - 1:C, :]
        carry_ref[bi, :, n_rkv:] = pz[C - 1:C, :]

    block_ones = _block_ones()
    lane_lo = lax.broadcasted_iota(jnp.int32, (C, PAIR), 1) < HEAD
    ti = lax.broadcasted_iota(jnp.int32, (C, PAIR), 0)
    si = lax.broadcasted_iota(jnp.int32, (C, PAIR), 1) % C
    strict_lower = si < ti
    lower2 = (lax.broadcasted_iota(jnp.int32, (C, 2 * PAIR), 1) % C
              <= lax.broadcasted_iota(jnp.int32, (C, 2 * PAIR), 0))
    eye = jnp.where(si == ti, 1.0, 0.0)
    cat0 = lambda x, y: jnp.concatenate([x, y], axis=0)

    r, kf, v, log_decay, kk, a = _pair_vectors(xs, lo, par_ref, d_a, block_ones)
    cl = _cumsum_rows(log_decay)
    cend = [x[C - 1:C, :] for x in cl]
    e_neg = [jnp.exp(-x) for x in cl]
    e_end = [jnp.exp(ce - x) for ce, x in zip(cend, cl)]
    kka = [x * y for x, y in zip(kk, a)]
    stack = lambda xs_: [_stack_heads(x, lane_lo) for x in xs_]
    a_p = [(-x * jnp.exp(c_ - ld)).astype(BF16) for x, c_, ld in zip(kk, cl, log_decay)]
    r_p = [(x * jnp.exp(c_)).astype(BF16) for x, c_ in zip(r, cl)]
    a_s = stack(a_p)
    b_s = stack([x * e for x, e in zip(kka, e_neg)])
    k_s = stack([x * e for x, e in zip(kf, e_neg)])
    v_s = stack(v)
    bh_s = stack([x * e for x, e in zip(kka, e_end)])
    kh_s = stack([x * e for x, e in zip(kf, e_end)])
    pend_col = [jnp.transpose(jnp.broadcast_to(jnp.exp(ce), (PAIR, PAIR))) for ce in cend]

    m1 = [_dot_nt(cat0(x, y), cat0(z, w)) for x, y, z, w in zip(a_p, r_p, b_s, k_s)]
    a_ab = [jnp.where(strict_lower, m[:C, :PAIR], 0.0) for m in m1]
    a_ak = [jnp.where(strict_lower, m[:C, PAIR:], 0.0).astype(BF16) for m in m1]
    a_rbk = [jnp.where(lower2, m[C:], 0.0).astype(BF16) for m in m1]
    g = [_dot(x, y) for x, y in zip(a_ak, v_s)]

    apow = [_dot(x.astype(BF16), _stack_heads(x, lane_lo)) for x in a_ab]
    tinv = [eye + x for x in a_ab]
    n_levels = CHUNK.bit_length() - 1
    for lvl in range(1, n_levels):
        rhs = stack(apow)
        if lvl < n_levels - 1:
            both = [_dot(cat0(x.astype(BF16), t.astype(BF16)), y) for x, t, y in zip(apow, tinv, rhs)]
            apow = [x[:C] for x in both]
            tinv = [t + x[C:] for t, x in zip(tinv, both)]
        else:
            tinv = [t + _dot(t.astype(BF16), y) for t, y in zip(tinv, rhs)]

    g_s = stack(g)
    wx = [_dot(t.astype(BF16), jnp.concatenate([x, y], axis=1)) for t, x, y in zip(tinv, a_s, g_s)]
    seq_pair = [(bi, pi) for bi in range(nb) for pi in range(n_pairs)]
    st = [st_ref[bi, pi] for bi, pi in seq_pair]
    uy = [_dot(cat0(w[:, :PAIR].astype(BF16), x), s_.astype(BF16)) for w, x, s_ in zip(wx, r_p, st)]
    uv_s = [cat0(_stack_heads(x[:C] + w[:, PAIR:], lane_lo), y) for x, w, y in zip(uy, wx, v_s)]
    st_upd = [_dot_tn(cat0(x, y), z) for x, y, z in zip(bh_s, kh_s, uv_s)]
    y_uv = [_dot(x, z) for x, z in zip(a_rbk, uv_s)]
    st_new = [pend_col[i] * st[i] + st_upd[i] for i in range(len(seq_pair))]
    for s_, (bi, pi) in zip(st_new, seq_pair):
        st_ref[bi, pi] = s_

    @pl.when(c == pl.num_programs(1) - 1)
    def _():
        for s_, (bi, pi) in zip(st_new, seq_pair):
            s_t = s_.T
            sfin_ref[bi, 2 * pi] = s_t[:HEAD, :HEAD]
            sfin_ref[bi, 2 * pi + 1] = s_t[HEAD:, HEAD:]

    y = [x[C:] + z for x, z in zip(uy, y_uv)]
    o = _head_norm_bonus(y, r, kf, v, par_ref, d_a, block_ones)
    sls = _pair_slices(d_a)
    for oi, (bi, pi) in zip(o, seq_pair):
        oa_ref[bi, :, sls[pi]] = (oi * sga_refs[bi][:, sls[pi]]).astype(oa_ref.dtype)


def _wkv_prompt(p_rkv, p_lora, gates, mu, par, lora_w, batch, seq, d_a):
    n_chunks = seq // CHUNK
    shift_w = mu.shape[1]
    n_pairs = d_a // PAIR
    nb = WKV_SEQS_PER_STEP if batch % WKV_SEQS_PER_STEP == 0 else 1
    kern = functools.partial(_wkv_chunk_kernel, d_a=d_a, nb=nb)
    seq_rows = lambda bi: (lambda b, c: ((b * nb + bi) * n_chunks + c, 0))
    n_heads = d_a // HEAD
    oa, final_state = pl.pallas_call(
        kern,
        out_shape=(jax.ShapeDtypeStruct((batch, seq, d_a), BF16),
                   jax.ShapeDtypeStruct((batch, n_heads, HEAD, HEAD), F32)),
        grid=(batch // nb, n_chunks),
        in_specs=([pl.BlockSpec((CHUNK, 3 * d_a), seq_rows(bi)) for bi in range(nb)]
                  + [pl.BlockSpec((CHUNK, 2 * LORA), seq_rows(bi)) for bi in range(nb)]
                  + [pl.BlockSpec((CHUNK, d_a), seq_rows(bi)) for bi in range(nb)]
                  + [pl.BlockSpec((1, shift_w), lambda b, c: (0, 0)),
                     pl.BlockSpec((8, d_a), lambda b, c: (0, 0)),
                     pl.BlockSpec((2 * LORA, 2 * d_a), lambda b, c: (0, 0))]),
        out_specs=(pl.BlockSpec((nb, CHUNK, d_a), lambda b, c: (b, c, 0)),
                   pl.BlockSpec((nb, n_heads, HEAD, HEAD), lambda b, c: (b, 0, 0, 0))),
        scratch_shapes=[pltpu.VMEM((nb, n_pairs, PAIR, PAIR), F32), pltpu.VMEM((nb, 1, shift_w), F32)],
        compiler_params=_params(2),
        name="wkv_chunked",
    )(*([p_rkv] * nb + [p_lora] * nb + [gates] * nb), mu, par, lora_w)
    return oa.reshape(batch * seq, d_a), final_state


STEP_ROWS = 8


def _wkv_step_kernel(ps_ref, pl_ref, prev_ref, sga_ref, mu_ref, par_ref, lora_ref, s_ref, oa_ref, so_ref,
                     vt_ref, rkv_ref, y_ref, *, d_a):
    h = pl.program_id(0)
    n_heads = d_a // HEAD
    n_rkv = 3 * d_a
    sls = _pair_slices(d_a)

    @pl.when(h == 0)
    def _():
        xs = _token_shift(ps_ref[...], prev_ref[:, :n_rkv], mu_ref[:, :n_rkv])
        lo = _lora_out(_token_shift(pl_ref[...], prev_ref[:, n_rkv:], mu_ref[:, n_rkv:]), lora_ref[...])
        r, kf, v, log_decay, kk, a = _pair_vectors([xs], [lo], par_ref, d_a, _block_ones())
        for pi, sl in enumerate(sls):
            vecs = (-kk[pi], jnp.exp(log_decay[pi]), kk[pi] * a[pi], kf[pi], r[pi], v[pi])
            for i, x in enumerate(vecs):
                vt_ref[i, sl, :] = x.T
            for i, x in enumerate((r[pi], kf[pi], v[pi])):
                rkv_ref[i, :, sl] = x

    base = pl.multiple_of(h * HEAD, HEAD)
    a_t, w_t, b_t, k_t, r_t = (vt_ref[i, pl.ds(base, HEAD), :] for i in range(5))

    def rows(g, carry):
        i0 = pl.multiple_of(g * STEP_ROWS, STEP_ROWS)
        row0 = pl.multiple_of(base + i0, STEP_ROWS)
        v_rows = vt_ref[5, pl.ds(row0, STEP_ROWS), :]
        ys = []
        for ii in range(STEP_ROWS):
            s = s_ref[0, i0 + ii]
            sa = jnp.sum(s * a_t, axis=0, keepdims=True)
            s_new = s * w_t + sa * b_t + v_rows[ii:ii + 1, :] * k_t
            so_ref[0, i0 + ii] = s_new
            ys.append(jnp.sum(s_new * r_t, axis=0, keepdims=True))
        y_ref[pl.ds(row0, STEP_ROWS), :] = jnp.concatenate(ys, axis=0)
        return carry

    lax.fori_loop(0, HEAD // STEP_ROWS, rows, 0)

    @pl.when(h == n_heads - 1)
    def _():
        y = [y_ref[sl, :].T for sl in sls]
        r, kf, v = ([rkv_ref[i, :, sl] for sl in sls] for i in range(3))
        o = _head_norm_bonus(y, r, kf, v, par_ref, d_a, _block_ones())
        for oi, sl in zip(o, sls):
            oa_ref[:, sl] = (oi * sga_ref[:, sl]).astype(oa_ref.dtype)


def _wkv_sample(p_rkv, p_lora, shift_prev, gates, mu, par, lora_w, state_t, d_a, row0):
    batch, shift_w = shift_prev.shape
    n_heads = d_a // HEAD
    assert row0 % batch == 0 and state_t.shape == (n_heads, HEAD, HEAD, batch)
    blk0 = row0 // batch
    const = lambda h: (0, 0)
    kern = functools.partial(_wkv_step_kernel, d_a=d_a)
    return pl.pallas_call(
        kern,
        out_shape=(jax.ShapeDtypeStruct((batch, d_a), BF16),
                   jax.ShapeDtypeStruct(state_t.shape, F32)),
        grid=(n_heads,),
        in_specs=[pl.BlockSpec((batch, 3 * d_a), lambda h: (blk0, 0)),
                  pl.BlockSpec((batch, 2 * LORA), lambda h: (blk0, 0)),
                  pl.BlockSpec((batch, shift_w), const),
                  pl.BlockSpec((batch, d_a), lambda h: (blk0, 0)),
                  pl.BlockSpec((1, shift_w), const),
                  pl.BlockSpec((8, d_a), const),
                  pl.BlockSpec((2 * LORA, 2 * d_a), const),
                  pl.BlockSpec((1, HEAD, HEAD, batch), lambda h: (h, 0, 0, 0))],
        out_specs=(pl.BlockSpec((batch, d_a), const),
                   pl.BlockSpec((1, HEAD, HEAD, batch), lambda h: (h, 0, 0, 0))),
        scratch_shapes=[pltpu.VMEM((6, d_a, batch), F32), pltpu.VMEM((3, batch, d_a), F32),
                        pltpu.VMEM((d_a, batch), F32)],
        compiler_params=_params(1),
        name="wkv_step",
    )(p_rkv, p_lora, shift_prev, gates, mu, par, lora_w, state_t)


CONV_PAD = 32
CONV_STRIP = 16


def _conv_prompt_kernel(u_ref, w_ref, b_ref, o_ref, buf_ref, c_ref, *, tt, taps):
    t = pl.program_id(1)
    sub = buf_ref.shape[1]
    lanes = [slice(s * 128, (s + 1) * 128) for s in range(sub)]

    @pl.when(t == 0)
    def _():
        buf_ref[0:CONV_PAD] = jnp.zeros((CONV_PAD,) + buf_ref.shape[1:], F32)

    u = u_ref[...]
    buf_ref[CONV_PAD:CONV_PAD + tt] = jnp.swapaxes(jnp.stack([u[:, ls] for ls in lanes], axis=0), 0, 1)
    off = CONV_PAD - (taps - 1)
    w = [w_ref[k] for k in range(taps)]
    bias = b_ref[0]

    def strip(s, carry):
        t0 = s * CONV_STRIP
        acc = [bias] * CONV_STRIP
        for i in range(CONV_STRIP + taps - 1):
            x = buf_ref[off + t0 + i]
            for j in range(CONV_STRIP):
                if 0 <= i - j < taps:
                    acc[j] = acc[j] + w[i - j] * x
        for j in range(CONV_STRIP):
            c_ref[t0 + j] = acc[j]
        return carry

    lax.fori_loop(0, tt // CONV_STRIP, strip, 0)
    buf_ref[0:CONV_PAD] = buf_ref[tt:tt + CONV_PAD]
    c = jnp.swapaxes(c_ref[...], 0, 1)
    for s, ls in enumerate(lanes):
        o_ref[:, ls] = c[s]


def _conv_prompt(u, conv_w, conv_b, batch, seq, tt):
    d_b = u.shape[1]
    taps = conv_w.shape[0]
    nt = seq // tt
    sub = d_b // 128
    assert taps - 1 <= CONV_PAD and tt % CONV_STRIP == 0 and tt >= CONV_PAD
    kern = functools.partial(_conv_prompt_kernel, tt=tt, taps=taps)
    return pl.pallas_call(
        kern,
        out_shape=jax.ShapeDtypeStruct((batch * seq, d_b), F32),
        grid=(batch, nt),
        in_specs=[pl.BlockSpec((tt, d_b), lambda b, t: (b * nt + t, 0)),
                  pl.BlockSpec((taps, sub, 128), lambda b, t: (0, 0, 0)),
                  pl.BlockSpec((1, sub, 128), lambda b, t: (0, 0, 0))],
        out_specs=pl.BlockSpec((tt, d_b), lambda b, t: (b * nt + t, 0)),
        scratch_shapes=[pltpu.VMEM((tt + CONV_PAD, sub, 128), F32), pltpu.VMEM((tt, sub, 128), F32)],
        compiler_params=_params(2),
        name="conv_prompt",
    )(u, conv_w.reshape(taps, sub, 128), conv_b.reshape(1, sub, 128))


def _conv_step_kernel(u_ref, prev_ref, w_ref, b_ref, o_ref, hist_ref, *, taps):
    u = u_ref[...]
    c = b_ref[...] + w_ref[taps - 1:taps, :] * u
    for k in range(taps - 1):
        c = c + w_ref[k:k + 1, :] * prev_ref[k]
    o_ref[...] = c
    hist_ref[0:taps - 2] = prev_ref[1:taps - 1]
    hist_ref[taps - 2] = u


def _conv_sample(u, conv_prev_t, conv_w, conv_b, bb, row0):
    batch = conv_prev_t.shape[1]
    d_b = u.shape[1]
    taps = conv_w.shape[0]
    assert row0 % bb == 0 and batch % bb == 0
    blk0 = row0 // bb
    kern = functools.partial(_conv_step_kernel, taps=taps)
    return pl.pallas_call(
        kern,
        out_shape=(jax.ShapeDtypeStruct((batch, d_b), F32),
                   jax.ShapeDtypeStruct(conv_prev_t.shape, F32)),
        grid=(batch // bb,),
        in_specs=[pl.BlockSpec((bb, d_b), lambda i: (blk0 + i, 0)),
                  pl.BlockSpec((taps - 1, bb, d_b), lambda i: (0, i, 0)),
                  pl.BlockSpec((taps, d_b), lambda i: (0, 0)),
                  pl.BlockSpec((1, d_b), lambda i: (0, 0))],
        out_specs=(pl.BlockSpec((bb, d_b), lambda i: (i, 0)),
                   pl.BlockSpec((taps - 1, bb, d_b), lambda i: (0, i, 0))),
        compiler_params=_params(1),
        name="conv_step",
    )(u, conv_prev_t, conv_w, conv_b.reshape(1, d_b))


def _tail_kernel(oa_ref, c_ref, gb_ref, sga_ref, sgb_ref, x_ref, p_ref, cln_ref, wa_ref, wb_ref, wout_ref,
                 wpg_ref, wple_ref, fg_ref, o_ref):
    c = c_ref[...]
    mean = jnp.mean(c, axis=-1, keepdims=True)
    dc = c - mean
    var = jnp.mean(dc * dc, axis=-1, keepdims=True)
    cf = dc * lax.rsqrt(var + LN_EPS) * cln_ref[0:1, :] + cln_ref[1:2, :]
    cb = (cf * _sigmoid(cf) * gb_ref[...]).astype(BF16)
    m = sga_ref[...] * _dot(oa_ref[...], wa_ref[...]) + sgb_ref[...] * _dot(cb, wb_ref[...])
    h = x_ref[...] + _dot(m.astype(BF16), wout_ref[...])
    gate = _sigmoid(_dot(h.astype(BF16), wpg_ref[...]))
    h = h + gate * _dot(p_ref[...].astype(BF16), wple_ref[...])
    ms = jnp.mean(h * h, axis=-1, keepdims=True)
    o_ref[...] = h * lax.rsqrt(ms + RMS_EPS) * fg_ref[...]


def _tail(oa, c, gates, x, p, cln, wa, wb, wout, wpg, wple, fg, tm, gate_b_block, merge_block, row0=0):
    m, d = x.shape
    d_a = oa.shape[1]
    d_b = c.shape[1]
    ple = p.shape[1]
    blk0 = row0 // tm
    row = lambda i: (i, 0)
    const = lambda i: (0, 0)
    resident = lambda shape: pl.BlockSpec(shape, const, pipeline_mode=pl.Buffered(1))
    return pl.pallas_call(
        _tail_kernel,
        out_shape=jax.ShapeDtypeStruct((m, d), F32),
        grid=(m // tm,),
        in_specs=[pl.BlockSpec((tm, d_a), row), pl.BlockSpec((tm, d_b), row),
                  pl.BlockSpec((tm, d_b), lambda i: (blk0 + i, gate_b_block)),
                  pl.BlockSpec((tm, d), lambda i: (blk0 + i, merge_block)),
                  pl.BlockSpec((tm, d), lambda i: (blk0 + i, merge_block + 1)),
                  pl.BlockSpec((tm, d), row), pl.BlockSpec((tm, ple), row),
                  resident((2, d_b)),
                  resident((d_a, d)), resident((d_b, d)), resident((d, d)), resident((d, d)),
                  resident((ple, d)), resident((1, d))],
        out_specs=pl.BlockSpec((tm, d), row),
        compiler_params=_params(1),
        name="tail",
    )(oa, c, gates, gates, gates, x, p, cln, wa, wb, wout, wpg, wple, fg)


def _largest_tile(n, cap, align):
    t = min(n, cap)
    while n % t or t % align:
        t -= 1
    return t


def kernel(x_prompt, x_sample, state_shift, state_wkv, state_conv, p_prompt, p_sample, norm_g, w_in,
           shift_mu, w0, w_lora_b, a0, a_lora_b, k_k, k_a, r_k, lnx_g, lnx_b, w_proj_a, conv_w, conv_b,
           cln_g, cln_b, w_proj_b, w_out, w_ple, w_ple_gate, final_g):
    depth = w_in.shape[0]
    batch, seq, d = x_prompt.shape
    dec_batch, dec_seq, _ = x_sample.shape
    d_a = w_proj_a.shape[1]
    d_b = w_proj_b.shape[1]
    shift_w = shift_mu.shape[1]
    taps = conv_w.shape[1]
    assert depth == 1 and dec_seq == 1 and d_a == d_b and 2 * d_a == d
    assert shift_w == 3 * d_a + 2 * LORA and d_a % PAIR == 0 and seq % CHUNK == 0

    o1 = shift_w
    o2 = o1 + d_a
    o3 = o2 + 2 * d_b
    o4 = o3 + d_b
    w = w_in[0]
    lora_w = jnp.zeros((2 * LORA, 2 * d_a), F32)
    lora_w = lora_w.at[:LORA, :d_a].set(w_lora_b[0]).at[LORA:, d_a:].set(a_lora_b[0]).astype(BF16)
    zeros_a = jnp.zeros((d_a,), F32)
    par = jnp.stack([w0[0], a0[0], k_k[0], k_a[0], r_k[0].reshape(d_a), lnx_g[0], lnx_b[0], zeros_a])
    cln = jnp.stack([cln_g[0], cln_b[0]])
    mu = shift_mu
    g_in = norm_g[0].reshape(1, d)
    fg = final_g.reshape(1, d)
    gate_b_block = d_a // d_b
    merge_block = (d_a + d_b) // d

    m_p = batch * seq
    m_all = m_p + dec_batch
    x2 = x_prompt.reshape(m_p, d)
    xs2 = x_sample.reshape(dec_batch, d)
    n_rkv = 3 * d_a
    xn, p_lora = _rmsnorm_bf16(x2, xs2, g_in, w, n_rkv, 2 * LORA, _largest_tile(m_p, 1024, 8))
    tm = _largest_tile(m_all, 1100, 16)
    tn_shift = _largest_tile(n_rkv, 1024, 128)
    p_rkv = _project(xn, w, lambda j: j * tn_shift, n_rkv, tm, tn_shift)
    gate_cols = lambda j: jnp.where(j == 0, o1, o3 + (j - 1) * d_a)
    gates = _project(xn, w, gate_cols, d_a + d_b + 2 * d, tm, d_a, n_silu_blocks=2)
    u, (wa, wb, wout, wpg, wple) = _project_glu(
        xn, w, o2, o2 + d_b, d_b, tm, _largest_tile(d_b, 512, 128),
        casts=[w_proj_a[0], w_proj_b[0], w_out[0], w_ple_gate[0], w_ple[0]])

    oa, new_wkv_p = _wkv_prompt(p_rkv, p_lora, gates, mu, par, lora_w, batch, seq, d_a)
    c = _conv_prompt(u, conv_w[0], conv_b[0], batch, seq, _largest_tile(seq, 512, CONV_STRIP))
    y_prompt = _tail(oa, c, gates, x2, p_prompt[0].reshape(m_p, -1), cln, wa, wb, wout, wpg, wple, fg,
                     _largest_tile(m_p, 256, 8), gate_b_block, merge_block).reshape(batch, seq, d)
    last_rows = lambda a: jnp.concatenate([a[(b + 1) * seq - 1:(b + 1) * seq] for b in range(batch)])
    new_shift_p = jnp.concatenate([last_rows(p_rkv), last_rows(p_lora)], axis=1)[None]
    new_conv_p = jnp.stack([u[(b + 1) * seq - (taps - 1):(b + 1) * seq] for b in range(batch)])[None]

    tm_s = _largest_tile(dec_batch, 256, 8)
    assert m_p % tm_s == 0
    oa_s, new_wkv_t = _wkv_sample(p_rkv, p_lora, state_shift[0], gates, mu, par, lora_w,
                                  jnp.transpose(state_wkv[0], (1, 2, 3, 0)), d_a, m_p)
    new_wkv_s = jnp.transpose(new_wkv_t, (3, 0, 1, 2))
    c_s, new_conv_t = _conv_sample(u, jnp.transpose(state_conv[0], (1, 0, 2)), conv_w[0], conv_b[0],
                                   _largest_tile(dec_batch, 32, 8), m_p)
    new_conv_s = jnp.transpose(new_conv_t, (1, 0, 2))
    y_sample = _tail(oa_s, c_s, gates, xs2, p_sample[0].reshape(dec_batch, -1), cln, wa, wb, wout, wpg,
                     wple, fg, tm_s, gate_b_block, merge_block, row0=m_p).reshape(dec_batch, 1, d)
    new_shift_s = jnp.concatenate([p_rkv[m_p:], p_lora[m_p:]], axis=1)[None]
    new_conv_s = new_conv_s[None]

    return (y_prompt, y_sample, new_shift_p, new_wkv_p[None], new_conv_p, new_shift_s, new_wkv_s[None],
            new_conv_s)
```

```python
import functools
import math

import jax
import jax.numpy as jnp
from jax import lax
from jax.experimental import pallas as pl
from jax.experimental.pallas import tpu as pltpu

F32 = jnp.float32
BF16 = jnp.bfloat16

LANES = 128
SUBLANES = 8
BF16_SUBLANES = 16

HEAD = 64
PAIR = 2 * HEAD
LORA = 64
CHUNK = 64
WKV_SEQS_PER_STEP = 4
RMS_EPS = 1e-6
LN_EPS = 1e-5
GN_EPS = 64e-5
LOG2_DECAY_SCALE = math.exp(-0.5) / math.log(2.0)
VMEM_LIMIT = 56 * 1024 * 1024


def _params(n_axes, vmem=VMEM_LIMIT):
    return pltpu.CompilerParams(dimension_semantics=("arbitrary",) * n_axes, vmem_limit_bytes=vmem)


def _sigmoid(x):
    return 1.0 / (1.0 + jnp.exp(-x))


def _dot(a, b):
    return jnp.dot(a, b, preferred_element_type=F32)


def _dot_nt(a, b):
    return lax.dot_general(a, b, (((1,), (1,)), ((), ())), preferred_element_type=F32)


def _dot_tn(a, b):
    return lax.dot_general(a, b, (((0,), (0,)), ((), ())), preferred_element_type=F32)


def _rmsnorm_kernel(xp_ref, xs_ref, g_ref, w_ref, o_ref, ol_ref, wb_ref, *, n_prompt_blocks):
    i = pl.program_id(0)

    @pl.when(i == 0)
    def _():
        wb_ref[...] = w_ref[...].astype(BF16)

    def norm(x):
        ms = jnp.mean(x * x, axis=-1, keepdims=True)
        return (x * lax.rsqrt(ms + RMS_EPS) * g_ref[...]).astype(o_ref.dtype)

    @pl.when(i < n_prompt_blocks)
    def _():
        xn = norm(xp_ref[...])
        o_ref[...] = xn
        ol_ref[...] = _dot(xn, wb_ref[...])

    @pl.when(i == n_prompt_blocks)
    def _():
        xn = norm(xs_ref[...])
        o_ref[0:xs_ref.shape[0], :] = xn
        ol_ref[0:xs_ref.shape[0], :] = _dot(xn, wb_ref[...])


def _rmsnorm_bf16(x_prompt, x_sample, g, w, col, n_cols, tm):
    m_p, d = x_prompt.shape
    m_s = x_sample.shape[0]
    n_blocks = m_p // tm
    assert m_p % tm == 0 and m_s <= tm and col % LANES == 0
    kern = functools.partial(_rmsnorm_kernel, n_prompt_blocks=n_blocks)
    return pl.pallas_call(
        kern,
        out_shape=(jax.ShapeDtypeStruct((m_p + m_s, d), BF16),
                   jax.ShapeDtypeStruct((m_p + m_s, n_cols), F32)),
        grid=(n_blocks + 1,),
        in_specs=[pl.BlockSpec((tm, d), lambda i: (jnp.minimum(i, n_blocks - 1), 0)),
                  pl.BlockSpec((m_s, d), lambda i: (0, 0)),
                  pl.BlockSpec((1, d), lambda i: (0, 0)),
                  pl.BlockSpec((pl.Element(d), pl.Element(n_cols)), lambda i: (0, col))],
        out_specs=(pl.BlockSpec((tm, d), lambda i: (i, 0)),
                   pl.BlockSpec((tm, n_cols), lambda i: (i, 0))),
        scratch_shapes=[pltpu.VMEM((d, n_cols), BF16)],
        compiler_params=_params(1),
        name="rmsnorm",
    )(x_prompt, x_sample, g, w)


def _proj_kernel(x_ref, w_ref, o_ref, wb_ref, *, n_silu_blocks):
    @pl.when(pl.program_id(1) == 0)
    def _():
        wb_ref[...] = w_ref[...].astype(BF16)

    y = _dot(x_ref[...], wb_ref[...])
    if n_silu_blocks is None:
        o_ref[...] = y
    else:
        s = _sigmoid(y)
        o_ref[...] = jnp.where(pl.program_id(0) < n_silu_blocks, y * s, s)


def _project(xn, w, col_start, n_out, tm, tn, n_silu_blocks=None):
    m, d = xn.shape
    kern = functools.partial(_proj_kernel, n_silu_blocks=n_silu_blocks)
    return pl.pallas_call(
        kern,
        out_shape=jax.ShapeDtypeStruct((m, n_out), F32),
        grid=(n_out // tn, m // tm),
        in_specs=[pl.BlockSpec((tm, d), lambda j, i: (i, 0)),
                  pl.BlockSpec((pl.Element(d), pl.Element(tn)),
                               lambda j, i: (0, pl.multiple_of(col_start(j), LANES)))],
        out_specs=pl.BlockSpec((tm, tn), lambda j, i: (i, j)),
        scratch_shapes=[pltpu.VMEM((d, tn), BF16)],
        compiler_params=_params(2),
        name="in_proj",
    )(xn, w)


def _glu_kernel(*refs, n_casts):
    x_ref, wa_ref, wb_ref = refs[:3]
    cast_in = refs[3:3 + n_casts]
    o_ref = refs[3 + n_casts]
    cast_out = refs[4 + n_casts:4 + 2 * n_casts]
    wab_ref, wbb_ref = refs[4 + 2 * n_casts:]

    @pl.when(pl.program_id(1) == 0)
    def _():
        wab_ref[...] = wa_ref[...].astype(BF16)
        wbb_ref[...] = wb_ref[...].astype(BF16)

    x = x_ref[...]
    o_ref[...] = _dot(x, wab_ref[...]) * _sigmoid(_dot(x, wbb_ref[...]))
    for src, dst in zip(cast_in, cast_out):
        dst[...] = src[...].astype(BF16)


def _project_glu(xn, w, col_a, col_b, n_out, tm, tn, casts):
    m, d = xn.shape
    nj, ni = n_out // tn, m // tm
    steps = nj * ni
    for a in casts:
        assert a.shape[0] % (BF16_SUBLANES * steps) == 0, (a.shape, steps)
    step_rows = lambda a: pl.BlockSpec((a.shape[0] // steps, a.shape[1]), lambda j, i: (j * ni + i, 0))
    w_cols = lambda col: pl.BlockSpec((pl.Element(d), pl.Element(tn)),
                                      lambda j, i: (0, pl.multiple_of(col + j * tn, LANES)))
    kern = functools.partial(_glu_kernel, n_casts=len(casts))
    out = pl.pallas_call(
        kern,
        out_shape=[jax.ShapeDtypeStruct((m, n_out), F32)]
        + [jax.ShapeDtypeStruct(a.shape, BF16) for a in casts],
        grid=(nj, ni),
        in_specs=[pl.BlockSpec((tm, d), lambda j, i: (i, 0)), w_cols(col_a), w_cols(col_b)]
        + [step_rows(a) for a in casts],
        out_specs=[pl.BlockSpec((tm, tn), lambda j, i: (i, j))] + [step_rows(a) for a in casts],
        scratch_shapes=[pltpu.VMEM((d, tn), BF16), pltpu.VMEM((d, tn), BF16)],
        compiler_params=_params(2),
        name="in_proj_glu",
    )(xn, w, w, *casts)
    return out[0], out[1:]


def _head_sum(x, block_ones, split=False):
    hi = x.astype(BF16)
    if not split:
        return _dot(hi, block_ones)
    lo = (x - hi.astype(F32)).astype(BF16)
    return _dot(hi, block_ones) + _dot(lo, block_ones)


def _head_sums(xs, block_ones, split=False):
    rows = xs[0].shape[0]
    s = _head_sum(jnp.concatenate(xs, axis=0), block_ones, split)
    return [s[i * rows:(i + 1) * rows] for i in range(len(xs))]


def _block_ones(value=1.0):
    ri = lax.broadcasted_iota(jnp.int32, (PAIR, PAIR), 0)
    ci = lax.broadcasted_iota(jnp.int32, (PAIR, PAIR), 1)
    return jnp.where((ri < HEAD) == (ci < HEAD), value, 0.0).astype(BF16)


def _token_shift(p, prev, mu):
    return p + mu * (prev - p)


def _lora_out(z, lora_w):
    lane = lax.broadcasted_iota(jnp.int32, z.shape, 1)
    z = jnp.where(lane < LORA, jnp.tanh(z), z)
    return _dot(z.astype(BF16), lora_w)


def _pair_slices(d_a):
    return [slice(i * PAIR, (i + 1) * PAIR) for i in range(d_a // PAIR)]


def _pair_vectors(xs_list, lo_list, par_ref, d_a, block_ones):
    sls = _pair_slices(d_a)
    shifted = lambda sl, off: slice(off + sl.start, off + sl.stop)
    items = [(xs, lo, sl) for xs, lo in zip(xs_list, lo_list) for sl in sls]
    r = [xs[:, sl] for xs, _, sl in items]
    k = [xs[:, shifted(sl, d_a)] for xs, _, sl in items]
    v = [xs[:, shifted(sl, 2 * d_a)] for xs, _, sl in items]
    kkr = [ki * par_ref[2:3, sl] for ki, (_, _, sl) in zip(k, items)]
    ss = _head_sums([x * x for x in kkr], block_ones)
    kk = [x * jnp.minimum(lax.rsqrt(s), 1e12) for x, s in zip(kkr, ss)]
    log2_decay = [-LOG2_DECAY_SCALE * _sigmoid(par_ref[0:1, sl] + lo[:, sl]) for _, lo, sl in items]
    a = [_sigmoid(par_ref[1:2, sl] + lo[:, shifted(sl, d_a)]) for _, lo, sl in items]
    kf = [ki * (1.0 + (ai - 1.0) * par_ref[3:4, sl]) for ki, ai, (_, _, sl) in zip(k, a, items)]
    return r, kf, v, log2_decay, kk, a


def _head_norm_bonus(y, r, kf, v, par_ref, d_a, block_ones):
    sls = _pair_slices(d_a) * (len(y) // (d_a // PAIR))
    block_mean = _block_ones(1.0 / HEAD)
    mean = _head_sums(y, block_mean, split=True)
    d = [x - m for x, m in zip(y, mean)]
    var = _head_sums([x * x for x in d], block_mean)
    rk = _head_sums([ri * ki * par_ref[4:5, sl] for ri, ki, sl in zip(r, kf, sls)], block_ones)
    return [x * lax.rsqrt(vr + GN_EPS) * par_ref[5:6, sl] + par_ref[6:7, sl] + s * vi
            for x, vr, s, vi, sl in zip(d, var, rk, v, sls)]


def _stack_heads(x, lane_lo):
    xb = x.astype(BF16)
    zero = jnp.zeros_like(xb)
    return jnp.concatenate([jnp.where(lane_lo, xb, zero), jnp.where(lane_lo, zero, xb)], axis=0)


def _cumsum_rows(xs):
    n = xs[0].shape[0]
    tri = (lax.broadcasted_iota(jnp.int32, (n, n), 1) <= lax.broadcasted_iota(jnp.int32, (n, n), 0))
    tri = jnp.where(tri, 1.0, 0.0).astype(BF16)
    x = jnp.concatenate(xs, axis=1)
    hi = x.astype(BF16)
    lo = (x - hi.astype(F32)).astype(BF16)
    s = _dot(tri, hi) + _dot(tri, lo)
    return [s[:, i * PAIR:(i + 1) * PAIR] for i in range(len(xs))]


def _wkv_chunk_kernel(*refs, d_a, nb):
    ps_refs, pl_refs, sga_refs = refs[:nb], refs[nb:2 * nb], refs[2 * nb:3 * nb]
    mu_ref, par_ref, lora_ref, oa_ref, st_ref, carry_ref = refs[3 * nb:]
    c = pl.program_id(1)
    n_pairs = d_a // PAIR
    C = CHUNK
    n_rkv = 3 * d_a

    @pl.when(c == 0)
    def _():
        carry_ref[...] = jnp.zeros_like(carry_ref)
        st_ref[...] = jnp.zeros_like(st_ref)

    def shifted(p, prev_row, mu):
        row = lax.broadcasted_iota(jnp.int32, p.shape, 0)
        return _token_shift(p, jnp.where(row == 0, prev_row, pltpu.roll(p, 1, axis=0)), mu)

    xs, lo = [], []
    for bi in range(nb):
        p, pz = ps_refs[bi][...], pl_refs[bi][...]
        xs.append(shifted(p, carry_ref[bi, :, :n_rkv], mu_ref[:, :n_rkv]))
        lo.append(_lora_out(shifted(pz, carry_ref[bi, :, n_rkv:], mu_ref[:, n_rkv:]), lora_ref[...]))
        carry_ref[bi, :, :n_rkv] = p[C - 1:C, :]
        carry_ref[bi, :, n_rkv:] = pz[C - 1:C, :]

    block_ones = _block_ones()
    lane_lo = lax.broadcasted_iota(jnp.int32, (C, PAIR), 1) < HEAD
    ti = lax.broadcasted_iota(jnp.int32, (C, PAIR), 0)
    si = lax.broadcasted_iota(jnp.int32, (C, PAIR), 1) % C
    strict_lower = si < ti
    lower2 = (lax.broadcasted_iota(jnp.int32, (C, 2 * PAIR), 1) % C
              <= lax.broadcasted_iota(jnp.int32, (C, 2 * PAIR), 0))
    eye = jnp.where(si == ti, 1.0, 0.0)
    cat0 = lambda x, y: jnp.concatenate([x, y], axis=0)

    r, kf, v, log2_decay, kk, a = _pair_vectors(xs, lo, par_ref, d_a, block_ones)
    cl = _cumsum_rows(log2_decay)
    cend = [x[C - 1:C, :] for x in cl]
    e_neg = [jnp.exp2(-x) for x in cl]
    e_end = [jnp.exp2(ce - x) for ce, x in zip(cend, cl)]
    kka = [x * y for x, y in zip(kk, a)]
    stack = lambda xs_: [_stack_heads(x, lane_lo) for x in xs_]
    a_p = [(-x * jnp.exp2(c_ - ld)).astype(BF16) for x, c_, ld in zip(kk, cl, log2_decay)]
    r_p = [(x * jnp.exp2(c_)).astype(BF16) for x, c_ in zip(r, cl)]
    a_s = stack(a_p)
    b_s = stack([x * e for x, e in zip(kka, e_neg)])
    k_s = stack([x * e for x, e in zip(kf, e_neg)])
    v_s = stack(v)
    bh_s = stack([x * e for x, e in zip(kka, e_end)])
    kh_s = stack([x * e for x, e in zip(kf, e_end)])
    pend_col = [jnp.transpose(jnp.broadcast_to(jnp.exp2(ce), (PAIR, PAIR))) for ce in cend]

    m1 = [_dot_nt(cat0(x, y), cat0(z, w)) for x, y, z, w in zip(a_p, r_p, b_s, k_s)]
    a_ab = [jnp.where(strict_lower, m[:C, :PAIR], 0.0) for m in m1]
    keep = lambda mask, x: jnp.where(mask, x, jnp.zeros_like(x))
    a_ak = [keep(strict_lower, m[:C, PAIR:].astype(BF16)) for m in m1]
    a_rbk = [keep(lower2, m[C:].astype(BF16)) for m in m1]
    g = [_dot(x, y) for x, y in zip(a_ak, v_s)]

    apow = [_dot(x.astype(BF16), _stack_heads(x, lane_lo)) for x in a_ab]
    tinv = [eye + x for x in a_ab]
    n_levels = CHUNK.bit_length() - 1
    for lvl in range(1, n_levels):
        rhs = stack(apow)
        if lvl < n_levels - 1:
            both = [_dot(cat0(x.astype(BF16), t.astype(BF16)), y) for x, t, y in zip(apow, tinv, rhs)]
            apow = [x[:C] for x in both]
            tinv = [t + x[C:] for t, x in zip(tinv, both)]
        else:
            tinv = [t + _dot(t.astype(BF16), y) for t, y in zip(tinv, rhs)]

    g_s = stack(g)
    wx = [_dot(t.astype(BF16), jnp.concatenate([x, y], axis=1)) for t, x, y in zip(tinv, a_s, g_s)]
    seq_pair = [(bi, pi) for bi in range(nb) for pi in range(n_pairs)]
    st = [st_ref[bi, pi] for bi, pi in seq_pair]
    uy = [_dot(cat0(w[:, :PAIR].astype(BF16), x), s_.astype(BF16)) for w, x, s_ in zip(wx, r_p, st)]
    uv_s = [cat0(_stack_heads(x[:C] + w[:, PAIR:], lane_lo), y) for x, w, y in zip(uy, wx, v_s)]
    st_upd = [_dot_tn(cat0(x, y), z) for x, y, z in zip(bh_s, kh_s, uv_s)]
    y_uv = [_dot(x, z) for x, z in zip(a_rbk, uv_s)]
    for i, (bi, pi) in enumerate(seq_pair):
        st_ref[bi, pi] = pend_col[i] * st[i] + st_upd[i]
    y = [x[C:] + z for x, z in zip(uy, y_uv)]
    o = _head_norm_bonus(y, r, kf, v, par_ref, d_a, block_ones)
    sls = _pair_slices(d_a)
    for oi, (bi, pi) in zip(o, seq_pair):
        oa_ref[bi, :, sls[pi]] = (oi * sga_refs[bi][:, sls[pi]]).astype(oa_ref.dtype)


def _wkv_prompt(p_rkv, p_lora, gates, mu, par, lora_w, batch, seq, d_a):
    n_chunks = seq // CHUNK
    shift_w = mu.shape[1]
    n_pairs = d_a // PAIR
    nb = WKV_SEQS_PER_STEP if batch % WKV_SEQS_PER_STEP == 0 else 1
    kern = functools.partial(_wkv_chunk_kernel, d_a=d_a, nb=nb)
    seq_rows = lambda bi: (lambda b, c: ((b * nb + bi) * n_chunks + c, 0))
    oa, st = pl.pallas_call(
        kern,
        out_shape=(jax.ShapeDtypeStruct((batch, seq, d_a), BF16),
                   jax.ShapeDtypeStruct((batch, n_pairs, PAIR, PAIR), F32)),
        grid=(batch // nb, n_chunks),
        in_specs=([pl.BlockSpec((CHUNK, 3 * d_a), seq_rows(bi)) for bi in range(nb)]
                  + [pl.BlockSpec((CHUNK, 2 * LORA), seq_rows(bi)) for bi in range(nb)]
                  + [pl.BlockSpec((CHUNK, d_a), seq_rows(bi)) for bi in range(nb)]
                  + [pl.BlockSpec((1, shift_w), lambda b, c: (0, 0)),
                     pl.BlockSpec((8, d_a), lambda b, c: (0, 0)),
                     pl.BlockSpec((2 * LORA, 2 * d_a), lambda b, c: (0, 0))]),
        out_specs=(pl.BlockSpec((nb, CHUNK, d_a), lambda b, c: (b, c, 0)),
                   pl.BlockSpec((nb, n_pairs, PAIR, PAIR), lambda b, c: (b, 0, 0, 0))),
        scratch_shapes=[pltpu.VMEM((nb, 1, shift_w), F32)],
        compiler_params=_params(2),
        name="wkv_chunked",
    )(*([p_rkv] * nb + [p_lora] * nb + [gates] * nb), mu, par, lora_w)
    return oa.reshape(batch * seq, d_a), st


STEP_ROWS = 8


def _wkv_step_kernel(ps_ref, pl_ref, prev_ref, sga_ref, mu_ref, par_ref, lora_ref, s_ref, oa_ref, so_ref,
                     vt_ref, rkv_ref, y_ref, *, d_a):
    h = pl.program_id(0)
    n_heads = d_a // HEAD
    n_rkv = 3 * d_a
    sls = _pair_slices(d_a)

    @pl.when(h == 0)
    def _():
        xs = _token_shift(ps_ref[...], prev_ref[:, :n_rkv], mu_ref[:, :n_rkv])
        lo = _lora_out(_token_shift(pl_ref[...], prev_ref[:, n_rkv:], mu_ref[:, n_rkv:]), lora_ref[...])
        r, kf, v, log2_decay, kk, a = _pair_vectors([xs], [lo], par_ref, d_a, _block_ones())
        for pi, sl in enumerate(sls):
            vecs = (-kk[pi], jnp.exp2(log2_decay[pi]), kk[pi] * a[pi], kf[pi], r[pi], v[pi])
            for i, x in enumerate(vecs):
                vt_ref[i, sl, :] = x.T
            for i, x in enumerate((r[pi], kf[pi], v[pi])):
                rkv_ref[i, :, sl] = x

    base = pl.multiple_of(h * HEAD, HEAD)
    a_t, w_t, b_t, k_t, r_t = (vt_ref[i, pl.ds(base, HEAD), :] for i in range(5))

    def rows(g, carry):
        i0 = pl.multiple_of(g * STEP_ROWS, STEP_ROWS)
        row0 = pl.multiple_of(base + i0, STEP_ROWS)
        v_rows = vt_ref[5, pl.ds(row0, STEP_ROWS), :]
        ys = []
        for ii in range(STEP_ROWS):
            s = s_ref[0, i0 + ii]
            sa = jnp.sum(s * a_t, axis=0, keepdims=True)
            s_new = s * w_t + sa * b_t + v_rows[ii:ii + 1, :] * k_t
            so_ref[0, i0 + ii] = s_new
            ys.append(jnp.sum(s_new * r_t, axis=0, keepdims=True))
        y_ref[pl.ds(row0, STEP_ROWS), :] = jnp.concatenate(ys, axis=0)
        return carry

    lax.fori_loop(0, HEAD // STEP_ROWS, rows, 0)

    @pl.when(h == n_heads - 1)
    def _():
        y = [y_ref[sl, :].T for sl in sls]
        r, kf, v = ([rkv_ref[i, :, sl] for sl in sls] for i in range(3))
        o = _head_norm_bonus(y, r, kf, v, par_ref, d_a, _block_ones())
        for oi, sl in zip(o, sls):
            oa_ref[:, sl] = (oi * sga_ref[:, sl]).astype(oa_ref.dtype)


def _wkv_sample(p_rkv, p_lora, shift_prev, gates, mu, par, lora_w, state_t, d_a, row0):
    batch, shift_w = shift_prev.shape
    n_heads = d_a // HEAD
    assert row0 % batch == 0 and state_t.shape == (n_heads, HEAD, HEAD, batch)
    blk0 = row0 // batch
    const = lambda h: (0, 0)
    kern = functools.partial(_wkv_step_kernel, d_a=d_a)
    return pl.pallas_call(
        kern,
        out_shape=(jax.ShapeDtypeStruct((batch, d_a), BF16),
                   jax.ShapeDtypeStruct(state_t.shape, F32)),
        grid=(n_heads,),
        in_specs=[pl.BlockSpec((batch, 3 * d_a), lambda h: (blk0, 0)),
                  pl.BlockSpec((batch, 2 * LORA), lambda h: (blk0, 0)),
                  pl.BlockSpec((batch, shift_w), const),
                  pl.BlockSpec((batch, d_a), lambda h: (blk0, 0)),
                  pl.BlockSpec((1, shift_w), const),
                  pl.BlockSpec((8, d_a), const),
                  pl.BlockSpec((2 * LORA, 2 * d_a), const),
                  pl.BlockSpec((1, HEAD, HEAD, batch), lambda h: (h, 0, 0, 0))],
        out_specs=(pl.BlockSpec((batch, d_a), const),
                   pl.BlockSpec((1, HEAD, HEAD, batch), lambda h: (h, 0, 0, 0))),
        scratch_shapes=[pltpu.VMEM((6, d_a, batch), F32), pltpu.VMEM((3, batch, d_a), F32),
                        pltpu.VMEM((d_a, batch), F32)],
        compiler_params=_params(1),
        name="wkv_step",
    )(p_rkv, p_lora, shift_prev, gates, mu, par, lora_w, state_t)


CONV_PAD = 32
CONV_STRIP = 16


def _conv_prompt_kernel(u_ref, w_ref, b_ref, o_ref, buf_ref, c_ref, *, tt, taps):
    t = pl.program_id(1)
    sub = buf_ref.shape[1]
    lanes = [slice(s * LANES, (s + 1) * LANES) for s in range(sub)]

    @pl.when(t == 0)
    def _():
        buf_ref[0:CONV_PAD] = jnp.zeros((CONV_PAD,) + buf_ref.shape[1:], F32)

    u = u_ref[...]
    buf_ref[CONV_PAD:CONV_PAD + tt] = jnp.swapaxes(jnp.stack([u[:, ls] for ls in lanes], axis=0), 0, 1)
    off = CONV_PAD - (taps - 1)
    w = [w_ref[k] for k in range(taps)]
    bias = b_ref[0]

    def strip(s, carry):
        t0 = s * CONV_STRIP
        acc = [bias] * CONV_STRIP
        for i in range(CONV_STRIP + taps - 1):
            x = buf_ref[off + t0 + i]
            for j in range(CONV_STRIP):
                if 0 <= i - j < taps:
                    acc[j] = acc[j] + w[i - j] * x
        for j in range(CONV_STRIP):
            c_ref[t0 + j] = acc[j]
        return carry

    lax.fori_loop(0, tt // CONV_STRIP, strip, 0)
    buf_ref[0:CONV_PAD] = buf_ref[tt:tt + CONV_PAD]
    c = jnp.swapaxes(c_ref[...], 0, 1)
    for s, ls in enumerate(lanes):
        o_ref[:, ls] = c[s]


def _conv_prompt(u, conv_w, conv_b, batch, seq, tt):
    d_b = u.shape[1]
    taps = conv_w.shape[0]
    nt = seq // tt
    sub = d_b // LANES
    assert taps - 1 <= CONV_PAD and tt % CONV_STRIP == 0 and tt >= CONV_PAD
    kern = functools.partial(_conv_prompt_kernel, tt=tt, taps=taps)
    return pl.pallas_call(
        kern,
        out_shape=jax.ShapeDtypeStruct((batch * seq, d_b), F32),
        grid=(batch, nt),
        in_specs=[pl.BlockSpec((tt, d_b), lambda b, t: (b * nt + t, 0)),
                  pl.BlockSpec((taps, sub, LANES), lambda b, t: (0, 0, 0)),
                  pl.BlockSpec((1, sub, LANES), lambda b, t: (0, 0, 0))],
        out_specs=pl.BlockSpec((tt, d_b), lambda b, t: (b * nt + t, 0)),
        scratch_shapes=[pltpu.VMEM((tt + CONV_PAD, sub, LANES), F32), pltpu.VMEM((tt, sub, LANES), F32)],
        compiler_params=_params(2),
        name="conv_prompt",
    )(u, conv_w.reshape(taps, sub, LANES), conv_b.reshape(1, sub, LANES))


def _conv_step_kernel(u_ref, prev_ref, w_ref, b_ref, o_ref, hist_ref, *, taps):
    u = u_ref[...]
    c = b_ref[...] + w_ref[taps - 1:taps, :] * u
    for k in range(taps - 1):
        c = c + w_ref[k:k + 1, :] * prev_ref[k]
    o_ref[...] = c
    hist_ref[0:taps - 2] = prev_ref[1:taps - 1]
    hist_ref[taps - 2] = u


def _conv_sample(u, conv_prev_t, conv_w, conv_b, bb, row0):
    batch = conv_prev_t.shape[1]
    d_b = u.shape[1]
    taps = conv_w.shape[0]
    assert row0 % bb == 0 and batch % bb == 0
    blk0 = row0 // bb
    kern = functools.partial(_conv_step_kernel, taps=taps)
    return pl.pallas_call(
        kern,
        out_shape=(jax.ShapeDtypeStruct((batch, d_b), F32),
                   jax.ShapeDtypeStruct(conv_prev_t.shape, F32)),
        grid=(batch // bb,),
        in_specs=[pl.BlockSpec((bb, d_b), lambda i: (blk0 + i, 0)),
                  pl.BlockSpec((taps - 1, bb, d_b), lambda i: (0, i, 0)),
                  pl.BlockSpec((taps, d_b), lambda i: (0, 0)),
                  pl.BlockSpec((1, d_b), lambda i: (0, 0))],
        out_specs=(pl.BlockSpec((bb, d_b), lambda i: (i, 0)),
                   pl.BlockSpec((taps - 1, bb, d_b), lambda i: (0, i, 0))),
        compiler_params=_params(1),
        name="conv_step",
    )(u, conv_prev_t, conv_w, conv_b.reshape(1, d_b))


def _tail_kernel(oa_ref, c_ref, gb_ref, sga_ref, sgb_ref, x_ref, p_ref, cln_ref, wa_ref, wb_ref, wout_ref,
                 wpg_ref, wple_ref, fg_ref, o_ref):
    c = c_ref[...]
    mean = jnp.mean(c, axis=-1, keepdims=True)
    dc = c - mean
    var = jnp.mean(dc * dc, axis=-1, keepdims=True)
    cf = dc * lax.rsqrt(var + LN_EPS) * cln_ref[0:1, :] + cln_ref[1:2, :]
    cb = (cf * _sigmoid(cf) * gb_ref[...]).astype(BF16)
    m = sga_ref[...] * _dot(oa_ref[...], wa_ref[...]) + sgb_ref[...] * _dot(cb, wb_ref[...])
    h = x_ref[...] + _dot(m.astype(BF16), wout_ref[...])
    gate = _sigmoid(_dot(h.astype(BF16), wpg_ref[...]))
    h = h + gate * _dot(p_ref[...].astype(BF16), wple_ref[...])
    ms = jnp.mean(h * h, axis=-1, keepdims=True)
    o_ref[...] = h * lax.rsqrt(ms + RMS_EPS) * fg_ref[...]


def _tail(oa, c, gates, x, p, cln, wa, wb, wout, wpg, wple, fg, tm, gate_b_block, merge_block, row0=0):
    m, d = x.shape
    d_a = oa.shape[1]
    d_b = c.shape[1]
    ple = p.shape[1]
    blk0 = row0 // tm
    row = lambda i: (i, 0)
    const = lambda i: (0, 0)
    resident = lambda shape: pl.BlockSpec(shape, const, pipeline_mode=pl.Buffered(1))
    return pl.pallas_call(
        _tail_kernel,
        out_shape=jax.ShapeDtypeStruct((m, d), F32),
        grid=(m // tm,),
        in_specs=[pl.BlockSpec((tm, d_a), row), pl.BlockSpec((tm, d_b), row),
                  pl.BlockSpec((tm, d_b), lambda i: (blk0 + i, gate_b_block)),
                  pl.BlockSpec((tm, d), lambda i: (blk0 + i, merge_block)),
                  pl.BlockSpec((tm, d), lambda i: (blk0 + i, merge_block + 1)),
                  pl.BlockSpec((tm, d), row), pl.BlockSpec((tm, ple), row),
                  resident((2, d_b)),
                  resident((d_a, d)), resident((d_b, d)), resident((d, d)), resident((d, d)),
                  resident((ple, d)), resident((1, d))],
        out_specs=pl.BlockSpec((tm, d), row),
        compiler_params=_params(1),
        name="tail",
    )(oa, c, gates, gates, gates, x, p, cln, wa, wb, wout, wpg, wple, fg)


def _largest_tile(n, cap, align):
    t = min(n, cap)
    while n % t or t % align:
        t -= 1
    return t


def kernel(x_prompt, x_sample, state_shift, state_wkv, state_conv, p_prompt, p_sample, norm_g, w_in,
           shift_mu, w0, w_lora_b, a0, a_lora_b, k_k, k_a, r_k, lnx_g, lnx_b, w_proj_a, conv_w, conv_b,
           cln_g, cln_b, w_proj_b, w_out, w_ple, w_ple_gate, final_g):
    depth = w_in.shape[0]
    batch, seq, d = x_prompt.shape
    dec_batch, dec_seq, _ = x_sample.shape
    d_a = w_proj_a.shape[1]
    d_b = w_proj_b.shape[1]
    shift_w = shift_mu.shape[1]
    n_heads = d_a // HEAD
    n_pairs = d_a // PAIR
    taps = conv_w.shape[1]
    assert depth == 1 and dec_seq == 1 and d_a == d_b and 2 * d_a == d
    assert shift_w == 3 * d_a + 2 * LORA and d_a % PAIR == 0 and seq % CHUNK == 0

    o1 = shift_w
    o2 = o1 + d_a
    o3 = o2 + 2 * d_b
    o4 = o3 + d_b
    w = w_in[0]
    lora_w = jnp.zeros((2 * LORA, 2 * d_a), F32)
    lora_w = lora_w.at[:LORA, :d_a].set(w_lora_b[0]).at[LORA:, d_a:].set(a_lora_b[0]).astype(BF16)
    zeros_a = jnp.zeros((d_a,), F32)
    par = jnp.stack([w0[0], a0[0], k_k[0], k_a[0], r_k[0].reshape(d_a), lnx_g[0], lnx_b[0], zeros_a])
    cln = jnp.stack([cln_g[0], cln_b[0]])
    mu = shift_mu
    g_in = norm_g[0].reshape(1, d)
    fg = final_g.reshape(1, d)
    gate_b_block = d_a // d_b
    merge_block = (d_a + d_b) // d

    m_p = batch * seq
    m_all = m_p + dec_batch
    x2 = x_prompt.reshape(m_p, d)
    xs2 = x_sample.reshape(dec_batch, d)
    n_rkv = 3 * d_a
    xn, p_lora = _rmsnorm_bf16(x2, xs2, g_in, w, n_rkv, 2 * LORA, _largest_tile(m_p, 1024, SUBLANES))
    tm = _largest_tile(m_all, 1100, BF16_SUBLANES)
    tn_shift = _largest_tile(n_rkv, 1024, LANES)
    p_rkv = _project(xn, w, lambda j: j * tn_shift, n_rkv, tm, tn_shift)
    gate_cols = lambda j: jnp.where(j == 0, o1, o3 + (j - 1) * d_a)
    gates = _project(xn, w, gate_cols, d_a + d_b + 2 * d, tm, d_a, n_silu_blocks=2)
    u, (wa, wb, wout, wpg, wple) = _project_glu(
        xn, w, o2, o2 + d_b, d_b, tm, _largest_tile(d_b, 512, LANES),
        casts=[w_proj_a[0], w_proj_b[0], w_out[0], w_ple_gate[0], w_ple[0]])

    oa, st = _wkv_prompt(p_rkv, p_lora, gates, mu, par, lora_w, batch, seq, d_a)
    c = _conv_prompt(u, conv_w[0], conv_b[0], batch, seq, _largest_tile(seq, 512, CONV_STRIP))
    y_prompt = _tail(oa, c, gates, x2, p_prompt[0].reshape(m_p, -1), cln, wa, wb, wout, wpg, wple, fg,
                     _largest_tile(m_p, 256, BF16_SUBLANES), gate_b_block, merge_block).reshape(batch, seq, d)
    last_rows = lambda a: jnp.concatenate([a[(b + 1) * seq - 1:(b + 1) * seq] for b in range(batch)])
    new_shift_p = jnp.concatenate([last_rows(p_rkv), last_rows(p_lora)], axis=1)[None]
    st = st.reshape(batch, n_pairs, 2, HEAD, 2, HEAD)
    new_wkv_p = jnp.stack([st[:, :, 0, :, 0, :], st[:, :, 1, :, 1, :]], axis=2)
    new_wkv_p = jnp.swapaxes(new_wkv_p.reshape(batch, n_heads, HEAD, HEAD), -1, -2)[None]
    new_conv_p = jnp.stack([u[(b + 1) * seq - (taps - 1):(b + 1) * seq] for b in range(batch)])[None]

    tm_s = _largest_tile(dec_batch, 256, BF16_SUBLANES)
    assert m_p % tm_s == 0
    oa_s, new_wkv_t = _wkv_sample(p_rkv, p_lora, state_shift[0], gates, mu, par, lora_w,
                                  jnp.transpose(state_wkv[0], (1, 2, 3, 0)), d_a, m_p)
    new_wkv_s = jnp.transpose(new_wkv_t, (3, 0, 1, 2))
    c_s, new_conv_t = _conv_sample(u, jnp.transpose(state_conv[0], (1, 0, 2)), conv_w[0], conv_b[0],
                                   _largest_tile(dec_batch, 32, SUBLANES), m_p)
    new_conv_s = jnp.transpose(new_conv_t, (1, 0, 2))
    y_sample = _tail(oa_s, c_s, gates, xs2, p_sample[0].reshape(dec_batch, -1), cln, wa, wb, wout, wpg,
                     wple, fg, tm_s, gate_b_block, merge_block, row0=m_p).reshape(dec_batch, 1, d)
    new_shift_s = jnp.concatenate([p_rkv[m_p:], p_lora[m_p:]], axis=1)[None]
    new_conv_s = new_conv_s[None]

    return (y_prompt, y_sample, new_shift_p, new_wkv_p, new_conv_p, new_shift_s, new_wkv_s[None],
            new_conv_s)
```

```python
import functools
import math

import jax
import jax.numpy as jnp
from jax import lax
from jax.experimental import pallas as pl
from jax.experimental.pallas import tpu as pltpu

F32 = jnp.float32
BF16 = jnp.bfloat16

LANES = 128
SUBLANES = 8
BF16_SUBLANES = 16

HEAD = 64
PAIR = 2 * HEAD
LORA = 64
CHUNK = 64
WKV_SEQS_PER_STEP = 4
RMS_EPS = 1e-6
LN_EPS = 1e-5
GN_EPS = 64e-5
LOG2_DECAY_SCALE = math.exp(-0.5) / math.log(2.0)
VMEM_LIMIT = 56 * 1024 * 1024


def _params(n_axes, vmem=VMEM_LIMIT):
    return pltpu.CompilerParams(dimension_semantics=("arbitrary",) * n_axes, vmem_limit_bytes=vmem)


def _sigmoid(x):
    return 0.5 * jnp.tanh(0.5 * x) + 0.5


def _dot(a, b):
    return jnp.dot(a, b, preferred_element_type=F32)


def _dot_nt(a, b):
    return lax.dot_general(a, b, (((1,), (1,)), ((), ())), preferred_element_type=F32)


def _dot_tn(a, b):
    return lax.dot_general(a, b, (((0,), (0,)), ((), ())), preferred_element_type=F32)


def _rmsnorm_kernel(xp_ref, xs_ref, g_ref, w_ref, o_ref, ol_ref, wb_ref, *, n_prompt_blocks):
    i = pl.program_id(0)

    @pl.when(i == 0)
    def _():
        wb_ref[...] = w_ref[...].astype(BF16)

    def norm(x):
        ms = jnp.mean(x * x, axis=-1, keepdims=True)
        return (x * lax.rsqrt(ms + RMS_EPS) * g_ref[...]).astype(o_ref.dtype)

    @pl.when(i < n_prompt_blocks)
    def _():
        xn = norm(xp_ref[...])
        o_ref[...] = xn
        ol_ref[...] = _dot(xn, wb_ref[...])

    @pl.when(i == n_prompt_blocks)
    def _():
        xn = norm(xs_ref[...])
        o_ref[0:xs_ref.shape[0], :] = xn
        ol_ref[0:xs_ref.shape[0], :] = _dot(xn, wb_ref[...])


def _rmsnorm_bf16(x_prompt, x_sample, g, w, col, n_cols, tm):
    m_p, d = x_prompt.shape
    m_s = x_sample.shape[0]
    n_blocks = m_p // tm
    assert m_p % tm == 0 and m_s <= tm and col % LANES == 0
    kern = functools.partial(_rmsnorm_kernel, n_prompt_blocks=n_blocks)
    return pl.pallas_call(
        kern,
        out_shape=(jax.ShapeDtypeStruct((m_p + m_s, d), BF16),
                   jax.ShapeDtypeStruct((m_p + m_s, n_cols), F32)),
        grid=(n_blocks + 1,),
        in_specs=[pl.BlockSpec((tm, d), lambda i: (jnp.minimum(i, n_blocks - 1), 0)),
                  pl.BlockSpec((m_s, d), lambda i: (0, 0)),
                  pl.BlockSpec((1, d), lambda i: (0, 0)),
                  pl.BlockSpec((pl.Element(d), pl.Element(n_cols)), lambda i: (0, col))],
        out_specs=(pl.BlockSpec((tm, d), lambda i: (i, 0)),
                   pl.BlockSpec((tm, n_cols), lambda i: (i, 0))),
        scratch_shapes=[pltpu.VMEM((d, n_cols), BF16)],
        compiler_params=_params(1),
        name="rmsnorm",
    )(x_prompt, x_sample, g, w)


def _proj_kernel(x_ref, w_ref, o_ref, wb_ref, *, n_silu_blocks):
    @pl.when(pl.program_id(1) == 0)
    def _():
        wb_ref[...] = w_ref[...].astype(BF16)

    y = _dot(x_ref[...], wb_ref[...])
    if n_silu_blocks is None:
        o_ref[...] = y
    else:
        s = _sigmoid(y)
        o_ref[...] = jnp.where(pl.program_id(0) < n_silu_blocks, y * s, s)


def _project(xn, w, col_start, n_out, tm, tn, n_silu_blocks=None):
    m, d = xn.shape
    kern = functools.partial(_proj_kernel, n_silu_blocks=n_silu_blocks)
    return pl.pallas_call(
        kern,
        out_shape=jax.ShapeDtypeStruct((m, n_out), F32),
        grid=(n_out // tn, m // tm),
        in_specs=[pl.BlockSpec((tm, d), lambda j, i: (i, 0)),
                  pl.BlockSpec((pl.Element(d), pl.Element(tn)),
                               lambda j, i: (0, pl.multiple_of(col_start(j), LANES)))],
        out_specs=pl.BlockSpec((tm, tn), lambda j, i: (i, j)),
        scratch_shapes=[pltpu.VMEM((d, tn), BF16)],
        compiler_params=_params(2),
        name="in_proj",
    )(xn, w)


def _glu_kernel(*refs, n_casts):
    x_ref, wa_ref, wb_ref = refs[:3]
    cast_in = refs[3:3 + n_casts]
    o_ref = refs[3 + n_casts]
    cast_out = refs[4 + n_casts:4 + 2 * n_casts]
    wab_ref, wbb_ref = refs[4 + 2 * n_casts:]

    @pl.when(pl.program_id(1) == 0)
    def _():
        wab_ref[...] = wa_ref[...].astype(BF16)
        wbb_ref[...] = wb_ref[...].astype(BF16)

    x = x_ref[...]
    o_ref[...] = _dot(x, wab_ref[...]) * _sigmoid(_dot(x, wbb_ref[...]))
    for src, dst in zip(cast_in, cast_out):
        dst[...] = src[...].astype(BF16)


def _project_glu(xn, w, col_a, col_b, n_out, tm, tn, casts):
    m, d = xn.shape
    nj, ni = n_out // tn, m // tm
    steps = nj * ni
    for a in casts:
        assert a.shape[0] % (BF16_SUBLANES * steps) == 0, (a.shape, steps)
    step_rows = lambda a: pl.BlockSpec((a.shape[0] // steps, a.shape[1]), lambda j, i: (j * ni + i, 0))
    w_cols = lambda col: pl.BlockSpec((pl.Element(d), pl.Element(tn)),
                                      lambda j, i: (0, pl.multiple_of(col + j * tn, LANES)))
    kern = functools.partial(_glu_kernel, n_casts=len(casts))
    out = pl.pallas_call(
        kern,
        out_shape=[jax.ShapeDtypeStruct((m, n_out), F32)]
        + [jax.ShapeDtypeStruct(a.shape, BF16) for a in casts],
        grid=(nj, ni),
        in_specs=[pl.BlockSpec((tm, d), lambda j, i: (i, 0)), w_cols(col_a), w_cols(col_b)]
        + [step_rows(a) for a in casts],
        out_specs=[pl.BlockSpec((tm, tn), lambda j, i: (i, j))] + [step_rows(a) for a in casts],
        scratch_shapes=[pltpu.VMEM((d, tn), BF16), pltpu.VMEM((d, tn), BF16)],
        compiler_params=_params(2),
        name="in_proj_glu",
    )(xn, w, w, *casts)
    return out[0], out[1:]


def _head_sum(x, block_ones, split=False):
    hi = x.astype(BF16)
    if not split:
        return _dot(hi, block_ones)
    lo = (x - hi.astype(F32)).astype(BF16)
    return _dot(hi, block_ones) + _dot(lo, block_ones)


def _head_sums(xs, block_ones, split=False):
    rows = xs[0].shape[0]
    s = _head_sum(jnp.concatenate(xs, axis=0), block_ones, split)
    return [s[i * rows:(i + 1) * rows] for i in range(len(xs))]


def _block_ones(value=1.0):
    ri = lax.broadcasted_iota(jnp.int32, (PAIR, PAIR), 0)
    ci = lax.broadcasted_iota(jnp.int32, (PAIR, PAIR), 1)
    return jnp.where((ri < HEAD) == (ci < HEAD), value, 0.0).astype(BF16)


def _token_shift(p, prev, mu):
    return p + mu * (prev - p)


def _lora_out(z, lora_w):
    lane = lax.broadcasted_iota(jnp.int32, z.shape, 1)
    z = jnp.where(lane < LORA, jnp.tanh(z), z)
    return _dot(z.astype(BF16), lora_w)


def _pair_slices(d_a):
    return [slice(i * PAIR, (i + 1) * PAIR) for i in range(d_a // PAIR)]


def _pair_vectors(xs_list, lo_list, par_ref, d_a, block_ones):
    sls = _pair_slices(d_a)
    shifted = lambda sl, off: slice(off + sl.start, off + sl.stop)
    items = [(xs, lo, sl) for xs, lo in zip(xs_list, lo_list) for sl in sls]
    r = [xs[:, sl] for xs, _, sl in items]
    k = [xs[:, shifted(sl, d_a)] for xs, _, sl in items]
    v = [xs[:, shifted(sl, 2 * d_a)] for xs, _, sl in items]
    kkr = [ki * par_ref[2:3, sl] for ki, (_, _, sl) in zip(k, items)]
    ss = _head_sums([x * x for x in kkr], block_ones)
    kk = [x * jnp.minimum(lax.rsqrt(s), 1e12) for x, s in zip(kkr, ss)]
    log2_decay = [-LOG2_DECAY_SCALE * _sigmoid(par_ref[0:1, sl] + lo[:, sl]) for _, lo, sl in items]
    a = [_sigmoid(par_ref[1:2, sl] + lo[:, shifted(sl, d_a)]) for _, lo, sl in items]
    kf = [ki * (1.0 + (ai - 1.0) * par_ref[3:4, sl]) for ki, ai, (_, _, sl) in zip(k, a, items)]
    return r, kf, v, log2_decay, kk, a


def _head_norm_bonus(y, r, kf, v, par_ref, d_a, block_ones):
    sls = _pair_slices(d_a) * (len(y) // (d_a // PAIR))
    block_mean = _block_ones(1.0 / HEAD)
    mean = _head_sums(y, block_mean, split=True)
    d = [x - m for x, m in zip(y, mean)]
    var = _head_sums([x * x for x in d], block_mean)
    rk = _head_sums([ri * ki * par_ref[4:5, sl] for ri, ki, sl in zip(r, kf, sls)], block_ones)
    return [x * lax.rsqrt(vr + GN_EPS) * par_ref[5:6, sl] + par_ref[6:7, sl] + s * vi
            for x, vr, s, vi, sl in zip(d, var, rk, v, sls)]


def _stack_heads(x, lane_lo):
    xb = x.astype(BF16)
    zero = jnp.zeros_like(xb)
    return jnp.concatenate([jnp.where(lane_lo, xb, zero), jnp.where(lane_lo, zero, xb)], axis=0)


def _cumsum_rows(xs):
    n = xs[0].shape[0]
    tri = (lax.broadcasted_iota(jnp.int32, (n, n), 1) <= lax.broadcasted_iota(jnp.int32, (n, n), 0))
    tri = jnp.where(tri, 1.0, 0.0).astype(BF16)
    x = jnp.concatenate(xs, axis=1)
    hi = x.astype(BF16)
    lo = (x - hi.astype(F32)).astype(BF16)
    s = _dot(tri, hi) + _dot(tri, lo)
    return [s[:, i * PAIR:(i + 1) * PAIR] for i in range(len(xs))]


def _wkv_chunk_kernel(*refs, d_a, nb):
    ps_refs, pl_refs, sga_refs = refs[:nb], refs[nb:2 * nb], refs[2 * nb:3 * nb]
    mu_ref, par_ref, lora_ref, oa_ref, st_ref, carry_ref = refs[3 * nb:]
    c = pl.program_id(1)
    n_pairs = d_a // PAIR
    C = CHUNK
    n_rkv = 3 * d_a

    @pl.when(c == 0)
    def _():
        carry_ref[...] = jnp.zeros_like(carry_ref)
        st_ref[...] = jnp.zeros_like(st_ref)

    def shifted(p, prev_row, mu):
        row = lax.broadcasted_iota(jnp.int32, p.shape, 0)
        return _token_shift(p, jnp.where(row == 0, prev_row, pltpu.roll(p, 1, axis=0)), mu)

    xs, lo = [], []
    for bi in range(nb):
        p, pz = ps_refs[bi][...], pl_refs[bi][...]
        xs.append(shifted(p, carry_ref[bi, :, :n_rkv], mu_ref[:, :n_rkv]))
        lo.append(_lora_out(shifted(pz, carry_ref[bi, :, n_rkv:], mu_ref[:, n_rkv:]), lora_ref[...]))
        carry_ref[bi, :, :n_rkv] = p[C - 1:C, :]
        carry_ref[bi, :, n_rkv:] = pz[C - 1:C, :]

    block_ones = _block_ones()
    lane_lo = lax.broadcasted_iota(jnp.int32, (C, PAIR), 1) < HEAD
    ti = lax.broadcasted_iota(jnp.int32, (C, PAIR), 0)
    si = lax.broadcasted_iota(jnp.int32, (C, PAIR), 1) % C
    strict_lower = si < ti
    lower2 = (lax.broadcasted_iota(jnp.int32, (C, 2 * PAIR), 1) % C
              <= lax.broadcasted_iota(jnp.int32, (C, 2 * PAIR), 0))
    eye = jnp.where(si == ti, 1.0, 0.0)
    cat0 = lambda x, y: jnp.concatenate([x, y], axis=0)

    r, kf, v, log2_decay, kk, a = _pair_vectors(xs, lo, par_ref, d_a, block_ones)
    cl = _cumsum_rows(log2_decay)
    cend = [x[C - 1:C, :] for x in cl]
    e_neg = [jnp.exp2(-x) for x in cl]
    e_end = [jnp.exp2(ce - x) for ce, x in zip(cend, cl)]
    kka = [x * y for x, y in zip(kk, a)]
    stack = lambda xs_: [_stack_heads(x, lane_lo) for x in xs_]
    a_p = [(-x * jnp.exp2(c_ - ld)).astype(BF16) for x, c_, ld in zip(kk, cl, log2_decay)]
    r_p = [(x * jnp.exp2(c_)).astype(BF16) for x, c_ in zip(r, cl)]
    a_s = stack(a_p)
    b_s = stack([x * e for x, e in zip(kka, e_neg)])
    k_s = stack([x * e for x, e in zip(kf, e_neg)])
    v_s = stack(v)
    bh_s = stack([x * e for x, e in zip(kka, e_end)])
    kh_s = stack([x * e for x, e in zip(kf, e_end)])
    pend_col = [jnp.transpose(jnp.broadcast_to(jnp.exp2(ce), (PAIR, PAIR))) for ce in cend]

    m1 = [_dot_nt(cat0(x, y), cat0(z, w)) for x, y, z, w in zip(a_p, r_p, b_s, k_s)]
    a_ab = [jnp.where(strict_lower, m[:C, :PAIR], 0.0) for m in m1]
    keep = lambda mask, x: jnp.where(mask, x, jnp.zeros_like(x))
    a_ak = [keep(strict_lower, m[:C, PAIR:].astype(BF16)) for m in m1]
    a_rbk = [keep(lower2, m[C:].astype(BF16)) for m in m1]
    g = [_dot(x, y) for x, y in zip(a_ak, v_s)]

    apow = [_dot(x.astype(BF16), _stack_heads(x, lane_lo)) for x in a_ab]
    tinv = [eye + x for x in a_ab]
    n_levels = CHUNK.bit_length() - 1
    for lvl in range(1, n_levels):
        rhs = stack(apow)
        if lvl < n_levels - 1:
            both = [_dot(cat0(x.astype(BF16), t.astype(BF16)), y) for x, t, y in zip(apow, tinv, rhs)]
            apow = [x[:C] for x in both]
            tinv = [t + x[C:] for t, x in zip(tinv, both)]
        else:
            tinv = [t + _dot(t.astype(BF16), y) for t, y in zip(tinv, rhs)]

    g_s = stack(g)
    wx = [_dot(t.astype(BF16), jnp.concatenate([x, y], axis=1)) for t, x, y in zip(tinv, a_s, g_s)]
    seq_pair = [(bi, pi) for bi in range(nb) for pi in range(n_pairs)]
    st = [st_ref[bi, pi] for bi, pi in seq_pair]
    uy = [_dot(cat0(w[:, :PAIR].astype(BF16), x), s_.astype(BF16)) for w, x, s_ in zip(wx, r_p, st)]
    uv_s = [cat0(_stack_heads(x[:C] + w[:, PAIR:], lane_lo), y) for x, w, y in zip(uy, wx, v_s)]
    st_upd = [_dot_tn(cat0(x, y), z) for x, y, z in zip(bh_s, kh_s, uv_s)]
    y_uv = [_dot(x, z) for x, z in zip(a_rbk, uv_s)]
    for i, (bi, pi) in enumerate(seq_pair):
        st_ref[bi, pi] = pend_col[i] * st[i] + st_upd[i]
    y = [x[C:] + z for x, z in zip(uy, y_uv)]
    o = _head_norm_bonus(y, r, kf, v, par_ref, d_a, block_ones)
    sls = _pair_slices(d_a)
    for oi, (bi, pi) in zip(o, seq_pair):
        oa_ref[bi, :, sls[pi]] = (oi * sga_refs[bi][:, sls[pi]]).astype(oa_ref.dtype)


def _wkv_prompt(p_rkv, p_lora, gates, mu, par, lora_w, batch, seq, d_a):
    n_chunks = seq // CHUNK
    shift_w = mu.shape[1]
    n_pairs = d_a // PAIR
    nb = WKV_SEQS_PER_STEP if batch % WKV_SEQS_PER_STEP == 0 else 1
    kern = functools.partial(_wkv_chunk_kernel, d_a=d_a, nb=nb)
    seq_rows = lambda bi: (lambda b, c: ((b * nb + bi) * n_chunks + c, 0))
    oa, st = pl.pallas_call(
        kern,
        out_shape=(jax.ShapeDtypeStruct((batch, seq, d_a), BF16),
                   jax.ShapeDtypeStruct((batch, n_pairs, PAIR, PAIR), F32)),
        grid=(batch // nb, n_chunks),
        in_specs=([pl.BlockSpec((CHUNK, 3 * d_a), seq_rows(bi)) for bi in range(nb)]
                  + [pl.BlockSpec((CHUNK, 2 * LORA), seq_rows(bi)) for bi in range(nb)]
                  + [pl.BlockSpec((CHUNK, d_a), seq_rows(bi)) for bi in range(nb)]
                  + [pl.BlockSpec((1, shift_w), lambda b, c: (0, 0)),
                     pl.BlockSpec((8, d_a), lambda b, c: (0, 0)),
                     pl.BlockSpec((2 * LORA, 2 * d_a), lambda b, c: (0, 0))]),
        out_specs=(pl.BlockSpec((nb, CHUNK, d_a), lambda b, c: (b, c, 0)),
                   pl.BlockSpec((nb, n_pairs, PAIR, PAIR), lambda b, c: (b, 0, 0, 0))),
        scratch_shapes=[pltpu.VMEM((nb, 1, shift_w), F32)],
        compiler_params=_params(2),
        name="wkv_chunked",
    )(*([p_rkv] * nb + [p_lora] * nb + [gates] * nb), mu, par, lora_w)
    return oa.reshape(batch * seq, d_a), st


STEP_ROWS = 8


def _wkv_step_kernel(ps_ref, pl_ref, prev_ref, sga_ref, mu_ref, par_ref, lora_ref, s_ref, oa_ref, so_ref,
                     vt_ref, rkv_ref, y_ref, *, d_a):
    h = pl.program_id(0)
    n_heads = d_a // HEAD
    n_rkv = 3 * d_a
    sls = _pair_slices(d_a)

    @pl.when(h == 0)
    def _():
        xs = _token_shift(ps_ref[...], prev_ref[:, :n_rkv], mu_ref[:, :n_rkv])
        lo = _lora_out(_token_shift(pl_ref[...], prev_ref[:, n_rkv:], mu_ref[:, n_rkv:]), lora_ref[...])
        r, kf, v, log2_decay, kk, a = _pair_vectors([xs], [lo], par_ref, d_a, _block_ones())
        for pi, sl in enumerate(sls):
            vecs = (-kk[pi], jnp.exp2(log2_decay[pi]), kk[pi] * a[pi], kf[pi], r[pi], v[pi])
            for i, x in enumerate(vecs):
                vt_ref[i, sl, :] = x.T
            for i, x in enumerate((r[pi], kf[pi], v[pi])):
                rkv_ref[i, :, sl] = x

    base = pl.multiple_of(h * HEAD, HEAD)
    a_t, w_t, b_t, k_t, r_t = (vt_ref[i, pl.ds(base, HEAD), :] for i in range(5))

    def rows(g, carry):
        i0 = pl.multiple_of(g * STEP_ROWS, STEP_ROWS)
        row0 = pl.multiple_of(base + i0, STEP_ROWS)
        v_rows = vt_ref[5, pl.ds(row0, STEP_ROWS), :]
        ys = []
        for ii in range(STEP_ROWS):
            s = s_ref[0, i0 + ii]
            sa = jnp.sum(s * a_t, axis=0, keepdims=True)
            s_new = s * w_t + sa * b_t + v_rows[ii:ii + 1, :] * k_t
            so_ref[0, i0 + ii] = s_new
            ys.append(jnp.sum(s_new * r_t, axis=0, keepdims=True))
        y_ref[pl.ds(row0, STEP_ROWS), :] = jnp.concatenate(ys, axis=0)
        return carry

    lax.fori_loop(0, HEAD // STEP_ROWS, rows, 0)

    @pl.when(h == n_heads - 1)
    def _():
        y = [y_ref[sl, :].T for sl in sls]
        r, kf, v = ([rkv_ref[i, :, sl] for sl in sls] for i in range(3))
        o = _head_norm_bonus(y, r, kf, v, par_ref, d_a, _block_ones())
        for oi, sl in zip(o, sls):
            oa_ref[:, sl] = (oi * sga_ref[:, sl]).astype(oa_ref.dtype)


def _wkv_sample(p_rkv, p_lora, shift_prev, gates, mu, par, lora_w, state_t, d_a, row0):
    batch, shift_w = shift_prev.shape
    n_heads = d_a // HEAD
    assert row0 % batch == 0 and state_t.shape == (n_heads, HEAD, HEAD, batch)
    blk0 = row0 // batch
    const = lambda h: (0, 0)
    kern = functools.partial(_wkv_step_kernel, d_a=d_a)
    return pl.pallas_call(
        kern,
        out_shape=(jax.ShapeDtypeStruct((batch, d_a), BF16),
                   jax.ShapeDtypeStruct(state_t.shape, F32)),
        grid=(n_heads,),
        in_specs=[pl.BlockSpec((batch, 3 * d_a), lambda h: (blk0, 0)),
                  pl.BlockSpec((batch, 2 * LORA), lambda h: (blk0, 0)),
                  pl.BlockSpec((batch, shift_w), const),
                  pl.BlockSpec((batch, d_a), lambda h: (blk0, 0)),
                  pl.BlockSpec((1, shift_w), const),
                  pl.BlockSpec((8, d_a), const),
                  pl.BlockSpec((2 * LORA, 2 * d_a), const),
                  pl.BlockSpec((1, HEAD, HEAD, batch), lambda h: (h, 0, 0, 0))],
        out_specs=(pl.BlockSpec((batch, d_a), const),
                   pl.BlockSpec((1, HEAD, HEAD, batch), lambda h: (h, 0, 0, 0))),
        scratch_shapes=[pltpu.VMEM((6, d_a, batch), F32), pltpu.VMEM((3, batch, d_a), F32),
                        pltpu.VMEM((d_a, batch), F32)],
        compiler_params=_params(1),
        name="wkv_step",
    )(p_rkv, p_lora, shift_prev, gates, mu, par, lora_w, state_t)


CONV_PAD = 32
CONV_STRIP = 16


def _conv_prompt_kernel(u_ref, w_ref, b_ref, o_ref, buf_ref, c_ref, *, tt, taps):
    t = pl.program_id(1)
    sub = buf_ref.shape[1]
    lanes = [slice(s * LANES, (s + 1) * LANES) for s in range(sub)]

    @pl.when(t == 0)
    def _():
        buf_ref[0:CONV_PAD] = jnp.zeros((CONV_PAD,) + buf_ref.shape[1:], F32)

    u = u_ref[...]
    buf_ref[CONV_PAD:CONV_PAD + tt] = jnp.swapaxes(jnp.stack([u[:, ls] for ls in lanes], axis=0), 0, 1)
    off = CONV_PAD - (taps - 1)
    w = [w_ref[k] for k in range(taps)]
    bias = b_ref[0]

    def strip(s, carry):
        t0 = s * CONV_STRIP
        acc = [bias] * CONV_STRIP
        for i in range(CONV_STRIP + taps - 1):
            x = buf_ref[off + t0 + i]
            for j in range(CONV_STRIP):
                if 0 <= i - j < taps:
                    acc[j] = acc[j] + w[i - j] * x
        for j in range(CONV_STRIP):
            c_ref[t0 + j] = acc[j]
        return carry

    lax.fori_loop(0, tt // CONV_STRIP, strip, 0, unroll=4)
    buf_ref[0:CONV_PAD] = buf_ref[tt:tt + CONV_PAD]
    c = jnp.swapaxes(c_ref[...], 0, 1)
    for s, ls in enumerate(lanes):
        o_ref[:, ls] = c[s]


def _conv_prompt(u, conv_w, conv_b, batch, seq, tt):
    d_b = u.shape[1]
    taps = conv_w.shape[0]
    nt = seq // tt
    sub = d_b // LANES
    assert taps - 1 <= CONV_PAD and tt % CONV_STRIP == 0 and tt >= CONV_PAD
    kern = functools.partial(_conv_prompt_kernel, tt=tt, taps=taps)
    return pl.pallas_call(
        kern,
        out_shape=jax.ShapeDtypeStruct((batch * seq, d_b), F32),
        grid=(batch, nt),
        in_specs=[pl.BlockSpec((tt, d_b), lambda b, t: (b * nt + t, 0)),
                  pl.BlockSpec((taps, sub, LANES), lambda b, t: (0, 0, 0)),
                  pl.BlockSpec((1, sub, LANES), lambda b, t: (0, 0, 0))],
        out_specs=pl.BlockSpec((tt, d_b), lambda b, t: (b * nt + t, 0)),
        scratch_shapes=[pltpu.VMEM((tt + CONV_PAD, sub, LANES), F32), pltpu.VMEM((tt, sub, LANES), F32)],
        compiler_params=_params(2),
        name="conv_prompt",
    )(u, conv_w.reshape(taps, sub, LANES), conv_b.reshape(1, sub, LANES))


def _conv_step_kernel(u_ref, prev_ref, w_ref, b_ref, o_ref, hist_ref, *, taps):
    u = u_ref[...]
    c = b_ref[...] + w_ref[taps - 1:taps, :] * u
    for k in range(taps - 1):
        c = c + w_ref[k:k + 1, :] * prev_ref[k]
    o_ref[...] = c
    hist_ref[0:taps - 2] = prev_ref[1:taps - 1]
    hist_ref[taps - 2] = u


def _conv_sample(u, conv_prev_t, conv_w, conv_b, bb, row0):
    batch = conv_prev_t.shape[1]
    d_b = u.shape[1]
    taps = conv_w.shape[0]
    assert row0 % bb == 0 and batch % bb == 0
    blk0 = row0 // bb
    kern = functools.partial(_conv_step_kernel, taps=taps)
    return pl.pallas_call(
        kern,
        out_shape=(jax.ShapeDtypeStruct((batch, d_b), F32),
                   jax.ShapeDtypeStruct(conv_prev_t.shape, F32)),
        grid=(batch // bb,),
        in_specs=[pl.BlockSpec((bb, d_b), lambda i: (blk0 + i, 0)),
                  pl.BlockSpec((taps - 1, bb, d_b), lambda i: (0, i, 0)),
                  pl.BlockSpec((taps, d_b), lambda i: (0, 0)),
                  pl.BlockSpec((1, d_b), lambda i: (0, 0))],
        out_specs=(pl.BlockSpec((bb, d_b), lambda i: (i, 0)),
                   pl.BlockSpec((taps - 1, bb, d_b), lambda i: (0, i, 0))),
        compiler_params=_params(1),
        name="conv_step",
    )(u, conv_prev_t, conv_w, conv_b.reshape(1, d_b))


def _tail_kernel(oa_ref, c_ref, gb_ref, sga_ref, sgb_ref, x_ref, p_ref, cln_ref, wa_ref, wb_ref, wout_ref,
                 wpg_ref, wple_ref, fg_ref, o_ref):
    c = c_ref[...]
    mean = jnp.mean(c, axis=-1, keepdims=True)
    dc = c - mean
    var = jnp.mean(dc * dc, axis=-1, keepdims=True)
    cf = dc * lax.rsqrt(var + LN_EPS) * cln_ref[0:1, :] + cln_ref[1:2, :]
    cb = (cf * _sigmoid(cf) * gb_ref[...]).astype(BF16)
    m = sga_ref[...] * _dot(oa_ref[...], wa_ref[...]) + sgb_ref[...] * _dot(cb, wb_ref[...])
    h = x_ref[...] + _dot(m.astype(BF16), wout_ref[...])
    gate = _sigmoid(_dot(h.astype(BF16), wpg_ref[...]))
    h = h + gate * _dot(p_ref[...].astype(BF16), wple_ref[...])
    ms = jnp.mean(h * h, axis=-1, keepdims=True)
    o_ref[...] = h * lax.rsqrt(ms + RMS_EPS) * fg_ref[...]


def _tail(oa, c, gates, x, p, cln, wa, wb, wout, wpg, wple, fg, tm, gate_b_block, merge_block, row0=0):
    m, d = x.shape
    d_a = oa.shape[1]
    d_b = c.shape[1]
    ple = p.shape[1]
    blk0 = row0 // tm
    row = lambda i: (i, 0)
    const = lambda i: (0, 0)
    resident = lambda shape: pl.BlockSpec(shape, const, pipeline_mode=pl.Buffered(1))
    return pl.pallas_call(
        _tail_kernel,
        out_shape=jax.ShapeDtypeStruct((m, d), F32),
        grid=(m // tm,),
        in_specs=[pl.BlockSpec((tm, d_a), row), pl.BlockSpec((tm, d_b), row),
                  pl.BlockSpec((tm, d_b), lambda i: (blk0 + i, gate_b_block)),
                  pl.BlockSpec((tm, d), lambda i: (blk0 + i, merge_block)),
                  pl.BlockSpec((tm, d), lambda i: (blk0 + i, merge_block + 1)),
                  pl.BlockSpec((tm, d), row), pl.BlockSpec((tm, ple), row),
                  resident((2, d_b)),
                  resident((d_a, d)), resident((d_b, d)), resident((d, d)), resident((d, d)),
                  resident((ple, d)), resident((1, d))],
        out_specs=pl.BlockSpec((tm, d), row),
        compiler_params=_params(1),
        name="tail",
    )(oa, c, gates, gates, gates, x, p, cln, wa, wb, wout, wpg, wple, fg)


def _largest_tile(n, cap, align):
    t = min(n, cap)
    while n % t or t % align:
        t -= 1
    return t


def kernel(x_prompt, x_sample, state_shift, state_wkv, state_conv, p_prompt, p_sample, norm_g, w_in,
           shift_mu, w0, w_lora_b, a0, a_lora_b, k_k, k_a, r_k, lnx_g, lnx_b, w_proj_a, conv_w, conv_b,
           cln_g, cln_b, w_proj_b, w_out, w_ple, w_ple_gate, final_g):
    depth = w_in.shape[0]
    batch, seq, d = x_prompt.shape
    dec_batch, dec_seq, _ = x_sample.shape
    d_a = w_proj_a.shape[1]
    d_b = w_proj_b.shape[1]
    shift_w = shift_mu.shape[1]
    n_heads = d_a // HEAD
    n_pairs = d_a // PAIR
    taps = conv_w.shape[1]
    assert depth == 1 and dec_seq == 1 and d_a == d_b and 2 * d_a == d
    assert shift_w == 3 * d_a + 2 * LORA and d_a % PAIR == 0 and seq % CHUNK == 0

    o1 = shift_w
    o2 = o1 + d_a
    o3 = o2 + 2 * d_b
    o4 = o3 + d_b
    w = w_in[0]
    lora_w = jnp.zeros((2 * LORA, 2 * d_a), F32)
    lora_w = lora_w.at[:LORA, :d_a].set(w_lora_b[0]).at[LORA:, d_a:].set(a_lora_b[0]).astype(BF16)
    zeros_a = jnp.zeros((d_a,), F32)
    par = jnp.stack([w0[0], a0[0], k_k[0], k_a[0], r_k[0].reshape(d_a), lnx_g[0], lnx_b[0], zeros_a])
    cln = jnp.stack([cln_g[0], cln_b[0]])
    mu = shift_mu
    g_in = norm_g[0].reshape(1, d)
    fg = final_g.reshape(1, d)
    gate_b_block = d_a // d_b
    merge_block = (d_a + d_b) // d

    m_p = batch * seq
    m_all = m_p + dec_batch
    x2 = x_prompt.reshape(m_p, d)
    xs2 = x_sample.reshape(dec_batch, d)
    n_rkv = 3 * d_a
    xn, p_lora = _rmsnorm_bf16(x2, xs2, g_in, w, n_rkv, 2 * LORA, _largest_tile(m_p, 1024, SUBLANES))
    tm = _largest_tile(m_all, 1100, BF16_SUBLANES)
    tn_shift = _largest_tile(n_rkv, 1024, LANES)
    p_rkv = _project(xn, w, lambda j: j * tn_shift, n_rkv, tm, tn_shift)
    gate_cols = lambda j: jnp.where(j == 0, o1, o3 + (j - 1) * d_a)
    gates = _project(xn, w, gate_cols, d_a + d_b + 2 * d, tm, d_a, n_silu_blocks=2)
    u, (wa, wb, wout, wpg, wple) = _project_glu(
        xn, w, o2, o2 + d_b, d_b, tm, _largest_tile(d_b, 512, LANES),
        casts=[w_proj_a[0], w_proj_b[0], w_out[0], w_ple_gate[0], w_ple[0]])

    oa, st = _wkv_prompt(p_rkv, p_lora, gates, mu, par, lora_w, batch, seq, d_a)
    c = _conv_prompt(u, conv_w[0], conv_b[0], batch, seq, _largest_tile(seq, 512, CONV_STRIP))
    y_prompt = _tail(oa, c, gates, x2, p_prompt[0].reshape(m_p, -1), cln, wa, wb, wout, wpg, wple, fg,
                     _largest_tile(m_p, 256, BF16_SUBLANES), gate_b_block, merge_block).reshape(batch, seq, d)
    last_rows = lambda a: jnp.concatenate([a[(b + 1) * seq - 1:(b + 1) * seq] for b in range(batch)])
    new_shift_p = jnp.concatenate([last_rows(p_rkv), last_rows(p_lora)], axis=1)[None]
    st = st.reshape(batch, n_pairs, 2, HEAD, 2, HEAD)
    new_wkv_p = jnp.stack([st[:, :, 0, :, 0, :], st[:, :, 1, :, 1, :]], axis=2)
    new_wkv_p = jnp.swapaxes(new_wkv_p.reshape(batch, n_heads, HEAD, HEAD), -1, -2)[None]
    new_conv_p = jnp.stack([u[(b + 1) * seq - (taps - 1):(b + 1) * seq] for b in range(batch)])[None]

    tm_s = _largest_tile(dec_batch, 256, BF16_SUBLANES)
    assert m_p % tm_s == 0
    oa_s, new_wkv_t = _wkv_sample(p_rkv, p_lora, state_shift[0], gates, mu, par, lora_w,
                                  jnp.transpose(state_wkv[0], (1, 2, 3, 0)), d_a, m_p)
    new_wkv_s = jnp.transpose(new_wkv_t, (3, 0, 1, 2))
    c_s, new_conv_t = _conv_sample(u, jnp.transpose(state_conv[0], (1, 0, 2)), conv_w[0], conv_b[0],
                                   _largest_tile(dec_batch, 32, SUBLANES), m_p)
    new_conv_s = jnp.transpose(new_conv_t, (1, 0, 2))
    y_sample = _tail(oa_s, c_s, gates, xs2, p_sample[0].reshape(dec_batch, -1), cln, wa, wb, wout, wpg,
                     wple, fg, tm_s, gate_b_block, merge_block, row0=m_p).reshape(dec_batch, 1, d)
    new_shift_s = jnp.concatenate([p_rkv[m_p:], p_lora[m_p:]], axis=1)[None]
    new_conv_s = new_conv_s[None]

    return (y_prompt, y_sample, new_shift_p, new_wkv_p, new_conv_p, new_shift_s, new_wkv_s[None],
            new_conv_s)
```

```python
import functools
import math

import jax
import jax.numpy as jnp
from jax import lax
from jax.experimental import pallas as pl
from jax.experimental.pallas import tpu as pltpu

F32 = jnp.float32
BF16 = jnp.bfloat16

LANES = 128
SUBLANES = 8
BF16_SUBLANES = 16

HEAD = 64
PAIR = 2 * HEAD
LORA = 64
CHUNK = 64
WKV_SEQS_PER_STEP = 4
RMS_EPS = 1e-6
LN_EPS = 1e-5
GN_EPS = 64e-5
LOG2_DECAY_SCALE = math.exp(-0.5) / math.log(2.0)
VMEM_LIMIT = 56 * 1024 * 1024


def _params(n_axes, vmem=VMEM_LIMIT):
    return pltpu.CompilerParams(dimension_semantics=("arbitrary",) * n_axes, vmem_limit_bytes=vmem)


def _sigmoid(x):
    return 0.5 * jnp.tanh(0.5 * x) + 0.5


def _dot(a, b):
    return jnp.dot(a, b, preferred_element_type=F32)


def _dot_nt(a, b):
    return lax.dot_general(a, b, (((1,), (1,)), ((), ())), preferred_element_type=F32)


def _dot_tn(a, b):
    return lax.dot_general(a, b, (((0,), (0,)), ((), ())), preferred_element_type=F32)


def _rmsnorm_kernel(xp_ref, xs_ref, g_ref, w_ref, o_ref, ol_ref, wb_ref, *, n_prompt_blocks):
    i = pl.program_id(0)

    @pl.when(i == 0)
    def _():
        wb_ref[...] = w_ref[...].astype(BF16)

    def norm(x):
        ms = jnp.mean(x * x, axis=-1, keepdims=True)
        return (x * lax.rsqrt(ms + RMS_EPS) * g_ref[...]).astype(o_ref.dtype)

    @pl.when(i < n_prompt_blocks)
    def _():
        xn = norm(xp_ref[...])
        o_ref[...] = xn
        ol_ref[...] = _dot(xn, wb_ref[...])

    @pl.when(i == n_prompt_blocks)
    def _():
        xn = norm(xs_ref[...])
        o_ref[0:xs_ref.shape[0], :] = xn
        ol_ref[0:xs_ref.shape[0], :] = _dot(xn, wb_ref[...])


def _rmsnorm_bf16(x_prompt, x_sample, g, w, col, n_cols, tm):
    m_p, d = x_prompt.shape
    m_s = x_sample.shape[0]
    n_blocks = m_p // tm
    assert m_p % tm == 0 and m_s <= tm and col % LANES == 0
    kern = functools.partial(_rmsnorm_kernel, n_prompt_blocks=n_blocks)
    return pl.pallas_call(
        kern,
        out_shape=(jax.ShapeDtypeStruct((m_p + m_s, d), BF16),
                   jax.ShapeDtypeStruct((m_p + m_s, n_cols), F32)),
        grid=(n_blocks + 1,),
        in_specs=[pl.BlockSpec((tm, d), lambda i: (jnp.minimum(i, n_blocks - 1), 0)),
                  pl.BlockSpec((m_s, d), lambda i: (0, 0)),
                  pl.BlockSpec((1, d), lambda i: (0, 0)),
                  pl.BlockSpec((pl.Element(d), pl.Element(n_cols)), lambda i: (0, col))],
        out_specs=(pl.BlockSpec((tm, d), lambda i: (i, 0)),
                   pl.BlockSpec((tm, n_cols), lambda i: (i, 0))),
        scratch_shapes=[pltpu.VMEM((d, n_cols), BF16)],
        compiler_params=_params(1),
        name="rmsnorm",
    )(x_prompt, x_sample, g, w)


def _proj_kernel(x_ref, w_ref, o_ref, wb_ref, *, n_silu_blocks):
    @pl.when(pl.program_id(1) == 0)
    def _():
        wb_ref[...] = w_ref[...].astype(BF16)

    y = _dot(x_ref[...], wb_ref[...])
    if n_silu_blocks is None:
        o_ref[...] = y
    else:
        s = _sigmoid(y)
        o_ref[...] = jnp.where(pl.program_id(0) < n_silu_blocks, y * s, s)


def _project(xn, w, col_start, n_out, tm, tn, n_silu_blocks=None):
    m, d = xn.shape
    kern = functools.partial(_proj_kernel, n_silu_blocks=n_silu_blocks)
    return pl.pallas_call(
        kern,
        out_shape=jax.ShapeDtypeStruct((m, n_out), F32),
        grid=(n_out // tn, m // tm),
        in_specs=[pl.BlockSpec((tm, d), lambda j, i: (i, 0)),
                  pl.BlockSpec((pl.Element(d), pl.Element(tn)),
                               lambda j, i: (0, pl.multiple_of(col_start(j), LANES)))],
        out_specs=pl.BlockSpec((tm, tn), lambda j, i: (i, j)),
        scratch_shapes=[pltpu.VMEM((d, tn), BF16)],
        compiler_params=_params(2),
        name="in_proj",
    )(xn, w)


def _glu_kernel(*refs, n_casts):
    x_ref, wa_ref, wb_ref = refs[:3]
    cast_in = refs[3:3 + n_casts]
    o_ref = refs[3 + n_casts]
    cast_out = refs[4 + n_casts:4 + 2 * n_casts]
    wab_ref, wbb_ref = refs[4 + 2 * n_casts:]

    @pl.when(pl.program_id(1) == 0)
    def _():
        wab_ref[...] = wa_ref[...].astype(BF16)
        wbb_ref[...] = wb_ref[...].astype(BF16)

    x = x_ref[...]
    o_ref[...] = _dot(x, wab_ref[...]) * _sigmoid(_dot(x, wbb_ref[...]))
    for src, dst in zip(cast_in, cast_out):
        dst[...] = src[...].astype(BF16)


def _project_glu(xn, w, col_a, col_b, n_out, tm, tn, casts):
    m, d = xn.shape
    nj, ni = n_out // tn, m // tm
    steps = nj * ni
    for a in casts:
        assert a.shape[0] % (BF16_SUBLANES * steps) == 0, (a.shape, steps)
    step_rows = lambda a: pl.BlockSpec((a.shape[0] // steps, a.shape[1]), lambda j, i: (j * ni + i, 0))
    w_cols = lambda col: pl.BlockSpec((pl.Element(d), pl.Element(tn)),
                                      lambda j, i: (0, pl.multiple_of(col + j * tn, LANES)))
    kern = functools.partial(_glu_kernel, n_casts=len(casts))
    out = pl.pallas_call(
        kern,
        out_shape=[jax.ShapeDtypeStruct((m, n_out), F32)]
        + [jax.ShapeDtypeStruct(a.shape, BF16) for a in casts],
        grid=(nj, ni),
        in_specs=[pl.BlockSpec((tm, d), lambda j, i: (i, 0)), w_cols(col_a), w_cols(col_b)]
        + [step_rows(a) for a in casts],
        out_specs=[pl.BlockSpec((tm, tn), lambda j, i: (i, j))] + [step_rows(a) for a in casts],
        scratch_shapes=[pltpu.VMEM((d, tn), BF16), pltpu.VMEM((d, tn), BF16)],
        compiler_params=_params(2),
        name="in_proj_glu",
    )(xn, w, w, *casts)
    return out[0], out[1:]


def _head_sum(x, block_ones, split=False):
    hi = x.astype(BF16)
    if not split:
        return _dot(hi, block_ones)
    lo = (x - hi.astype(F32)).astype(BF16)
    return _dot(hi, block_ones) + _dot(lo, block_ones)


def _head_sums(xs, block_ones, split=False):
    rows = xs[0].shape[0]
    s = _head_sum(jnp.concatenate(xs, axis=0), block_ones, split)
    return [s[i * rows:(i + 1) * rows] for i in range(len(xs))]


def _block_ones(value=1.0):
    ri = lax.broadcasted_iota(jnp.int32, (PAIR, PAIR), 0)
    ci = lax.broadcasted_iota(jnp.int32, (PAIR, PAIR), 1)
    return jnp.where((ri < HEAD) == (ci < HEAD), value, 0.0).astype(BF16)


def _token_shift(p, prev, mu):
    return p + mu * (prev - p)


def _lora_out(z, lora_w):
    lane = lax.broadcasted_iota(jnp.int32, z.shape, 1)
    z = jnp.where(lane < LORA, jnp.tanh(z), z)
    return _dot(z.astype(BF16), lora_w)


def _pair_slices(d_a):
    return [slice(i * PAIR, (i + 1) * PAIR) for i in range(d_a // PAIR)]


def _pair_vectors(xs_list, lo_list, par_ref, d_a, block_ones):
    sls = _pair_slices(d_a)
    shifted = lambda sl, off: slice(off + sl.start, off + sl.stop)
    items = [(xs, lo, sl) for xs, lo in zip(xs_list, lo_list) for sl in sls]
    r = [xs[:, sl] for xs, _, sl in items]
    k = [xs[:, shifted(sl, d_a)] for xs, _, sl in items]
    v = [xs[:, shifted(sl, 2 * d_a)] for xs, _, sl in items]
    kkr = [ki * par_ref[2:3, sl] for ki, (_, _, sl) in zip(k, items)]
    ss = _head_sums([x * x for x in kkr], block_ones)
    kk = [x * jnp.minimum(lax.rsqrt(s), 1e12) for x, s in zip(kkr, ss)]
    log2_decay = [-LOG2_DECAY_SCALE * _sigmoid(par_ref[0:1, sl] + lo[:, sl]) for _, lo, sl in items]
    a = [_sigmoid(par_ref[1:2, sl] + lo[:, shifted(sl, d_a)]) for _, lo, sl in items]
    kf = [ki * (1.0 + (ai - 1.0) * par_ref[3:4, sl]) for ki, ai, (_, _, sl) in zip(k, a, items)]
    return r, kf, v, log2_decay, kk, a


def _head_norm_bonus(y, r, kf, v, par_ref, d_a, block_ones):
    sls = _pair_slices(d_a) * (len(y) // (d_a // PAIR))
    block_mean = _block_ones(1.0 / HEAD)
    mean = _head_sums(y, block_mean, split=True)
    d = [x - m for x, m in zip(y, mean)]
    var = _head_sums([x * x for x in d], block_mean)
    rk = _head_sums([ri * ki * par_ref[4:5, sl] for ri, ki, sl in zip(r, kf, sls)], block_ones)
    return [x * lax.rsqrt(vr + GN_EPS) * par_ref[5:6, sl] + par_ref[6:7, sl] + s * vi
            for x, vr, s, vi, sl in zip(d, var, rk, v, sls)]


def _stack_heads(x, lane_lo):
    xb = x.astype(BF16)
    zero = jnp.zeros_like(xb)
    return jnp.concatenate([jnp.where(lane_lo, xb, zero), jnp.where(lane_lo, zero, xb)], axis=0)


def _cumsum_rows(xs):
    n = xs[0].shape[0]
    tri = (lax.broadcasted_iota(jnp.int32, (n, n), 1) <= lax.broadcasted_iota(jnp.int32, (n, n), 0))
    tri = jnp.where(tri, 1.0, 0.0).astype(BF16)
    x = jnp.concatenate(xs, axis=1)
    hi = x.astype(BF16)
    lo = (x - hi.astype(F32)).astype(BF16)
    s = _dot(tri, hi) + _dot(tri, lo)
    return [s[:, i * PAIR:(i + 1) * PAIR] for i in range(len(xs))]


def _wkv_chunk_kernel(*refs, d_a, nb):
    ps_refs, pl_refs, sga_refs = refs[:nb], refs[nb:2 * nb], refs[2 * nb:3 * nb]
    mu_ref, par_ref, lora_ref, oa_ref, st_ref, carry_ref = refs[3 * nb:]
    c = pl.program_id(1)
    n_pairs = d_a // PAIR
    C = CHUNK
    n_rkv = 3 * d_a

    @pl.when(c == 0)
    def _():
        carry_ref[...] = jnp.zeros_like(carry_ref)
        st_ref[...] = jnp.zeros_like(st_ref)

    def shifted(p, prev_row, mu):
        row = lax.broadcasted_iota(jnp.int32, p.shape, 0)
        return _token_shift(p, jnp.where(row == 0, prev_row, pltpu.roll(p, 1, axis=0)), mu)

    xs, lo = [], []
    for bi in range(nb):
        p, pz = ps_refs[bi][...], pl_refs[bi][...]
        xs.append(shifted(p, carry_ref[bi, :, :n_rkv], mu_ref[:, :n_rkv]))
        lo.append(_lora_out(shifted(pz, carry_ref[bi, :, n_rkv:], mu_ref[:, n_rkv:]), lora_ref[...]))
        carry_ref[bi, :, :n_rkv] = p[C - 1:C, :]
        carry_ref[bi, :, n_rkv:] = pz[C - 1:C, :]

    block_ones = _block_ones()
    lane_lo = lax.broadcasted_iota(jnp.int32, (C, PAIR), 1) < HEAD
    ti = lax.broadcasted_iota(jnp.int32, (C, PAIR), 0)
    si = lax.broadcasted_iota(jnp.int32, (C, PAIR), 1) % C
    strict_lower = si < ti
    lower2 = (lax.broadcasted_iota(jnp.int32, (C, 2 * PAIR), 1) % C
              <= lax.broadcasted_iota(jnp.int32, (C, 2 * PAIR), 0))
    eye = jnp.where(si == ti, 1.0, 0.0)
    cat0 = lambda x, y: jnp.concatenate([x, y], axis=0)

    r, kf, v, log2_decay, kk, a = _pair_vectors(xs, lo, par_ref, d_a, block_ones)
    cl = _cumsum_rows(log2_decay)
    cend = [x[C - 1:C, :] for x in cl]
    e_neg = [jnp.exp2(-x) for x in cl]
    e_end = [jnp.exp2(ce - x) for ce, x in zip(cend, cl)]
    kka = [x * y for x, y in zip(kk, a)]
    stack = lambda xs_: [_stack_heads(x, lane_lo) for x in xs_]
    a_p = [(-x * jnp.exp2(c_ - ld)).astype(BF16) for x, c_, ld in zip(kk, cl, log2_decay)]
    r_p = [(x * jnp.exp2(c_)).astype(BF16) for x, c_ in zip(r, cl)]
    a_s = stack(a_p)
    b_s = stack([x * e for x, e in zip(kka, e_neg)])
    k_s = stack([x * e for x, e in zip(kf, e_neg)])
    v_s = stack(v)
    bh_s = stack([x * e for x, e in zip(kka, e_end)])
    kh_s = stack([x * e for x, e in zip(kf, e_end)])
    pend_col = [jnp.transpose(jnp.broadcast_to(jnp.exp2(ce), (PAIR, PAIR))) for ce in cend]

    m1 = [_dot_nt(cat0(x, y), cat0(z, w)) for x, y, z, w in zip(a_p, r_p, b_s, k_s)]
    a_ab = [jnp.where(strict_lower, m[:C, :PAIR], 0.0) for m in m1]
    keep = lambda mask, x: jnp.where(mask, x, jnp.zeros_like(x))
    a_ak = [keep(strict_lower, m[:C, PAIR:].astype(BF16)) for m in m1]
    a_rbk = [keep(lower2, m[C:].astype(BF16)) for m in m1]
    g = [_dot(x, y) for x, y in zip(a_ak, v_s)]

    apow = [_dot(x.astype(BF16), _stack_heads(x, lane_lo)) for x in a_ab]
    tinv = [eye + x for x in a_ab]
    n_levels = CHUNK.bit_length() - 1
    for lvl in range(1, n_levels):
        rhs = stack(apow)
        if lvl < n_levels - 1:
            both = [_dot(cat0(x.astype(BF16), t.astype(BF16)), y) for x, t, y in zip(apow, tinv, rhs)]
            apow = [x[:C] for x in both]
            tinv = [t + x[C:] for t, x in zip(tinv, both)]
        else:
            tinv = [t + _dot(t.astype(BF16), y) for t, y in zip(tinv, rhs)]

    g_s = stack(g)
    wx = [_dot(t.astype(BF16), jnp.concatenate([x, y], axis=1)) for t, x, y in zip(tinv, a_s, g_s)]
    seq_pair = [(bi, pi) for bi in range(nb) for pi in range(n_pairs)]
    st = [st_ref[bi, pi] for bi, pi in seq_pair]
    uy = [_dot(cat0(w[:, :PAIR].astype(BF16), x), s_.astype(BF16)) for w, x, s_ in zip(wx, r_p, st)]
    uv_s = [cat0(_stack_heads(x[:C] + w[:, PAIR:], lane_lo), y) for x, w, y in zip(uy, wx, v_s)]
    st_upd = [_dot_tn(cat0(x, y), z) for x, y, z in zip(bh_s, kh_s, uv_s)]
    y_uv = [_dot(x, z) for x, z in zip(a_rbk, uv_s)]
    for i, (bi, pi) in enumerate(seq_pair):
        st_ref[bi, pi] = pend_col[i] * st[i] + st_upd[i]
    y = [x[C:] + z for x, z in zip(uy, y_uv)]
    o = _head_norm_bonus(y, r, kf, v, par_ref, d_a, block_ones)
    sls = _pair_slices(d_a)
    for oi, (bi, pi) in zip(o, seq_pair):
        oa_ref[bi, :, sls[pi]] = (oi * sga_refs[bi][:, sls[pi]]).astype(oa_ref.dtype)


def _wkv_prompt(p_rkv, p_lora, gates, mu, par, lora_w, batch, seq, d_a):
    n_chunks = seq // CHUNK
    shift_w = mu.shape[1]
    n_pairs = d_a // PAIR
    nb = WKV_SEQS_PER_STEP if batch % WKV_SEQS_PER_STEP == 0 else 1
    kern = functools.partial(_wkv_chunk_kernel, d_a=d_a, nb=nb)
    seq_rows = lambda bi: (lambda b, c: ((b * nb + bi) * n_chunks + c, 0))
    oa, st = pl.pallas_call(
        kern,
        out_shape=(jax.ShapeDtypeStruct((batch, seq, d_a), BF16),
                   jax.ShapeDtypeStruct((batch, n_pairs, PAIR, PAIR), F32)),
        grid=(batch // nb, n_chunks),
        in_specs=([pl.BlockSpec((CHUNK, 3 * d_a), seq_rows(bi)) for bi in range(nb)]
                  + [pl.BlockSpec((CHUNK, 2 * LORA), seq_rows(bi)) for bi in range(nb)]
                  + [pl.BlockSpec((CHUNK, d_a), seq_rows(bi)) for bi in range(nb)]
                  + [pl.BlockSpec((1, shift_w), lambda b, c: (0, 0)),
                     pl.BlockSpec((8, d_a), lambda b, c: (0, 0)),
                     pl.BlockSpec((2 * LORA, 2 * d_a), lambda b, c: (0, 0))]),
        out_specs=(pl.BlockSpec((nb, CHUNK, d_a), lambda b, c: (b, c, 0)),
                   pl.BlockSpec((nb, n_pairs, PAIR, PAIR), lambda b, c: (b, 0, 0, 0))),
        scratch_shapes=[pltpu.VMEM((nb, 1, shift_w), F32)],
        compiler_params=_params(2),
        name="wkv_chunked",
    )(*([p_rkv] * nb + [p_lora] * nb + [gates] * nb), mu, par, lora_w)
    return oa.reshape(batch * seq, d_a), st


STEP_ROWS = 8


def _wkv_step_kernel(ps_ref, pl_ref, prev_ref, sga_ref, mu_ref, par_ref, lora_ref, s_ref, oa_ref, so_ref,
                     vt_ref, rkv_ref, y_ref, *, d_a):
    h = pl.program_id(0)
    n_heads = d_a // HEAD
    n_rkv = 3 * d_a
    sls = _pair_slices(d_a)

    @pl.when(h == 0)
    def _():
        xs = _token_shift(ps_ref[...], prev_ref[:, :n_rkv], mu_ref[:, :n_rkv])
        lo = _lora_out(_token_shift(pl_ref[...], prev_ref[:, n_rkv:], mu_ref[:, n_rkv:]), lora_ref[...])
        r, kf, v, log2_decay, kk, a = _pair_vectors([xs], [lo], par_ref, d_a, _block_ones())
        for pi, sl in enumerate(sls):
            vecs = (-kk[pi], jnp.exp2(log2_decay[pi]), kk[pi] * a[pi], kf[pi], r[pi], v[pi])
            for i, x in enumerate(vecs):
                vt_ref[i, sl, :] = x.T
            for i, x in enumerate((r[pi], kf[pi], v[pi])):
                rkv_ref[i, :, sl] = x

    base = pl.multiple_of(h * HEAD, HEAD)
    a_t, w_t, b_t, k_t, r_t = (vt_ref[i, pl.ds(base, HEAD), :] for i in range(5))

    def rows(g, carry):
        i0 = pl.multiple_of(g * STEP_ROWS, STEP_ROWS)
        row0 = pl.multiple_of(base + i0, STEP_ROWS)
        v_rows = vt_ref[5, pl.ds(row0, STEP_ROWS), :]
        ys = []
        for ii in range(STEP_ROWS):
            s = s_ref[0, i0 + ii]
            sa = jnp.sum(s * a_t, axis=0, keepdims=True)
            s_new = s * w_t + sa * b_t + v_rows[ii:ii + 1, :] * k_t
            so_ref[0, i0 + ii] = s_new
            ys.append(jnp.sum(s_new * r_t, axis=0, keepdims=True))
        y_ref[pl.ds(row0, STEP_ROWS), :] = jnp.concatenate(ys, axis=0)
        return carry

    lax.fori_loop(0, HEAD // STEP_ROWS, rows, 0)

    @pl.when(h == n_heads - 1)
    def _():
        y = [y_ref[sl, :].T for sl in sls]
        r, kf, v = ([rkv_ref[i, :, sl] for sl in sls] for i in range(3))
        o = _head_norm_bonus(y, r, kf, v, par_ref, d_a, _block_ones())
        for oi, sl in zip(o, sls):
            oa_ref[:, sl] = (oi * sga_ref[:, sl]).astype(oa_ref.dtype)


def _wkv_sample(p_rkv, p_lora, shift_prev, gates, mu, par, lora_w, state_t, d_a, row0):
    batch, shift_w = shift_prev.shape
    n_heads = d_a // HEAD
    assert row0 % batch == 0 and state_t.shape == (n_heads, HEAD, HEAD, batch)
    blk0 = row0 // batch
    const = lambda h: (0, 0)
    kern = functools.partial(_wkv_step_kernel, d_a=d_a)
    return pl.pallas_call(
        kern,
        out_shape=(jax.ShapeDtypeStruct((batch, d_a), BF16),
                   jax.ShapeDtypeStruct(state_t.shape, F32)),
        grid=(n_heads,),
        in_specs=[pl.BlockSpec((batch, 3 * d_a), lambda h: (blk0, 0)),
                  pl.BlockSpec((batch, 2 * LORA), lambda h: (blk0, 0)),
                  pl.BlockSpec((batch, shift_w), const),
                  pl.BlockSpec((batch, d_a), lambda h: (blk0, 0)),
                  pl.BlockSpec((1, shift_w), const),
                  pl.BlockSpec((8, d_a), const),
                  pl.BlockSpec((2 * LORA, 2 * d_a), const),
                  pl.BlockSpec((1, HEAD, HEAD, batch), lambda h: (h, 0, 0, 0))],
        out_specs=(pl.BlockSpec((batch, d_a), const),
                   pl.BlockSpec((1, HEAD, HEAD, batch), lambda h: (h, 0, 0, 0))),
        scratch_shapes=[pltpu.VMEM((6, d_a, batch), F32), pltpu.VMEM((3, batch, d_a), F32),
                        pltpu.VMEM((d_a, batch), F32)],
        compiler_params=_params(1),
        name="wkv_step",
    )(p_rkv, p_lora, shift_prev, gates, mu, par, lora_w, state_t)


CONV_PAD = 32
CONV_STRIP = 16


def _conv_prompt_kernel(u_ref, w_ref, b_ref, o_ref, buf_ref, c_ref, *, tt, taps):
    t = pl.program_id(1)
    sub = buf_ref.shape[1]
    lanes = [slice(s * LANES, (s + 1) * LANES) for s in range(sub)]

    @pl.when(t == 0)
    def _():
        buf_ref[0:CONV_PAD] = jnp.zeros((CONV_PAD,) + buf_ref.shape[1:], F32)

    u = u_ref[...]
    buf_ref[CONV_PAD:CONV_PAD + tt] = jnp.swapaxes(jnp.stack([u[:, ls] for ls in lanes], axis=0), 0, 1)
    off = CONV_PAD - (taps - 1)
    w = [w_ref[k] for k in range(taps)]
    bias = b_ref[0]

    def strip(s, carry):
        t0 = s * CONV_STRIP
        acc = [bias] * CONV_STRIP
        for i in range(CONV_STRIP + taps - 1):
            x = buf_ref[off + t0 + i]
            for j in range(CONV_STRIP):
                if 0 <= i - j < taps:
                    acc[j] = acc[j] + w[i - j] * x
        for j in range(CONV_STRIP):
            c_ref[t0 + j] = acc[j]
        return carry

    lax.fori_loop(0, tt // CONV_STRIP, strip, 0, unroll=4)
    buf_ref[0:CONV_PAD] = buf_ref[tt:tt + CONV_PAD]
    c = jnp.swapaxes(c_ref[...], 0, 1)
    for s, ls in enumerate(lanes):
        o_ref[:, ls] = c[s]


def _conv_prompt(u, conv_w, conv_b, batch, seq, tt):
    d_b = u.shape[1]
    taps = conv_w.shape[0]
    nt = seq // tt
    sub = d_b // LANES
    assert taps - 1 <= CONV_PAD and tt % CONV_STRIP == 0 and tt >= CONV_PAD
    kern = functools.partial(_conv_prompt_kernel, tt=tt, taps=taps)
    return pl.pallas_call(
        kern,
        out_shape=jax.ShapeDtypeStruct((batch * seq, d_b), F32),
        grid=(batch, nt),
        in_specs=[pl.BlockSpec((tt, d_b), lambda b, t: (b * nt + t, 0)),
                  pl.BlockSpec((taps, sub, LANES), lambda b, t: (0, 0, 0)),
                  pl.BlockSpec((1, sub, LANES), lambda b, t: (0, 0, 0))],
        out_specs=pl.BlockSpec((tt, d_b), lambda b, t: (b * nt + t, 0)),
        scratch_shapes=[pltpu.VMEM((tt + CONV_PAD, sub, LANES), F32), pltpu.VMEM((tt, sub, LANES), F32)],
        compiler_params=_params(2),
        name="conv_prompt",
    )(u, conv_w.reshape(taps, sub, LANES), conv_b.reshape(1, sub, LANES))


def _conv_step_kernel(u_ref, prev_ref, w_ref, b_ref, o_ref, hist_ref, *, taps):
    u = u_ref[...]
    c = b_ref[...] + w_ref[taps - 1:taps, :] * u
    for k in range(taps - 1):
        c = c + w_ref[k:k + 1, :] * prev_ref[k]
    o_ref[...] = c
    hist_ref[0:taps - 2] = prev_ref[1:taps - 1]
    hist_ref[taps - 2] = u


def _conv_sample(u, conv_prev_t, conv_w, conv_b, bb, row0):
    batch = conv_prev_t.shape[1]
    d_b = u.shape[1]
    taps = conv_w.shape[0]
    assert row0 % bb == 0 and batch % bb == 0
    blk0 = row0 // bb
    kern = functools.partial(_conv_step_kernel, taps=taps)
    return pl.pallas_call(
        kern,
        out_shape=(jax.ShapeDtypeStruct((batch, d_b), F32),
                   jax.ShapeDtypeStruct(conv_prev_t.shape, F32)),
        grid=(batch // bb,),
        in_specs=[pl.BlockSpec((bb, d_b), lambda i: (blk0 + i, 0)),
                  pl.BlockSpec((taps - 1, bb, d_b), lambda i: (0, i, 0)),
                  pl.BlockSpec((taps, d_b), lambda i: (0, 0)),
                  pl.BlockSpec((1, d_b), lambda i: (0, 0))],
        out_specs=(pl.BlockSpec((bb, d_b), lambda i: (i, 0)),
                   pl.BlockSpec((taps - 1, bb, d_b), lambda i: (0, i, 0))),
        compiler_params=_params(1),
        name="conv_step",
    )(u, conv_prev_t, conv_w, conv_b.reshape(1, d_b))


def _tail_kernel(oa_ref, c_ref, gb_ref, sga_ref, sgb_ref, x_ref, p_ref, cln_ref, wa_ref, wb_ref, wout_ref,
                 wpg_ref, wple_ref, fg_ref, o_ref):
    c = c_ref[...]
    mean = jnp.mean(c, axis=-1, keepdims=True)
    dc = c - mean
    var = jnp.mean(dc * dc, axis=-1, keepdims=True)
    cf = dc * lax.rsqrt(var + LN_EPS) * cln_ref[0:1, :] + cln_ref[1:2, :]
    cb = (cf * _sigmoid(cf) * gb_ref[...]).astype(BF16)
    m = sga_ref[...] * _dot(oa_ref[...], wa_ref[...]) + sgb_ref[...] * _dot(cb, wb_ref[...])
    h = x_ref[...] + _dot(m.astype(BF16), wout_ref[...])
    gate = _sigmoid(_dot(h.astype(BF16), wpg_ref[...]))
    h = h + gate * _dot(p_ref[...].astype(BF16), wple_ref[...])
    ms = jnp.mean(h * h, axis=-1, keepdims=True)
    o_ref[...] = h * lax.rsqrt(ms + RMS_EPS) * fg_ref[...]


def _tail(oa, c, gates, x, p, cln, wa, wb, wout, wpg, wple, fg, tm, gate_b_block, merge_block, row0=0):
    m, d = x.shape
    d_a = oa.shape[1]
    d_b = c.shape[1]
    ple = p.shape[1]
    blk0 = row0 // tm
    row = lambda i: (i, 0)
    const = lambda i: (0, 0)
    resident = lambda shape: pl.BlockSpec(shape, const, pipeline_mode=pl.Buffered(1))
    return pl.pallas_call(
        _tail_kernel,
        out_shape=jax.ShapeDtypeStruct((m, d), F32),
        grid=(m // tm,),
        in_specs=[pl.BlockSpec((tm, d_a), row), pl.BlockSpec((tm, d_b), row),
                  pl.BlockSpec((tm, d_b), lambda i: (blk0 + i, gate_b_block)),
                  pl.BlockSpec((tm, d), lambda i: (blk0 + i, merge_block)),
                  pl.BlockSpec((tm, d), lambda i: (blk0 + i, merge_block + 1)),
                  pl.BlockSpec((tm, d), row), pl.BlockSpec((tm, ple), row),
                  resident((2, d_b)),
                  resident((d_a, d)), resident((d_b, d)), resident((d, d)), resident((d, d)),
                  resident((ple, d)), resident((1, d))],
        out_specs=pl.BlockSpec((tm, d), row),
        compiler_params=_params(1),
        name="tail",
    )(oa, c, gates, gates, gates, x, p, cln, wa, wb, wout, wpg, wple, fg)


def _largest_tile(n, cap, align):
    t = min(n, cap)
    while n % t or t % align:
        t -= 1
    return t


def kernel(x_prompt, x_sample, state_shift, state_wkv, state_conv, p_prompt, p_sample, norm_g, w_in,
           shift_mu, w0, w_lora_b, a0, a_lora_b, k_k, k_a, r_k, lnx_g, lnx_b, w_proj_a, conv_w, conv_b,
           cln_g, cln_b, w_proj_b, w_out, w_ple, w_ple_gate, final_g):
    depth = w_in.shape[0]
    batch, seq, d = x_prompt.shape
    dec_batch, dec_seq, _ = x_sample.shape
    d_a = w_proj_a.shape[1]
    d_b = w_proj_b.shape[1]
    shift_w = shift_mu.shape[1]
    n_heads = d_a // HEAD
    n_pairs = d_a // PAIR
    taps = conv_w.shape[1]
    assert depth == 1 and dec_seq == 1 and d_a == d_b and 2 * d_a == d
    assert shift_w == 3 * d_a + 2 * LORA and d_a % PAIR == 0 and seq % CHUNK == 0

    o1 = shift_w
    o2 = o1 + d_a
    o3 = o2 + 2 * d_b
    o4 = o3 + d_b
    w = w_in[0]
    lora_w = jnp.zeros((2 * LORA, 2 * d_a), F32)
    lora_w = lora_w.at[:LORA, :d_a].set(w_lora_b[0]).at[LORA:, d_a:].set(a_lora_b[0]).astype(BF16)
    zeros_a = jnp.zeros((d_a,), F32)
    par = jnp.stack([w0[0], a0[0], k_k[0], k_a[0], r_k[0].reshape(d_a), lnx_g[0], lnx_b[0], zeros_a])
    cln = jnp.stack([cln_g[0], cln_b[0]])
    mu = shift_mu
    g_in = norm_g[0].reshape(1, d)
    fg = final_g.reshape(1, d)
    gate_b_block = d_a // d_b
    merge_block = (d_a + d_b) // d

    m_p = batch * seq
    m_all = m_p + dec_batch
    x2 = x_prompt.reshape(m_p, d)
    xs2 = x_sample.reshape(dec_batch, d)
    n_rkv = 3 * d_a
    xn, p_lora = _rmsnorm_bf16(x2, xs2, g_in, w, n_rkv, 2 * LORA, _largest_tile(m_p, 1024, SUBLANES))
    tm = _largest_tile(m_all, 1100, BF16_SUBLANES)
    tn_shift = _largest_tile(n_rkv, 1024, LANES)
    p_rkv = _project(xn, w, lambda j: j * tn_shift, n_rkv, tm, tn_shift)
    gate_cols = lambda j: jnp.where(j == 0, o1, o3 + (j - 1) * d_a)
    gates = _project(xn, w, gate_cols, d_a + d_b + 2 * d, tm, d_a, n_silu_blocks=2)
    u, (wa, wb, wout, wpg, wple) = _project_glu(
        xn, w, o2, o2 + d_b, d_b, tm, _largest_tile(d_b, 512, LANES),
        casts=[w_proj_a[0], w_proj_b[0], w_out[0], w_ple_gate[0], w_ple[0]])

    oa, st = _wkv_prompt(p_rkv, p_lora, gates, mu, par, lora_w, batch, seq, d_a)
    c = _conv_prompt(u, conv_w[0], conv_b[0], batch, seq, _largest_tile(seq, 1024, CONV_STRIP))
    y_prompt = _tail(oa, c, gates, x2, p_prompt[0].reshape(m_p, -1), cln, wa, wb, wout, wpg, wple, fg,
                     _largest_tile(m_p, 256, BF16_SUBLANES), gate_b_block, merge_block).reshape(batch, seq, d)
    last_rows = lambda a: jnp.concatenate([a[(b + 1) * seq - 1:(b + 1) * seq] for b in range(batch)])
    new_shift_p = jnp.concatenate([last_rows(p_rkv), last_rows(p_lora)], axis=1)[None]
    st = st.reshape(batch, n_pairs, 2, HEAD, 2, HEAD)
    new_wkv_p = jnp.stack([st[:, :, 0, :, 0, :], st[:, :, 1, :, 1, :]], axis=2)
    new_wkv_p = jnp.swapaxes(new_wkv_p.reshape(batch, n_heads, HEAD, HEAD), -1, -2)[None]
    new_conv_p = jnp.stack([u[(b + 1) * seq - (taps - 1):(b + 1) * seq] for b in range(batch)])[None]

    tm_s = _largest_tile(dec_batch, 256, BF16_SUBLANES)
    assert m_p % tm_s == 0
    oa_s, new_wkv_t = _wkv_sample(p_rkv, p_lora, state_shift[0], gates, mu, par, lora_w,
                                  jnp.transpose(state_wkv[0], (1, 2, 3, 0)), d_a, m_p)
    new_wkv_s = jnp.transpose(new_wkv_t, (3, 0, 1, 2))
    c_s, new_conv_t = _conv_sample(u, jnp.transpose(state_conv[0], (1, 0, 2)), conv_w[0], conv_b[0],
                                   _largest_tile(dec_batch, 32, SUBLANES), m_p)
    new_conv_s = jnp.transpose(new_conv_t, (1, 0, 2))
    y_sample = _tail(oa_s, c_s, gates, xs2, p_sample[0].reshape(dec_batch, -1), cln, wa, wb, wout, wpg,
                     wple, fg, tm_s, gate_b_block, merge_block, row0=m_p).reshape(dec_batch, 1, d)
    new_shift_s = jnp.concatenate([p_rkv[m_p:], p_lora[m_p:]], axis=1)[None]
    new_conv_s = new_conv_s[None]

    return (y_prompt, y_sample, new_shift_p, new_wkv_p, new_conv_p, new_shift_s, new_wkv_s[None],
            new_conv_s)
```

```python
import functools
import math

import jax
import jax.numpy as jnp
from jax import lax
from jax.experimental import pallas as pl
from jax.experimental.pallas import tpu as pltpu

F32 = jnp.float32
BF16 = jnp.bfloat16

LANES = 128
SUBLANES = 8
BF16_SUBLANES = 16

HEAD = 64
PAIR = 2 * HEAD
LORA = 64
CHUNK = 64
WKV_SEQS_PER_STEP = 4
RMS_EPS = 1e-6
LN_EPS = 1e-5
GN_EPS = 64e-5
LOG2_DECAY_SCALE = math.exp(-0.5) / math.log(2.0)
VMEM_LIMIT = 56 * 1024 * 1024


def _params(n_axes, vmem=VMEM_LIMIT):
    return pltpu.CompilerParams(dimension_semantics=("arbitrary",) * n_axes, vmem_limit_bytes=vmem)


def _sigmoid(x):
    return 0.5 * jnp.tanh(0.5 * x) + 0.5


def _dot(a, b):
    return jnp.dot(a, b, preferred_element_type=F32)


def _dot_nt(a, b):
    return lax.dot_general(a, b, (((1,), (1,)), ((), ())), preferred_element_type=F32)


def _dot_tn(a, b):
    return lax.dot_general(a, b, (((0,), (0,)), ((), ())), preferred_element_type=F32)


def _rmsnorm_kernel(xp_ref, xs_ref, g_ref, w_ref, o_ref, ol_ref, wb_ref, *, n_prompt_blocks):
    i = pl.program_id(0)

    @pl.when(i == 0)
    def _():
        wb_ref[...] = w_ref[...].astype(BF16)

    def norm(x):
        ms = jnp.mean(x * x, axis=-1, keepdims=True)
        return (x * lax.rsqrt(ms + RMS_EPS) * g_ref[...]).astype(o_ref.dtype)

    @pl.when(i < n_prompt_blocks)
    def _():
        xn = norm(xp_ref[...])
        o_ref[...] = xn
        ol_ref[...] = _dot(xn, wb_ref[...])

    @pl.when(i == n_prompt_blocks)
    def _():
        xn = norm(xs_ref[...])
        o_ref[0:xs_ref.shape[0], :] = xn
        ol_ref[0:xs_ref.shape[0], :] = _dot(xn, wb_ref[...])


def _rmsnorm_bf16(x_prompt, x_sample, g, w, col, n_cols, tm):
    m_p, d = x_prompt.shape
    m_s = x_sample.shape[0]
    n_blocks = m_p // tm
    assert m_p % tm == 0 and m_s <= tm and col % LANES == 0
    kern = functools.partial(_rmsnorm_kernel, n_prompt_blocks=n_blocks)
    return pl.pallas_call(
        kern,
        out_shape=(jax.ShapeDtypeStruct((m_p + m_s, d), BF16),
                   jax.ShapeDtypeStruct((m_p + m_s, n_cols), F32)),
        grid=(n_blocks + 1,),
        in_specs=[pl.BlockSpec((tm, d), lambda i: (jnp.minimum(i, n_blocks - 1), 0)),
                  pl.BlockSpec((m_s, d), lambda i: (0, 0)),
                  pl.BlockSpec((1, d), lambda i: (0, 0)),
                  pl.BlockSpec((pl.Element(d), pl.Element(n_cols)), lambda i: (0, col))],
        out_specs=(pl.BlockSpec((tm, d), lambda i: (i, 0)),
                   pl.BlockSpec((tm, n_cols), lambda i: (i, 0))),
        scratch_shapes=[pltpu.VMEM((d, n_cols), BF16)],
        compiler_params=_params(1),
        name="rmsnorm",
    )(x_prompt, x_sample, g, w)


def _proj_kernel(x_ref, w_ref, o_ref, wb_ref, *, n_silu_blocks):
    @pl.when(pl.program_id(1) == 0)
    def _():
        wb_ref[...] = w_ref[...].astype(BF16)

    y = _dot(x_ref[...], wb_ref[...])
    if n_silu_blocks is None:
        o_ref[...] = y
    else:
        s = _sigmoid(y)
        o_ref[...] = jnp.where(pl.program_id(0) < n_silu_blocks, y * s, s)


def _project(xn, w, col_start, n_out, tm, tn, n_silu_blocks=None):
    m, d = xn.shape
    kern = functools.partial(_proj_kernel, n_silu_blocks=n_silu_blocks)
    return pl.pallas_call(
        kern,
        out_shape=jax.ShapeDtypeStruct((m, n_out), F32),
        grid=(n_out // tn, m // tm),
        in_specs=[pl.BlockSpec((tm, d), lambda j, i: (i, 0)),
                  pl.BlockSpec((pl.Element(d), pl.Element(tn)),
                               lambda j, i: (0, pl.multiple_of(col_start(j), LANES)))],
        out_specs=pl.BlockSpec((tm, tn), lambda j, i: (i, j)),
        scratch_shapes=[pltpu.VMEM((d, tn), BF16)],
        compiler_params=_params(2),
        name="in_proj",
    )(xn, w)


def _glu_kernel(*refs, n_casts):
    x_ref, wa_ref, wb_ref = refs[:3]
    cast_in = refs[3:3 + n_casts]
    o_ref = refs[3 + n_casts]
    cast_out = refs[4 + n_casts:4 + 2 * n_casts]
    wab_ref, wbb_ref = refs[4 + 2 * n_casts:]

    @pl.when(pl.program_id(1) == 0)
    def _():
        wab_ref[...] = wa_ref[...].astype(BF16)
        wbb_ref[...] = wb_ref[...].astype(BF16)

    x = x_ref[...]
    o_ref[...] = _dot(x, wab_ref[...]) * _sigmoid(_dot(x, wbb_ref[...]))
    for src, dst in zip(cast_in, cast_out):
        dst[...] = src[...].astype(BF16)


def _project_glu(xn, w, col_a, col_b, n_out, tm, tn, casts):
    m, d = xn.shape
    nj, ni = n_out // tn, m // tm
    steps = nj * ni
    for a in casts:
        assert a.shape[0] % (BF16_SUBLANES * steps) == 0, (a.shape, steps)
    step_rows = lambda a: pl.BlockSpec((a.shape[0] // steps, a.shape[1]), lambda j, i: (j * ni + i, 0))
    w_cols = lambda col: pl.BlockSpec((pl.Element(d), pl.Element(tn)),
                                      lambda j, i: (0, pl.multiple_of(col + j * tn, LANES)))
    kern = functools.partial(_glu_kernel, n_casts=len(casts))
    out = pl.pallas_call(
        kern,
        out_shape=[jax.ShapeDtypeStruct((m, n_out), F32)]
        + [jax.ShapeDtypeStruct(a.shape, BF16) for a in casts],
        grid=(nj, ni),
        in_specs=[pl.BlockSpec((tm, d), lambda j, i: (i, 0)), w_cols(col_a), w_cols(col_b)]
        + [step_rows(a) for a in casts],
        out_specs=[pl.BlockSpec((tm, tn), lambda j, i: (i, j))] + [step_rows(a) for a in casts],
        scratch_shapes=[pltpu.VMEM((d, tn), BF16), pltpu.VMEM((d, tn), BF16)],
        compiler_params=_params(2),
        name="in_proj_glu",
    )(xn, w, w, *casts)
    return out[0], out[1:]


def _head_sum(x, block_ones, split=False):
    hi = x.astype(BF16)
    if not split:
        return _dot(hi, block_ones)
    lo = (x - hi.astype(F32)).astype(BF16)
    return _dot(hi, block_ones) + _dot(lo, block_ones)


def _head_sums(xs, block_ones, split=False):
    rows = xs[0].shape[0]
    s = _head_sum(jnp.concatenate(xs, axis=0), block_ones, split)
    return [s[i * rows:(i + 1) * rows] for i in range(len(xs))]


def _block_ones(value=1.0):
    ri = lax.broadcasted_iota(jnp.int32, (PAIR, PAIR), 0)
    ci = lax.broadcasted_iota(jnp.int32, (PAIR, PAIR), 1)
    return jnp.where((ri < HEAD) == (ci < HEAD), value, 0.0).astype(BF16)


def _token_shift(p, prev, mu):
    return p + mu * (prev - p)


def _lora_out(z, lora_w):
    lane = lax.broadcasted_iota(jnp.int32, z.shape, 1)
    z = jnp.where(lane < LORA, jnp.tanh(z), z)
    return _dot(z.astype(BF16), lora_w)


def _pair_slices(d_a):
    return [slice(i * PAIR, (i + 1) * PAIR) for i in range(d_a // PAIR)]


def _pair_vectors(xs_list, lo_list, par_ref, d_a, block_ones):
    sls = _pair_slices(d_a)
    shifted = lambda sl, off: slice(off + sl.start, off + sl.stop)
    items = [(xs, lo, sl) for xs, lo in zip(xs_list, lo_list) for sl in sls]
    r = [xs[:, sl] for xs, _, sl in items]
    k = [xs[:, shifted(sl, d_a)] for xs, _, sl in items]
    v = [xs[:, shifted(sl, 2 * d_a)] for xs, _, sl in items]
    kkr = [ki * par_ref[2:3, sl] for ki, (_, _, sl) in zip(k, items)]
    ss = _head_sums([x * x for x in kkr], block_ones)
    kk = [x * jnp.minimum(lax.rsqrt(s), 1e12) for x, s in zip(kkr, ss)]
    log2_decay = [-LOG2_DECAY_SCALE * _sigmoid(par_ref[0:1, sl] + lo[:, sl]) for _, lo, sl in items]
    a = [_sigmoid(par_ref[1:2, sl] + lo[:, shifted(sl, d_a)]) for _, lo, sl in items]
    kf = [ki * (1.0 + (ai - 1.0) * par_ref[3:4, sl]) for ki, ai, (_, _, sl) in zip(k, a, items)]
    return r, kf, v, log2_decay, kk, a


def _head_norm_bonus(y, r, kf, v, par_ref, d_a, block_ones):
    sls = _pair_slices(d_a) * (len(y) // (d_a // PAIR))
    block_mean = _block_ones(1.0 / HEAD)
    mean = _head_sums(y, block_mean, split=True)
    d = [x - m for x, m in zip(y, mean)]
    var = _head_sums([x * x for x in d], block_mean)
    rk = _head_sums([ri * ki * par_ref[4:5, sl] for ri, ki, sl in zip(r, kf, sls)], block_ones)
    return [x * lax.rsqrt(vr + GN_EPS) * par_ref[5:6, sl] + par_ref[6:7, sl] + s * vi
            for x, vr, s, vi, sl in zip(d, var, rk, v, sls)]


def _stack_heads(x, lane_lo):
    xb = x.astype(BF16)
    zero = jnp.zeros_like(xb)
    return jnp.concatenate([jnp.where(lane_lo, xb, zero), jnp.where(lane_lo, zero, xb)], axis=0)


def _cumsum_rows(xs):
    n = xs[0].shape[0]
    tri = (lax.broadcasted_iota(jnp.int32, (n, n), 1) <= lax.broadcasted_iota(jnp.int32, (n, n), 0))
    tri = jnp.where(tri, 1.0, 0.0).astype(BF16)
    x = jnp.concatenate(xs, axis=1)
    hi = x.astype(BF16)
    lo = (x - hi.astype(F32)).astype(BF16)
    s = _dot(tri, hi) + _dot(tri, lo)
    return [s[:, i * PAIR:(i + 1) * PAIR] for i in range(len(xs))]


def _wkv_chunk_kernel(*refs, d_a, nb):
    ps_refs, pl_refs, sga_refs = refs[:nb], refs[nb:2 * nb], refs[2 * nb:3 * nb]
    mu_ref, par_ref, lora_ref, oa_ref, st_ref, carry_ref = refs[3 * nb:]
    c = pl.program_id(1)
    n_pairs = d_a // PAIR
    C = CHUNK
    n_rkv = 3 * d_a

    @pl.when(c == 0)
    def _():
        carry_ref[...] = jnp.zeros_like(carry_ref)
        st_ref[...] = jnp.zeros_like(st_ref)

    def shifted(p, prev_row, mu):
        row = lax.broadcasted_iota(jnp.int32, p.shape, 0)
        return _token_shift(p, jnp.where(row == 0, prev_row, pltpu.roll(p, 1, axis=0)), mu)

    xs, lo = [], []
    for bi in range(nb):
        p, pz = ps_refs[bi][...], pl_refs[bi][...]
        xs.append(shifted(p, carry_ref[bi, :, :n_rkv], mu_ref[:, :n_rkv]))
        lo.append(_lora_out(shifted(pz, carry_ref[bi, :, n_rkv:], mu_ref[:, n_rkv:]), lora_ref[...]))
        carry_ref[bi, :, :n_rkv] = p[C - 1:C, :]
        carry_ref[bi, :, n_rkv:] = pz[C - 1:C, :]

    block_ones = _block_ones()
    lane_lo = lax.broadcasted_iota(jnp.int32, (C, PAIR), 1) < HEAD
    ti = lax.broadcasted_iota(jnp.int32, (C, PAIR), 0)
    si = lax.broadcasted_iota(jnp.int32, (C, PAIR), 1) % C
    strict_lower = si < ti
    lower2 = (lax.broadcasted_iota(jnp.int32, (C, 2 * PAIR), 1) % C
              <= lax.broadcasted_iota(jnp.int32, (C, 2 * PAIR), 0))
    eye = jnp.where(si == ti, 1.0, 0.0)
    cat0 = lambda x, y: jnp.concatenate([x, y], axis=0)

    r, kf, v, log2_decay, kk, a = _pair_vectors(xs, lo, par_ref, d_a, block_ones)
    cl = _cumsum_rows(log2_decay)
    cend = [x[C - 1:C, :] for x in cl]
    e_neg = [jnp.exp2(-x) for x in cl]
    kka = [x * y for x, y in zip(kk, a)]
    stack = lambda xs_: [_stack_heads(x, lane_lo) for x in xs_]
    a_p = [(-x * jnp.exp2(c_ - ld)).astype(BF16) for x, c_, ld in zip(kk, cl, log2_decay)]
    r_p = [(x * jnp.exp2(c_)).astype(BF16) for x, c_ in zip(r, cl)]
    a_s = stack(a_p)
    bk_s = [cat0(x, y) for x, y in zip(stack([x * e for x, e in zip(kka, e_neg)]),
                                       stack([x * e for x, e in zip(kf, e_neg)]))]
    v_s = stack(v)
    pend_col = [jnp.transpose(jnp.broadcast_to(jnp.exp2(ce), (PAIR, PAIR))) for ce in cend]

    m1 = [_dot_nt(cat0(x, y), z) for x, y, z in zip(a_p, r_p, bk_s)]
    a_ab = [jnp.where(strict_lower, m[:C, :PAIR], 0.0) for m in m1]
    keep = lambda mask, x: jnp.where(mask, x, jnp.zeros_like(x))
    a_ak = [keep(strict_lower, m[:C, PAIR:].astype(BF16)) for m in m1]
    a_rbk = [keep(lower2, m[C:].astype(BF16)) for m in m1]
    g = [_dot(x, y) for x, y in zip(a_ak, v_s)]

    apow = [_dot(x.astype(BF16), _stack_heads(x, lane_lo)) for x in a_ab]
    tinv = [eye + x for x in a_ab]
    n_levels = CHUNK.bit_length() - 1
    for lvl in range(1, n_levels):
        rhs = stack(apow)
        if lvl < n_levels - 1:
            both = [_dot(cat0(x.astype(BF16), t.astype(BF16)), y) for x, t, y in zip(apow, tinv, rhs)]
            apow = [x[:C] for x in both]
            tinv = [t + x[C:] for t, x in zip(tinv, both)]
        else:
            tinv = [t + _dot(t.astype(BF16), y) for t, y in zip(tinv, rhs)]

    g_s = stack(g)
    wx = [_dot(t.astype(BF16), jnp.concatenate([x, y], axis=1)) for t, x, y in zip(tinv, a_s, g_s)]
    seq_pair = [(bi, pi) for bi in range(nb) for pi in range(n_pairs)]
    st = [st_ref[bi, pi] for bi, pi in seq_pair]
    uy = [_dot(cat0(w[:, :PAIR].astype(BF16), x), s_.astype(BF16)) for w, x, s_ in zip(wx, r_p, st)]
    uv_s = [cat0(_stack_heads(x[:C] + w[:, PAIR:], lane_lo), y) for x, w, y in zip(uy, wx, v_s)]
    st_upd = [_dot_tn(x, z) for x, z in zip(bk_s, uv_s)]
    y_uv = [_dot(x, z) for x, z in zip(a_rbk, uv_s)]
    for i, (bi, pi) in enumerate(seq_pair):
        st_ref[bi, pi] = pend_col[i] * (st[i] + st_upd[i])
    y = [x[C:] + z for x, z in zip(uy, y_uv)]
    o = _head_norm_bonus(y, r, kf, v, par_ref, d_a, block_ones)
    sls = _pair_slices(d_a)
    for oi, (bi, pi) in zip(o, seq_pair):
        oa_ref[bi, :, sls[pi]] = (oi * sga_refs[bi][:, sls[pi]]).astype(oa_ref.dtype)


def _wkv_prompt(p_rkv, p_lora, gates, mu, par, lora_w, batch, seq, d_a):
    n_chunks = seq // CHUNK
    shift_w = mu.shape[1]
    n_pairs = d_a // PAIR
    nb = WKV_SEQS_PER_STEP if batch % WKV_SEQS_PER_STEP == 0 else 1
    kern = functools.partial(_wkv_chunk_kernel, d_a=d_a, nb=nb)
    seq_rows = lambda bi: (lambda b, c: ((b * nb + bi) * n_chunks + c, 0))
    oa, st = pl.pallas_call(
        kern,
        out_shape=(jax.ShapeDtypeStruct((batch, seq, d_a), BF16),
                   jax.ShapeDtypeStruct((batch, n_pairs, PAIR, PAIR), F32)),
        grid=(batch // nb, n_chunks),
        in_specs=([pl.BlockSpec((CHUNK, 3 * d_a), seq_rows(bi)) for bi in range(nb)]
                  + [pl.BlockSpec((CHUNK, 2 * LORA), seq_rows(bi)) for bi in range(nb)]
                  + [pl.BlockSpec((CHUNK, d_a), seq_rows(bi)) for bi in range(nb)]
                  + [pl.BlockSpec((1, shift_w), lambda b, c: (0, 0)),
                     pl.BlockSpec((8, d_a), lambda b, c: (0, 0)),
                     pl.BlockSpec((2 * LORA, 2 * d_a), lambda b, c: (0, 0))]),
        out_specs=(pl.BlockSpec((nb, CHUNK, d_a), lambda b, c: (b, c, 0)),
                   pl.BlockSpec((nb, n_pairs, PAIR, PAIR), lambda b, c: (b, 0, 0, 0))),
        scratch_shapes=[pltpu.VMEM((nb, 1, shift_w), F32)],
        compiler_params=_params(2),
        name="wkv_chunked",
    )(*([p_rkv] * nb + [p_lora] * nb + [gates] * nb), mu, par, lora_w)
    return oa.reshape(batch * seq, d_a), st


STEP_ROWS = 8


def _wkv_step_kernel(ps_ref, pl_ref, prev_ref, sga_ref, mu_ref, par_ref, lora_ref, s_ref, oa_ref, so_ref,
                     vt_ref, rkv_ref, y_ref, *, d_a):
    h = pl.program_id(0)
    n_heads = d_a // HEAD
    n_rkv = 3 * d_a
    sls = _pair_slices(d_a)

    @pl.when(h == 0)
    def _():
        xs = _token_shift(ps_ref[...], prev_ref[:, :n_rkv], mu_ref[:, :n_rkv])
        lo = _lora_out(_token_shift(pl_ref[...], prev_ref[:, n_rkv:], mu_ref[:, n_rkv:]), lora_ref[...])
        r, kf, v, log2_decay, kk, a = _pair_vectors([xs], [lo], par_ref, d_a, _block_ones())
        for pi, sl in enumerate(sls):
            vecs = (-kk[pi], jnp.exp2(log2_decay[pi]), kk[pi] * a[pi], kf[pi], r[pi], v[pi])
            for i, x in enumerate(vecs):
                vt_ref[i, sl, :] = x.T
            for i, x in enumerate((r[pi], kf[pi], v[pi])):
                rkv_ref[i, :, sl] = x

    base = pl.multiple_of(h * HEAD, HEAD)
    a_t, w_t, b_t, k_t, r_t = (vt_ref[i, pl.ds(base, HEAD), :] for i in range(5))

    def rows(g, carry):
        i0 = pl.multiple_of(g * STEP_ROWS, STEP_ROWS)
        row0 = pl.multiple_of(base + i0, STEP_ROWS)
        v_rows = vt_ref[5, pl.ds(row0, STEP_ROWS), :]
        ys = []
        for ii in range(STEP_ROWS):
            s = s_ref[0, i0 + ii]
            sa = jnp.sum(s * a_t, axis=0, keepdims=True)
            s_new = s * w_t + sa * b_t + v_rows[ii:ii + 1, :] * k_t
            so_ref[0, i0 + ii] = s_new
            ys.append(jnp.sum(s_new * r_t, axis=0, keepdims=True))
        y_ref[pl.ds(row0, STEP_ROWS), :] = jnp.concatenate(ys, axis=0)
        return carry

    lax.fori_loop(0, HEAD // STEP_ROWS, rows, 0)

    @pl.when(h == n_heads - 1)
    def _():
        y = [y_ref[sl, :].T for sl in sls]
        r, kf, v = ([rkv_ref[i, :, sl] for sl in sls] for i in range(3))
        o = _head_norm_bonus(y, r, kf, v, par_ref, d_a, _block_ones())
        for oi, sl in zip(o, sls):
            oa_ref[:, sl] = (oi * sga_ref[:, sl]).astype(oa_ref.dtype)


def _wkv_sample(p_rkv, p_lora, shift_prev, gates, mu, par, lora_w, state_t, d_a, row0):
    batch, shift_w = shift_prev.shape
    n_heads = d_a // HEAD
    assert row0 % batch == 0 and state_t.shape == (n_heads, HEAD, HEAD, batch)
    blk0 = row0 // batch
    const = lambda h: (0, 0)
    kern = functools.partial(_wkv_step_kernel, d_a=d_a)
    return pl.pallas_call(
        kern,
        out_shape=(jax.ShapeDtypeStruct((batch, d_a), BF16),
                   jax.ShapeDtypeStruct(state_t.shape, F32)),
        grid=(n_heads,),
        in_specs=[pl.BlockSpec((batch, 3 * d_a), lambda h: (blk0, 0)),
                  pl.BlockSpec((batch, 2 * LORA), lambda h: (blk0, 0)),
                  pl.BlockSpec((batch, shift_w), const),
                  pl.BlockSpec((batch, d_a), lambda h: (blk0, 0)),
                  pl.BlockSpec((1, shift_w), const),
                  pl.BlockSpec((8, d_a), const),
                  pl.BlockSpec((2 * LORA, 2 * d_a), const),
                  pl.BlockSpec((1, HEAD, HEAD, batch), lambda h: (h, 0, 0, 0))],
        out_specs=(pl.BlockSpec((batch, d_a), const),
                   pl.BlockSpec((1, HEAD, HEAD, batch), lambda h: (h, 0, 0, 0))),
        scratch_shapes=[pltpu.VMEM((6, d_a, batch), F32), pltpu.VMEM((3, batch, d_a), F32),
                        pltpu.VMEM((d_a, batch), F32)],
        compiler_params=_params(1),
        name="wkv_step",
    )(p_rkv, p_lora, shift_prev, gates, mu, par, lora_w, state_t)


CONV_PAD = 32
CONV_STRIP = 16


def _conv_prompt_kernel(u_ref, w_ref, b_ref, o_ref, buf_ref, c_ref, *, tt, taps):
    t = pl.program_id(1)
    sub = buf_ref.shape[1]
    lanes = [slice(s * LANES, (s + 1) * LANES) for s in range(sub)]

    @pl.when(t == 0)
    def _():
        buf_ref[0:CONV_PAD] = jnp.zeros((CONV_PAD,) + buf_ref.shape[1:], F32)

    u = u_ref[...]
    buf_ref[CONV_PAD:CONV_PAD + tt] = jnp.swapaxes(jnp.stack([u[:, ls] for ls in lanes], axis=0), 0, 1)
    off = CONV_PAD - (taps - 1)
    w = [w_ref[k] for k in range(taps)]
    bias = b_ref[0]

    def strip(s, carry):
        t0 = s * CONV_STRIP
        acc = [bias] * CONV_STRIP
        for i in range(CONV_STRIP + taps - 1):
            x = buf_ref[off + t0 + i]
            for j in range(CONV_STRIP):
                if 0 <= i - j < taps:
                    acc[j] = acc[j] + w[i - j] * x
        for j in range(CONV_STRIP):
            c_ref[t0 + j] = acc[j]
        return carry

    lax.fori_loop(0, tt // CONV_STRIP, strip, 0, unroll=4)
    buf_ref[0:CONV_PAD] = buf_ref[tt:tt + CONV_PAD]
    c = jnp.swapaxes(c_ref[...], 0, 1)
    for s, ls in enumerate(lanes):
        o_ref[:, ls] = c[s]


def _conv_prompt(u, conv_w, conv_b, batch, seq, tt):
    d_b = u.shape[1]
    taps = conv_w.shape[0]
    nt = seq // tt
    sub = d_b // LANES
    assert taps - 1 <= CONV_PAD and tt % CONV_STRIP == 0 and tt >= CONV_PAD
    kern = functools.partial(_conv_prompt_kernel, tt=tt, taps=taps)
    return pl.pallas_call(
        kern,
        out_shape=jax.ShapeDtypeStruct((batch * seq, d_b), F32),
        grid=(batch, nt),
        in_specs=[pl.BlockSpec((tt, d_b), lambda b, t: (b * nt + t, 0)),
                  pl.BlockSpec((taps, sub, LANES), lambda b, t: (0, 0, 0)),
                  pl.BlockSpec((1, sub, LANES), lambda b, t: (0, 0, 0))],
        out_specs=pl.BlockSpec((tt, d_b), lambda b, t: (b * nt + t, 0)),
        scratch_shapes=[pltpu.VMEM((tt + CONV_PAD, sub, LANES), F32), pltpu.VMEM((tt, sub, LANES), F32)],
        compiler_params=_params(2),
        name="conv_prompt",
    )(u, conv_w.reshape(taps, sub, LANES), conv_b.reshape(1, sub, LANES))


def _conv_step_kernel(u_ref, prev_ref, w_ref, b_ref, o_ref, hist_ref, *, taps):
    u = u_ref[...]
    c = b_ref[...] + w_ref[taps - 1:taps, :] * u
    for k in range(taps - 1):
        c = c + w_ref[k:k + 1, :] * prev_ref[k]
    o_ref[...] = c
    hist_ref[0:taps - 2] = prev_ref[1:taps - 1]
    hist_ref[taps - 2] = u


def _conv_sample(u, conv_prev_t, conv_w, conv_b, bb, row0):
    batch = conv_prev_t.shape[1]
    d_b = u.shape[1]
    taps = conv_w.shape[0]
    assert row0 % bb == 0 and batch % bb == 0
    blk0 = row0 // bb
    kern = functools.partial(_conv_step_kernel, taps=taps)
    return pl.pallas_call(
        kern,
        out_shape=(jax.ShapeDtypeStruct((batch, d_b), F32),
                   jax.ShapeDtypeStruct(conv_prev_t.shape, F32)),
        grid=(batch // bb,),
        in_specs=[pl.BlockSpec((bb, d_b), lambda i: (blk0 + i, 0)),
                  pl.BlockSpec((taps - 1, bb, d_b), lambda i: (0, i, 0)),
                  pl.BlockSpec((taps, d_b), lambda i: (0, 0)),
                  pl.BlockSpec((1, d_b), lambda i: (0, 0))],
        out_specs=(pl.BlockSpec((bb, d_b), lambda i: (i, 0)),
                   pl.BlockSpec((taps - 1, bb, d_b), lambda i: (0, i, 0))),
        compiler_params=_params(1),
        name="conv_step",
    )(u, conv_prev_t, conv_w, conv_b.reshape(1, d_b))


def _tail_kernel(oa_ref, c_ref, gb_ref, sga_ref, sgb_ref, x_ref, p_ref, cln_ref, wa_ref, wb_ref, wout_ref,
                 wpg_ref, wple_ref, fg_ref, o_ref):
    c = c_ref[...]
    mean = jnp.mean(c, axis=-1, keepdims=True)
    dc = c - mean
    var = jnp.mean(dc * dc, axis=-1, keepdims=True)
    cf = dc * lax.rsqrt(var + LN_EPS) * cln_ref[0:1, :] + cln_ref[1:2, :]
    cb = (cf * _sigmoid(cf) * gb_ref[...]).astype(BF16)
    m = sga_ref[...] * _dot(oa_ref[...], wa_ref[...]) + sgb_ref[...] * _dot(cb, wb_ref[...])
    h = x_ref[...] + _dot(m.astype(BF16), wout_ref[...])
    gate = _sigmoid(_dot(h.astype(BF16), wpg_ref[...]))
    h = h + gate * _dot(p_ref[...].astype(BF16), wple_ref[...])
    ms = jnp.mean(h * h, axis=-1, keepdims=True)
    o_ref[...] = h * lax.rsqrt(ms + RMS_EPS) * fg_ref[...]


def _tail(oa, c, gates, x, p, cln, wa, wb, wout, wpg, wple, fg, tm, gate_b_block, merge_block, row0=0):
    m, d = x.shape
    d_a = oa.shape[1]
    d_b = c.shape[1]
    ple = p.shape[1]
    blk0 = row0 // tm
    row = lambda i: (i, 0)
    const = lambda i: (0, 0)
    resident = lambda shape: pl.BlockSpec(shape, const, pipeline_mode=pl.Buffered(1))
    return pl.pallas_call(
        _tail_kernel,
        out_shape=jax.ShapeDtypeStruct((m, d), F32),
        grid=(m // tm,),
        in_specs=[pl.BlockSpec((tm, d_a), row), pl.BlockSpec((tm, d_b), row),
                  pl.BlockSpec((tm, d_b), lambda i: (blk0 + i, gate_b_block)),
                  pl.BlockSpec((tm, d), lambda i: (blk0 + i, merge_block)),
                  pl.BlockSpec((tm, d), lambda i: (blk0 + i, merge_block + 1)),
                  pl.BlockSpec((tm, d), row), pl.BlockSpec((tm, ple), row),
                  resident((2, d_b)),
                  resident((d_a, d)), resident((d_b, d)), resident((d, d)), resident((d, d)),
                  resident((ple, d)), resident((1, d))],
        out_specs=pl.BlockSpec((tm, d), row),
        compiler_params=_params(1),
        name="tail",
    )(oa, c, gates, gates, gates, x, p, cln, wa, wb, wout, wpg, wple, fg)


def _largest_tile(n, cap, align):
    t = min(n, cap)
    while n % t or t % align:
        t -= 1
    return t


def kernel(x_prompt, x_sample, state_shift, state_wkv, state_conv, p_prompt, p_sample, norm_g, w_in,
           shift_mu, w0, w_lora_b, a0, a_lora_b, k_k, k_a, r_k, lnx_g, lnx_b, w_proj_a, conv_w, conv_b,
           cln_g, cln_b, w_proj_b, w_out, w_ple, w_ple_gate, final_g):
    depth = w_in.shape[0]
    batch, seq, d = x_prompt.shape
    dec_batch, dec_seq, _ = x_sample.shape
    d_a = w_proj_a.shape[1]
    d_b = w_proj_b.shape[1]
    shift_w = shift_mu.shape[1]
    n_heads = d_a // HEAD
    n_pairs = d_a // PAIR
    taps = conv_w.shape[1]
    assert depth == 1 and dec_seq == 1 and d_a == d_b and 2 * d_a == d
    assert shift_w == 3 * d_a + 2 * LORA and d_a % PAIR == 0 and seq % CHUNK == 0

    o1 = shift_w
    o2 = o1 + d_a
    o3 = o2 + 2 * d_b
    o4 = o3 + d_b
    w = w_in[0]
    lora_w = jnp.zeros((2 * LORA, 2 * d_a), F32)
    lora_w = lora_w.at[:LORA, :d_a].set(w_lora_b[0]).at[LORA:, d_a:].set(a_lora_b[0]).astype(BF16)
    zeros_a = jnp.zeros((d_a,), F32)
    par = jnp.stack([w0[0], a0[0], k_k[0], k_a[0], r_k[0].reshape(d_a), lnx_g[0], lnx_b[0], zeros_a])
    cln = jnp.stack([cln_g[0], cln_b[0]])
    mu = shift_mu
    g_in = norm_g[0].reshape(1, d)
    fg = final_g.reshape(1, d)
    gate_b_block = d_a // d_b
    merge_block = (d_a + d_b) // d

    m_p = batch * seq
    m_all = m_p + dec_batch
    x2 = x_prompt.reshape(m_p, d)
    xs2 = x_sample.reshape(dec_batch, d)
    n_rkv = 3 * d_a
    xn, p_lora = _rmsnorm_bf16(x2, xs2, g_in, w, n_rkv, 2 * LORA, _largest_tile(m_p, 1024, SUBLANES))
    tm = _largest_tile(m_all, 1100, BF16_SUBLANES)
    tn_shift = _largest_tile(n_rkv, 1024, LANES)
    p_rkv = _project(xn, w, lambda j: j * tn_shift, n_rkv, tm, tn_shift)
    gate_cols = lambda j: jnp.where(j == 0, o1, o3 + (j - 1) * d_a)
    gates = _project(xn, w, gate_cols, d_a + d_b + 2 * d, tm, d_a, n_silu_blocks=2)
    u, (wa, wb, wout, wpg, wple) = _project_glu(
        xn, w, o2, o2 + d_b, d_b, tm, _largest_tile(d_b, 512, LANES),
        casts=[w_proj_a[0], w_proj_b[0], w_out[0], w_ple_gate[0], w_ple[0]])

    oa, st = _wkv_prompt(p_rkv, p_lora, gates, mu, par, lora_w, batch, seq, d_a)
    c = _conv_prompt(u, conv_w[0], conv_b[0], batch, seq, _largest_tile(seq, 512, CONV_STRIP))
    y_prompt = _tail(oa, c, gates, x2, p_prompt[0].reshape(m_p, -1), cln, wa, wb, wout, wpg, wple, fg,
                     _largest_tile(m_p, 256, BF16_SUBLANES), gate_b_block, merge_block).reshape(batch, seq, d)
    last_rows = lambda a: jnp.concatenate([a[(b + 1) * seq - 1:(b + 1) * seq] for b in range(batch)])
    new_shift_p = jnp.concatenate([last_rows(p_rkv), last_rows(p_lora)], axis=1)[None]
    st = st.reshape(batch, n_pairs, 2, HEAD, 2, HEAD)
    new_wkv_p = jnp.stack([st[:, :, 0, :, 0, :], st[:, :, 1, :, 1, :]], axis=2)
    new_wkv_p = jnp.swapaxes(new_wkv_p.reshape(batch, n_heads, HEAD, HEAD), -1, -2)[None]
    new_conv_p = jnp.stack([u[(b + 1) * seq - (taps - 1):(b + 1) * seq] for b in range(batch)])[None]

    tm_s = _largest_tile(dec_batch, 256, BF16_SUBLANES)
    assert m_p % tm_s == 0
    oa_s, new_wkv_t = _wkv_sample(p_rkv, p_lora, state_shift[0], gates, mu, par, lora_w,
                                  jnp.transpose(state_wkv[0], (1, 2, 3, 0)), d_a, m_p)
    new_wkv_s = jnp.transpose(new_wkv_t, (3, 0, 1, 2))
    c_s, new_conv_t = _conv_sample(u, jnp.transpose(state_conv[0], (1, 0, 2)), conv_w[0], conv_b[0],
                                   _largest_tile(dec_batch, 32, SUBLANES), m_p)
    new_conv_s = jnp.transpose(new_conv_t, (1, 0, 2))
    y_sample = _tail(oa_s, c_s, gates, xs2, p_sample[0].reshape(dec_batch, -1), cln, wa, wb, wout, wpg,
                     wple, fg, tm_s, gate_b_block, merge_block, row0=m_p).reshape(dec_batch, 1, d)
    new_shift_s = jnp.concatenate([p_rkv[m_p:], p_lora[m_p:]], axis=1)[None]
    new_conv_s = new_conv_s[None]

    return (y_prompt, y_sample, new_shift_p, new_wkv_p, new_conv_p, new_shift_s, new_wkv_s[None],
            new_conv_s)
```

```python
import functools
import math

import jax
import jax.numpy as jnp
from jax import lax
from jax.experimental import pallas as pl
from jax.experimental.pallas import tpu as pltpu

F32 = jnp.float32
BF16 = jnp.bfloat16

LANES = 128
SUBLANES = 8
BF16_SUBLANES = 16

HEAD = 64
PAIR = 2 * HEAD
LORA = 64
CHUNK = 64
WKV_SEQS_PER_STEP = 4
WKV_CHUNKS_PER_STEP = 2
RMS_EPS = 1e-6
LN_EPS = 1e-5
GN_EPS = 64e-5
LOG2_DECAY_SCALE = math.exp(-0.5) / math.log(2.0)
VMEM_LIMIT = 56 * 1024 * 1024


def _params(n_axes, vmem=VMEM_LIMIT):
    return pltpu.CompilerParams(dimension_semantics=("arbitrary",) * n_axes, vmem_limit_bytes=vmem)


def _sigmoid(x):
    return 0.5 * jnp.tanh(0.5 * x) + 0.5


def _dot(a, b):
    return jnp.dot(a, b, preferred_element_type=F32)


def _dot_nt(a, b):
    return lax.dot_general(a, b, (((1,), (1,)), ((), ())), preferred_element_type=F32)


def _dot_tn(a, b):
    return lax.dot_general(a, b, (((0,), (0,)), ((), ())), preferred_element_type=F32)


def _rmsnorm_kernel(xp_ref, xs_ref, g_ref, w_ref, o_ref, ol_ref, wb_ref, *, n_prompt_blocks):
    i = pl.program_id(0)

    @pl.when(i == 0)
    def _():
        wb_ref[...] = w_ref[...].astype(BF16)

    def norm(x):
        ms = jnp.mean(x * x, axis=-1, keepdims=True)
        return (x * lax.rsqrt(ms + RMS_EPS) * g_ref[...]).astype(o_ref.dtype)

    @pl.when(i < n_prompt_blocks)
    def _():
        xn = norm(xp_ref[...])
        o_ref[...] = xn
        ol_ref[...] = _dot(xn, wb_ref[...])

    @pl.when(i == n_prompt_blocks)
    def _():
        xn = norm(xs_ref[...])
        o_ref[0:xs_ref.shape[0], :] = xn
        ol_ref[0:xs_ref.shape[0], :] = _dot(xn, wb_ref[...])


def _rmsnorm_bf16(x_prompt, x_sample, g, w, col, n_cols, tm):
    m_p, d = x_prompt.shape
    m_s = x_sample.shape[0]
    n_blocks = m_p // tm
    assert m_p % tm == 0 and m_s <= tm and col % LANES == 0
    kern = functools.partial(_rmsnorm_kernel, n_prompt_blocks=n_blocks)
    return pl.pallas_call(
        kern,
        out_shape=(jax.ShapeDtypeStruct((m_p + m_s, d), BF16),
                   jax.ShapeDtypeStruct((m_p + m_s, n_cols), F32)),
        grid=(n_blocks + 1,),
        in_specs=[pl.BlockSpec((tm, d), lambda i: (jnp.minimum(i, n_blocks - 1), 0)),
                  pl.BlockSpec((m_s, d), lambda i: (0, 0)),
                  pl.BlockSpec((1, d), lambda i: (0, 0)),
                  pl.BlockSpec((pl.Element(d), pl.Element(n_cols)), lambda i: (0, col))],
        out_specs=(pl.BlockSpec((tm, d), lambda i: (i, 0)),
                   pl.BlockSpec((tm, n_cols), lambda i: (i, 0))),
        scratch_shapes=[pltpu.VMEM((d, n_cols), BF16)],
        compiler_params=_params(1),
        name="rmsnorm",
    )(x_prompt, x_sample, g, w)


def _proj_kernel(x_ref, w_ref, o_ref, wb_ref, *, n_silu_blocks):
    @pl.when(pl.program_id(1) == 0)
    def _():
        wb_ref[...] = w_ref[...].astype(BF16)

    y = _dot(x_ref[...], wb_ref[...])
    if n_silu_blocks is None:
        o_ref[...] = y
    else:
        s = _sigmoid(y)
        o_ref[...] = jnp.where(pl.program_id(0) < n_silu_blocks, y * s, s)


def _project(xn, w, col_start, n_out, tm, tn, n_silu_blocks=None):
    m, d = xn.shape
    kern = functools.partial(_proj_kernel, n_silu_blocks=n_silu_blocks)
    return pl.pallas_call(
        kern,
        out_shape=jax.ShapeDtypeStruct((m, n_out), F32),
        grid=(n_out // tn, m // tm),
        in_specs=[pl.BlockSpec((tm, d), lambda j, i: (i, 0)),
                  pl.BlockSpec((pl.Element(d), pl.Element(tn)),
                               lambda j, i: (0, pl.multiple_of(col_start(j), LANES)))],
        out_specs=pl.BlockSpec((tm, tn), lambda j, i: (i, j)),
        scratch_shapes=[pltpu.VMEM((d, tn), BF16)],
        compiler_params=_params(2),
        name="in_proj",
    )(xn, w)


def _glu_kernel(*refs, n_casts):
    x_ref, wa_ref, wb_ref = refs[:3]
    cast_in = refs[3:3 + n_casts]
    o_ref = refs[3 + n_casts]
    cast_out = refs[4 + n_casts:4 + 2 * n_casts]
    wab_ref, wbb_ref = refs[4 + 2 * n_casts:]

    @pl.when(pl.program_id(1) == 0)
    def _():
        wab_ref[...] = wa_ref[...].astype(BF16)
        wbb_ref[...] = wb_ref[...].astype(BF16)

    x = x_ref[...]
    o_ref[...] = _dot(x, wab_ref[...]) * _sigmoid(_dot(x, wbb_ref[...]))
    for src, dst in zip(cast_in, cast_out):
        dst[...] = src[...].astype(BF16)


def _project_glu(xn, w, col_a, col_b, n_out, tm, tn, casts):
    m, d = xn.shape
    nj, ni = n_out // tn, m // tm
    steps = nj * ni
    for a in casts:
        assert a.shape[0] % (BF16_SUBLANES * steps) == 0, (a.shape, steps)
    step_rows = lambda a: pl.BlockSpec((a.shape[0] // steps, a.shape[1]), lambda j, i: (j * ni + i, 0))
    w_cols = lambda col: pl.BlockSpec((pl.Element(d), pl.Element(tn)),
                                      lambda j, i: (0, pl.multiple_of(col + j * tn, LANES)))
    kern = functools.partial(_glu_kernel, n_casts=len(casts))
    out = pl.pallas_call(
        kern,
        out_shape=[jax.ShapeDtypeStruct((m, n_out), F32)]
        + [jax.ShapeDtypeStruct(a.shape, BF16) for a in casts],
        grid=(nj, ni),
        in_specs=[pl.BlockSpec((tm, d), lambda j, i: (i, 0)), w_cols(col_a), w_cols(col_b)]
        + [step_rows(a) for a in casts],
        out_specs=[pl.BlockSpec((tm, tn), lambda j, i: (i, j))] + [step_rows(a) for a in casts],
        scratch_shapes=[pltpu.VMEM((d, tn), BF16), pltpu.VMEM((d, tn), BF16)],
        compiler_params=_params(2),
        name="in_proj_glu",
    )(xn, w, w, *casts)
    return out[0], out[1:]


def _head_sum(x, block_ones, split=False):
    hi = x.astype(BF16)
    if not split:
        return _dot(hi, block_ones)
    lo = (x - hi.astype(F32)).astype(BF16)
    return _dot(hi, block_ones) + _dot(lo, block_ones)


def _head_sums(xs, block_ones, split=False):
    rows = xs[0].shape[0]
    s = _head_sum(jnp.concatenate(xs, axis=0), block_ones, split)
    return [s[i * rows:(i + 1) * rows] for i in range(len(xs))]


def _block_ones(value=1.0):
    ri = lax.broadcasted_iota(jnp.int32, (PAIR, PAIR), 0)
    ci = lax.broadcasted_iota(jnp.int32, (PAIR, PAIR), 1)
    return jnp.where((ri < HEAD) == (ci < HEAD), value, 0.0).astype(BF16)


def _token_shift(p, prev, mu):
    return p + mu * (prev - p)


def _lora_out(z, lora_w):
    lane = lax.broadcasted_iota(jnp.int32, z.shape, 1)
    z = jnp.where(lane < LORA, jnp.tanh(z), z)
    return _dot(z.astype(BF16), lora_w)


def _pair_slices(d_a):
    return [slice(i * PAIR, (i + 1) * PAIR) for i in range(d_a // PAIR)]


def _pair_vectors(xs_list, lo_list, par_ref, d_a, block_ones):
    sls = _pair_slices(d_a)
    shifted = lambda sl, off: slice(off + sl.start, off + sl.stop)
    items = [(xs, lo, sl) for xs, lo in zip(xs_list, lo_list) for sl in sls]
    r = [xs[:, sl] for xs, _, sl in items]
    k = [xs[:, shifted(sl, d_a)] for xs, _, sl in items]
    v = [xs[:, shifted(sl, 2 * d_a)] for xs, _, sl in items]
    kkr = [ki * par_ref[2:3, sl] for ki, (_, _, sl) in zip(k, items)]
    ss = _head_sums([x * x for x in kkr], block_ones)
    kk = [x * jnp.minimum(lax.rsqrt(s), 1e12) for x, s in zip(kkr, ss)]
    log2_decay = [-LOG2_DECAY_SCALE * _sigmoid(par_ref[0:1, sl] + lo[:, sl]) for _, lo, sl in items]
    a = [_sigmoid(par_ref[1:2, sl] + lo[:, shifted(sl, d_a)]) for _, lo, sl in items]
    kf = [ki * (1.0 + (ai - 1.0) * par_ref[3:4, sl]) for ki, ai, (_, _, sl) in zip(k, a, items)]
    return r, kf, v, log2_decay, kk, a


def _head_norm_bonus(y, r, kf, v, par_ref, d_a, block_ones):
    sls = _pair_slices(d_a) * (len(y) // (d_a // PAIR))
    block_mean = _block_ones(1.0 / HEAD)
    mean = _head_sums(y, block_mean, split=True)
    d = [x - m for x, m in zip(y, mean)]
    var = _head_sums([x * x for x in d], block_mean)
    rk = _head_sums([ri * ki * par_ref[4:5, sl] for ri, ki, sl in zip(r, kf, sls)], block_ones)
    return [x * lax.rsqrt(vr + GN_EPS) * par_ref[5:6, sl] + par_ref[6:7, sl] + s * vi
            for x, vr, s, vi, sl in zip(d, var, rk, v, sls)]


def _stack_heads(x, lane_lo):
    xb = x.astype(BF16)
    zero = jnp.zeros_like(xb)
    return jnp.concatenate([jnp.where(lane_lo, xb, zero), jnp.where(lane_lo, zero, xb)], axis=0)


def _cumsum_rows(xs):
    n = xs[0].shape[0]
    tri = (lax.broadcasted_iota(jnp.int32, (n, n), 1) <= lax.broadcasted_iota(jnp.int32, (n, n), 0))
    tri = jnp.where(tri, 1.0, 0.0).astype(BF16)
    x = jnp.concatenate(xs, axis=1)
    hi = x.astype(BF16)
    lo = (x - hi.astype(F32)).astype(BF16)
    s = _dot(tri, hi) + _dot(tri, lo)
    return [s[:, i * PAIR:(i + 1) * PAIR] for i in range(len(xs))]


def _wkv_chunk_kernel(*refs, d_a, nb, nc):
    ps_refs, pl_refs, sga_refs = refs[:nb], refs[nb:2 * nb], refs[2 * nb:3 * nb]
    mu_ref, par_ref, lora_ref, oa_ref, st_ref, carry_ref = refs[3 * nb:]
    c = pl.program_id(1)
    n_pairs = d_a // PAIR
    C = CHUNK
    rows = nc * C
    n_rkv = 3 * d_a

    @pl.when(c == 0)
    def _():
        carry_ref[...] = jnp.zeros_like(carry_ref)
        st_ref[...] = jnp.zeros_like(st_ref)

    def shifted(p, prev_row, mu):
        row = lax.broadcasted_iota(jnp.int32, p.shape, 0)
        return _token_shift(p, jnp.where(row == 0, prev_row, pltpu.roll(p, 1, axis=0)), mu)

    xs_seq, lo_seq = [], []
    for bi in range(nb):
        p, pz = ps_refs[bi][...], pl_refs[bi][...]
        xs_seq.append(shifted(p, carry_ref[bi, :, :n_rkv], mu_ref[:, :n_rkv]))
        lo_seq.append(_lora_out(shifted(pz, carry_ref[bi, :, n_rkv:], mu_ref[:, n_rkv:]), lora_ref[...]))
        carry_ref[bi, :, :n_rkv] = p[rows - 1:rows, :]
        carry_ref[bi, :, n_rkv:] = pz[rows - 1:rows, :]
    chunk_rows = [slice(j * C, (j + 1) * C) for j in range(nc)]
    xs = [x[cr] for cr in chunk_rows for x in xs_seq]
    lo = [x[cr] for cr in chunk_rows for x in lo_seq]

    block_ones = _block_ones()
    lane_lo = lax.broadcasted_iota(jnp.int32, (C, PAIR), 1) < HEAD
    ti = lax.broadcasted_iota(jnp.int32, (C, PAIR), 0)
    si = lax.broadcasted_iota(jnp.int32, (C, PAIR), 1) % C
    strict_lower = si < ti
    lower2 = (lax.broadcasted_iota(jnp.int32, (C, 2 * PAIR), 1) % C
              <= lax.broadcasted_iota(jnp.int32, (C, 2 * PAIR), 0))
    eye = jnp.where(si == ti, 1.0, 0.0)
    cat0 = lambda x, y: jnp.concatenate([x, y], axis=0)

    r, kf, v, log2_decay, kk, a = _pair_vectors(xs, lo, par_ref, d_a, block_ones)
    cl = _cumsum_rows(log2_decay)
    cend = [x[C - 1:C, :] for x in cl]
    e_neg = [jnp.exp2(-x) for x in cl]
    kka = [x * y for x, y in zip(kk, a)]
    stack = lambda xs_: [_stack_heads(x, lane_lo) for x in xs_]
    a_p = [(-x * jnp.exp2(c_ - ld)).astype(BF16) for x, c_, ld in zip(kk, cl, log2_decay)]
    r_p = [(x * jnp.exp2(c_)).astype(BF16) for x, c_ in zip(r, cl)]
    a_s = stack(a_p)
    bk_s = [cat0(x, y) for x, y in zip(stack([x * e for x, e in zip(kka, e_neg)]),
                                       stack([x * e for x, e in zip(kf, e_neg)]))]
    v_s = stack(v)
    pend_col = [jnp.transpose(jnp.broadcast_to(jnp.exp2(ce), (PAIR, PAIR))) for ce in cend]

    m1 = [_dot_nt(cat0(x, y), z) for x, y, z in zip(a_p, r_p, bk_s)]
    a_ab = [jnp.where(strict_lower, m[:C, :PAIR], 0.0) for m in m1]
    keep = lambda mask, x: jnp.where(mask, x, jnp.zeros_like(x))
    a_ak = [keep(strict_lower, m[:C, PAIR:].astype(BF16)) for m in m1]
    a_rbk = [keep(lower2, m[C:].astype(BF16)) for m in m1]
    g = [_dot(x, y) for x, y in zip(a_ak, v_s)]

    apow = [_dot(x.astype(BF16), _stack_heads(x, lane_lo)) for x in a_ab]
    tinv = [eye + x for x in a_ab]
    n_levels = CHUNK.bit_length() - 1
    for lvl in range(1, n_levels):
        rhs = stack(apow)
        if lvl < n_levels - 1:
            both = [_dot(cat0(x.astype(BF16), t.astype(BF16)), y) for x, t, y in zip(apow, tinv, rhs)]
            apow = [x[:C] for x in both]
            tinv = [t + x[C:] for t, x in zip(tinv, both)]
        else:
            tinv = [t + _dot(t.astype(BF16), y) for t, y in zip(tinv, rhs)]

    g_s = stack(g)
    wx = [_dot(t.astype(BF16), jnp.concatenate([x, y], axis=1)) for t, x, y in zip(tinv, a_s, g_s)]
    seq_pair = [(bi, pi) for bi in range(nb) for pi in range(n_pairs)]
    n_sp = len(seq_pair)
    st = [st_ref[bi, pi] for bi, pi in seq_pair]
    y = []
    for j in range(nc):
        this = slice(j * n_sp, (j + 1) * n_sp)
        uy = [_dot(cat0(w[:, :PAIR].astype(BF16), x), s_.astype(BF16))
              for w, x, s_ in zip(wx[this], r_p[this], st)]
        uv_s = [cat0(_stack_heads(x[:C] + w[:, PAIR:], lane_lo), z)
                for x, w, z in zip(uy, wx[this], v_s[this])]
        st_upd = [_dot_tn(x, z) for x, z in zip(bk_s[this], uv_s)]
        y_uv = [_dot(x, z) for x, z in zip(a_rbk[this], uv_s)]
        st = [p_ * (s_ + d_) for p_, s_, d_ in zip(pend_col[this], st, st_upd)]
        y += [x[C:] + z for x, z in zip(uy, y_uv)]
    for s_, (bi, pi) in zip(st, seq_pair):
        st_ref[bi, pi] = s_
    o = _head_norm_bonus(y, r, kf, v, par_ref, d_a, block_ones)
    sls = _pair_slices(d_a)
    for idx, oi in enumerate(o):
        j, (bi, pi) = idx // n_sp, seq_pair[idx % n_sp]
        oa_ref[bi, chunk_rows[j], sls[pi]] = (oi * sga_refs[bi][chunk_rows[j], sls[pi]]).astype(oa_ref.dtype)


def _wkv_prompt(p_rkv, p_lora, gates, mu, par, lora_w, batch, seq, d_a):
    n_chunks = seq // CHUNK
    shift_w = mu.shape[1]
    n_pairs = d_a // PAIR
    nb = WKV_SEQS_PER_STEP if batch % WKV_SEQS_PER_STEP == 0 else 1
    nc = WKV_CHUNKS_PER_STEP if n_chunks % WKV_CHUNKS_PER_STEP == 0 else 1
    steps = n_chunks // nc
    rows = nc * CHUNK
    kern = functools.partial(_wkv_chunk_kernel, d_a=d_a, nb=nb, nc=nc)
    seq_rows = lambda bi: (lambda b, c: ((b * nb + bi) * steps + c, 0))
    oa, st = pl.pallas_call(
        kern,
        out_shape=(jax.ShapeDtypeStruct((batch, seq, d_a), BF16),
                   jax.ShapeDtypeStruct((batch, n_pairs, PAIR, PAIR), F32)),
        grid=(batch // nb, steps),
        in_specs=([pl.BlockSpec((rows, 3 * d_a), seq_rows(bi)) for bi in range(nb)]
                  + [pl.BlockSpec((rows, 2 * LORA), seq_rows(bi)) for bi in range(nb)]
                  + [pl.BlockSpec((rows, d_a), seq_rows(bi)) for bi in range(nb)]
                  + [pl.BlockSpec((1, shift_w), lambda b, c: (0, 0)),
                     pl.BlockSpec((8, d_a), lambda b, c: (0, 0)),
                     pl.BlockSpec((2 * LORA, 2 * d_a), lambda b, c: (0, 0))]),
        out_specs=(pl.BlockSpec((nb, rows, d_a), lambda b, c: (b, c, 0)),
                   pl.BlockSpec((nb, n_pairs, PAIR, PAIR), lambda b, c: (b, 0, 0, 0))),
        scratch_shapes=[pltpu.VMEM((nb, 1, shift_w), F32)],
        compiler_params=_params(2),
        name="wkv_chunked",
    )(*([p_rkv] * nb + [p_lora] * nb + [gates] * nb), mu, par, lora_w)
    return oa.reshape(batch * seq, d_a), st


STEP_ROWS = 8


def _wkv_step_kernel(ps_ref, pl_ref, prev_ref, sga_ref, mu_ref, par_ref, lora_ref, s_ref, oa_ref, so_ref,
                     vt_ref, rkv_ref, y_ref, *, d_a):
    h = pl.program_id(0)
    n_heads = d_a // HEAD
    n_rkv = 3 * d_a
    sls = _pair_slices(d_a)

    @pl.when(h == 0)
    def _():
        xs = _token_shift(ps_ref[...], prev_ref[:, :n_rkv], mu_ref[:, :n_rkv])
        lo = _lora_out(_token_shift(pl_ref[...], prev_ref[:, n_rkv:], mu_ref[:, n_rkv:]), lora_ref[...])
        r, kf, v, log2_decay, kk, a = _pair_vectors([xs], [lo], par_ref, d_a, _block_ones())
        for pi, sl in enumerate(sls):
            vecs = (-kk[pi], jnp.exp2(log2_decay[pi]), kk[pi] * a[pi], kf[pi], r[pi], v[pi])
            for i, x in enumerate(vecs):
                vt_ref[i, sl, :] = x.T
            for i, x in enumerate((r[pi], kf[pi], v[pi])):
                rkv_ref[i, :, sl] = x

    base = pl.multiple_of(h * HEAD, HEAD)
    a_t, w_t, b_t, k_t, r_t = (vt_ref[i, pl.ds(base, HEAD), :] for i in range(5))

    def rows(g, carry):
        i0 = pl.multiple_of(g * STEP_ROWS, STEP_ROWS)
        row0 = pl.multiple_of(base + i0, STEP_ROWS)
        v_rows = vt_ref[5, pl.ds(row0, STEP_ROWS), :]
        ys = []
        for ii in range(STEP_ROWS):
            s = s_ref[0, i0 + ii]
            sa = jnp.sum(s * a_t, axis=0, keepdims=True)
            s_new = s * w_t + sa * b_t + v_rows[ii:ii + 1, :] * k_t
            so_ref[0, i0 + ii] = s_new
            ys.append(jnp.sum(s_new * r_t, axis=0, keepdims=True))
        y_ref[pl.ds(row0, STEP_ROWS), :] = jnp.concatenate(ys, axis=0)
        return carry

    lax.fori_loop(0, HEAD // STEP_ROWS, rows, 0)

    @pl.when(h == n_heads - 1)
    def _():
        y = [y_ref[sl, :].T for sl in sls]
        r, kf, v = ([rkv_ref[i, :, sl] for sl in sls] for i in range(3))
        o = _head_norm_bonus(y, r, kf, v, par_ref, d_a, _block_ones())
        for oi, sl in zip(o, sls):
            oa_ref[:, sl] = (oi * sga_ref[:, sl]).astype(oa_ref.dtype)


def _wkv_sample(p_rkv, p_lora, shift_prev, gates, mu, par, lora_w, state_t, d_a, row0):
    batch, shift_w = shift_prev.shape
    n_heads = d_a // HEAD
    assert row0 % batch == 0 and state_t.shape == (n_heads, HEAD, HEAD, batch)
    blk0 = row0 // batch
    const = lambda h: (0, 0)
    kern = functools.partial(_wkv_step_kernel, d_a=d_a)
    return pl.pallas_call(
        kern,
        out_shape=(jax.ShapeDtypeStruct((batch, d_a), BF16),
                   jax.ShapeDtypeStruct(state_t.shape, F32)),
        grid=(n_heads,),
        in_specs=[pl.BlockSpec((batch, 3 * d_a), lambda h: (blk0, 0)),
                  pl.BlockSpec((batch, 2 * LORA), lambda h: (blk0, 0)),
                  pl.BlockSpec((batch, shift_w), const),
                  pl.BlockSpec((batch, d_a), lambda h: (blk0, 0)),
                  pl.BlockSpec((1, shift_w), const),
                  pl.BlockSpec((8, d_a), const),
                  pl.BlockSpec((2 * LORA, 2 * d_a), const),
                  pl.BlockSpec((1, HEAD, HEAD, batch), lambda h: (h, 0, 0, 0))],
        out_specs=(pl.BlockSpec((batch, d_a), const),
                   pl.BlockSpec((1, HEAD, HEAD, batch), lambda h: (h, 0, 0, 0))),
        scratch_shapes=[pltpu.VMEM((6, d_a, batch), F32), pltpu.VMEM((3, batch, d_a), F32),
                        pltpu.VMEM((d_a, batch), F32)],
        compiler_params=_params(1),
        name="wkv_step",
    )(p_rkv, p_lora, shift_prev, gates, mu, par, lora_w, state_t)


CONV_PAD = 32
CONV_STRIP = 16


def _conv_prompt_kernel(u_ref, w_ref, b_ref, o_ref, buf_ref, c_ref, *, tt, taps):
    t = pl.program_id(1)
    sub = buf_ref.shape[1]
    lanes = [slice(s * LANES, (s + 1) * LANES) for s in range(sub)]

    @pl.when(t == 0)
    def _():
        buf_ref[0:CONV_PAD] = jnp.zeros((CONV_PAD,) + buf_ref.shape[1:], F32)

    u = u_ref[...]
    buf_ref[CONV_PAD:CONV_PAD + tt] = jnp.swapaxes(jnp.stack([u[:, ls] for ls in lanes], axis=0), 0, 1)
    off = CONV_PAD - (taps - 1)
    w = [w_ref[k] for k in range(taps)]
    bias = b_ref[0]

    def strip(s, carry):
        t0 = s * CONV_STRIP
        acc = [bias] * CONV_STRIP
        for i in range(CONV_STRIP + taps - 1):
            x = buf_ref[off + t0 + i]
            for j in range(CONV_STRIP):
                if 0 <= i - j < taps:
                    acc[j] = acc[j] + w[i - j] * x
        for j in range(CONV_STRIP):
            c_ref[t0 + j] = acc[j]
        return carry

    lax.fori_loop(0, tt // CONV_STRIP, strip, 0, unroll=4)
    buf_ref[0:CONV_PAD] = buf_ref[tt:tt + CONV_PAD]
    c = jnp.swapaxes(c_ref[...], 0, 1)
    for s, ls in enumerate(lanes):
        o_ref[:, ls] = c[s]


def _conv_prompt(u, conv_w, conv_b, batch, seq, tt):
    d_b = u.shape[1]
    taps = conv_w.shape[0]
    nt = seq // tt
    sub = d_b // LANES
    assert taps - 1 <= CONV_PAD and tt % CONV_STRIP == 0 and tt >= CONV_PAD
    kern = functools.partial(_conv_prompt_kernel, tt=tt, taps=taps)
    return pl.pallas_call(
        kern,
        out_shape=jax.ShapeDtypeStruct((batch * seq, d_b), F32),
        grid=(batch, nt),
        in_specs=[pl.BlockSpec((tt, d_b), lambda b, t: (b * nt + t, 0)),
                  pl.BlockSpec((taps, sub, LANES), lambda b, t: (0, 0, 0)),
                  pl.BlockSpec((1, sub, LANES), lambda b, t: (0, 0, 0))],
        out_specs=pl.BlockSpec((tt, d_b), lambda b, t: (b * nt + t, 0)),
        scratch_shapes=[pltpu.VMEM((tt + CONV_PAD, sub, LANES), F32), pltpu.VMEM((tt, sub, LANES), F32)],
        compiler_params=_params(2),
        name="conv_prompt",
    )(u, conv_w.reshape(taps, sub, LANES), conv_b.reshape(1, sub, LANES))


def _conv_step_kernel(u_ref, prev_ref, w_ref, b_ref, o_ref, hist_ref, *, taps):
    u = u_ref[...]
    c = b_ref[...] + w_ref[taps - 1:taps, :] * u
    for k in range(taps - 1):
        c = c + w_ref[k:k + 1, :] * prev_ref[k]
    o_ref[...] = c
    hist_ref[0:taps - 2] = prev_ref[1:taps - 1]
    hist_ref[taps - 2] = u


def _conv_sample(u, conv_prev_t, conv_w, conv_b, bb, row0):
    batch = conv_prev_t.shape[1]
    d_b = u.shape[1]
    taps = conv_w.shape[0]
    assert row0 % bb == 0 and batch % bb == 0
    blk0 = row0 // bb
    kern = functools.partial(_conv_step_kernel, taps=taps)
    return pl.pallas_call(
        kern,
        out_shape=(jax.ShapeDtypeStruct((batch, d_b), F32),
                   jax.ShapeDtypeStruct(conv_prev_t.shape, F32)),
        grid=(batch // bb,),
        in_specs=[pl.BlockSpec((bb, d_b), lambda i: (blk0 + i, 0)),
                  pl.BlockSpec((taps - 1, bb, d_b), lambda i: (0, i, 0)),
                  pl.BlockSpec((taps, d_b), lambda i: (0, 0)),
                  pl.BlockSpec((1, d_b), lambda i: (0, 0))],
        out_specs=(pl.BlockSpec((bb, d_b), lambda i: (i, 0)),
                   pl.BlockSpec((taps - 1, bb, d_b), lambda i: (0, i, 0))),
        compiler_params=_params(1),
        name="conv_step",
    )(u, conv_prev_t, conv_w, conv_b.reshape(1, d_b))


def _tail_kernel(oa_ref, c_ref, gb_ref, sga_ref, sgb_ref, x_ref, p_ref, cln_ref, wa_ref, wb_ref, wout_ref,
                 wpg_ref, wple_ref, fg_ref, o_ref):
    c = c_ref[...]
    mean = jnp.mean(c, axis=-1, keepdims=True)
    dc = c - mean
    var = jnp.mean(dc * dc, axis=-1, keepdims=True)
    cf = dc * lax.rsqrt(var + LN_EPS) * cln_ref[0:1, :] + cln_ref[1:2, :]
    cb = (cf * _sigmoid(cf) * gb_ref[...]).astype(BF16)
    m = sga_ref[...] * _dot(oa_ref[...], wa_ref[...]) + sgb_ref[...] * _dot(cb, wb_ref[...])
    h = x_ref[...] + _dot(m.astype(BF16), wout_ref[...])
    gate = _sigmoid(_dot(h.astype(BF16), wpg_ref[...]))
    h = h + gate * _dot(p_ref[...].astype(BF16), wple_ref[...])
    ms = jnp.mean(h * h, axis=-1, keepdims=True)
    o_ref[...] = h * lax.rsqrt(ms + RMS_EPS) * fg_ref[...]


def _tail(oa, c, gates, x, p, cln, wa, wb, wout, wpg, wple, fg, tm, gate_b_block, merge_block, row0=0):
    m, d = x.shape
    d_a = oa.shape[1]
    d_b = c.shape[1]
    ple = p.shape[1]
    blk0 = row0 // tm
    row = lambda i: (i, 0)
    const = lambda i: (0, 0)
    resident = lambda shape: pl.BlockSpec(shape, const, pipeline_mode=pl.Buffered(1))
    return pl.pallas_call(
        _tail_kernel,
        out_shape=jax.ShapeDtypeStruct((m, d), F32),
        grid=(m // tm,),
        in_specs=[pl.BlockSpec((tm, d_a), row), pl.BlockSpec((tm, d_b), row),
                  pl.BlockSpec((tm, d_b), lambda i: (blk0 + i, gate_b_block)),
                  pl.BlockSpec((tm, d), lambda i: (blk0 + i, merge_block)),
                  pl.BlockSpec((tm, d), lambda i: (blk0 + i, merge_block + 1)),
                  pl.BlockSpec((tm, d), row), pl.BlockSpec((tm, ple), row),
                  resident((2, d_b)),
                  resident((d_a, d)), resident((d_b, d)), resident((d, d)), resident((d, d)),
                  resident((ple, d)), resident((1, d))],
        out_specs=pl.BlockSpec((tm, d), row),
        compiler_params=_params(1),
        name="tail",
    )(oa, c, gates, gates, gates, x, p, cln, wa, wb, wout, wpg, wple, fg)


def _largest_tile(n, cap, align):
    t = min(n, cap)
    while n % t or t % align:
        t -= 1
    return t


def kernel(x_prompt, x_sample, state_shift, state_wkv, state_conv, p_prompt, p_sample, norm_g, w_in,
           shift_mu, w0, w_lora_b, a0, a_lora_b, k_k, k_a, r_k, lnx_g, lnx_b, w_proj_a, conv_w, conv_b,
           cln_g, cln_b, w_proj_b, w_out, w_ple, w_ple_gate, final_g):
    depth = w_in.shape[0]
    batch, seq, d = x_prompt.shape
    dec_batch, dec_seq, _ = x_sample.shape
    d_a = w_proj_a.shape[1]
    d_b = w_proj_b.shape[1]
    shift_w = shift_mu.shape[1]
    n_heads = d_a // HEAD
    n_pairs = d_a // PAIR
    taps = conv_w.shape[1]
    assert depth == 1 and dec_seq == 1 and d_a == d_b and 2 * d_a == d
    assert shift_w == 3 * d_a + 2 * LORA and d_a % PAIR == 0 and seq % CHUNK == 0

    o1 = shift_w
    o2 = o1 + d_a
    o3 = o2 + 2 * d_b
    o4 = o3 + d_b
    w = w_in[0]
    lora_w = jnp.zeros((2 * LORA, 2 * d_a), F32)
    lora_w = lora_w.at[:LORA, :d_a].set(w_lora_b[0]).at[LORA:, d_a:].set(a_lora_b[0]).astype(BF16)
    zeros_a = jnp.zeros((d_a,), F32)
    par = jnp.stack([w0[0], a0[0], k_k[0], k_a[0], r_k[0].reshape(d_a), lnx_g[0], lnx_b[0], zeros_a])
    cln = jnp.stack([cln_g[0], cln_b[0]])
    mu = shift_mu
    g_in = norm_g[0].reshape(1, d)
    fg = final_g.reshape(1, d)
    gate_b_block = d_a // d_b
    merge_block = (d_a + d_b) // d

    m_p = batch * seq
    m_all = m_p + dec_batch
    x2 = x_prompt.reshape(m_p, d)
    xs2 = x_sample.reshape(dec_batch, d)
    n_rkv = 3 * d_a
    xn, p_lora = _rmsnorm_bf16(x2, xs2, g_in, w, n_rkv, 2 * LORA, _largest_tile(m_p, 1024, SUBLANES))
    tm = _largest_tile(m_all, 1100, BF16_SUBLANES)
    tn_shift = _largest_tile(n_rkv, 1024, LANES)
    p_rkv = _project(xn, w, lambda j: j * tn_shift, n_rkv, tm, tn_shift)
    gate_cols = lambda j: jnp.where(j == 0, o1, o3 + (j - 1) * d_a)
    gates = _project(xn, w, gate_cols, d_a + d_b + 2 * d, tm, d_a, n_silu_blocks=2)
    u, (wa, wb, wout, wpg, wple) = _project_glu(
        xn, w, o2, o2 + d_b, d_b, tm, _largest_tile(d_b, 512, LANES),
        casts=[w_proj_a[0], w_proj_b[0], w_out[0], w_ple_gate[0], w_ple[0]])

    oa, st = _wkv_prompt(p_rkv, p_lora, gates, mu, par, lora_w, batch, seq, d_a)
    c = _conv_prompt(u, conv_w[0], conv_b[0], batch, seq, _largest_tile(seq, 512, CONV_STRIP))
    y_prompt = _tail(oa, c, gates, x2, p_prompt[0].reshape(m_p, -1), cln, wa, wb, wout, wpg, wple, fg,
                     _largest_tile(m_p, 256, BF16_SUBLANES), gate_b_block, merge_block).reshape(batch, seq, d)
    last_rows = lambda a: jnp.concatenate([a[(b + 1) * seq - 1:(b + 1) * seq] for b in range(batch)])
    new_shift_p = jnp.concatenate([last_rows(p_rkv), last_rows(p_lora)], axis=1)[None]
    st = st.reshape(batch, n_pairs, 2, HEAD, 2, HEAD)
    new_wkv_p = jnp.stack([st[:, :, 0, :, 0, :], st[:, :, 1, :, 1, :]], axis=2)
    new_wkv_p = jnp.swapaxes(new_wkv_p.reshape(batch, n_heads, HEAD, HEAD), -1, -2)[None]
    new_conv_p = jnp.stack([u[(b + 1) * seq - (taps - 1):(b + 1) * seq] for b in range(batch)])[None]

    tm_s = _largest_tile(dec_batch, 256, BF16_SUBLANES)
    assert m_p % tm_s == 0
    oa_s, new_wkv_t = _wkv_sample(p_rkv, p_lora, state_shift[0], gates, mu, par, lora_w,
                                  jnp.transpose(state_wkv[0], (1, 2, 3, 0)), d_a, m_p)
    new_wkv_s = jnp.transpose(new_wkv_t, (3, 0, 1, 2))
    c_s, new_conv_t = _conv_sample(u, jnp.transpose(state_conv[0], (1, 0, 2)), conv_w[0], conv_b[0],
                                   _largest_tile(dec_batch, 32, SUBLANES), m_p)
    new_conv_s = jnp.transpose(new_conv_t, (1, 0, 2))
    y_sample = _tail(oa_s, c_s, gates, xs2, p_sample[0].reshape(dec_batch, -1), cln, wa, wb, wout, wpg,
                     wple, fg, tm_s, gate_b_block, merge_block, row0=m_p).reshape(dec_batch, 1, d)
    new_shift_s = jnp.concatenate([p_rkv[m_p:], p_lora[m_p:]], axis=1)[None]
    new_conv_s = new_conv_s[None]

    return (y_prompt, y_sample, new_shift_p, new_wkv_p, new_conv_p, new_shift_s, new_wkv_s[None],
            new_conv_s)
```

```python
import functools
import math

import jax
import jax.numpy as jnp
from jax import lax
from jax.experimental import pallas as pl
from jax.experimental.pallas import tpu as pltpu

F32 = jnp.float32
BF16 = jnp.bfloat16

LANES = 128
SUBLANES = 8
BF16_SUBLANES = 16

HEAD = 64
PAIR = 2 * HEAD
LORA = 64
CHUNK = 64
WKV_SEQS_PER_STEP = 4
RMS_EPS = 1e-6
LN_EPS = 1e-5
GN_EPS = 64e-5
LOG2_DECAY_SCALE = math.exp(-0.5) / math.log(2.0)
VMEM_LIMIT = 56 * 1024 * 1024


def _params(n_axes, vmem=VMEM_LIMIT):
    return pltpu.CompilerParams(dimension_semantics=("arbitrary",) * n_axes, vmem_limit_bytes=vmem)


def _sigmoid(x):
    return 0.5 * jnp.tanh(0.5 * x) + 0.5


def _dot(a, b):
    return jnp.dot(a, b, preferred_element_type=F32)


def _dot_nt(a, b):
    return lax.dot_general(a, b, (((1,), (1,)), ((), ())), preferred_element_type=F32)


def _dot_tn(a, b):
    return lax.dot_general(a, b, (((0,), (0,)), ((), ())), preferred_element_type=F32)


def _rmsnorm_kernel(xp_ref, xs_ref, g_ref, w_ref, o_ref, ol_ref, wb_ref, *, n_prompt_blocks):
    i = pl.program_id(0)

    @pl.when(i == 0)
    def _():
        wb_ref[...] = w_ref[...].astype(BF16)

    def norm(x):
        ms = jnp.mean(x * x, axis=-1, keepdims=True)
        return (x * lax.rsqrt(ms + RMS_EPS) * g_ref[...]).astype(o_ref.dtype)

    @pl.when(i < n_prompt_blocks)
    def _():
        xn = norm(xp_ref[...])
        o_ref[...] = xn
        ol_ref[...] = _dot(xn, wb_ref[...])

    @pl.when(i == n_prompt_blocks)
    def _():
        xn = norm(xs_ref[...])
        o_ref[0:xs_ref.shape[0], :] = xn
        ol_ref[0:xs_ref.shape[0], :] = _dot(xn, wb_ref[...])


def _rmsnorm_bf16(x_prompt, x_sample, g, w, col, n_cols, tm):
    m_p, d = x_prompt.shape
    m_s = x_sample.shape[0]
    n_blocks = m_p // tm
    assert m_p % tm == 0 and m_s <= tm and col % LANES == 0
    kern = functools.partial(_rmsnorm_kernel, n_prompt_blocks=n_blocks)
    return pl.pallas_call(
        kern,
        out_shape=(jax.ShapeDtypeStruct((m_p + m_s, d), BF16),
                   jax.ShapeDtypeStruct((m_p + m_s, n_cols), F32)),
        grid=(n_blocks + 1,),
        in_specs=[pl.BlockSpec((tm, d), lambda i: (jnp.minimum(i, n_blocks - 1), 0)),
                  pl.BlockSpec((m_s, d), lambda i: (0, 0)),
                  pl.BlockSpec((1, d), lambda i: (0, 0)),
                  pl.BlockSpec((pl.Element(d), pl.Element(n_cols)), lambda i: (0, col))],
        out_specs=(pl.BlockSpec((tm, d), lambda i: (i, 0)),
                   pl.BlockSpec((tm, n_cols), lambda i: (i, 0))),
        scratch_shapes=[pltpu.VMEM((d, n_cols), BF16)],
        compiler_params=_params(1),
        name="rmsnorm",
    )(x_prompt, x_sample, g, w)


def _proj_kernel(x_ref, w_ref, o_ref, wb_ref, *, n_plain, silu_blocks):
    j = pl.program_id(0)

    @pl.when(pl.program_id(1) == 0)
    def _():
        wb_ref[...] = w_ref[...].astype(BF16)

    @pl.when(j < n_plain)
    def _():
        o_ref[...] = _dot(x_ref[...], wb_ref[...])

    @pl.when(j >= n_plain)
    def _():
        y = _dot(x_ref[...], wb_ref[...])
        s = _sigmoid(y)
        is_silu = functools.reduce(lambda p, q: p | q, [j == b for b in silu_blocks])
        o_ref[...] = jnp.where(is_silu, y * s, s)


def _project(xn, w, col_start, n_out, tm, tn, n_plain, silu_blocks):
    m, d = xn.shape
    kern = functools.partial(_proj_kernel, n_plain=n_plain, silu_blocks=silu_blocks)
    return pl.pallas_call(
        kern,
        out_shape=jax.ShapeDtypeStruct((m, n_out), F32),
        grid=(n_out // tn, m // tm),
        in_specs=[pl.BlockSpec((tm, d), lambda j, i: (i, 0)),
                  pl.BlockSpec((pl.Element(d), pl.Element(tn)),
                               lambda j, i: (0, pl.multiple_of(col_start(j), LANES)))],
        out_specs=pl.BlockSpec((tm, tn), lambda j, i: (i, j)),
        scratch_shapes=[pltpu.VMEM((d, tn), BF16)],
        compiler_params=_params(2),
        name="in_proj",
    )(xn, w)


def _glu_kernel(*refs, n_casts):
    x_ref, wa_ref, wb_ref = refs[:3]
    cast_in = refs[3:3 + n_casts]
    o_ref = refs[3 + n_casts]
    cast_out = refs[4 + n_casts:4 + 2 * n_casts]
    wab_ref, wbb_ref = refs[4 + 2 * n_casts:]

    @pl.when(pl.program_id(1) == 0)
    def _():
        wab_ref[...] = wa_ref[...].astype(BF16)
        wbb_ref[...] = wb_ref[...].astype(BF16)

    x = x_ref[...]
    o_ref[...] = _dot(x, wab_ref[...]) * _sigmoid(_dot(x, wbb_ref[...]))
    for src, dst in zip(cast_in, cast_out):
        dst[...] = src[...].astype(BF16)


def _project_glu(xn, w, col_a, col_b, n_out, tm, tn, casts):
    m, d = xn.shape
    nj, ni = n_out // tn, m // tm
    steps = nj * ni
    for a in casts:
        assert a.shape[0] % (BF16_SUBLANES * steps) == 0, (a.shape, steps)
    step_rows = lambda a: pl.BlockSpec((a.shape[0] // steps, a.shape[1]), lambda j, i: (j * ni + i, 0))
    w_cols = lambda col: pl.BlockSpec((pl.Element(d), pl.Element(tn)),
                                      lambda j, i: (0, pl.multiple_of(col + j * tn, LANES)))
    kern = functools.partial(_glu_kernel, n_casts=len(casts))
    out = pl.pallas_call(
        kern,
        out_shape=[jax.ShapeDtypeStruct((m, n_out), F32)]
        + [jax.ShapeDtypeStruct(a.shape, BF16) for a in casts],
        grid=(nj, ni),
        in_specs=[pl.BlockSpec((tm, d), lambda j, i: (i, 0)), w_cols(col_a), w_cols(col_b)]
        + [step_rows(a) for a in casts],
        out_specs=[pl.BlockSpec((tm, tn), lambda j, i: (i, j))] + [step_rows(a) for a in casts],
        scratch_shapes=[pltpu.VMEM((d, tn), BF16), pltpu.VMEM((d, tn), BF16)],
        compiler_params=_params(2),
        name="in_proj_glu",
    )(xn, w, w, *casts)
    return out[0], out[1:]


def _head_sum(x, block_ones, split=False):
    hi = x.astype(BF16)
    if not split:
        return _dot(hi, block_ones)
    lo = (x - hi.astype(F32)).astype(BF16)
    return _dot(hi, block_ones) + _dot(lo, block_ones)


def _head_sums(xs, block_ones, split=False):
    rows = xs[0].shape[0]
    s = _head_sum(jnp.concatenate(xs, axis=0), block_ones, split)
    return [s[i * rows:(i + 1) * rows] for i in range(len(xs))]


def _block_ones(value=1.0):
    ri = lax.broadcasted_iota(jnp.int32, (PAIR, PAIR), 0)
    ci = lax.broadcasted_iota(jnp.int32, (PAIR, PAIR), 1)
    return jnp.where((ri < HEAD) == (ci < HEAD), value, 0.0).astype(BF16)


def _token_shift(p, prev, mu):
    return p + mu * (prev - p)


def _lora_out(z, lora_w):
    lane = lax.broadcasted_iota(jnp.int32, z.shape, 1)
    z = jnp.where(lane < LORA, jnp.tanh(z), z)
    return _dot(z.astype(BF16), lora_w)


def _pair_slices(d_a):
    return [slice(i * PAIR, (i + 1) * PAIR) for i in range(d_a // PAIR)]


def _pair_vectors(xs_list, lo_list, par_ref, d_a, block_ones):
    sls = _pair_slices(d_a)
    shifted = lambda sl, off: slice(off + sl.start, off + sl.stop)
    items = [(xs, lo, sl) for xs, lo in zip(xs_list, lo_list) for sl in sls]
    r = [xs[:, sl] for xs, _, sl in items]
    k = [xs[:, shifted(sl, d_a)] for xs, _, sl in items]
    v = [xs[:, shifted(sl, 2 * d_a)] for xs, _, sl in items]
    kkr = [ki * par_ref[2:3, sl] for ki, (_, _, sl) in zip(k, items)]
    ss = _head_sums([x * x for x in kkr], block_ones)
    kk = [x * jnp.minimum(lax.rsqrt(s), 1e12) for x, s in zip(kkr, ss)]
    log2_decay = [-LOG2_DECAY_SCALE * _sigmoid(par_ref[0:1, sl] + lo[:, sl]) for _, lo, sl in items]
    a = [_sigmoid(par_ref[1:2, sl] + lo[:, shifted(sl, d_a)]) for _, lo, sl in items]
    kf = [ki * (1.0 + (ai - 1.0) * par_ref[3:4, sl]) for ki, ai, (_, _, sl) in zip(k, a, items)]
    return r, kf, v, log2_decay, kk, a


def _head_norm_bonus(y, r, kf, v, par_ref, d_a, block_ones):
    sls = _pair_slices(d_a) * (len(y) // (d_a // PAIR))
    block_mean = _block_ones(1.0 / HEAD)
    mean = _head_sums(y, block_mean, split=True)
    d = [x - m for x, m in zip(y, mean)]
    var = _head_sums([x * x for x in d], block_mean)
    rk = _head_sums([ri * ki * par_ref[4:5, sl] for ri, ki, sl in zip(r, kf, sls)], block_ones)
    return [x * lax.rsqrt(vr + GN_EPS) * par_ref[5:6, sl] + par_ref[6:7, sl] + s * vi
            for x, vr, s, vi, sl in zip(d, var, rk, v, sls)]


def _stack_heads(x, lane_lo):
    xb = x.astype(BF16)
    zero = jnp.zeros_like(xb)
    return jnp.concatenate([jnp.where(lane_lo, xb, zero), jnp.where(lane_lo, zero, xb)], axis=0)


def _cumsum_rows(xs):
    n = xs[0].shape[0]
    tri = (lax.broadcasted_iota(jnp.int32, (n, n), 1) <= lax.broadcasted_iota(jnp.int32, (n, n), 0))
    tri = jnp.where(tri, 1.0, 0.0).astype(BF16)
    x = jnp.concatenate(xs, axis=1)
    hi = x.astype(BF16)
    lo = (x - hi.astype(F32)).astype(BF16)
    s = _dot(tri, hi) + _dot(tri, lo)
    return [s[:, i * PAIR:(i + 1) * PAIR] for i in range(len(xs))]


def _wkv_chunk_kernel(*refs, d_a, nb):
    ps_refs, pl_refs, sga_refs = refs[:nb], refs[nb:2 * nb], refs[2 * nb:3 * nb]
    mu_ref, par_ref, lora_ref, oa_ref, st_ref, carry_ref = refs[3 * nb:]
    c = pl.program_id(1)
    n_pairs = d_a // PAIR
    C = CHUNK
    n_rkv = 3 * d_a

    @pl.when(c == 0)
    def _():
        carry_ref[...] = jnp.zeros_like(carry_ref)
        st_ref[...] = jnp.zeros_like(st_ref)

    def shifted(p, prev_row, mu):
        row = lax.broadcasted_iota(jnp.int32, p.shape, 0)
        return _token_shift(p, jnp.where(row == 0, prev_row, pltpu.roll(p, 1, axis=0)), mu)

    xs, lo = [], []
    for bi in range(nb):
        p, pz = ps_refs[bi][...], pl_refs[bi][...]
        xs.append(shifted(p, carry_ref[bi, :, :n_rkv], mu_ref[:, :n_rkv]))
        lo.append(_lora_out(shifted(pz, carry_ref[bi, :, n_rkv:], mu_ref[:, n_rkv:]), lora_ref[...]))
        carry_ref[bi, :, :n_rkv] = p[C - 1:C, :]
        carry_ref[bi, :, n_rkv:] = pz[C - 1:C, :]

    block_ones = _block_ones()
    lane_lo = lax.broadcasted_iota(jnp.int32, (C, PAIR), 1) < HEAD
    ti = lax.broadcasted_iota(jnp.int32, (C, PAIR), 0)
    si = lax.broadcasted_iota(jnp.int32, (C, PAIR), 1) % C
    strict_lower = si < ti
    lower2 = (lax.broadcasted_iota(jnp.int32, (C, 2 * PAIR), 1) % C
              <= lax.broadcasted_iota(jnp.int32, (C, 2 * PAIR), 0))
    eye = jnp.where(si == ti, 1.0, 0.0)
    cat0 = lambda x, y: jnp.concatenate([x, y], axis=0)

    r, kf, v, log2_decay, kk, a = _pair_vectors(xs, lo, par_ref, d_a, block_ones)
    cl = _cumsum_rows(log2_decay)
    cend = [x[C - 1:C, :] for x in cl]
    e_neg = [jnp.exp2(-x) for x in cl]
    kka = [x * y for x, y in zip(kk, a)]
    stack = lambda xs_: [_stack_heads(x, lane_lo) for x in xs_]
    a_p = [(-x * jnp.exp2(c_ - ld)).astype(BF16) for x, c_, ld in zip(kk, cl, log2_decay)]
    r_p = [(x * jnp.exp2(c_)).astype(BF16) for x, c_ in zip(r, cl)]
    a_s = stack(a_p)
    bk_s = [cat0(x, y) for x, y in zip(stack([x * e for x, e in zip(kka, e_neg)]),
                                       stack([x * e for x, e in zip(kf, e_neg)]))]
    v_s = stack(v)
    pend_col = [jnp.transpose(jnp.broadcast_to(jnp.exp2(ce), (PAIR, PAIR))) for ce in cend]

    m1 = [_dot_nt(cat0(x, y), z) for x, y, z in zip(a_p, r_p, bk_s)]
    a_ab = [jnp.where(strict_lower, m[:C, :PAIR], 0.0) for m in m1]
    keep = lambda mask, x: jnp.where(mask, x, jnp.zeros_like(x))
    a_ak = [keep(strict_lower, m[:C, PAIR:].astype(BF16)) for m in m1]
    a_rbk = [keep(lower2, m[C:].astype(BF16)) for m in m1]
    g = [_dot(x, y) for x, y in zip(a_ak, v_s)]

    apow = [_dot(x.astype(BF16), _stack_heads(x, lane_lo)) for x in a_ab]
    tinv = [eye + x for x in a_ab]
    n_levels = CHUNK.bit_length() - 1
    for lvl in range(1, n_levels):
        rhs = stack(apow)
        if lvl < n_levels - 1:
            both = [_dot(cat0(x.astype(BF16), t.astype(BF16)), y) for x, t, y in zip(apow, tinv, rhs)]
            apow = [x[:C] for x in both]
            tinv = [t + x[C:] for t, x in zip(tinv, both)]
        else:
            tinv = [t + _dot(t.astype(BF16), y) for t, y in zip(tinv, rhs)]

    g_s = stack(g)
    wx = [_dot(t.astype(BF16), jnp.concatenate([x, y], axis=1)) for t, x, y in zip(tinv, a_s, g_s)]
    seq_pair = [(bi, pi) for bi in range(nb) for pi in range(n_pairs)]
    st = [st_ref[bi, pi] for bi, pi in seq_pair]
    uy = [_dot(cat0(w[:, :PAIR].astype(BF16), x), s_.astype(BF16)) for w, x, s_ in zip(wx, r_p, st)]
    uv_s = [cat0(_stack_heads(x[:C] + w[:, PAIR:], lane_lo), y) for x, w, y in zip(uy, wx, v_s)]
    st_upd = [_dot_tn(x, z) for x, z in zip(bk_s, uv_s)]
    y_uv = [_dot(x, z) for x, z in zip(a_rbk, uv_s)]
    for i, (bi, pi) in enumerate(seq_pair):
        st_ref[bi, pi] = pend_col[i] * (st[i] + st_upd[i])
    y = [x[C:] + z for x, z in zip(uy, y_uv)]
    o = _head_norm_bonus(y, r, kf, v, par_ref, d_a, block_ones)
    sls = _pair_slices(d_a)
    for oi, (bi, pi) in zip(o, seq_pair):
        oa_ref[bi, :, sls[pi]] = (oi * sga_refs[bi][:, sls[pi]]).astype(oa_ref.dtype)


def _wkv_prompt(p_rkv, p_lora, gates, mu, par, lora_w, batch, seq, d_a, gate_a_block):
    n_chunks = seq // CHUNK
    shift_w = mu.shape[1]
    n_pairs = d_a // PAIR
    nb = WKV_SEQS_PER_STEP if batch % WKV_SEQS_PER_STEP == 0 else 1
    kern = functools.partial(_wkv_chunk_kernel, d_a=d_a, nb=nb)
    seq_rows = lambda bi, col=0: (lambda b, c: ((b * nb + bi) * n_chunks + c, col))
    oa, st = pl.pallas_call(
        kern,
        out_shape=(jax.ShapeDtypeStruct((batch, seq, d_a), BF16),
                   jax.ShapeDtypeStruct((batch, n_pairs, PAIR, PAIR), F32)),
        grid=(batch // nb, n_chunks),
        in_specs=([pl.BlockSpec((CHUNK, 3 * d_a), seq_rows(bi)) for bi in range(nb)]
                  + [pl.BlockSpec((CHUNK, 2 * LORA), seq_rows(bi)) for bi in range(nb)]
                  + [pl.BlockSpec((CHUNK, d_a), seq_rows(bi, gate_a_block)) for bi in range(nb)]
                  + [pl.BlockSpec((1, shift_w), lambda b, c: (0, 0)),
                     pl.BlockSpec((8, d_a), lambda b, c: (0, 0)),
                     pl.BlockSpec((2 * LORA, 2 * d_a), lambda b, c: (0, 0))]),
        out_specs=(pl.BlockSpec((nb, CHUNK, d_a), lambda b, c: (b, c, 0)),
                   pl.BlockSpec((nb, n_pairs, PAIR, PAIR), lambda b, c: (b, 0, 0, 0))),
        scratch_shapes=[pltpu.VMEM((nb, 1, shift_w), F32)],
        compiler_params=_params(2),
        name="wkv_chunked",
    )(*([p_rkv] * nb + [p_lora] * nb + [gates] * nb), mu, par, lora_w)
    return oa.reshape(batch * seq, d_a), st


STEP_ROWS = 8


def _wkv_step_kernel(ps_ref, pl_ref, prev_ref, sga_ref, mu_ref, par_ref, lora_ref, s_ref, oa_ref, so_ref,
                     vt_ref, rkv_ref, y_ref, *, d_a):
    h = pl.program_id(0)
    n_heads = d_a // HEAD
    n_rkv = 3 * d_a
    sls = _pair_slices(d_a)

    @pl.when(h == 0)
    def _():
        xs = _token_shift(ps_ref[...], prev_ref[:, :n_rkv], mu_ref[:, :n_rkv])
        lo = _lora_out(_token_shift(pl_ref[...], prev_ref[:, n_rkv:], mu_ref[:, n_rkv:]), lora_ref[...])
        r, kf, v, log2_decay, kk, a = _pair_vectors([xs], [lo], par_ref, d_a, _block_ones())
        for pi, sl in enumerate(sls):
            vecs = (-kk[pi], jnp.exp2(log2_decay[pi]), kk[pi] * a[pi], kf[pi], r[pi], v[pi])
            for i, x in enumerate(vecs):
                vt_ref[i, sl, :] = x.T
            for i, x in enumerate((r[pi], kf[pi], v[pi])):
                rkv_ref[i, :, sl] = x

    base = pl.multiple_of(h * HEAD, HEAD)
    a_t, w_t, b_t, k_t, r_t = (vt_ref[i, pl.ds(base, HEAD), :] for i in range(5))

    def rows(g, carry):
        i0 = pl.multiple_of(g * STEP_ROWS, STEP_ROWS)
        row0 = pl.multiple_of(base + i0, STEP_ROWS)
        v_rows = vt_ref[5, pl.ds(row0, STEP_ROWS), :]
        ys = []
        for ii in range(STEP_ROWS):
            s = s_ref[0, i0 + ii]
            sa = jnp.sum(s * a_t, axis=0, keepdims=True)
            s_new = s * w_t + sa * b_t + v_rows[ii:ii + 1, :] * k_t
            so_ref[0, i0 + ii] = s_new
            ys.append(jnp.sum(s_new * r_t, axis=0, keepdims=True))
        y_ref[pl.ds(row0, STEP_ROWS), :] = jnp.concatenate(ys, axis=0)
        return carry

    lax.fori_loop(0, HEAD // STEP_ROWS, rows, 0)

    @pl.when(h == n_heads - 1)
    def _():
        y = [y_ref[sl, :].T for sl in sls]
        r, kf, v = ([rkv_ref[i, :, sl] for sl in sls] for i in range(3))
        o = _head_norm_bonus(y, r, kf, v, par_ref, d_a, _block_ones())
        for oi, sl in zip(o, sls):
            oa_ref[:, sl] = (oi * sga_ref[:, sl]).astype(oa_ref.dtype)


def _wkv_sample(p_rkv, p_lora, shift_prev, gates, mu, par, lora_w, state_t, d_a, row0, gate_a_block):
    batch, shift_w = shift_prev.shape
    n_heads = d_a // HEAD
    assert row0 % batch == 0 and state_t.shape == (n_heads, HEAD, HEAD, batch)
    blk0 = row0 // batch
    const = lambda h: (0, 0)
    kern = functools.partial(_wkv_step_kernel, d_a=d_a)
    return pl.pallas_call(
        kern,
        out_shape=(jax.ShapeDtypeStruct((batch, d_a), BF16),
                   jax.ShapeDtypeStruct(state_t.shape, F32)),
        grid=(n_heads,),
        in_specs=[pl.BlockSpec((batch, 3 * d_a), lambda h: (blk0, 0)),
                  pl.BlockSpec((batch, 2 * LORA), lambda h: (blk0, 0)),
                  pl.BlockSpec((batch, shift_w), const),
                  pl.BlockSpec((batch, d_a), lambda h: (blk0, gate_a_block)),
                  pl.BlockSpec((1, shift_w), const),
                  pl.BlockSpec((8, d_a), const),
                  pl.BlockSpec((2 * LORA, 2 * d_a), const),
                  pl.BlockSpec((1, HEAD, HEAD, batch), lambda h: (h, 0, 0, 0))],
        out_specs=(pl.BlockSpec((batch, d_a), const),
                   pl.BlockSpec((1, HEAD, HEAD, batch), lambda h: (h, 0, 0, 0))),
        scratch_shapes=[pltpu.VMEM((6, d_a, batch), F32), pltpu.VMEM((3, batch, d_a), F32),
                        pltpu.VMEM((d_a, batch), F32)],
        compiler_params=_params(1),
        name="wkv_step",
    )(p_rkv, p_lora, shift_prev, gates, mu, par, lora_w, state_t)


CONV_PAD = 32
CONV_STRIP = 16


def _conv_prompt_kernel(u_ref, w_ref, b_ref, o_ref, buf_ref, c_ref, *, tt, taps):
    t = pl.program_id(1)
    sub = buf_ref.shape[1]
    lanes = [slice(s * LANES, (s + 1) * LANES) for s in range(sub)]

    @pl.when(t == 0)
    def _():
        buf_ref[0:CONV_PAD] = jnp.zeros((CONV_PAD,) + buf_ref.shape[1:], F32)

    u = u_ref[...]
    buf_ref[CONV_PAD:CONV_PAD + tt] = jnp.swapaxes(jnp.stack([u[:, ls] for ls in lanes], axis=0), 0, 1)
    off = CONV_PAD - (taps - 1)
    w = [w_ref[k] for k in range(taps)]
    bias = b_ref[0]

    def strip(s, carry):
        t0 = s * CONV_STRIP
        acc = [bias] * CONV_STRIP
        for i in range(CONV_STRIP + taps - 1):
            x = buf_ref[off + t0 + i]
            for j in range(CONV_STRIP):
                if 0 <= i - j < taps:
                    acc[j] = acc[j] + w[i - j] * x
        for j in range(CONV_STRIP):
            c_ref[t0 + j] = acc[j]
        return carry

    lax.fori_loop(0, tt // CONV_STRIP, strip, 0, unroll=4)
    buf_ref[0:CONV_PAD] = buf_ref[tt:tt + CONV_PAD]
    c = jnp.swapaxes(c_ref[...], 0, 1)
    for s, ls in enumerate(lanes):
        o_ref[:, ls] = c[s]


def _conv_prompt(u, conv_w, conv_b, batch, seq, tt):
    d_b = u.shape[1]
    taps = conv_w.shape[0]
    nt = seq // tt
    sub = d_b // LANES
    assert taps - 1 <= CONV_PAD and tt % CONV_STRIP == 0 and tt >= CONV_PAD
    kern = functools.partial(_conv_prompt_kernel, tt=tt, taps=taps)
    return pl.pallas_call(
        kern,
        out_shape=jax.ShapeDtypeStruct((batch * seq, d_b), F32),
        grid=(batch, nt),
        in_specs=[pl.BlockSpec((tt, d_b), lambda b, t: (b * nt + t, 0)),
                  pl.BlockSpec((taps, sub, LANES), lambda b, t: (0, 0, 0)),
                  pl.BlockSpec((1, sub, LANES), lambda b, t: (0, 0, 0))],
        out_specs=pl.BlockSpec((tt, d_b), lambda b, t: (b * nt + t, 0)),
        scratch_shapes=[pltpu.VMEM((tt + CONV_PAD, sub, LANES), F32), pltpu.VMEM((tt, sub, LANES), F32)],
        compiler_params=_params(2),
        name="conv_prompt",
    )(u, conv_w.reshape(taps, sub, LANES), conv_b.reshape(1, sub, LANES))


def _conv_step_kernel(u_ref, prev_ref, w_ref, b_ref, o_ref, hist_ref, *, taps):
    u = u_ref[...]
    c = b_ref[...] + w_ref[taps - 1:taps, :] * u
    for k in range(taps - 1):
        c = c + w_ref[k:k + 1, :] * prev_ref[k]
    o_ref[...] = c
    hist_ref[0:taps - 2] = prev_ref[1:taps - 1]
    hist_ref[taps - 2] = u


def _conv_sample(u, conv_prev_t, conv_w, conv_b, bb, row0):
    batch = conv_prev_t.shape[1]
    d_b = u.shape[1]
    taps = conv_w.shape[0]
    assert row0 % bb == 0 and batch % bb == 0
    blk0 = row0 // bb
    kern = functools.partial(_conv_step_kernel, taps=taps)
    return pl.pallas_call(
        kern,
        out_shape=(jax.ShapeDtypeStruct((batch, d_b), F32),
                   jax.ShapeDtypeStruct(conv_prev_t.shape, F32)),
        grid=(batch // bb,),
        in_specs=[pl.BlockSpec((bb, d_b), lambda i: (blk0 + i, 0)),
                  pl.BlockSpec((taps - 1, bb, d_b), lambda i: (0, i, 0)),
                  pl.BlockSpec((taps, d_b), lambda i: (0, 0)),
                  pl.BlockSpec((1, d_b), lambda i: (0, 0))],
        out_specs=(pl.BlockSpec((bb, d_b), lambda i: (i, 0)),
                   pl.BlockSpec((taps - 1, bb, d_b), lambda i: (0, i, 0))),
        compiler_params=_params(1),
        name="conv_step",
    )(u, conv_prev_t, conv_w, conv_b.reshape(1, d_b))


def _tail_kernel(oa_ref, c_ref, gb_ref, sga_ref, sgb_ref, x_ref, p_ref, cln_ref, wa_ref, wb_ref, wout_ref,
                 wpg_ref, wple_ref, fg_ref, o_ref):
    c = c_ref[...]
    mean = jnp.mean(c, axis=-1, keepdims=True)
    dc = c - mean
    var = jnp.mean(dc * dc, axis=-1, keepdims=True)
    cf = dc * lax.rsqrt(var + LN_EPS) * cln_ref[0:1, :] + cln_ref[1:2, :]
    cb = (cf * _sigmoid(cf) * gb_ref[...]).astype(BF16)
    m = sga_ref[...] * _dot(oa_ref[...], wa_ref[...]) + sgb_ref[...] * _dot(cb, wb_ref[...])
    h = x_ref[...] + _dot(m.astype(BF16), wout_ref[...])
    gate = _sigmoid(_dot(h.astype(BF16), wpg_ref[...]))
    h = h + gate * _dot(p_ref[...].astype(BF16), wple_ref[...])
    ms = jnp.mean(h * h, axis=-1, keepdims=True)
    o_ref[...] = h * lax.rsqrt(ms + RMS_EPS) * fg_ref[...]


def _tail(oa, c, gates, x, p, cln, wa, wb, wout, wpg, wple, fg, tm, gate_b_block, merge_block, row0=0):
    m, d = x.shape
    d_a = oa.shape[1]
    d_b = c.shape[1]
    ple = p.shape[1]
    blk0 = row0 // tm
    row = lambda i: (i, 0)
    const = lambda i: (0, 0)
    resident = lambda shape: pl.BlockSpec(shape, const, pipeline_mode=pl.Buffered(1))
    return pl.pallas_call(
        _tail_kernel,
        out_shape=jax.ShapeDtypeStruct((m, d), F32),
        grid=(m // tm,),
        in_specs=[pl.BlockSpec((tm, d_a), row), pl.BlockSpec((tm, d_b), row),
                  pl.BlockSpec((tm, d_b), lambda i: (blk0 + i, gate_b_block)),
                  pl.BlockSpec((tm, d), lambda i: (blk0 + i, merge_block)),
                  pl.BlockSpec((tm, d), lambda i: (blk0 + i, merge_block + 1)),
                  pl.BlockSpec((tm, d), row), pl.BlockSpec((tm, ple), row),
                  resident((2, d_b)),
                  resident((d_a, d)), resident((d_b, d)), resident((d, d)), resident((d, d)),
                  resident((ple, d)), resident((1, d))],
        out_specs=pl.BlockSpec((tm, d), row),
        compiler_params=_params(1),
        name="tail",
    )(oa, c, gates, gates, gates, x, p, cln, wa, wb, wout, wpg, wple, fg)


def _largest_tile(n, cap, align):
    t = min(n, cap)
    while n % t or t % align:
        t -= 1
    return t


def kernel(x_prompt, x_sample, state_shift, state_wkv, state_conv, p_prompt, p_sample, norm_g, w_in,
           shift_mu, w0, w_lora_b, a0, a_lora_b, k_k, k_a, r_k, lnx_g, lnx_b, w_proj_a, conv_w, conv_b,
           cln_g, cln_b, w_proj_b, w_out, w_ple, w_ple_gate, final_g):
    depth = w_in.shape[0]
    batch, seq, d = x_prompt.shape
    dec_batch, dec_seq, _ = x_sample.shape
    d_a = w_proj_a.shape[1]
    d_b = w_proj_b.shape[1]
    shift_w = shift_mu.shape[1]
    n_heads = d_a // HEAD
    n_pairs = d_a // PAIR
    taps = conv_w.shape[1]
    assert depth == 1 and dec_seq == 1 and d_a == d_b and 2 * d_a == d
    assert shift_w == 3 * d_a + 2 * LORA and d_a % PAIR == 0 and seq % CHUNK == 0

    o1 = shift_w
    o2 = o1 + d_a
    o3 = o2 + 2 * d_b
    o4 = o3 + d_b
    w = w_in[0]
    lora_w = jnp.zeros((2 * LORA, 2 * d_a), F32)
    lora_w = lora_w.at[:LORA, :d_a].set(w_lora_b[0]).at[LORA:, d_a:].set(a_lora_b[0]).astype(BF16)
    zeros_a = jnp.zeros((d_a,), F32)
    par = jnp.stack([w0[0], a0[0], k_k[0], k_a[0], r_k[0].reshape(d_a), lnx_g[0], lnx_b[0], zeros_a])
    cln = jnp.stack([cln_g[0], cln_b[0]])
    mu = shift_mu
    g_in = norm_g[0].reshape(1, d)
    fg = final_g.reshape(1, d)

    m_p = batch * seq
    m_all = m_p + dec_batch
    x2 = x_prompt.reshape(m_p, d)
    xs2 = x_sample.reshape(dec_batch, d)
    n_rkv = 3 * d_a
    xn, p_lora = _rmsnorm_bf16(x2, xs2, g_in, w, n_rkv, 2 * LORA, _largest_tile(m_p, 1024, SUBLANES))
    tm = _largest_tile(m_all, 1100, BF16_SUBLANES)
    n_plain = n_rkv // d_a
    gate_a_block = n_plain
    merge_block = (n_rkv + d_a) // d
    gate_b_block = (n_rkv + d_a + 2 * d) // d_b
    assert n_rkv % d_a == 0 and (n_rkv + d_a) % d == 0
    proj_cols = lambda j: jnp.where(j < n_plain, j * d_a,
                                    jnp.where(j == gate_a_block, o1,
                                              jnp.where(j == gate_b_block, o3, o4 + (j - gate_a_block - 1) * d_a)))
    proj = _project(xn, w, proj_cols, n_rkv + d_a + 2 * d + d_b, tm, d_a, n_plain,
                    silu_blocks=(gate_a_block, gate_b_block))
    p_rkv = gates = proj
    u, (wa, wb, wout, wpg, wple) = _project_glu(
        xn, w, o2, o2 + d_b, d_b, tm, _largest_tile(d_b, 512, LANES),
        casts=[w_proj_a[0], w_proj_b[0], w_out[0], w_ple_gate[0], w_ple[0]])

    oa, st = _wkv_prompt(p_rkv, p_lora, gates, mu, par, lora_w, batch, seq, d_a, gate_a_block)
    c = _conv_prompt(u, conv_w[0], conv_b[0], batch, seq, _largest_tile(seq, 512, CONV_STRIP))
    y_prompt = _tail(oa, c, gates, x2, p_prompt[0].reshape(m_p, -1), cln, wa, wb, wout, wpg, wple, fg,
                     _largest_tile(m_p, 256, BF16_SUBLANES), gate_b_block, merge_block).reshape(batch, seq, d)
    last_rows = lambda a: jnp.concatenate([a[(b + 1) * seq - 1:(b + 1) * seq] for b in range(batch)])
    new_shift_p = jnp.concatenate([last_rows(p_rkv)[:, :n_rkv], last_rows(p_lora)], axis=1)[None]
    st = st.reshape(batch, n_pairs, 2, HEAD, 2, HEAD)
    new_wkv_p = jnp.stack([st[:, :, 0, :, 0, :], st[:, :, 1, :, 1, :]], axis=2)
    new_wkv_p = jnp.swapaxes(new_wkv_p.reshape(batch, n_heads, HEAD, HEAD), -1, -2)[None]
    new_conv_p = jnp.stack([u[(b + 1) * seq - (taps - 1):(b + 1) * seq] for b in range(batch)])[None]

    tm_s = _largest_tile(dec_batch, 256, BF16_SUBLANES)
    assert m_p % tm_s == 0
    oa_s, new_wkv_t = _wkv_sample(p_rkv, p_lora, state_shift[0], gates, mu, par, lora_w,
                                  jnp.transpose(state_wkv[0], (1, 2, 3, 0)), d_a, m_p, gate_a_block)
    new_wkv_s = jnp.transpose(new_wkv_t, (3, 0, 1, 2))
    c_s, new_conv_t = _conv_sample(u, jnp.transpose(state_conv[0], (1, 0, 2)), conv_w[0], conv_b[0],
                                   _largest_tile(dec_batch, 32, SUBLANES), m_p)
    new_conv_s = jnp.transpose(new_conv_t, (1, 0, 2))
    y_sample = _tail(oa_s, c_s, gates, xs2, p_sample[0].reshape(dec_batch, -1), cln, wa, wb, wout, wpg,
                     wple, fg, tm_s, gate_b_block, merge_block, row0=m_p).reshape(dec_batch, 1, d)
    new_shift_s = jnp.concatenate([p_rkv[m_p:, :n_rkv], p_lora[m_p:]], axis=1)[None]
    new_conv_s = new_conv_s[None]

    return (y_prompt, y_sample, new_shift_p, new_wkv_p, new_conv_p, new_shift_s, new_wkv_s[None],
            new_conv_s)
```

```python
import functools
import math

import jax
import jax.numpy as jnp
from jax import lax
from jax.experimental import pallas as pl
from jax.experimental.pallas import tpu as pltpu

F32 = jnp.float32
BF16 = jnp.bfloat16

LANES = 128
SUBLANES = 8
BF16_SUBLANES = 16

HEAD = 64
PAIR = 2 * HEAD
LORA = 64
CHUNK = 64
WKV_SEQS_PER_STEP = 4
RMS_EPS = 1e-6
LN_EPS = 1e-5
GN_EPS = 64e-5
LOG2_DECAY_SCALE = math.exp(-0.5) / math.log(2.0)
VMEM_LIMIT = 56 * 1024 * 1024


def _params(n_axes, vmem=VMEM_LIMIT):
    return pltpu.CompilerParams(dimension_semantics=("arbitrary",) * n_axes, vmem_limit_bytes=vmem)


def _sigmoid(x):
    return 0.5 * jnp.tanh(0.5 * x) + 0.5


def _dot(a, b):
    return jnp.dot(a, b, preferred_element_type=F32)


def _dot_nt(a, b):
    return lax.dot_general(a, b, (((1,), (1,)), ((), ())), preferred_element_type=F32)


def _dot_tn(a, b):
    return lax.dot_general(a, b, (((0,), (0,)), ((), ())), preferred_element_type=F32)


def _rmsnorm_kernel(xp_ref, xs_ref, g_ref, w_ref, o_ref, ol_ref, wb_ref, *, n_prompt_blocks):
    i = pl.program_id(0)

    @pl.when(i == 0)
    def _():
        wb_ref[...] = w_ref[...].astype(BF16)

    def norm(x):
        ms = jnp.mean(x * x, axis=-1, keepdims=True)
        return (x * lax.rsqrt(ms + RMS_EPS) * g_ref[...]).astype(o_ref.dtype)

    @pl.when(i < n_prompt_blocks)
    def _():
        xn = norm(xp_ref[...])
        o_ref[...] = xn
        ol_ref[...] = _dot(xn, wb_ref[...])

    @pl.when(i == n_prompt_blocks)
    def _():
        xn = norm(xs_ref[...])
        o_ref[0:xs_ref.shape[0], :] = xn
        ol_ref[0:xs_ref.shape[0], :] = _dot(xn, wb_ref[...])


def _rmsnorm_bf16(x_prompt, x_sample, g, w, col, n_cols, tm):
    m_p, d = x_prompt.shape
    m_s = x_sample.shape[0]
    n_blocks = m_p // tm
    assert m_p % tm == 0 and m_s <= tm and col % LANES == 0
    kern = functools.partial(_rmsnorm_kernel, n_prompt_blocks=n_blocks)
    return pl.pallas_call(
        kern,
        out_shape=(jax.ShapeDtypeStruct((m_p + m_s, d), BF16),
                   jax.ShapeDtypeStruct((m_p + m_s, n_cols), F32)),
        grid=(n_blocks + 1,),
        in_specs=[pl.BlockSpec((tm, d), lambda i: (jnp.minimum(i, n_blocks - 1), 0)),
                  pl.BlockSpec((m_s, d), lambda i: (0, 0)),
                  pl.BlockSpec((1, d), lambda i: (0, 0)),
                  pl.BlockSpec((pl.Element(d), pl.Element(n_cols)), lambda i: (0, col))],
        out_specs=(pl.BlockSpec((tm, d), lambda i: (i, 0)),
                   pl.BlockSpec((tm, n_cols), lambda i: (i, 0))),
        scratch_shapes=[pltpu.VMEM((d, n_cols), BF16)],
        compiler_params=_params(1),
        name="rmsnorm",
    )(x_prompt, x_sample, g, w)


def _cast_specs(casts, steps, ni):
    specs = []
    for a in casts:
        rows = BF16_SUBLANES * pl.cdiv(a.shape[0], BF16_SUBLANES * steps)
        assert a.shape[0] % rows == 0, (a.shape, steps)
        last = a.shape[0] // rows - 1
        specs.append(pl.BlockSpec((rows, a.shape[1]), lambda j, i, last=last: (jnp.minimum(j * ni + i, last), 0)))
    return specs


def _proj_kernel(*refs, n_plain, silu_blocks, n_casts):
    x_ref, w_ref = refs[:2]
    cast_in = refs[2:2 + n_casts]
    o_ref = refs[2 + n_casts]
    cast_out = refs[3 + n_casts:3 + 2 * n_casts]
    wb_ref = refs[3 + 2 * n_casts]
    j = pl.program_id(0)
    for src, dst in zip(cast_in, cast_out):
        dst[...] = src[...].astype(BF16)

    @pl.when(pl.program_id(1) == 0)
    def _():
        wb_ref[...] = w_ref[...].astype(BF16)

    @pl.when(j < n_plain)
    def _():
        o_ref[...] = _dot(x_ref[...], wb_ref[...])

    @pl.when(j >= n_plain)
    def _():
        y = _dot(x_ref[...], wb_ref[...])
        s = _sigmoid(y)
        is_silu = functools.reduce(lambda p, q: p | q, [j == b for b in silu_blocks])
        o_ref[...] = jnp.where(is_silu, y * s, s)


def _project(xn, w, col_start, n_out, tm, tn, n_plain, silu_blocks, casts):
    m, d = xn.shape
    nj, ni = n_out // tn, m // tm
    cast_specs = _cast_specs(casts, nj * ni, ni)
    kern = functools.partial(_proj_kernel, n_plain=n_plain, silu_blocks=silu_blocks, n_casts=len(casts))
    out = pl.pallas_call(
        kern,
        out_shape=[jax.ShapeDtypeStruct((m, n_out), F32)]
        + [jax.ShapeDtypeStruct(a.shape, BF16) for a in casts],
        grid=(nj, ni),
        in_specs=[pl.BlockSpec((tm, d), lambda j, i: (i, 0)),
                  pl.BlockSpec((pl.Element(d), pl.Element(tn)),
                               lambda j, i: (0, pl.multiple_of(col_start(j), LANES)))] + cast_specs,
        out_specs=[pl.BlockSpec((tm, tn), lambda j, i: (i, j))] + cast_specs,
        scratch_shapes=[pltpu.VMEM((d, tn), BF16)],
        compiler_params=_params(2),
        name="in_proj",
    )(xn, w, *casts)
    return out[0], out[1:]


def _glu_kernel(*refs, n_casts):
    x_ref, wa_ref, wb_ref = refs[:3]
    cast_in = refs[3:3 + n_casts]
    o_ref = refs[3 + n_casts]
    cast_out = refs[4 + n_casts:4 + 2 * n_casts]
    wab_ref, wbb_ref = refs[4 + 2 * n_casts:]

    @pl.when(pl.program_id(1) == 0)
    def _():
        wab_ref[...] = wa_ref[...].astype(BF16)
        wbb_ref[...] = wb_ref[...].astype(BF16)

    x = x_ref[...]
    o_ref[...] = _dot(x, wab_ref[...]) * _sigmoid(_dot(x, wbb_ref[...]))
    for src, dst in zip(cast_in, cast_out):
        dst[...] = src[...].astype(BF16)


def _project_glu(xn, w, col_a, col_b, n_out, tm, tn, casts):
    m, d = xn.shape
    nj, ni = n_out // tn, m // tm
    cast_specs = _cast_specs(casts, nj * ni, ni)
    w_cols = lambda col: pl.BlockSpec((pl.Element(d), pl.Element(tn)),
                                      lambda j, i: (0, pl.multiple_of(col + j * tn, LANES)))
    kern = functools.partial(_glu_kernel, n_casts=len(casts))
    out = pl.pallas_call(
        kern,
        out_shape=[jax.ShapeDtypeStruct((m, n_out), F32)]
        + [jax.ShapeDtypeStruct(a.shape, BF16) for a in casts],
        grid=(nj, ni),
        in_specs=[pl.BlockSpec((tm, d), lambda j, i: (i, 0)), w_cols(col_a), w_cols(col_b)] + cast_specs,
        out_specs=[pl.BlockSpec((tm, tn), lambda j, i: (i, j))] + cast_specs,
        scratch_shapes=[pltpu.VMEM((d, tn), BF16), pltpu.VMEM((d, tn), BF16)],
        compiler_params=_params(2),
        name="in_proj_glu",
    )(xn, w, w, *casts)
    return out[0], out[1:]


def _head_sum(x, block_ones, split=False):
    hi = x.astype(BF16)
    if not split:
        return _dot(hi, block_ones)
    lo = (x - hi.astype(F32)).astype(BF16)
    return _dot(hi, block_ones) + _dot(lo, block_ones)


def _head_sums(xs, block_ones, split=False):
    rows = xs[0].shape[0]
    s = _head_sum(jnp.concatenate(xs, axis=0), block_ones, split)
    return [s[i * rows:(i + 1) * rows] for i in range(len(xs))]


def _block_ones(value=1.0):
    ri = lax.broadcasted_iota(jnp.int32, (PAIR, PAIR), 0)
    ci = lax.broadcasted_iota(jnp.int32, (PAIR, PAIR), 1)
    return jnp.where((ri < HEAD) == (ci < HEAD), value, 0.0).astype(BF16)


def _token_shift(p, prev, mu):
    return p + mu * (prev - p)


def _lora_out(z, lora_w):
    lane = lax.broadcasted_iota(jnp.int32, z.shape, 1)
    z = jnp.where(lane < LORA, jnp.tanh(z), z)
    return _dot(z.astype(BF16), lora_w)


def _pair_slices(d_a):
    return [slice(i * PAIR, (i + 1) * PAIR) for i in range(d_a // PAIR)]


def _pair_vectors(xs_list, lo_list, par_ref, d_a, block_ones):
    sls = _pair_slices(d_a)
    shifted = lambda sl, off: slice(off + sl.start, off + sl.stop)
    items = [(xs, lo, sl) for xs, lo in zip(xs_list, lo_list) for sl in sls]
    r = [xs[:, sl] for xs, _, sl in items]
    k = [xs[:, shifted(sl, d_a)] for xs, _, sl in items]
    v = [xs[:, shifted(sl, 2 * d_a)] for xs, _, sl in items]
    kkr = [ki * par_ref[2:3, sl] for ki, (_, _, sl) in zip(k, items)]
    ss = _head_sums([x * x for x in kkr], block_ones)
    kk = [x * jnp.minimum(lax.rsqrt(s), 1e12) for x, s in zip(kkr, ss)]
    log2_decay = [-LOG2_DECAY_SCALE * _sigmoid(par_ref[0:1, sl] + lo[:, sl]) for _, lo, sl in items]
    a = [_sigmoid(par_ref[1:2, sl] + lo[:, shifted(sl, d_a)]) for _, lo, sl in items]
    kf = [ki * (1.0 + (ai - 1.0) * par_ref[3:4, sl]) for ki, ai, (_, _, sl) in zip(k, a, items)]
    return r, kf, v, log2_decay, kk, a


def _head_norm_bonus(y, r, kf, v, par_ref, d_a, block_ones):
    sls = _pair_slices(d_a) * (len(y) // (d_a // PAIR))
    block_mean = _block_ones(1.0 / HEAD)
    mean = _head_sums(y, block_mean, split=True)
    d = [x - m for x, m in zip(y, mean)]
    var = _head_sums([x * x for x in d], block_mean)
    rk = _head_sums([ri * ki * par_ref[4:5, sl] for ri, ki, sl in zip(r, kf, sls)], block_ones)
    return [x * lax.rsqrt(vr + GN_EPS) * par_ref[5:6, sl] + par_ref[6:7, sl] + s * vi
            for x, vr, s, vi, sl in zip(d, var, rk, v, sls)]


def _stack_heads(x, lane_lo):
    xb = x.astype(BF16)
    zero = jnp.zeros_like(xb)
    return jnp.concatenate([jnp.where(lane_lo, xb, zero), jnp.where(lane_lo, zero, xb)], axis=0)


def _cumsum_rows(xs):
    n = xs[0].shape[0]
    tri = (lax.broadcasted_iota(jnp.int32, (n, n), 1) <= lax.broadcasted_iota(jnp.int32, (n, n), 0))
    tri = jnp.where(tri, 1.0, 0.0).astype(BF16)
    x = jnp.concatenate(xs, axis=1)
    hi = x.astype(BF16)
    lo = (x - hi.astype(F32)).astype(BF16)
    s = _dot(tri, hi) + _dot(tri, lo)
    return [s[:, i * PAIR:(i + 1) * PAIR] for i in range(len(xs))]


def _wkv_chunk_kernel(*refs, d_a, nb):
    ps_refs, pl_refs, sga_refs = refs[:nb], refs[nb:2 * nb], refs[2 * nb:3 * nb]
    mu_ref, par_ref, lora_ref, oa_ref, st_ref, carry_ref = refs[3 * nb:]
    c = pl.program_id(1)
    n_pairs = d_a // PAIR
    C = CHUNK
    n_rkv = 3 * d_a

    @pl.when(c == 0)
    def _():
        carry_ref[...] = jnp.zeros_like(carry_ref)
        st_ref[...] = jnp.zeros_like(st_ref)

    def shifted(p, prev_row, mu):
        row = lax.broadcasted_iota(jnp.int32, p.shape, 0)
        return _token_shift(p, jnp.where(row == 0, prev_row, pltpu.roll(p, 1, axis=0)), mu)

    xs, lo = [], []
    for bi in range(nb):
        p, pz = ps_refs[bi][...], pl_refs[bi][...]
        xs.append(shifted(p, carry_ref[bi, :, :n_rkv], mu_ref[:, :n_rkv]))
        lo.append(_lora_out(shifted(pz, carry_ref[bi, :, n_rkv:], mu_ref[:, n_rkv:]), lora_ref[...]))
        carry_ref[bi, :, :n_rkv] = p[C - 1:C, :]
        carry_ref[bi, :, n_rkv:] = pz[C - 1:C, :]

    block_ones = _block_ones()
    lane_lo = lax.broadcasted_iota(jnp.int32, (C, PAIR), 1) < HEAD
    ti = lax.broadcasted_iota(jnp.int32, (C, PAIR), 0)
    si = lax.broadcasted_iota(jnp.int32, (C, PAIR), 1) % C
    strict_lower = si < ti
    lower2 = (lax.broadcasted_iota(jnp.int32, (C, 2 * PAIR), 1) % C
              <= lax.broadcasted_iota(jnp.int32, (C, 2 * PAIR), 0))
    eye = jnp.where(si == ti, 1.0, 0.0)
    cat0 = lambda x, y: jnp.concatenate([x, y], axis=0)

    r, kf, v, log2_decay, kk, a = _pair_vectors(xs, lo, par_ref, d_a, block_ones)
    cl = _cumsum_rows(log2_decay)
    cend = [x[C - 1:C, :] for x in cl]
    e_neg = [jnp.exp2(-x) for x in cl]
    kka = [x * y for x, y in zip(kk, a)]
    stack = lambda xs_: [_stack_heads(x, lane_lo) for x in xs_]
    a_p = [(-x * jnp.exp2(c_ - ld)).astype(BF16) for x, c_, ld in zip(kk, cl, log2_decay)]
    r_p = [(x * jnp.exp2(c_)).astype(BF16) for x, c_ in zip(r, cl)]
    a_s = stack(a_p)
    bk_s = [cat0(x, y) for x, y in zip(stack([x * e for x, e in zip(kka, e_neg)]),
                                       stack([x * e for x, e in zip(kf, e_neg)]))]
    v_s = stack(v)
    pend_col = [jnp.transpose(jnp.broadcast_to(jnp.exp2(ce), (PAIR, PAIR))) for ce in cend]

    m1 = [_dot_nt(cat0(x, y), z) for x, y, z in zip(a_p, r_p, bk_s)]
    a_ab = [jnp.where(strict_lower, m[:C, :PAIR], 0.0) for m in m1]
    keep = lambda mask, x: jnp.where(mask, x, jnp.zeros_like(x))
    a_ak = [keep(strict_lower, m[:C, PAIR:].astype(BF16)) for m in m1]
    a_rbk = [keep(lower2, m[C:].astype(BF16)) for m in m1]
    g = [_dot(x, y) for x, y in zip(a_ak, v_s)]

    apow = [_dot(x.astype(BF16), _stack_heads(x, lane_lo)) for x in a_ab]
    tinv = [eye + x for x in a_ab]
    n_levels = CHUNK.bit_length() - 1
    for lvl in range(1, n_levels):
        rhs = stack(apow)
        if lvl < n_levels - 1:
            both = [_dot(cat0(x.astype(BF16), t.astype(BF16)), y) for x, t, y in zip(apow, tinv, rhs)]
            apow = [x[:C] for x in both]
            tinv = [t + x[C:] for t, x in zip(tinv, both)]
        else:
            tinv = [t + _dot(t.astype(BF16), y) for t, y in zip(tinv, rhs)]

    g_s = stack(g)
    wx = [_dot(t.astype(BF16), jnp.concatenate([x, y], axis=1)) for t, x, y in zip(tinv, a_s, g_s)]
    seq_pair = [(bi, pi) for bi in range(nb) for pi in range(n_pairs)]
    st = [st_ref[bi, pi] for bi, pi in seq_pair]
    uy = [_dot(cat0(w[:, :PAIR].astype(BF16), x), s_.astype(BF16)) for w, x, s_ in zip(wx, r_p, st)]
    uv_s = [cat0(_stack_heads(x[:C] + w[:, PAIR:], lane_lo), y) for x, w, y in zip(uy, wx, v_s)]
    st_upd = [_dot_tn(x, z) for x, z in zip(bk_s, uv_s)]
    y_uv = [_dot(x, z) for x, z in zip(a_rbk, uv_s)]
    for i, (bi, pi) in enumerate(seq_pair):
        st_ref[bi, pi] = pend_col[i] * (st[i] + st_upd[i])
    y = [x[C:] + z for x, z in zip(uy, y_uv)]
    o = _head_norm_bonus(y, r, kf, v, par_ref, d_a, block_ones)
    sls = _pair_slices(d_a)
    for oi, (bi, pi) in zip(o, seq_pair):
        oa_ref[bi, :, sls[pi]] = (oi * sga_refs[bi][:, sls[pi]]).astype(oa_ref.dtype)


def _wkv_prompt(p_rkv, p_lora, gates, mu, par, lora_w, batch, seq, d_a, gate_a_block):
    n_chunks = seq // CHUNK
    shift_w = mu.shape[1]
    n_pairs = d_a // PAIR
    nb = WKV_SEQS_PER_STEP if batch % WKV_SEQS_PER_STEP == 0 else 1
    kern = functools.partial(_wkv_chunk_kernel, d_a=d_a, nb=nb)
    seq_rows = lambda bi, col=0: (lambda b, c: ((b * nb + bi) * n_chunks + c, col))
    oa, st = pl.pallas_call(
        kern,
        out_shape=(jax.ShapeDtypeStruct((batch, seq, d_a), BF16),
                   jax.ShapeDtypeStruct((batch, n_pairs, PAIR, PAIR), F32)),
        grid=(batch // nb, n_chunks),
        in_specs=([pl.BlockSpec((CHUNK, 3 * d_a), seq_rows(bi)) for bi in range(nb)]
                  + [pl.BlockSpec((CHUNK, 2 * LORA), seq_rows(bi)) for bi in range(nb)]
                  + [pl.BlockSpec((CHUNK, d_a), seq_rows(bi, gate_a_block)) for bi in range(nb)]
                  + [pl.BlockSpec((1, shift_w), lambda b, c: (0, 0)),
                     pl.BlockSpec((8, d_a), lambda b, c: (0, 0)),
                     pl.BlockSpec((2 * LORA, 2 * d_a), lambda b, c: (0, 0))]),
        out_specs=(pl.BlockSpec((nb, CHUNK, d_a), lambda b, c: (b, c, 0)),
                   pl.BlockSpec((nb, n_pairs, PAIR, PAIR), lambda b, c: (b, 0, 0, 0))),
        scratch_shapes=[pltpu.VMEM((nb, 1, shift_w), F32)],
        compiler_params=_params(2),
        name="wkv_chunked",
    )(*([p_rkv] * nb + [p_lora] * nb + [gates] * nb), mu, par, lora_w)
    return oa.reshape(batch * seq, d_a), st


STEP_ROWS = 8


def _wkv_step_kernel(ps_ref, pl_ref, prev_ref, sga_ref, mu_ref, par_ref, lora_ref, s_ref, oa_ref, so_ref,
                     vt_ref, rkv_ref, y_ref, *, d_a):
    h = pl.program_id(0)
    n_heads = d_a // HEAD
    n_rkv = 3 * d_a
    sls = _pair_slices(d_a)

    @pl.when(h == 0)
    def _():
        xs = _token_shift(ps_ref[...], prev_ref[:, :n_rkv], mu_ref[:, :n_rkv])
        lo = _lora_out(_token_shift(pl_ref[...], prev_ref[:, n_rkv:], mu_ref[:, n_rkv:]), lora_ref[...])
        r, kf, v, log2_decay, kk, a = _pair_vectors([xs], [lo], par_ref, d_a, _block_ones())
        for pi, sl in enumerate(sls):
            vecs = (-kk[pi], jnp.exp2(log2_decay[pi]), kk[pi] * a[pi], kf[pi], r[pi], v[pi])
            for i, x in enumerate(vecs):
                vt_ref[i, sl, :] = x.T
            for i, x in enumerate((r[pi], kf[pi], v[pi])):
                rkv_ref[i, :, sl] = x

    base = pl.multiple_of(h * HEAD, HEAD)
    a_t, w_t, b_t, k_t, r_t = (vt_ref[i, pl.ds(base, HEAD), :] for i in range(5))

    def rows(g, carry):
        i0 = pl.multiple_of(g * STEP_ROWS, STEP_ROWS)
        row0 = pl.multiple_of(base + i0, STEP_ROWS)
        v_rows = vt_ref[5, pl.ds(row0, STEP_ROWS), :]
        ys = []
        for ii in range(STEP_ROWS):
            s = s_ref[0, i0 + ii]
            sa = jnp.sum(s * a_t, axis=0, keepdims=True)
            s_new = s * w_t + sa * b_t + v_rows[ii:ii + 1, :] * k_t
            so_ref[0, i0 + ii] = s_new
            ys.append(jnp.sum(s_new * r_t, axis=0, keepdims=True))
        y_ref[pl.ds(row0, STEP_ROWS), :] = jnp.concatenate(ys, axis=0)
        return carry

    lax.fori_loop(0, HEAD // STEP_ROWS, rows, 0)

    @pl.when(h == n_heads - 1)
    def _():
        y = [y_ref[sl, :].T for sl in sls]
        r, kf, v = ([rkv_ref[i, :, sl] for sl in sls] for i in range(3))
        o = _head_norm_bonus(y, r, kf, v, par_ref, d_a, _block_ones())
        for oi, sl in zip(o, sls):
            oa_ref[:, sl] = (oi * sga_ref[:, sl]).astype(oa_ref.dtype)


def _wkv_sample(p_rkv, p_lora, shift_prev, gates, mu, par, lora_w, state_t, d_a, row0, gate_a_block):
    batch, shift_w = shift_prev.shape
    n_heads = d_a // HEAD
    assert row0 % batch == 0 and state_t.shape == (n_heads, HEAD, HEAD, batch)
    blk0 = row0 // batch
    const = lambda h: (0, 0)
    kern = functools.partial(_wkv_step_kernel, d_a=d_a)
    return pl.pallas_call(
        kern,
        out_shape=(jax.ShapeDtypeStruct((batch, d_a), BF16),
                   jax.ShapeDtypeStruct(state_t.shape, F32)),
        grid=(n_heads,),
        in_specs=[pl.BlockSpec((batch, 3 * d_a), lambda h: (blk0, 0)),
                  pl.BlockSpec((batch, 2 * LORA), lambda h: (blk0, 0)),
                  pl.BlockSpec((batch, shift_w), const),
                  pl.BlockSpec((batch, d_a), lambda h: (blk0, gate_a_block)),
                  pl.BlockSpec((1, shift_w), const),
                  pl.BlockSpec((8, d_a), const),
                  pl.BlockSpec((2 * LORA, 2 * d_a), const),
                  pl.BlockSpec((1, HEAD, HEAD, batch), lambda h: (h, 0, 0, 0))],
        out_specs=(pl.BlockSpec((batch, d_a), const),
                   pl.BlockSpec((1, HEAD, HEAD, batch), lambda h: (h, 0, 0, 0))),
        scratch_shapes=[pltpu.VMEM((6, d_a, batch), F32), pltpu.VMEM((3, batch, d_a), F32),
                        pltpu.VMEM((d_a, batch), F32)],
        compiler_params=_params(1),
        name="wkv_step",
    )(p_rkv, p_lora, shift_prev, gates, mu, par, lora_w, state_t)


CONV_PAD = 32
CONV_STRIP = 16


def _conv_prompt_kernel(u_ref, w_ref, b_ref, o_ref, buf_ref, c_ref, *, tt, taps):
    t = pl.program_id(1)
    sub = buf_ref.shape[1]
    lanes = [slice(s * LANES, (s + 1) * LANES) for s in range(sub)]

    @pl.when(t == 0)
    def _():
        buf_ref[0:CONV_PAD] = jnp.zeros((CONV_PAD,) + buf_ref.shape[1:], F32)

    u = u_ref[...]
    buf_ref[CONV_PAD:CONV_PAD + tt] = jnp.swapaxes(jnp.stack([u[:, ls] for ls in lanes], axis=0), 0, 1)
    off = CONV_PAD - (taps - 1)
    w = [w_ref[k] for k in range(taps)]
    bias = b_ref[0]

    def strip(s, carry):
        t0 = s * CONV_STRIP
        acc = [bias] * CONV_STRIP
        for i in range(CONV_STRIP + taps - 1):
            x = buf_ref[off + t0 + i]
            for j in range(CONV_STRIP):
                if 0 <= i - j < taps:
                    acc[j] = acc[j] + w[i - j] * x
        for j in range(CONV_STRIP):
            c_ref[t0 + j] = acc[j]
        return carry

    lax.fori_loop(0, tt // CONV_STRIP, strip, 0, unroll=4)
    buf_ref[0:CONV_PAD] = buf_ref[tt:tt + CONV_PAD]
    c = jnp.swapaxes(c_ref[...], 0, 1)
    for s, ls in enumerate(lanes):
        o_ref[:, ls] = c[s]


def _conv_prompt(u, conv_w, conv_b, batch, seq, tt):
    d_b = u.shape[1]
    taps = conv_w.shape[0]
    nt = seq // tt
    sub = d_b // LANES
    assert taps - 1 <= CONV_PAD and tt % CONV_STRIP == 0 and tt >= CONV_PAD
    kern = functools.partial(_conv_prompt_kernel, tt=tt, taps=taps)
    return pl.pallas_call(
        kern,
        out_shape=jax.ShapeDtypeStruct((batch * seq, d_b), F32),
        grid=(batch, nt),
        in_specs=[pl.BlockSpec((tt, d_b), lambda b, t: (b * nt + t, 0)),
                  pl.BlockSpec((taps, sub, LANES), lambda b, t: (0, 0, 0)),
                  pl.BlockSpec((1, sub, LANES), lambda b, t: (0, 0, 0))],
        out_specs=pl.BlockSpec((tt, d_b), lambda b, t: (b * nt + t, 0)),
        scratch_shapes=[pltpu.VMEM((tt + CONV_PAD, sub, LANES), F32), pltpu.VMEM((tt, sub, LANES), F32)],
        compiler_params=_params(2),
        name="conv_prompt",
    )(u, conv_w.reshape(taps, sub, LANES), conv_b.reshape(1, sub, LANES))


def _conv_step_kernel(u_ref, prev_ref, w_ref, b_ref, o_ref, hist_ref, *, taps):
    u = u_ref[...]
    c = b_ref[...] + w_ref[taps - 1:taps, :] * u
    for k in range(taps - 1):
        c = c + w_ref[k:k + 1, :] * prev_ref[k]
    o_ref[...] = c
    hist_ref[0:taps - 2] = prev_ref[1:taps - 1]
    hist_ref[taps - 2] = u


def _conv_sample(u, conv_prev_t, conv_w, conv_b, bb, row0):
    batch = conv_prev_t.shape[1]
    d_b = u.shape[1]
    taps = conv_w.shape[0]
    assert row0 % bb == 0 and batch % bb == 0
    blk0 = row0 // bb
    kern = functools.partial(_conv_step_kernel, taps=taps)
    return pl.pallas_call(
        kern,
        out_shape=(jax.ShapeDtypeStruct((batch, d_b), F32),
                   jax.ShapeDtypeStruct(conv_prev_t.shape, F32)),
        grid=(batch // bb,),
        in_specs=[pl.BlockSpec((bb, d_b), lambda i: (blk0 + i, 0)),
                  pl.BlockSpec((taps - 1, bb, d_b), lambda i: (0, i, 0)),
                  pl.BlockSpec((taps, d_b), lambda i: (0, 0)),
                  pl.BlockSpec((1, d_b), lambda i: (0, 0))],
        out_specs=(pl.BlockSpec((bb, d_b), lambda i: (i, 0)),
                   pl.BlockSpec((taps - 1, bb, d_b), lambda i: (0, i, 0))),
        compiler_params=_params(1),
        name="conv_step",
    )(u, conv_prev_t, conv_w, conv_b.reshape(1, d_b))


def _tail_kernel(oa_ref, c_ref, gb_ref, sga_ref, sgb_ref, x_ref, p_ref, cln_ref, wa_ref, wb_ref, wout_ref,
                 wpg_ref, wple_ref, fg_ref, o_ref):
    c = c_ref[...]
    mean = jnp.mean(c, axis=-1, keepdims=True)
    dc = c - mean
    var = jnp.mean(dc * dc, axis=-1, keepdims=True)
    cf = dc * lax.rsqrt(var + LN_EPS) * cln_ref[0:1, :] + cln_ref[1:2, :]
    cb = (cf * _sigmoid(cf) * gb_ref[...]).astype(BF16)
    m = sga_ref[...] * _dot(oa_ref[...], wa_ref[...]) + sgb_ref[...] * _dot(cb, wb_ref[...])
    h = x_ref[...] + _dot(m.astype(BF16), wout_ref[...])
    gate = _sigmoid(_dot(h.astype(BF16), wpg_ref[...]))
    h = h + gate * _dot(p_ref[...].astype(BF16), wple_ref[...])
    ms = jnp.mean(h * h, axis=-1, keepdims=True)
    o_ref[...] = h * lax.rsqrt(ms + RMS_EPS) * fg_ref[...]


def _tail(oa, c, gates, x, p, cln, wa, wb, wout, wpg, wple, fg, tm, gate_b_block, merge_block, row0=0):
    m, d = x.shape
    d_a = oa.shape[1]
    d_b = c.shape[1]
    ple = p.shape[1]
    blk0 = row0 // tm
    row = lambda i: (i, 0)
    const = lambda i: (0, 0)
    resident = lambda shape: pl.BlockSpec(shape, const, pipeline_mode=pl.Buffered(1))
    return pl.pallas_call(
        _tail_kernel,
        out_shape=jax.ShapeDtypeStruct((m, d), F32),
        grid=(m // tm,),
        in_specs=[pl.BlockSpec((tm, d_a), row), pl.BlockSpec((tm, d_b), row),
                  pl.BlockSpec((tm, d_b), lambda i: (blk0 + i, gate_b_block)),
                  pl.BlockSpec((tm, d), lambda i: (blk0 + i, merge_block)),
                  pl.BlockSpec((tm, d), lambda i: (blk0 + i, merge_block + 1)),
                  pl.BlockSpec((tm, d), row), pl.BlockSpec((tm, ple), row),
                  resident((2, d_b)),
                  resident((d_a, d)), resident((d_b, d)), resident((d, d)), resident((d, d)),
                  resident((ple, d)), resident((1, d))],
        out_specs=pl.BlockSpec((tm, d), row),
        compiler_params=_params(1),
        name="tail",
    )(oa, c, gates, gates, gates, x, p, cln, wa, wb, wout, wpg, wple, fg)


def _largest_tile(n, cap, align):
    t = min(n, cap)
    while n % t or t % align:
        t -= 1
    return t


def kernel(x_prompt, x_sample, state_shift, state_wkv, state_conv, p_prompt, p_sample, norm_g, w_in,
           shift_mu, w0, w_lora_b, a0, a_lora_b, k_k, k_a, r_k, lnx_g, lnx_b, w_proj_a, conv_w, conv_b,
           cln_g, cln_b, w_proj_b, w_out, w_ple, w_ple_gate, final_g):
    depth = w_in.shape[0]
    batch, seq, d = x_prompt.shape
    dec_batch, dec_seq, _ = x_sample.shape
    d_a = w_proj_a.shape[1]
    d_b = w_proj_b.shape[1]
    shift_w = shift_mu.shape[1]
    n_heads = d_a // HEAD
    n_pairs = d_a // PAIR
    taps = conv_w.shape[1]
    assert depth == 1 and dec_seq == 1 and d_a == d_b and 2 * d_a == d
    assert shift_w == 3 * d_a + 2 * LORA and d_a % PAIR == 0 and seq % CHUNK == 0

    o1 = shift_w
    o2 = o1 + d_a
    o3 = o2 + 2 * d_b
    o4 = o3 + d_b
    w = w_in[0]
    lora_w = jnp.zeros((2 * LORA, 2 * d_a), F32)
    lora_w = lora_w.at[:LORA, :d_a].set(w_lora_b[0]).at[LORA:, d_a:].set(a_lora_b[0]).astype(BF16)
    zeros_a = jnp.zeros((d_a,), F32)
    par = jnp.stack([w0[0], a0[0], k_k[0], k_a[0], r_k[0].reshape(d_a), lnx_g[0], lnx_b[0], zeros_a])
    cln = jnp.stack([cln_g[0], cln_b[0]])
    mu = shift_mu
    g_in = norm_g[0].reshape(1, d)
    fg = final_g.reshape(1, d)

    m_p = batch * seq
    m_all = m_p + dec_batch
    x2 = x_prompt.reshape(m_p, d)
    xs2 = x_sample.reshape(dec_batch, d)
    n_rkv = 3 * d_a
    xn, p_lora = _rmsnorm_bf16(x2, xs2, g_in, w, n_rkv, 2 * LORA, _largest_tile(m_p, 1024, SUBLANES))
    tm = _largest_tile(m_all, 1100, BF16_SUBLANES)
    n_plain = n_rkv // d_a
    gate_a_block = n_plain
    merge_block = (n_rkv + d_a) // d
    gate_b_block = (n_rkv + d_a + 2 * d) // d_b
    assert n_rkv % d_a == 0 and (n_rkv + d_a) % d == 0
    proj_cols = lambda j: jnp.where(j < n_plain, j * d_a,
                                    jnp.where(j == gate_a_block, o1,
                                              jnp.where(j == gate_b_block, o3, o4 + (j - gate_a_block - 1) * d_a)))
    proj, (wa, wb, wout, wpg) = _project(xn, w, proj_cols, n_rkv + d_a + 2 * d + d_b, tm, d_a, n_plain,
                                         silu_blocks=(gate_a_block, gate_b_block),
                                         casts=[w_proj_a[0], w_proj_b[0], w_out[0], w_ple_gate[0]])
    p_rkv = gates = proj
    u, (wple,) = _project_glu(xn, w, o2, o2 + d_b, d_b, tm, _largest_tile(d_b, 512, LANES), casts=[w_ple[0]])

    oa, st = _wkv_prompt(p_rkv, p_lora, gates, mu, par, lora_w, batch, seq, d_a, gate_a_block)
    c = _conv_prompt(u, conv_w[0], conv_b[0], batch, seq, _largest_tile(seq, 512, CONV_STRIP))
    y_prompt = _tail(oa, c, gates, x2, p_prompt[0].reshape(m_p, -1), cln, wa, wb, wout, wpg, wple, fg,
                     _largest_tile(m_p, 256, BF16_SUBLANES), gate_b_block, merge_block).reshape(batch, seq, d)
    last_rows = lambda a: jnp.concatenate([a[(b + 1) * seq - 1:(b + 1) * seq] for b in range(batch)])
    new_shift_p = jnp.concatenate([last_rows(p_rkv)[:, :n_rkv], last_rows(p_lora)], axis=1)[None]
    st = st.reshape(batch, n_pairs, 2, HEAD, 2, HEAD)
    new_wkv_p = jnp.stack([st[:, :, 0, :, 0, :], st[:, :, 1, :, 1, :]], axis=2)
    new_wkv_p = jnp.swapaxes(new_wkv_p.reshape(batch, n_heads, HEAD, HEAD), -1, -2)[None]
    new_conv_p = jnp.stack([u[(b + 1) * seq - (taps - 1):(b + 1) * seq] for b in range(batch)])[None]

    tm_s = _largest_tile(dec_batch, 256, BF16_SUBLANES)
    assert m_p % tm_s == 0
    oa_s, new_wkv_t = _wkv_sample(p_rkv, p_lora, state_shift[0], gates, mu, par, lora_w,
                                  jnp.transpose(state_wkv[0], (1, 2, 3, 0)), d_a, m_p, gate_a_block)
    new_wkv_s = jnp.transpose(new_wkv_t, (3, 0, 1, 2))
    c_s, new_conv_t = _conv_sample(u, jnp.transpose(state_conv[0], (1, 0, 2)), conv_w[0], conv_b[0],
                                   _largest_tile(dec_batch, 32, SUBLANES), m_p)
    new_conv_s = jnp.transpose(new_conv_t, (1, 0, 2))
    y_sample = _tail(oa_s, c_s, gates, xs2, p_sample[0].reshape(dec_batch, -1), cln, wa, wb, wout, wpg,
                     wple, fg, tm_s, gate_b_block, merge_block, row0=m_p).reshape(dec_batch, 1, d)
    new_shift_s = jnp.concatenate([p_rkv[m_p:, :n_rkv], p_lora[m_p:]], axis=1)[None]
    new_conv_s = new_conv_s[None]

    return (y_prompt, y_sample, new_shift_p, new_wkv_p, new_conv_p, new_shift_s, new_wkv_s[None],
            new_conv_s)
```

```python
import functools
import math

import jax
import jax.numpy as jnp
from jax import lax
from jax.experimental import pallas as pl
from jax.experimental.pallas import tpu as pltpu

F32 = jnp.float32
BF16 = jnp.bfloat16

LANES = 128
SUBLANES = 8
BF16_SUBLANES = 16

HEAD = 64
PAIR = 2 * HEAD
LORA = 64
CHUNK = 64
WKV_SEQS_PER_STEP = 4
RMS_EPS = 1e-6
LN_EPS = 1e-5
GN_EPS = 64e-5
LOG2_DECAY_SCALE = math.exp(-0.5) / math.log(2.0)
VMEM_LIMIT = 56 * 1024 * 1024


def _params(n_axes, vmem=VMEM_LIMIT):
    return pltpu.CompilerParams(dimension_semantics=("arbitrary",) * n_axes, vmem_limit_bytes=vmem)


def _sigmoid(x):
    return 0.5 * jnp.tanh(0.5 * x) + 0.5


def _dot(a, b):
    return jnp.dot(a, b, preferred_element_type=F32)


def _dot_nt(a, b):
    return lax.dot_general(a, b, (((1,), (1,)), ((), ())), preferred_element_type=F32)


def _dot_tn(a, b):
    return lax.dot_general(a, b, (((0,), (0,)), ((), ())), preferred_element_type=F32)


def _rmsnorm_kernel(xp_ref, xs_ref, g_ref, w_ref, o_ref, ol_ref, wb_ref, *, n_prompt_blocks):
    i = pl.program_id(0)

    @pl.when(i == 0)
    def _():
        wb_ref[...] = w_ref[...].astype(BF16)

    def norm(x):
        ms = jnp.mean(x * x, axis=-1, keepdims=True)
        return (x * lax.rsqrt(ms + RMS_EPS) * g_ref[...]).astype(o_ref.dtype)

    @pl.when(i < n_prompt_blocks)
    def _():
        xn = norm(xp_ref[...])
        o_ref[...] = xn
        ol_ref[...] = _dot(xn, wb_ref[...])

    @pl.when(i == n_prompt_blocks)
    def _():
        xn = norm(xs_ref[...])
        o_ref[0:xs_ref.shape[0], :] = xn
        ol_ref[0:xs_ref.shape[0], :] = _dot(xn, wb_ref[...])


def _rmsnorm_bf16(x_prompt, x_sample, g, w, col, n_cols, tm):
    m_p, d = x_prompt.shape
    m_s = x_sample.shape[0]
    n_blocks = m_p // tm
    assert m_p % tm == 0 and m_s <= tm and col % LANES == 0
    kern = functools.partial(_rmsnorm_kernel, n_prompt_blocks=n_blocks)
    return pl.pallas_call(
        kern,
        out_shape=(jax.ShapeDtypeStruct((m_p + m_s, d), BF16),
                   jax.ShapeDtypeStruct((m_p + m_s, n_cols), F32)),
        grid=(n_blocks + 1,),
        in_specs=[pl.BlockSpec((tm, d), lambda i: (jnp.minimum(i, n_blocks - 1), 0)),
                  pl.BlockSpec((m_s, d), lambda i: (0, 0)),
                  pl.BlockSpec((1, d), lambda i: (0, 0)),
                  pl.BlockSpec((pl.Element(d), pl.Element(n_cols)), lambda i: (0, col))],
        out_specs=(pl.BlockSpec((tm, d), lambda i: (i, 0)),
                   pl.BlockSpec((tm, n_cols), lambda i: (i, 0))),
        scratch_shapes=[pltpu.VMEM((d, n_cols), BF16)],
        compiler_params=_params(1),
        name="rmsnorm",
    )(x_prompt, x_sample, g, w)


def _proj_kernel(x_ref, w_hbm, o_ref, wf_ref, wb_ref, sem, *, n_plain, silu_blocks, col_start):
    j = pl.program_id(0)
    tn = wb_ref.shape[1]

    def fetch(jj, slot):
        cols = pl.ds(pl.multiple_of(col_start(jj), LANES), tn)
        return pltpu.make_async_copy(w_hbm.at[:, cols], wf_ref.at[slot], sem.at[slot])

    @pl.when(pl.program_id(1) == 0)
    def _():
        @pl.when(j == 0)
        def _():
            fetch(0, 0).start()

        @pl.when(j + 1 < pl.num_programs(0))
        def _():
            fetch(j + 1, (j + 1) % 2).start()

        fetch(j, j % 2).wait()
        wb_ref[...] = wf_ref[j % 2].astype(BF16)

    @pl.when(j < n_plain)
    def _():
        o_ref[...] = _dot(x_ref[...], wb_ref[...])

    @pl.when(j >= n_plain)
    def _():
        y = _dot(x_ref[...], wb_ref[...])
        s = _sigmoid(y)
        is_silu = functools.reduce(lambda p, q: p | q, [j == b for b in silu_blocks])
        o_ref[...] = jnp.where(is_silu, y * s, s)


def _project(xn, w, col_start, n_out, tm, tn, n_plain, silu_blocks):
    m, d = xn.shape
    kern = functools.partial(_proj_kernel, n_plain=n_plain, silu_blocks=silu_blocks, col_start=col_start)
    return pl.pallas_call(
        kern,
        out_shape=jax.ShapeDtypeStruct((m, n_out), F32),
        grid=(n_out // tn, m // tm),
        in_specs=[pl.BlockSpec((tm, d), lambda j, i: (i, 0)),
                  pl.BlockSpec(memory_space=pl.ANY)],
        out_specs=pl.BlockSpec((tm, tn), lambda j, i: (i, j)),
        scratch_shapes=[pltpu.VMEM((2, d, tn), F32), pltpu.VMEM((d, tn), BF16), pltpu.SemaphoreType.DMA((2,))],
        compiler_params=_params(2),
        name="in_proj",
    )(xn, w)


def _glu_kernel(*refs, n_casts):
    x_ref, wa_ref, wb_ref = refs[:3]
    cast_in = refs[3:3 + n_casts]
    o_ref = refs[3 + n_casts]
    cast_out = refs[4 + n_casts:4 + 2 * n_casts]
    wab_ref, wbb_ref = refs[4 + 2 * n_casts:]

    @pl.when(pl.program_id(1) == 0)
    def _():
        wab_ref[...] = wa_ref[...].astype(BF16)
        wbb_ref[...] = wb_ref[...].astype(BF16)

    x = x_ref[...]
    o_ref[...] = _dot(x, wab_ref[...]) * _sigmoid(_dot(x, wbb_ref[...]))
    for src, dst in zip(cast_in, cast_out):
        dst[...] = src[...].astype(BF16)


def _project_glu(xn, w, col_a, col_b, n_out, tm, tn, casts):
    m, d = xn.shape
    nj, ni = n_out // tn, m // tm
    steps = nj * ni
    for a in casts:
        assert a.shape[0] % (BF16_SUBLANES * steps) == 0, (a.shape, steps)
    step_rows = lambda a: pl.BlockSpec((a.shape[0] // steps, a.shape[1]), lambda j, i: (j * ni + i, 0))
    w_cols = lambda col: pl.BlockSpec((pl.Element(d), pl.Element(tn)),
                                      lambda j, i: (0, pl.multiple_of(col + j * tn, LANES)))
    kern = functools.partial(_glu_kernel, n_casts=len(casts))
    out = pl.pallas_call(
        kern,
        out_shape=[jax.ShapeDtypeStruct((m, n_out), F32)]
        + [jax.ShapeDtypeStruct(a.shape, BF16) for a in casts],
        grid=(nj, ni),
        in_specs=[pl.BlockSpec((tm, d), lambda j, i: (i, 0)), w_cols(col_a), w_cols(col_b)]
        + [step_rows(a) for a in casts],
        out_specs=[pl.BlockSpec((tm, tn), lambda j, i: (i, j))] + [step_rows(a) for a in casts],
        scratch_shapes=[pltpu.VMEM((d, tn), BF16), pltpu.VMEM((d, tn), BF16)],
        compiler_params=_params(2),
        name="in_proj_glu",
    )(xn, w, w, *casts)
    return out[0], out[1:]


def _head_sum(x, block_ones, split=False):
    hi = x.astype(BF16)
    if not split:
        return _dot(hi, block_ones)
    lo = (x - hi.astype(F32)).astype(BF16)
    return _dot(hi, block_ones) + _dot(lo, block_ones)


def _head_sums(xs, block_ones, split=False):
    rows = xs[0].shape[0]
    s = _head_sum(jnp.concatenate(xs, axis=0), block_ones, split)
    return [s[i * rows:(i + 1) * rows] for i in range(len(xs))]


def _block_ones(value=1.0):
    ri = lax.broadcasted_iota(jnp.int32, (PAIR, PAIR), 0)
    ci = lax.broadcasted_iota(jnp.int32, (PAIR, PAIR), 1)
    return jnp.where((ri < HEAD) == (ci < HEAD), value, 0.0).astype(BF16)


def _token_shift(p, prev, mu):
    return p + mu * (prev - p)


def _lora_out(z, lora_w):
    lane = lax.broadcasted_iota(jnp.int32, z.shape, 1)
    z = jnp.where(lane < LORA, jnp.tanh(z), z)
    return _dot(z.astype(BF16), lora_w)


def _pair_slices(d_a):
    return [slice(i * PAIR, (i + 1) * PAIR) for i in range(d_a // PAIR)]


def _pair_vectors(xs_list, lo_list, par_ref, d_a, block_ones):
    sls = _pair_slices(d_a)
    shifted = lambda sl, off: slice(off + sl.start, off + sl.stop)
    items = [(xs, lo, sl) for xs, lo in zip(xs_list, lo_list) for sl in sls]
    r = [xs[:, sl] for xs, _, sl in items]
    k = [xs[:, shifted(sl, d_a)] for xs, _, sl in items]
    v = [xs[:, shifted(sl, 2 * d_a)] for xs, _, sl in items]
    kkr = [ki * par_ref[2:3, sl] for ki, (_, _, sl) in zip(k, items)]
    ss = _head_sums([x * x for x in kkr], block_ones)
    kk = [x * jnp.minimum(lax.rsqrt(s), 1e12) for x, s in zip(kkr, ss)]
    log2_decay = [-LOG2_DECAY_SCALE * _sigmoid(par_ref[0:1, sl] + lo[:, sl]) for _, lo, sl in items]
    a = [_sigmoid(par_ref[1:2, sl] + lo[:, shifted(sl, d_a)]) for _, lo, sl in items]
    kf = [ki * (1.0 + (ai - 1.0) * par_ref[3:4, sl]) for ki, ai, (_, _, sl) in zip(k, a, items)]
    return r, kf, v, log2_decay, kk, a


def _head_norm_bonus(y, r, kf, v, par_ref, d_a, block_ones):
    sls = _pair_slices(d_a) * (len(y) // (d_a // PAIR))
    block_mean = _block_ones(1.0 / HEAD)
    mean = _head_sums(y, block_mean, split=True)
    d = [x - m for x, m in zip(y, mean)]
    var = _head_sums([x * x for x in d], block_mean)
    rk = _head_sums([ri * ki * par_ref[4:5, sl] for ri, ki, sl in zip(r, kf, sls)], block_ones)
    return [x * lax.rsqrt(vr + GN_EPS) * par_ref[5:6, sl] + par_ref[6:7, sl] + s * vi
            for x, vr, s, vi, sl in zip(d, var, rk, v, sls)]


def _stack_heads(x, lane_lo):
    xb = x.astype(BF16)
    zero = jnp.zeros_like(xb)
    return jnp.concatenate([jnp.where(lane_lo, xb, zero), jnp.where(lane_lo, zero, xb)], axis=0)


def _cumsum_rows(xs):
    n = xs[0].shape[0]
    tri = (lax.broadcasted_iota(jnp.int32, (n, n), 1) <= lax.broadcasted_iota(jnp.int32, (n, n), 0))
    tri = jnp.where(tri, 1.0, 0.0).astype(BF16)
    x = jnp.concatenate(xs, axis=1)
    hi = x.astype(BF16)
    lo = (x - hi.astype(F32)).astype(BF16)
    s = _dot(tri, hi) + _dot(tri, lo)
    return [s[:, i * PAIR:(i + 1) * PAIR] for i in range(len(xs))]


def _wkv_chunk_kernel(*refs, d_a, nb):
    ps_refs, pl_refs, sga_refs = refs[:nb], refs[nb:2 * nb], refs[2 * nb:3 * nb]
    mu_ref, par_ref, lora_ref, oa_ref, st_ref, carry_ref = refs[3 * nb:]
    c = pl.program_id(1)
    n_pairs = d_a // PAIR
    C = CHUNK
    n_rkv = 3 * d_a

    @pl.when(c == 0)
    def _():
        carry_ref[...] = jnp.zeros_like(carry_ref)
        st_ref[...] = jnp.zeros_like(st_ref)

    def shifted(p, prev_row, mu):
        row = lax.broadcasted_iota(jnp.int32, p.shape, 0)
        return _token_shift(p, jnp.where(row == 0, prev_row, pltpu.roll(p, 1, axis=0)), mu)

    xs, lo = [], []
    for bi in range(nb):
        p, pz = ps_refs[bi][...], pl_refs[bi][...]
        xs.append(shifted(p, carry_ref[bi, :, :n_rkv], mu_ref[:, :n_rkv]))
        lo.append(_lora_out(shifted(pz, carry_ref[bi, :, n_rkv:], mu_ref[:, n_rkv:]), lora_ref[...]))
        carry_ref[bi, :, :n_rkv] = p[C - 1:C, :]
        carry_ref[bi, :, n_rkv:] = pz[C - 1:C, :]

    block_ones = _block_ones()
    lane_lo = lax.broadcasted_iota(jnp.int32, (C, PAIR), 1) < HEAD
    ti = lax.broadcasted_iota(jnp.int32, (C, PAIR), 0)
    si = lax.broadcasted_iota(jnp.int32, (C, PAIR), 1) % C
    strict_lower = si < ti
    lower2 = (lax.broadcasted_iota(jnp.int32, (C, 2 * PAIR), 1) % C
              <= lax.broadcasted_iota(jnp.int32, (C, 2 * PAIR), 0))
    eye = jnp.where(si == ti, 1.0, 0.0)
    cat0 = lambda x, y: jnp.concatenate([x, y], axis=0)

    r, kf, v, log2_decay, kk, a = _pair_vectors(xs, lo, par_ref, d_a, block_ones)
    cl = _cumsum_rows(log2_decay)
    cend = [x[C - 1:C, :] for x in cl]
    e_neg = [jnp.exp2(-x) for x in cl]
    kka = [x * y for x, y in zip(kk, a)]
    stack = lambda xs_: [_stack_heads(x, lane_lo) for x in xs_]
    a_p = [(-x * jnp.exp2(c_ - ld)).astype(BF16) for x, c_, ld in zip(kk, cl, log2_decay)]
    r_p = [(x * jnp.exp2(c_)).astype(BF16) for x, c_ in zip(r, cl)]
    a_s = stack(a_p)
    bk_s = [cat0(x, y) for x, y in zip(stack([x * e for x, e in zip(kka, e_neg)]),
                                       stack([x * e for x, e in zip(kf, e_neg)]))]
    v_s = stack(v)
    pend_col = [jnp.transpose(jnp.broadcast_to(jnp.exp2(ce), (PAIR, PAIR))) for ce in cend]

    m1 = [_dot_nt(cat0(x, y), z) for x, y, z in zip(a_p, r_p, bk_s)]
    a_ab = [jnp.where(strict_lower, m[:C, :PAIR], 0.0) for m in m1]
    keep = lambda mask, x: jnp.where(mask, x, jnp.zeros_like(x))
    a_ak = [keep(strict_lower, m[:C, PAIR:].astype(BF16)) for m in m1]
    a_rbk = [keep(lower2, m[C:].astype(BF16)) for m in m1]
    g = [_dot(x, y) for x, y in zip(a_ak, v_s)]

    apow = [_dot(x.astype(BF16), _stack_heads(x, lane_lo)) for x in a_ab]
    tinv = [eye + x for x in a_ab]
    n_levels = CHUNK.bit_length() - 1
    for lvl in range(1, n_levels):
        rhs = stack(apow)
        if lvl < n_levels - 1:
            both = [_dot(cat0(x.astype(BF16), t.astype(BF16)), y) for x, t, y in zip(apow, tinv, rhs)]
            apow = [x[:C] for x in both]
            tinv = [t + x[C:] for t, x in zip(tinv, both)]
        else:
            tinv = [t + _dot(t.astype(BF16), y) for t, y in zip(tinv, rhs)]

    g_s = stack(g)
    wx = [_dot(t.astype(BF16), jnp.concatenate([x, y], axis=1)) for t, x, y in zip(tinv, a_s, g_s)]
    seq_pair = [(bi, pi) for bi in range(nb) for pi in range(n_pairs)]
    st = [st_ref[bi, pi] for bi, pi in seq_pair]
    uy = [_dot(cat0(w[:, :PAIR].astype(BF16), x), s_.astype(BF16)) for w, x, s_ in zip(wx, r_p, st)]
    uv_s = [cat0(_stack_heads(x[:C] + w[:, PAIR:], lane_lo), y) for x, w, y in zip(uy, wx, v_s)]
    st_upd = [_dot_tn(x, z) for x, z in zip(bk_s, uv_s)]
    y_uv = [_dot(x, z) for x, z in zip(a_rbk, uv_s)]
    for i, (bi, pi) in enumerate(seq_pair):
        st_ref[bi, pi] = pend_col[i] * (st[i] + st_upd[i])
    y = [x[C:] + z for x, z in zip(uy, y_uv)]
    o = _head_norm_bonus(y, r, kf, v, par_ref, d_a, block_ones)
    sls = _pair_slices(d_a)
    for oi, (bi, pi) in zip(o, seq_pair):
        oa_ref[bi, :, sls[pi]] = (oi * sga_refs[bi][:, sls[pi]]).astype(oa_ref.dtype)


def _wkv_prompt(p_rkv, p_lora, gates, mu, par, lora_w, batch, seq, d_a, gate_a_block):
    n_chunks = seq // CHUNK
    shift_w = mu.shape[1]
    n_pairs = d_a // PAIR
    nb = WKV_SEQS_PER_STEP if batch % WKV_SEQS_PER_STEP == 0 else 1
    kern = functools.partial(_wkv_chunk_kernel, d_a=d_a, nb=nb)
    seq_rows = lambda bi, col=0: (lambda b, c: ((b * nb + bi) * n_chunks + c, col))
    oa, st = pl.pallas_call(
        kern,
        out_shape=(jax.ShapeDtypeStruct((batch, seq, d_a), BF16),
                   jax.ShapeDtypeStruct((batch, n_pairs, PAIR, PAIR), F32)),
        grid=(batch // nb, n_chunks),
        in_specs=([pl.BlockSpec((CHUNK, 3 * d_a), seq_rows(bi)) for bi in range(nb)]
                  + [pl.BlockSpec((CHUNK, 2 * LORA), seq_rows(bi)) for bi in range(nb)]
                  + [pl.BlockSpec((CHUNK, d_a), seq_rows(bi, gate_a_block)) for bi in range(nb)]
                  + [pl.BlockSpec((1, shift_w), lambda b, c: (0, 0)),
                     pl.BlockSpec((8, d_a), lambda b, c: (0, 0)),
                     pl.BlockSpec((2 * LORA, 2 * d_a), lambda b, c: (0, 0))]),
        out_specs=(pl.BlockSpec((nb, CHUNK, d_a), lambda b, c: (b, c, 0)),
                   pl.BlockSpec((nb, n_pairs, PAIR, PAIR), lambda b, c: (b, 0, 0, 0))),
        scratch_shapes=[pltpu.VMEM((nb, 1, shift_w), F32)],
        compiler_params=_params(2),
        name="wkv_chunked",
    )(*([p_rkv] * nb + [p_lora] * nb + [gates] * nb), mu, par, lora_w)
    return oa.reshape(batch * seq, d_a), st


STEP_ROWS = 8


def _wkv_step_kernel(ps_ref, pl_ref, prev_ref, sga_ref, mu_ref, par_ref, lora_ref, s_ref, oa_ref, so_ref,
                     vt_ref, rkv_ref, y_ref, *, d_a):
    h = pl.program_id(0)
    n_heads = d_a // HEAD
    n_rkv = 3 * d_a
    sls = _pair_slices(d_a)

    @pl.when(h == 0)
    def _():
        xs = _token_shift(ps_ref[...], prev_ref[:, :n_rkv], mu_ref[:, :n_rkv])
        lo = _lora_out(_token_shift(pl_ref[...], prev_ref[:, n_rkv:], mu_ref[:, n_rkv:]), lora_ref[...])
        r, kf, v, log2_decay, kk, a = _pair_vectors([xs], [lo], par_ref, d_a, _block_ones())
        for pi, sl in enumerate(sls):
            vecs = (-kk[pi], jnp.exp2(log2_decay[pi]), kk[pi] * a[pi], kf[pi], r[pi], v[pi])
            for i, x in enumerate(vecs):
                vt_ref[i, sl, :] = x.T
            for i, x in enumerate((r[pi], kf[pi], v[pi])):
                rkv_ref[i, :, sl] = x

    base = pl.multiple_of(h * HEAD, HEAD)
    a_t, w_t, b_t, k_t, r_t = (vt_ref[i, pl.ds(base, HEAD), :] for i in range(5))

    def rows(g, carry):
        i0 = pl.multiple_of(g * STEP_ROWS, STEP_ROWS)
        row0 = pl.multiple_of(base + i0, STEP_ROWS)
        v_rows = vt_ref[5, pl.ds(row0, STEP_ROWS), :]
        ys = []
        for ii in range(STEP_ROWS):
            s = s_ref[0, i0 + ii]
            sa = jnp.sum(s * a_t, axis=0, keepdims=True)
            s_new = s * w_t + sa * b_t + v_rows[ii:ii + 1, :] * k_t
            so_ref[0, i0 + ii] = s_new
            ys.append(jnp.sum(s_new * r_t, axis=0, keepdims=True))
        y_ref[pl.ds(row0, STEP_ROWS), :] = jnp.concatenate(ys, axis=0)
        return carry

    lax.fori_loop(0, HEAD // STEP_ROWS, rows, 0)

    @pl.when(h == n_heads - 1)
    def _():
        y = [y_ref[sl, :].T for sl in sls]
        r, kf, v = ([rkv_ref[i, :, sl] for sl in sls] for i in range(3))
        o = _head_norm_bonus(y, r, kf, v, par_ref, d_a, _block_ones())
        for oi, sl in zip(o, sls):
            oa_ref[:, sl] = (oi * sga_ref[:, sl]).astype(oa_ref.dtype)


def _wkv_sample(p_rkv, p_lora, shift_prev, gates, mu, par, lora_w, state_t, d_a, row0, gate_a_block):
    batch, shift_w = shift_prev.shape
    n_heads = d_a // HEAD
    assert row0 % batch == 0 and state_t.shape == (n_heads, HEAD, HEAD, batch)
    blk0 = row0 // batch
    const = lambda h: (0, 0)
    kern = functools.partial(_wkv_step_kernel, d_a=d_a)
    return pl.pallas_call(
        kern,
        out_shape=(jax.ShapeDtypeStruct((batch, d_a), BF16),
                   jax.ShapeDtypeStruct(state_t.shape, F32)),
        grid=(n_heads,),
        in_specs=[pl.BlockSpec((batch, 3 * d_a), lambda h: (blk0, 0)),
                  pl.BlockSpec((batch, 2 * LORA), lambda h: (blk0, 0)),
                  pl.BlockSpec((batch, shift_w), const),
                  pl.BlockSpec((batch, d_a), lambda h: (blk0, gate_a_block)),
                  pl.BlockSpec((1, shift_w), const),
                  pl.BlockSpec((8, d_a), const),
                  pl.BlockSpec((2 * LORA, 2 * d_a), const),
                  pl.BlockSpec((1, HEAD, HEAD, batch), lambda h: (h, 0, 0, 0))],
        out_specs=(pl.BlockSpec((batch, d_a), const),
                   pl.BlockSpec((1, HEAD, HEAD, batch), lambda h: (h, 0, 0, 0))),
        scratch_shapes=[pltpu.VMEM((6, d_a, batch), F32), pltpu.VMEM((3, batch, d_a), F32),
                        pltpu.VMEM((d_a, batch), F32)],
        compiler_params=_params(1),
        name="wkv_step",
    )(p_rkv, p_lora, shift_prev, gates, mu, par, lora_w, state_t)


CONV_PAD = 32
CONV_STRIP = 16


def _conv_prompt_kernel(u_ref, w_ref, b_ref, o_ref, buf_ref, c_ref, *, tt, taps):
    t = pl.program_id(1)
    sub = buf_ref.shape[1]
    lanes = [slice(s * LANES, (s + 1) * LANES) for s in range(sub)]

    @pl.when(t == 0)
    def _():
        buf_ref[0:CONV_PAD] = jnp.zeros((CONV_PAD,) + buf_ref.shape[1:], F32)

    u = u_ref[...]
    buf_ref[CONV_PAD:CONV_PAD + tt] = jnp.swapaxes(jnp.stack([u[:, ls] for ls in lanes], axis=0), 0, 1)
    off = CONV_PAD - (taps - 1)
    w = [w_ref[k] for k in range(taps)]
    bias = b_ref[0]

    def strip(s, carry):
        t0 = s * CONV_STRIP
        acc = [bias] * CONV_STRIP
        for i in range(CONV_STRIP + taps - 1):
            x = buf_ref[off + t0 + i]
            for j in range(CONV_STRIP):
                if 0 <= i - j < taps:
                    acc[j] = acc[j] + w[i - j] * x
        for j in range(CONV_STRIP):
            c_ref[t0 + j] = acc[j]
        return carry

    lax.fori_loop(0, tt // CONV_STRIP, strip, 0, unroll=4)
    buf_ref[0:CONV_PAD] = buf_ref[tt:tt + CONV_PAD]
    c = jnp.swapaxes(c_ref[...], 0, 1)
    for s, ls in enumerate(lanes):
        o_ref[:, ls] = c[s]


def _conv_prompt(u, conv_w, conv_b, batch, seq, tt):
    d_b = u.shape[1]
    taps = conv_w.shape[0]
    nt = seq // tt
    sub = d_b // LANES
    assert taps - 1 <= CONV_PAD and tt % CONV_STRIP == 0 and tt >= CONV_PAD
    kern = functools.partial(_conv_prompt_kernel, tt=tt, taps=taps)
    return pl.pallas_call(
        kern,
        out_shape=jax.ShapeDtypeStruct((batch * seq, d_b), F32),
        grid=(batch, nt),
        in_specs=[pl.BlockSpec((tt, d_b), lambda b, t: (b * nt + t, 0)),
                  pl.BlockSpec((taps, sub, LANES), lambda b, t: (0, 0, 0)),
                  pl.BlockSpec((1, sub, LANES), lambda b, t: (0, 0, 0))],
        out_specs=pl.BlockSpec((tt, d_b), lambda b, t: (b * nt + t, 0)),
        scratch_shapes=[pltpu.VMEM((tt + CONV_PAD, sub, LANES), F32), pltpu.VMEM((tt, sub, LANES), F32)],
        compiler_params=_params(2),
        name="conv_prompt",
    )(u, conv_w.reshape(taps, sub, LANES), conv_b.reshape(1, sub, LANES))


def _conv_step_kernel(u_ref, prev_ref, w_ref, b_ref, o_ref, hist_ref, *, taps):
    u = u_ref[...]
    c = b_ref[...] + w_ref[taps - 1:taps, :] * u
    for k in range(taps - 1):
        c = c + w_ref[k:k + 1, :] * prev_ref[k]
    o_ref[...] = c
    hist_ref[0:taps - 2] = prev_ref[1:taps - 1]
    hist_ref[taps - 2] = u


def _conv_sample(u, conv_prev_t, conv_w, conv_b, bb, row0):
    batch = conv_prev_t.shape[1]
    d_b = u.shape[1]
    taps = conv_w.shape[0]
    assert row0 % bb == 0 and batch % bb == 0
    blk0 = row0 // bb
    kern = functools.partial(_conv_step_kernel, taps=taps)
    return pl.pallas_call(
        kern,
        out_shape=(jax.ShapeDtypeStruct((batch, d_b), F32),
                   jax.ShapeDtypeStruct(conv_prev_t.shape, F32)),
        grid=(batch // bb,),
        in_specs=[pl.BlockSpec((bb, d_b), lambda i: (blk0 + i, 0)),
                  pl.BlockSpec((taps - 1, bb, d_b), lambda i: (0, i, 0)),
                  pl.BlockSpec((taps, d_b), lambda i: (0, 0)),
                  pl.BlockSpec((1, d_b), lambda i: (0, 0))],
        out_specs=(pl.BlockSpec((bb, d_b), lambda i: (i, 0)),
                   pl.BlockSpec((taps - 1, bb, d_b), lambda i: (0, i, 0))),
        compiler_params=_params(1),
        name="conv_step",
    )(u, conv_prev_t, conv_w, conv_b.reshape(1, d_b))


def _tail_kernel(oa_ref, c_ref, gb_ref, sga_ref, sgb_ref, x_ref, p_ref, cln_ref, wa_ref, wb_ref, wout_ref,
                 wpg_ref, wple_ref, fg_ref, o_ref):
    c = c_ref[...]
    mean = jnp.mean(c, axis=-1, keepdims=True)
    dc = c - mean
    var = jnp.mean(dc * dc, axis=-1, keepdims=True)
    cf = dc * lax.rsqrt(var + LN_EPS) * cln_ref[0:1, :] + cln_ref[1:2, :]
    cb = (cf * _sigmoid(cf) * gb_ref[...]).astype(BF16)
    m = sga_ref[...] * _dot(oa_ref[...], wa_ref[...]) + sgb_ref[...] * _dot(cb, wb_ref[...])
    h = x_ref[...] + _dot(m.astype(BF16), wout_ref[...])
    gate = _sigmoid(_dot(h.astype(BF16), wpg_ref[...]))
    h = h + gate * _dot(p_ref[...].astype(BF16), wple_ref[...])
    ms = jnp.mean(h * h, axis=-1, keepdims=True)
    o_ref[...] = h * lax.rsqrt(ms + RMS_EPS) * fg_ref[...]


def _tail_stream_kernel(oa_ref, c_ref, gb_ref, sga_ref, sgb_ref, x_ref, p_ref, cln_ref, wa_hbm, wb_hbm, wout_hbm,
                        wpg_hbm, wple_hbm, fg_ref, o_ref, wa_ref, wb_ref, wout_ref, wpg_ref, wple_ref, sem):
    copies = [pltpu.make_async_copy(src, dst, sem.at[k]) for k, (src, dst) in enumerate(
        zip((wa_hbm, wb_hbm, wout_hbm, wpg_hbm, wple_hbm), (wa_ref, wb_ref, wout_ref, wpg_ref, wple_ref)))]
    for cp in copies:
        cp.start()
    c = c_ref[...]
    mean = jnp.mean(c, axis=-1, keepdims=True)
    dc = c - mean
    var = jnp.mean(dc * dc, axis=-1, keepdims=True)
    cf = dc * lax.rsqrt(var + LN_EPS) * cln_ref[0:1, :] + cln_ref[1:2, :]
    cb = (cf * _sigmoid(cf) * gb_ref[...]).astype(BF16)
    copies[0].wait()
    ya = _dot(oa_ref[...], wa_ref[...])
    copies[1].wait()
    m = sga_ref[...] * ya + sgb_ref[...] * _dot(cb, wb_ref[...])
    copies[2].wait()
    h = x_ref[...] + _dot(m.astype(BF16), wout_ref[...])
    copies[3].wait()
    gate = _sigmoid(_dot(h.astype(BF16), wpg_ref[...]))
    copies[4].wait()
    h = h + gate * _dot(p_ref[...].astype(BF16), wple_ref[...])
    ms = jnp.mean(h * h, axis=-1, keepdims=True)
    o_ref[...] = h * lax.rsqrt(ms + RMS_EPS) * fg_ref[...]


def _tail(oa, c, gates, x, p, cln, wa, wb, wout, wpg, wple, fg, tm, gate_b_block, merge_block, row0=0):
    m, d = x.shape
    d_a = oa.shape[1]
    d_b = c.shape[1]
    ple = p.shape[1]
    blk0 = row0 // tm
    row = lambda i: (i, 0)
    const = lambda i: (0, 0)
    resident = lambda shape: pl.BlockSpec(shape, const, pipeline_mode=pl.Buffered(1))
    w_shapes = ((d_a, d), (d_b, d), (d, d), (d, d), (ple, d))
    single_step = m == tm
    if single_step:
        w_specs = [pl.BlockSpec(memory_space=pl.ANY)] * len(w_shapes)
        scratch = [pltpu.VMEM(s, BF16) for s in w_shapes] + [pltpu.SemaphoreType.DMA((len(w_shapes),))]
    else:
        w_specs, scratch = [resident(s) for s in w_shapes], []
    return pl.pallas_call(
        _tail_stream_kernel if single_step else _tail_kernel,
        out_shape=jax.ShapeDtypeStruct((m, d), F32),
        grid=(m // tm,),
        in_specs=[pl.BlockSpec((tm, d_a), row), pl.BlockSpec((tm, d_b), row),
                  pl.BlockSpec((tm, d_b), lambda i: (blk0 + i, gate_b_block)),
                  pl.BlockSpec((tm, d), lambda i: (blk0 + i, merge_block)),
                  pl.BlockSpec((tm, d), lambda i: (blk0 + i, merge_block + 1)),
                  pl.BlockSpec((tm, d), row), pl.BlockSpec((tm, ple), row),
                  resident((2, d_b))] + w_specs + [resident((1, d))],
        out_specs=pl.BlockSpec((tm, d), row),
        scratch_shapes=scratch,
        compiler_params=_params(1),
        name="tail",
    )(oa, c, gates, gates, gates, x, p, cln, wa, wb, wout, wpg, wple, fg)


def _largest_tile(n, cap, align):
    t = min(n, cap)
    while n % t or t % align:
        t -= 1
    return t


def kernel(x_prompt, x_sample, state_shift, state_wkv, state_conv, p_prompt, p_sample, norm_g, w_in,
           shift_mu, w0, w_lora_b, a0, a_lora_b, k_k, k_a, r_k, lnx_g, lnx_b, w_proj_a, conv_w, conv_b,
           cln_g, cln_b, w_proj_b, w_out, w_ple, w_ple_gate, final_g):
    depth = w_in.shape[0]
    batch, seq, d = x_prompt.shape
    dec_batch, dec_seq, _ = x_sample.shape
    d_a = w_proj_a.shape[1]
    d_b = w_proj_b.shape[1]
    shift_w = shift_mu.shape[1]
    n_heads = d_a // HEAD
    n_pairs = d_a // PAIR
    taps = conv_w.shape[1]
    assert depth == 1 and dec_seq == 1 and d_a == d_b and 2 * d_a == d
    assert shift_w == 3 * d_a + 2 * LORA and d_a % PAIR == 0 and seq % CHUNK == 0

    o1 = shift_w
    o2 = o1 + d_a
    o3 = o2 + 2 * d_b
    o4 = o3 + d_b
    w = w_in[0]
    lora_w = jnp.zeros((2 * LORA, 2 * d_a), F32)
    lora_w = lora_w.at[:LORA, :d_a].set(w_lora_b[0]).at[LORA:, d_a:].set(a_lora_b[0]).astype(BF16)
    zeros_a = jnp.zeros((d_a,), F32)
    par = jnp.stack([w0[0], a0[0], k_k[0], k_a[0], r_k[0].reshape(d_a), lnx_g[0], lnx_b[0], zeros_a])
    cln = jnp.stack([cln_g[0], cln_b[0]])
    mu = shift_mu
    g_in = norm_g[0].reshape(1, d)
    fg = final_g.reshape(1, d)

    m_p = batch * seq
    m_all = m_p + dec_batch
    x2 = x_prompt.reshape(m_p, d)
    xs2 = x_sample.reshape(dec_batch, d)
    n_rkv = 3 * d_a
    xn, p_lora = _rmsnorm_bf16(x2, xs2, g_in, w, n_rkv, 2 * LORA, _largest_tile(m_p, 1024, SUBLANES))
    tm = _largest_tile(m_all, 1100, BF16_SUBLANES)
    n_plain = n_rkv // d_a
    gate_a_block = n_plain
    merge_block = (n_rkv + d_a) // d
    gate_b_block = (n_rkv + d_a + 2 * d) // d_b
    assert n_rkv % d_a == 0 and (n_rkv + d_a) % d == 0
    proj_cols = lambda j: jnp.where(j < n_plain, j * d_a,
                                    jnp.where(j == gate_a_block, o1,
                                              jnp.where(j == gate_b_block, o3, o4 + (j - gate_a_block - 1) * d_a)))
    proj = _project(xn, w, proj_cols, n_rkv + d_a + 2 * d + d_b, tm, d_a, n_plain,
                    silu_blocks=(gate_a_block, gate_b_block))
    p_rkv = gates = proj
    u, (wa, wb, wout, wpg, wple) = _project_glu(
        xn, w, o2, o2 + d_b, d_b, tm, _largest_tile(d_b, 512, LANES),
        casts=[w_proj_a[0], w_proj_b[0], w_out[0], w_ple_gate[0], w_ple[0]])

    oa, st = _wkv_prompt(p_rkv, p_lora, gates, mu, par, lora_w, batch, seq, d_a, gate_a_block)
    c = _conv_prompt(u, conv_w[0], conv_b[0], batch, seq, _largest_tile(seq, 512, CONV_STRIP))
    y_prompt = _tail(oa, c, gates, x2, p_prompt[0].reshape(m_p, -1), cln, wa, wb, wout, wpg, wple, fg,
                     _largest_tile(m_p, 256, BF16_SUBLANES), gate_b_block, merge_block).reshape(batch, seq, d)
    last_rows = lambda a: jnp.concatenate([a[(b + 1) * seq - 1:(b + 1) * seq] for b in range(batch)])
    new_shift_p = jnp.concatenate([last_rows(p_rkv)[:, :n_rkv], last_rows(p_lora)], axis=1)[None]
    st = st.reshape(batch, n_pairs, 2, HEAD, 2, HEAD)
    new_wkv_p = jnp.stack([st[:, :, 0, :, 0, :], st[:, :, 1, :, 1, :]], axis=2)
    new_wkv_p = jnp.swapaxes(new_wkv_p.reshape(batch, n_heads, HEAD, HEAD), -1, -2)[None]
    new_conv_p = jnp.stack([u[(b + 1) * seq - (taps - 1):(b + 1) * seq] for b in range(batch)])[None]

    tm_s = _largest_tile(dec_batch, 256, BF16_SUBLANES)
    assert m_p % tm_s == 0
    oa_s, new_wkv_t = _wkv_sample(p_rkv, p_lora, state_shift[0], gates, mu, par, lora_w,
                                  jnp.transpose(state_wkv[0], (1, 2, 3, 0)), d_a, m_p, gate_a_block)
    new_wkv_s = jnp.transpose(new_wkv_t, (3, 0, 1, 2))
    c_s, new_conv_t = _conv_sample(u, jnp.transpose(state_conv[0], (1, 0, 2)), conv_w[0], conv_b[0],
                                   _largest_tile(dec_batch, 32, SUBLANES), m_p)
    new_conv_s = jnp.transpose(new_conv_t, (1, 0, 2))
    y_sample = _tail(oa_s, c_s, gates, xs2, p_sample[0].reshape(dec_batch, -1), cln, wa, wb, wout, wpg,
                     wple, fg, tm_s, gate_b_block, merge_block, row0=m_p).reshape(dec_batch, 1, d)
    new_shift_s = jnp.concatenate([p_rkv[m_p:, :n_rkv], p_lora[m_p:]], axis=1)[None]
    new_conv_s = new_conv_s[None]

    return (y_prompt, y_sample, new_shift_p, new_wkv_p, new_conv_p, new_shift_s, new_wkv_s[None],
            new_conv_s)
```
